```python
import jax, jax.numpy as jnp
from jax import lax
import numpy as np

D_MODEL = 1024
BATCH = 2
SEQ = 8192
DEPTH = 2

CHUNK = 64
QBLK = 128
N_ATTN_HEADS = 8
ATTN_HEAD_DIM = 64
N_IDX_HEADS = 8
IDX_HEAD_DIM = 64
TOPK_MAX = 256
N_GLA_HEADS = 4
GLA_HEAD_K = 64
GLA_HEAD_V = 128
GLA_GATE_RANK = 16
GLA_GATE_TAU = 16.0
D_FF = 2816
PLE_DIM = 256
EPS = 1e-6

ATTN_W = N_ATTN_HEADS * ATTN_HEAD_DIM
IDX_Q_W = N_IDX_HEADS * IDX_HEAD_DIM
GLA_K_W = N_GLA_HEADS * GLA_HEAD_K
GLA_V_W = N_GLA_HEADS * GLA_HEAD_V
SPLIT_SIZES = (ATTN_W, ATTN_W, ATTN_W, IDX_Q_W, IDX_HEAD_DIM, N_IDX_HEADS,
               GLA_K_W, GLA_K_W, GLA_V_W, GLA_V_W, GLA_GATE_RANK, D_MODEL, D_MODEL)
W_IN = 5720

kernel_name = "hybrid_dsa_gla_macaron_ple"


def rmsnorm(x, g):
    xf = x.astype(jnp.float32)
    y = xf * lax.rsqrt(jnp.mean(xf * xf, axis=-1, keepdims=True) + EPS)
    return (y * g.astype(jnp.float32)).astype(x.dtype)


def swiglu(h, w_gate, w_up, w_down):
    return (jax.nn.silu(h @ w_gate) * (h @ w_up)) @ w_down


def dsa_attention(q, k, v, qi, ki, wi):
    B, S, H, Dh = q.shape
    topk = min(TOPK_MAX, S // 4)
    nblk = S // QBLK
    kchunk = jnp.arange(S) // CHUNK
    ki32 = ki.astype(jnp.float32)
    idx_scale = IDX_HEAD_DIM ** -0.5 * N_IDX_HEADS ** -0.5
    att_scale = Dh ** -0.5

    def block(i):
        start = i * QBLK
        qb = lax.dynamic_slice_in_dim(q, start, QBLK, axis=1)
        qib = lax.dynamic_slice_in_dim(qi, start, QBLK, axis=1).astype(jnp.float32)
        wib = lax.dynamic_slice_in_dim(wi, start, QBLK, axis=1).astype(jnp.float32)
        sc = jnp.einsum('bqhd,bsd->bqhs', qib, ki32)
        I = jnp.einsum('bqh,bqhs->bqs', wib * idx_scale, jax.nn.relu(sc))
        qchunk = (start + jnp.arange(QBLK)) // CHUNK
        adm = kchunk[None, :] <= qchunk[:, None]
        I = jnp.where(adm[None], I, -jnp.inf)
        vals, idx = lax.top_k(I, topk)
        valid = vals > -jnp.inf
        ksel = jax.vmap(lambda kk, ii: kk[ii])(k, idx)
        vsel = jax.vmap(lambda vv, ii: vv[ii])(v, idx)
        logits = jnp.einsum('bqhd,bqkhd->bqhk', qb, ksel).astype(jnp.float32) * att_scale
        logits = jnp.where(valid[:, :, None, :], logits, -jnp.inf)
        prob = jax.nn.softmax(logits, axis=-1).astype(v.dtype)
        return jnp.einsum('bqhk,bqkhd->bqhd', prob, vsel)

    out = lax.map(block, jnp.arange(nblk))
    return out.transpose(1, 0, 2, 3, 4).reshape(B, S, H, Dh)


def gla_chunked(q, k, v, logg):
    B, S, H, dk = q.shape
    dv = v.shape[-1]
    n = S // CHUNK

    def to_chunks(t):
        return t.astype(jnp.float32).reshape(B, n, CHUNK, H, t.shape[-1]).transpose(1, 0, 3, 2, 4)

    qc, kc, vc, gc = to_chunks(q * (dk ** -0.5)), to_chunks(k), to_chunks(v), to_chunks(logg)
    causal = jnp.tril(jnp.ones((CHUNK, CHUNK), dtype=bool))

    def step(state, inp):
        qb, kb, vb, gb = inp
        b = jnp.cumsum(gb, axis=2)
        o_inter = jnp.einsum('bhtd,bhde->bhte', qb * jnp.exp(b), state)
        diff = b[:, :, :, None, :] - b[:, :, None, :, :]
        decay = jnp.exp(jnp.where(causal[None, None, :, :, None], diff, -jnp.inf))
        attn = jnp.einsum('bhtd,bhsd,bhtsd->bhts', qb, kb, decay)
        o = o_inter + jnp.einsum('bhts,bhse->bhte', attn, vb)
        b_last = b[:, :, -1:, :]
        state = jnp.exp(b_last[:, :, 0, :])[..., None] * state + \
            jnp.einsum('bhsd,bhse->bhde', kb * jnp.exp(b_last - b), vb)
        return state, o

    state0 = jnp.zeros((B, H, dk, dv), jnp.float32)
    _, o = lax.scan(step, state0, (qc, kc, vc, gc))
    return o.transpose(1, 0, 3, 2, 4).reshape(B, S, H, dv).astype(v.dtype)


def token_mixer(h, w_in, gla_gate_w2, gla_gate_b, gla_norm, w_branch_a, w_branch_b, w_out):
    B, S, _ = h.shape
    z = h @ w_in
    cuts = np.cumsum(np.array(SPLIT_SIZES))[:-1].tolist()
    (aq, ak, av, iq, ik, iw, gq, gk, gv, gr, ga, mga, mgb) = jnp.split(z, cuts, axis=-1)
    hd = lambda t, nh: t.reshape(B, S, nh, -1)
    oa = dsa_attention(hd(aq, N_ATTN_HEADS), hd(ak, N_ATTN_HEADS), hd(av, N_ATTN_HEADS),
                       hd(iq, N_IDX_HEADS), ik, iw)
    ya = oa.reshape(B, S, ATTN_W) @ w_branch_a
    glog = jax.nn.log_sigmoid((ga @ gla_gate_w2 + gla_gate_b).astype(jnp.float32)) / GLA_GATE_TAU
    ob = gla_chunked(hd(gq, N_GLA_HEADS), hd(gk, N_GLA_HEADS), hd(gv, N_GLA_HEADS),
                     glog.reshape(B, S, N_GLA_HEADS, GLA_HEAD_K))
    ob = rmsnorm(ob, gla_norm).reshape(B, S, GLA_V_W) * jax.nn.silu(gr)
    yb = ob @ w_branch_b
    merged = jax.nn.sigmoid(mga) * ya + jax.nn.sigmoid(mgb) * yb
    return merged @ w_out


def setup_inputs(seed: int = 0) -> dict:
    key = jax.random.key(seed)
    ks = jax.random.split(key, 24)
    nrm = lambda k, shape, fan: jax.random.normal(k, shape, jnp.float32) * (fan ** -0.5)
    gain = lambda k, shape: 1.0 + 0.01 * jax.random.normal(k, shape, jnp.float32)
    L = DEPTH
    return {
        "x": jax.random.normal(ks[0], (BATCH, SEQ, D_MODEL), jnp.float32),
        "p": jax.random.normal(ks[1], (DEPTH, BATCH, SEQ, PLE_DIM), jnp.float32),
        "w_in": nrm(ks[2], (L, D_MODEL, W_IN), D_MODEL),
        "gla_gate_w2": nrm(ks[3], (L, GLA_GATE_RANK, GLA_K_W), GLA_GATE_RANK),
        "gla_gate_b": 0.1 * jax.random.normal(ks[4], (L, GLA_K_W), jnp.float32),
        "gla_norm": gain(ks[5], (L, GLA_HEAD_V)),
        "w_branch_a": nrm(ks[6], (L, ATTN_W, D_MODEL), ATTN_W),
        "w_branch_b": nrm(ks[7], (L, GLA_V_W, D_MODEL), GLA_V_W),
        "w_out": nrm(ks[8], (L, D_MODEL, D_MODEL), D_MODEL),
        "norm_ff1": gain(ks[9], (L, D_MODEL)),
        "norm_mix": gain(ks[10], (L, D_MODEL)),
        "norm_ff2": gain(ks[11], (L, D_MODEL)),
        "norm_ple": gain(ks[12], (L, D_MODEL)),
        "ff1_w_gate": nrm(ks[13], (L, D_MODEL, D_FF), D_MODEL),
        "ff1_w_up": nrm(ks[14], (L, D_MODEL, D_FF), D_MODEL),
        "ff1_w_down": nrm(ks[15], (L, D_FF, D_MODEL), D_FF),
        "ff2_w_gate": nrm(ks[16], (L, D_MODEL, D_FF), D_MODEL),
        "ff2_w_up": nrm(ks[17], (L, D_MODEL, D_FF), D_MODEL),
        "ff2_w_down": nrm(ks[18], (L, D_FF, D_MODEL), D_FF),
        "ple_w_proj": nrm(ks[19], (L, PLE_DIM, D_MODEL), PLE_DIM),
        "ple_w_gate": nrm(ks[20], (L, D_MODEL, D_MODEL), D_MODEL),
        "norm_final": gain(ks[21], (D_MODEL,)),
    }


def reference(x, p, w_in, gla_gate_w2, gla_gate_b, gla_norm, w_branch_a, w_branch_b, w_out,
              norm_ff1, norm_mix, norm_ff2, norm_ple, ff1_w_gate, ff1_w_up, ff1_w_down,
              ff2_w_gate, ff2_w_up, ff2_w_down, ple_w_proj, ple_w_gate, norm_final):
    for i in range(DEPTH):
        x = x + 0.5 * swiglu(rmsnorm(x, norm_ff1[i]), ff1_w_gate[i], ff1_w_up[i], ff1_w_down[i])
        x = x + token_mixer(rmsnorm(x, norm_mix[i]), w_in[i], gla_gate_w2[i], gla_gate_b[i],
                            gla_norm[i], w_branch_a[i], w_branch_b[i], w_out[i])
        x = x + 0.5 * swiglu(rmsnorm(x, norm_ff2[i]), ff2_w_gate[i], ff2_w_up[i], ff2_w_down[i])
        g = jax.nn.sigmoid(rmsnorm(x, norm_ple[i]) @ ple_w_gate[i])
        x = x + g * (p[i] @ ple_w_proj[i])
    return rmsnorm(x, norm_final)
```

```python
import functools

import jax
import jax.numpy as jnp
import numpy as np
from jax import lax
from jax.experimental import pallas as pl
from jax.experimental.pallas import tpu as pltpu

D_MODEL = 1024
D_FF = 2816
PLE_DIM = 256
DEPTH = 2
EPS = 1e-6

CHUNK = 64
N_ATTN_HEADS = 8
ATTN_HEAD_DIM = 64
N_IDX_HEADS = 8
IDX_HEAD_DIM = 64
TOPK_MAX = 256
N_GLA_HEADS = 4
GLA_HEAD_K = 64
GLA_HEAD_V = 128
GLA_GATE_RANK = 16
GLA_GATE_TAU = 16.0

ATTN_W = N_ATTN_HEADS * ATTN_HEAD_DIM
IDX_Q_W = N_IDX_HEADS * IDX_HEAD_DIM
GLA_K_W = N_GLA_HEADS * GLA_HEAD_K
GLA_V_W = N_GLA_HEADS * GLA_HEAD_V
SPLIT_SIZES = (ATTN_W, ATTN_W, ATTN_W, IDX_Q_W, IDX_HEAD_DIM, N_IDX_HEADS,
               GLA_K_W, GLA_K_W, GLA_V_W, GLA_V_W, GLA_GATE_RANK, D_MODEL, D_MODEL)

LANES = 128
SUBLANES = 8
VMEM_LIMIT_BYTES = 56 * 1024 * 1024

COL_TILE = 512
Z_MGA, Z_MGB = 0, 1
Z_AQ, Z_AK, Z_AV, Z_IQ, Z_GQK, Z_GV, Z_GR = 4, 5, 6, 7, 8, 9, 10
Z_WIDTH = 11 * COL_TILE
SM_IW0 = 0
SM_GA0 = N_IDX_HEADS

TOKEN_TILE = 512
FF_TILE = 1408
DSA_BLOCK = 256
GLA_TILE = 512

NEG_BIG = -1e30
F32_LOWEST = float(np.finfo(np.float32).min)
INT32_MIN = int(np.iinfo(np.int32).min)

f32 = jnp.float32
bf16 = jnp.bfloat16


def _rms(x, g):
    return x * lax.rsqrt(jnp.mean(x * x, axis=-1, keepdims=True) + EPS) * g


def _sigmoid(x):
    return 1.0 / (1.0 + jnp.exp(-x))


def _cparams(sem):
    return pltpu.CompilerParams(dimension_semantics=sem, vmem_limit_bytes=VMEM_LIMIT_BYTES)


def _ffn_kernel(x_ref, g_ref, wg_ref, wu_ref, wd_ref, o_ref, h_ref, acc_ref):
    f = pl.program_id(1)

    @pl.when(f == 0)
    def _():
        h_ref[...] = _rms(x_ref[...], g_ref[...]).astype(bf16)
        acc_ref[...] = jnp.zeros_like(acc_ref)

    h = h_ref[...]
    gate = jnp.dot(h, wg_ref[...], preferred_element_type=f32)
    up = jnp.dot(h, wu_ref[...], preferred_element_type=f32)
    a = (gate * _sigmoid(gate) * up).astype(bf16)
    acc_ref[...] += jnp.dot(a, wd_ref[...], preferred_element_type=f32)

    @pl.when(f == pl.num_programs(1) - 1)
    def _():
        o_ref[...] = x_ref[...] + 0.5 * acc_ref[...]


def _ffn(x, g, wg, wu, wd):
    t = x.shape[0]
    return pl.pallas_call(
        _ffn_kernel,
        grid=(t // TOKEN_TILE, D_FF // FF_TILE),
        in_specs=[
            pl.BlockSpec((TOKEN_TILE, D_MODEL), lambda i, f: (i, 0)),
            pl.BlockSpec((1, D_MODEL), lambda i, f: (0, 0)),
            pl.BlockSpec((D_MODEL, FF_TILE), lambda i, f: (0, f)),
            pl.BlockSpec((D_MODEL, FF_TILE), lambda i, f: (0, f)),
            pl.BlockSpec((FF_TILE, D_MODEL), lambda i, f: (f, 0)),
        ],
        out_specs=pl.BlockSpec((TOKEN_TILE, D_MODEL), lambda i, f: (i, 0)),
        out_shape=jax.ShapeDtypeStruct((t, D_MODEL), f32),
        scratch_shapes=[pltpu.VMEM((TOKEN_TILE, D_MODEL), bf16),
                        pltpu.VMEM((TOKEN_TILE, D_MODEL), f32)],
        compiler_params=_cparams(("parallel", "arbitrary")),
        name="ffn",
    )(x, g, wg, wu, wd)


def _proj_kernel(x_ref, g_ref, w_ref, ws_ref, z_ref, ik_ref, sm_ref, h_ref):
    j = pl.program_id(1)

    @pl.when(j == 0)
    def _():
        h = _rms(x_ref[...], g_ref[...]).astype(bf16)
        h_ref[...] = h
        s = jnp.dot(h, ws_ref[...], preferred_element_type=f32)
        ik_ref[...] = s[:, :IDX_HEAD_DIM].astype(bf16)
        sm_ref[...] = s[:, LANES:]

    z_ref[...] = jnp.dot(h_ref[...], w_ref[...], preferred_element_type=f32).astype(bf16)


def _proj(x, g, w_main, w_small):
    t = x.shape[0]
    return pl.pallas_call(
        _proj_kernel,
        grid=(t // TOKEN_TILE, Z_WIDTH // COL_TILE),
        in_specs=[
            pl.BlockSpec((TOKEN_TILE, D_MODEL), lambda i, j: (i, 0)),
            pl.BlockSpec((1, D_MODEL), lambda i, j: (0, 0)),
            pl.BlockSpec((D_MODEL, COL_TILE), lambda i, j: (0, j)),
            pl.BlockSpec((D_MODEL, 2 * LANES), lambda i, j: (0, 0)),
        ],
        out_specs=[
            pl.BlockSpec((TOKEN_TILE, COL_TILE), lambda i, j: (i, j)),
            pl.BlockSpec((TOKEN_TILE, IDX_HEAD_DIM), lambda i, j: (i, 0)),
            pl.BlockSpec((TOKEN_TILE, LANES), lambda i, j: (i, 0)),
        ],
        out_shape=[
            jax.ShapeDtypeStruct((t, Z_WIDTH), bf16),
            jax.ShapeDtypeStruct((t, IDX_HEAD_DIM), bf16),
            jax.ShapeDtypeStruct((t, LANES), f32),
        ],
        scratch_shapes=[pltpu.VMEM((TOKEN_TILE, D_MODEL), bf16)],
        compiler_params=_cparams(("parallel", "arbitrary")),
        name="proj",
    )(x, g, w_main, w_small)


def _dsa_kernel(qT_ref, iqT_ref, iwT_ref, k_ref, vT_ref, ik_ref, o_ref,
                ibuf, qTm, m_ref, l_ref, acc_ref, thr_ref, *, seq_len):
    qb = DSA_BLOCK
    kt_rows = DSA_BLOCK
    i = pl.program_id(1)
    n_tiles = i + 1
    idx_bits = int(np.log2(seq_len))

    rowid = lax.broadcasted_iota(jnp.int32, (LANES, qb), 0)
    for h in range(N_ATTN_HEADS):
        pair = qT_ref[(h // 2) * LANES:(h // 2 + 1) * LANES, :]
        keep = (rowid < ATTN_HEAD_DIM) if h % 2 == 0 else (rowid >= ATTN_HEAD_DIM)
        qTm[h] = jnp.where(keep, pair, jnp.zeros_like(pair))
    m_ref[...] = jnp.full(m_ref.shape, NEG_BIG, f32)
    l_ref[...] = jnp.zeros_like(l_ref)
    acc_ref[...] = jnp.zeros_like(acc_ref)

    def idx_tile(kt, diagonal):
        r0 = pl.multiple_of(kt * kt_rows, kt_rows)
        ki_t = ik_ref[pl.ds(r0, kt_rows), :]
        sc = jnp.zeros((kt_rows, qb), f32)
        for h in range(N_IDX_HEADS):
            s = jnp.dot(ki_t, iqT_ref[h * IDX_HEAD_DIM:(h + 1) * IDX_HEAD_DIM, :],
                        preferred_element_type=f32)
            sc = sc + iwT_ref[h:h + 1, :] * jnp.maximum(s, 0.0)
        if diagonal:
            kc = lax.broadcasted_iota(jnp.int32, (kt_rows, qb), 0) // CHUNK
            qc = lax.broadcasted_iota(jnp.int32, (kt_rows, qb), 1) // CHUNK
            sc = jnp.where(kc <= qc, sc, -jnp.inf)
        ibuf[pl.ds(r0, kt_rows), :] = sc

    def idx_body(kt, carry):
        idx_tile(kt, False)
        return carry

    lax.fori_loop(0, i, idx_body, 0)
    idx_tile(i, True)

    def count(pred):
        def body(r, acc):
            r0 = pl.multiple_of(r * kt_rows, kt_rows)
            tile = ibuf[pl.ds(r0, kt_rows), :]
            rows = r0 + lax.broadcasted_iota(jnp.int32, (kt_rows, qb), 0)
            ind = jnp.where(pred(tile, rows), 1.0, 0.0)
            return acc + jnp.sum(ind.reshape(kt_rows // SUBLANES, SUBLANES, qb), axis=0)
        acc = lax.fori_loop(0, n_tiles, body, jnp.zeros((SUBLANES, qb), f32))
        return jnp.sum(acc, axis=0, keepdims=True)

    qpos = i * qb + lax.broadcasted_iota(jnp.int32, (1, qb), 1)
    n_adm = (qpos // CHUNK + 1) * CHUNK
    select_all = n_adm <= TOPK_MAX
    kf = float(TOPK_MAX)

    thr_ref[...] = jnp.full((1, qb), F32_LOWEST, f32)

    @pl.when(n_tiles * kt_rows > TOPK_MAX)
    def _():
        def key_to_f32(key):
            bits = jnp.where(key < 0, key ^ jnp.int32(0x7FFFFFFF), key)
            return lax.bitcast_convert_type(bits, f32)

        def bit_body(t, prefix):
            step = lax.shift_left(jnp.int32(1), jnp.int32(31) - t)
            cand = prefix + step
            cand_f = key_to_f32(cand)
            cnt = count(lambda tile, rows: tile >= cand_f)
            return jnp.where(cnt >= kf, cand, prefix)

        prefix = lax.fori_loop(0, 32, bit_body, jnp.full((1, qb), INT32_MIN, jnp.int32))
        thr = jnp.where(select_all, F32_LOWEST, key_to_f32(prefix))
        thr_ref[...] = thr

        cnt_ge = count(lambda tile, rows: tile >= thr)
        excess = jnp.max(jnp.where(select_all, 0.0, cnt_ge - kf))

        @pl.when(excess > 0.0)
        def _():
            need = kf - count(lambda tile, rows: tile > thr)

            def tie_body(t, pos):
                cand = pos | lax.shift_left(jnp.int32(1), jnp.int32(idx_bits - 1) - t)
                below = count(lambda tile, rows: jnp.where(tile == thr, rows, seq_len) < cand)
                return jnp.where(below < need, cand, pos)

            last = lax.fori_loop(0, idx_bits, tie_body, jnp.zeros((1, qb), jnp.int32))

            def drop_body(r, carry):
                r0 = pl.multiple_of(r * kt_rows, kt_rows)
                tile = ibuf[pl.ds(r0, kt_rows), :]
                rows = r0 + lax.broadcasted_iota(jnp.int32, (kt_rows, qb), 0)
                dropped = jnp.where(tile == thr, rows, -1) > last
                ibuf[pl.ds(r0, kt_rows), :] = jnp.where(dropped, -jnp.inf, tile)
                return carry

            lax.fori_loop(0, n_tiles, drop_body, 0)

    thr = thr_ref[...]

    def att_body(kt, carry):
        r0 = pl.multiple_of(kt * kt_rows, kt_rows)
        bias = jnp.where(ibuf[pl.ds(r0, kt_rows), :] >= thr, 0.0, NEG_BIG)
        for h in range(N_ATTN_HEADS):
            k_pair = k_ref[pl.ds(r0, kt_rows), (h // 2) * LANES:(h // 2 + 1) * LANES]
            s = jnp.dot(k_pair, qTm[h], preferred_element_type=f32) + bias
            m_old = m_ref[h]
            m_new = jnp.maximum(m_old, jnp.max(s, axis=0, keepdims=True))
            alpha = jnp.exp2(m_old - m_new)
            p = jnp.exp2(s - m_new)
            l_ref[h] = alpha * l_ref[h] + jnp.sum(p, axis=0, keepdims=True)
            hs = slice(h * ATTN_HEAD_DIM, (h + 1) * ATTN_HEAD_DIM)
            pv = jnp.dot(vT_ref[hs, pl.ds(r0, kt_rows)], p.astype(bf16),
                         preferred_element_type=f32)
            acc_ref[hs, :] = alpha * acc_ref[hs, :] + pv
            m_ref[h] = m_new
        return carry

    lax.fori_loop(0, n_tiles, att_body, 0)

    for h in range(N_ATTN_HEADS):
        hs = slice(h * ATTN_HEAD_DIM, (h + 1) * ATTN_HEAD_DIM)
        o_ref[hs, :] = (acc_ref[hs, :] / l_ref[h]).astype(o_ref.dtype)


def _dsa(qT, iqT, iwT, z3, vT, ik3):
    b, _, s = qT.shape
    qb = DSA_BLOCK
    return pl.pallas_call(
        functools.partial(_dsa_kernel, seq_len=s),
        grid=(b, s // qb),
        in_specs=[
            pl.BlockSpec((None, ATTN_W, qb), lambda bi, i: (bi, 0, i)),
            pl.BlockSpec((None, IDX_Q_W, qb), lambda bi, i: (bi, 0, i)),
            pl.BlockSpec((None, N_IDX_HEADS, qb), lambda bi, i: (bi, 0, i)),
            pl.BlockSpec((None, s, COL_TILE), lambda bi, i: (bi, 0, Z_AK)),
            pl.BlockSpec((None, ATTN_W, s), lambda bi, i: (bi, 0, 0)),
            pl.BlockSpec((None, s, IDX_HEAD_DIM), lambda bi, i: (bi, 0, 0)),
        ],
        out_specs=pl.BlockSpec((None, ATTN_W, qb), lambda bi, i: (bi, 0, i)),
        out_shape=jax.ShapeDtypeStruct((b, ATTN_W, s), bf16),
        scratch_shapes=[
            pltpu.VMEM((s, qb), f32),
            pltpu.VMEM((N_ATTN_HEADS, LANES, qb), bf16),
            pltpu.VMEM((N_ATTN_HEADS, 1, qb), f32),
            pltpu.VMEM((N_ATTN_HEADS, 1, qb), f32),
            pltpu.VMEM((ATTN_W, qb), f32),
            pltpu.VMEM((1, qb), f32),
        ],
        compiler_params=_cparams(("parallel", "arbitrary")),
        name="dsa",
    )(qT, iqT, iwT, z3, vT, ik3)


def _gla_kernel(qk_ref, v_ref, r_ref, sm_ref, w2_ref, gb_ref, gn_ref, o_ref, state_ref):
    c = CHUNK

    @pl.when(pl.program_id(1) == 0)
    def _():
        state_ref[...] = jnp.zeros_like(state_ref)

    row = lax.broadcasted_iota(jnp.int32, (c, c), 0)
    col = lax.broadcasted_iota(jnp.int32, (c, c), 1)
    causal = row >= col
    tril = jnp.where(causal, 1.0, 0.0)
    lane = lax.broadcasted_iota(jnp.int32, (c, LANES), 1)
    half = (lane < GLA_HEAD_K, lane >= GLA_HEAD_K)
    contract_last = (((1,), (1,)), ((), ()))
    contract_first = (((0,), (0,)), ((), ()))

    for ci in range(GLA_TILE // c):
        rows = slice(ci * c, (ci + 1) * c)
        qk = qk_ref[rows, :].astype(f32)
        q = qk[:, :GLA_K_W]
        k = qk[:, GLA_K_W:]
        ga = sm_ref[rows, :][:, SM_GA0:SM_GA0 + GLA_GATE_RANK]
        glin = jnp.dot(ga, w2_ref[...], preferred_element_type=f32,
                       precision=lax.Precision.HIGHEST) + gb_ref[...]
        logg = (jnp.minimum(glin, 0.0) - jnp.log(1.0 + jnp.exp(-jnp.abs(glin)))) / GLA_GATE_TAU
        bcum = jnp.dot(tril, logg, preferred_element_type=f32, precision=lax.Precision.HIGHEST)
        b_last = bcum[c - 1:c, :]
        b_mid = bcum[c // 2 - 1:c // 2, :]
        q_in = q * jnp.exp(bcum)
        q_mid = q * jnp.exp(bcum - b_mid)
        k_mid = k * jnp.exp(b_mid - bcum)
        k_out = k * jnp.exp(b_last - bcum)
        decay = jnp.exp(b_last)

        for pr in range(N_GLA_HEADS // 2):
            ps = slice(pr * LANES, (pr + 1) * LANES)
            st = state_ref[pr]
            st_b = st.astype(bf16)
            k_mid_p = k_mid[:, ps].astype(bf16)
            upd = jnp.zeros_like(st)
            for hh in range(2):
                h = pr * 2 + hh
                vs = slice(h * GLA_HEAD_V, (h + 1) * GLA_HEAD_V)
                v_h = v_ref[rows, vs]
                q_mid_m = jnp.where(half[hh], q_mid[:, ps], 0.0).astype(bf16)
                q_in_m = jnp.where(half[hh], q_in[:, ps], 0.0).astype(bf16)
                k_out_m = jnp.where(half[hh], k_out[:, ps], 0.0).astype(bf16)
                attn = lax.dot_general(q_mid_m, k_mid_p, contract_last,
                                       preferred_element_type=f32)
                attn = jnp.where(causal, attn, 0.0).astype(bf16)
                o = lax.dot_general(q_in_m, st_b, contract_last, preferred_element_type=f32)
                o = o + jnp.dot(attn, v_h, preferred_element_type=f32)
                upd = upd + lax.dot_general(v_h, k_out_m, contract_first,
                                            preferred_element_type=f32)
                y = _rms(o, gn_ref[...])
                r = r_ref[rows, vs].astype(f32)
                o_ref[rows, vs] = (y * r * _sigmoid(r)).astype(o_ref.dtype)
            state_ref[pr] = decay[:, ps] * st + upd


def _gla(z3, sm3, w2, gb, gn):
    b, s, _ = z3.shape
    tb = GLA_TILE
    return pl.pallas_call(
        _gla_kernel,
        grid=(b, s // tb),
        in_specs=[
            pl.BlockSpec((None, tb, COL_TILE), lambda bi, j: (bi, j, Z_GQK)),
            pl.BlockSpec((None, tb, COL_TILE), lambda bi, j: (bi, j, Z_GV)),
            pl.BlockSpec((None, tb, COL_TILE), lambda bi, j: (bi, j, Z_GR)),
            pl.BlockSpec((None, tb, LANES), lambda bi, j: (bi, j, 0)),
            pl.BlockSpec((GLA_GATE_RANK, GLA_K_W), lambda bi, j: (0, 0)),
            pl.BlockSpec((1, GLA_K_W), lambda bi, j: (0, 0)),
            pl.BlockSpec((1, GLA_HEAD_V), lambda bi, j: (0, 0)),
        ],
        out_specs=pl.BlockSpec((None, tb, GLA_V_W), lambda bi, j: (bi, j, 0)),
        out_shape=jax.ShapeDtypeStruct((b, s, GLA_V_W), bf16),
        scratch_shapes=[pltpu.VMEM((N_GLA_HEADS // 2, GLA_HEAD_V, LANES), f32)],
        compiler_params=_cparams(("parallel", "arbitrary")),
        name="gla",
    )(z3, z3, z3, sm3, w2, gb, gn)


def _merge_kernel(x_ref, oa_ref, ob_ref, ga_ref, gb_ref, wa_ref, wb_ref, wo_ref, o_ref):
    ya = jnp.dot(oa_ref[...], wa_ref[...], preferred_element_type=f32)
    yb = jnp.dot(ob_ref[...], wb_ref[...], preferred_element_type=f32)
    mg = _sigmoid(ga_ref[...].astype(f32)) * ya + _sigmoid(gb_ref[...].astype(f32)) * yb
    o_ref[...] = x_ref[...] + jnp.dot(mg.astype(bf16), wo_ref[...], preferred_element_type=f32)


def _merge(x, oa, ob, z, wa, wb, wo):
    t = x.shape[0]
    tm = TOKEN_TILE
    return pl.pallas_call(
        _merge_kernel,
        grid=(t // tm,),
        in_specs=[
            pl.BlockSpec((tm, D_MODEL), lambda i: (i, 0)),
            pl.BlockSpec((tm, ATTN_W), lambda i: (i, 0)),
            pl.BlockSpec((tm, GLA_V_W), lambda i: (i, 0)),
            pl.BlockSpec((tm, D_MODEL), lambda i: (i, Z_MGA)),
            pl.BlockSpec((tm, D_MODEL), lambda i: (i, Z_MGB)),
            pl.BlockSpec((ATTN_W, D_MODEL), lambda i: (0, 0)),
            pl.BlockSpec((GLA_V_W, D_MODEL), lambda i: (0, 0)),
            pl.BlockSpec((D_MODEL, D_MODEL), lambda i: (0, 0)),
        ],
        out_specs=pl.BlockSpec((tm, D_MODEL), lambda i: (i, 0)),
        out_shape=jax.ShapeDtypeStruct((t, D_MODEL), f32),
        compiler_params=_cparams(("parallel",)),
        name="merge",
    )(x, oa, ob, z, z, wa, wb, wo)


def _ple_kernel(x_ref, p_ref, g_ref, wg_ref, wp_ref, gf_ref, o_ref, *, final):
    x = x_ref[...]
    h = _rms(x, g_ref[...]).astype(bf16)
    gate = _sigmoid(jnp.dot(h, wg_ref[...], preferred_element_type=f32))
    e = jnp.dot(p_ref[...].astype(bf16), wp_ref[...], preferred_element_type=f32)
    y = x + gate * e
    if final:
        y = _rms(y, gf_ref[...])
    o_ref[...] = y


def _ple(x, p, g, wg, wp, gf, final):
    t = x.shape[0]
    tm = TOKEN_TILE
    return pl.pallas_call(
        functools.partial(_ple_kernel, final=final),
        grid=(t // tm,),
        in_specs=[
            pl.BlockSpec((tm, D_MODEL), lambda i: (i, 0)),
            pl.BlockSpec((tm, PLE_DIM), lambda i: (i, 0)),
            pl.BlockSpec((1, D_MODEL), lambda i: (0, 0)),
            pl.BlockSpec((D_MODEL, D_MODEL), lambda i: (0, 0)),
            pl.BlockSpec((PLE_DIM, D_MODEL), lambda i: (0, 0)),
            pl.BlockSpec((1, D_MODEL), lambda i: (0, 0)),
        ],
        out_specs=pl.BlockSpec((tm, D_MODEL), lambda i: (i, 0)),
        out_shape=jax.ShapeDtypeStruct((t, D_MODEL), f32),
        compiler_params=_cparams(("parallel",)),
        name="ple",
    )(x, p, g, wg, wp, gf)


def _split_w_in(w_in):
    cuts = np.cumsum(np.array(SPLIT_SIZES))[:-1].tolist()
    aq, ak, av, iq, ik, iw, gq, gk, gv, gr, ga, mga, mgb = jnp.split(w_in, cuts, axis=-1)
    att_scale = ATTN_HEAD_DIM ** -0.5 * float(np.log2(np.e))
    idx_scale = IDX_HEAD_DIM ** -0.5 * N_IDX_HEADS ** -0.5
    main = jnp.concatenate(
        [mga, mgb, aq * att_scale, ak, av, iq, gq * (GLA_HEAD_K ** -0.5), gk, gv, gr], axis=-1)
    zeros = lambda n: jnp.zeros(w_in.shape[:-1] + (n,), w_in.dtype)
    small = jnp.concatenate(
        [ik, zeros(LANES - IDX_HEAD_DIM), iw * idx_scale, ga,
         zeros(LANES - N_IDX_HEADS - GLA_GATE_RANK)], axis=-1)
    return main.astype(bf16), small.astype(bf16)


def kernel(x, p, w_in, gla_gate_w2, gla_gate_b, gla_norm, w_branch_a, w_branch_b, w_out,
           norm_ff1, norm_mix, norm_ff2, norm_ple, ff1_w_gate, ff1_w_up, ff1_w_down,
           ff2_w_gate, ff2_w_up, ff2_w_down, ple_w_proj, ple_w_gate, norm_final):
    b, s, d = x.shape
    t = b * s
    depth = w_in.shape[0]
    w_main, w_small = _split_w_in(w_in)
    cast = lambda w: w.astype(bf16)
    ff1 = (cast(ff1_w_gate), cast(ff1_w_up), cast(ff1_w_down))
    ff2 = (cast(ff2_w_gate), cast(ff2_w_up), cast(ff2_w_down))
    wa, wb, wo = cast(w_branch_a), cast(w_branch_b), cast(w_out)
    wpg, wpp = cast(ple_w_gate), cast(ple_w_proj)
    row = lambda v: v.reshape(1, -1)

    xf = x.reshape(t, d)
    for l in range(depth):
        xf = _ffn(xf, row(norm_ff1[l]), ff1[0][l], ff1[1][l], ff1[2][l])

        z, ik, sm = _proj(xf, row(norm_mix[l]), w_main[l], w_small[l])
        z3 = z.reshape(b, s, Z_WIDTH)
        sm3 = sm.reshape(b, s, LANES)
        feature_major = lambda a: jnp.swapaxes(a, 1, 2)
        qT = feature_major(z3[:, :, Z_AQ * COL_TILE:(Z_AQ + 1) * COL_TILE])
        vT = feature_major(z3[:, :, Z_AV * COL_TILE:(Z_AV + 1) * COL_TILE])
        iqT = feature_major(z3[:, :, Z_IQ * COL_TILE:(Z_IQ + 1) * COL_TILE])
        iwT = feature_major(sm3[:, :, SM_IW0:SM_IW0 + N_IDX_HEADS])
        oaT = _dsa(qT, iqT, iwT, z3, vT, ik.reshape(b, s, IDX_HEAD_DIM))
        oa = feature_major(oaT).reshape(t, ATTN_W)

        ob = _gla(z3, sm3, gla_gate_w2[l], row(gla_gate_b[l]), row(gla_norm[l]))
        xf = _merge(xf, oa, ob.reshape(t, GLA_V_W), z, wa[l], wb[l], wo[l])

        xf = _ffn(xf, row(norm_ff2[l]), ff2[0][l], ff2[1][l], ff2[2][l])
        xf = _ple(xf, p[l].reshape(t, PLE_DIM), row(norm_ple[l]), wpg[l], wpp[l],
                  row(norm_final), final=(l == depth - 1))
    return xf.reshape(b, s, d)
```

```python
import functools
import statistics

import jax
import jax.numpy as jnp
import numpy as np
from jax import lax
from jax.experimental import pallas as pl
from jax.experimental.pallas import tpu as pltpu

D_MODEL = 1024
D_FF = 2816
PLE_DIM = 256
DEPTH = 2
EPS = 1e-6

CHUNK = 64
N_ATTN_HEADS = 8
ATTN_HEAD_DIM = 64
N_IDX_HEADS = 8
IDX_HEAD_DIM = 64
TOPK_MAX = 256
N_GLA_HEADS = 4
GLA_HEAD_K = 64
GLA_HEAD_V = 128
GLA_GATE_RANK = 16
GLA_GATE_TAU = 16.0

ATTN_W = N_ATTN_HEADS * ATTN_HEAD_DIM
IDX_Q_W = N_IDX_HEADS * IDX_HEAD_DIM
GLA_K_W = N_GLA_HEADS * GLA_HEAD_K
GLA_V_W = N_GLA_HEADS * GLA_HEAD_V
SPLIT_SIZES = (ATTN_W, ATTN_W, ATTN_W, IDX_Q_W, IDX_HEAD_DIM, N_IDX_HEADS,
               GLA_K_W, GLA_K_W, GLA_V_W, GLA_V_W, GLA_GATE_RANK, D_MODEL, D_MODEL)

LANES = 128
SUBLANES = 8
VMEM_LIMIT_BYTES = 56 * 1024 * 1024

COL_TILE = 512
Z_MGA, Z_MGB = 0, 1
Z_AQ, Z_AK, Z_AV, Z_IQ, Z_GQK, Z_GV, Z_GR = 4, 5, 6, 7, 8, 9, 10
Z_WIDTH = 11 * COL_TILE
SM_IW0 = 0
SM_GA0 = N_IDX_HEADS

TOKEN_TILE = 512
FF_TILE = 1408
DSA_BLOCK = 256
GLA_TILE = 512
SEARCH_MAX_ITERS = 24

NEG_BIG = -1e30
F32_LOWEST = float(np.finfo(np.float32).min)
INT32_MIN = int(np.iinfo(np.int32).min)

f32 = jnp.float32
bf16 = jnp.bfloat16


def _rms(x, g):
    return x * lax.rsqrt(jnp.mean(x * x, axis=-1, keepdims=True) + EPS) * g


def _sigmoid(x):
    return 1.0 / (1.0 + jnp.exp(-x))


def _cparams(sem):
    return pltpu.CompilerParams(dimension_semantics=sem, vmem_limit_bytes=VMEM_LIMIT_BYTES)


def _ffn_kernel(x_ref, g_ref, wg_ref, wu_ref, wd_ref, o_ref, h_ref, acc_ref):
    f = pl.program_id(1)

    @pl.when(f == 0)
    def _():
        h_ref[...] = _rms(x_ref[...], g_ref[...]).astype(bf16)
        acc_ref[...] = jnp.zeros_like(acc_ref)

    h = h_ref[...]
    gate = jnp.dot(h, wg_ref[...], preferred_element_type=f32)
    up = jnp.dot(h, wu_ref[...], preferred_element_type=f32)
    a = (gate * _sigmoid(gate) * up).astype(bf16)
    acc_ref[...] += jnp.dot(a, wd_ref[...], preferred_element_type=f32)

    @pl.when(f == pl.num_programs(1) - 1)
    def _():
        o_ref[...] = x_ref[...] + 0.5 * acc_ref[...]


def _ffn(x, g, wg, wu, wd):
    t = x.shape[0]
    return pl.pallas_call(
        _ffn_kernel,
        grid=(t // TOKEN_TILE, D_FF // FF_TILE),
        in_specs=[
            pl.BlockSpec((TOKEN_TILE, D_MODEL), lambda i, f: (i, 0)),
            pl.BlockSpec((1, D_MODEL), lambda i, f: (0, 0)),
            pl.BlockSpec((D_MODEL, FF_TILE), lambda i, f: (0, f)),
            pl.BlockSpec((D_MODEL, FF_TILE), lambda i, f: (0, f)),
            pl.BlockSpec((FF_TILE, D_MODEL), lambda i, f: (f, 0)),
        ],
        out_specs=pl.BlockSpec((TOKEN_TILE, D_MODEL), lambda i, f: (i, 0)),
        out_shape=jax.ShapeDtypeStruct((t, D_MODEL), f32),
        scratch_shapes=[pltpu.VMEM((TOKEN_TILE, D_MODEL), bf16),
                        pltpu.VMEM((TOKEN_TILE, D_MODEL), f32)],
        compiler_params=_cparams(("parallel", "arbitrary")),
        name="ffn",
    )(x, g, wg, wu, wd)


def _proj_kernel(x_ref, g_ref, w_ref, ws_ref, z_ref, ik_ref, sm_ref, h_ref):
    j = pl.program_id(1)

    @pl.when(j == 0)
    def _():
        h = _rms(x_ref[...], g_ref[...]).astype(bf16)
        h_ref[...] = h
        s = jnp.dot(h, ws_ref[...], preferred_element_type=f32)
        ik_ref[...] = s[:, :IDX_HEAD_DIM].astype(bf16)
        sm_ref[...] = s[:, LANES:]

    z_ref[...] = jnp.dot(h_ref[...], w_ref[...], preferred_element_type=f32).astype(bf16)


def _proj(x, g, w_main, w_small):
    t = x.shape[0]
    return pl.pallas_call(
        _proj_kernel,
        grid=(t // TOKEN_TILE, Z_WIDTH // COL_TILE),
        in_specs=[
            pl.BlockSpec((TOKEN_TILE, D_MODEL), lambda i, j: (i, 0)),
            pl.BlockSpec((1, D_MODEL), lambda i, j: (0, 0)),
            pl.BlockSpec((D_MODEL, COL_TILE), lambda i, j: (0, j)),
            pl.BlockSpec((D_MODEL, 2 * LANES), lambda i, j: (0, 0)),
        ],
        out_specs=[
            pl.BlockSpec((TOKEN_TILE, COL_TILE), lambda i, j: (i, j)),
            pl.BlockSpec((TOKEN_TILE, IDX_HEAD_DIM), lambda i, j: (i, 0)),
            pl.BlockSpec((TOKEN_TILE, LANES), lambda i, j: (i, 0)),
        ],
        out_shape=[
            jax.ShapeDtypeStruct((t, Z_WIDTH), bf16),
            jax.ShapeDtypeStruct((t, IDX_HEAD_DIM), bf16),
            jax.ShapeDtypeStruct((t, LANES), f32),
        ],
        scratch_shapes=[pltpu.VMEM((TOKEN_TILE, D_MODEL), bf16)],
        compiler_params=_cparams(("parallel", "arbitrary")),
        name="proj",
    )(x, g, w_main, w_small)


def _dsa_kernel(qT_ref, iqT_ref, iwT_ref, tab_ref, k_ref, vT_ref, ik_ref, o_ref,
                ibuf, qTm, s_ref, bias_ref, m_ref, l_ref, alpha_ref, acc_ref, thr_ref,
                pending_ref, *, seq_len):
    qb = DSA_BLOCK
    kt_rows = DSA_BLOCK
    i = pl.program_id(1)
    n_tiles = i + 1
    idx_bits = int(np.log2(seq_len))

    rowid = lax.broadcasted_iota(jnp.int32, (LANES, qb), 0)
    for h in range(N_ATTN_HEADS):
        pair = qT_ref[(h // 2) * LANES:(h // 2 + 1) * LANES, :]
        keep = (rowid < ATTN_HEAD_DIM) if h % 2 == 0 else (rowid >= ATTN_HEAD_DIM)
        qTm[h] = jnp.where(keep, pair, jnp.zeros_like(pair))
    m_ref[...] = jnp.full(m_ref.shape, NEG_BIG, f32)
    l_ref[...] = jnp.zeros_like(l_ref)
    acc_ref[...] = jnp.zeros_like(acc_ref)

    def fold(x, op):
        return op(x.reshape(kt_rows // SUBLANES, SUBLANES, qb), axis=0)

    def idx_tile(kt, diagonal, stats):
        r0 = pl.multiple_of(kt * kt_rows, kt_rows)
        ki_t = ik_ref[pl.ds(r0, kt_rows), :]
        sc = jnp.zeros((kt_rows, qb), f32)
        for h in range(N_IDX_HEADS):
            s = jnp.dot(ki_t, iqT_ref[h * IDX_HEAD_DIM:(h + 1) * IDX_HEAD_DIM, :],
                        preferred_element_type=f32)
            sc = sc + iwT_ref[h:h + 1, :] * jnp.maximum(s, 0.0)
        sc_hi = sc_lo = sc_0 = sc
        if diagonal:
            kc = lax.broadcasted_iota(jnp.int32, (kt_rows, qb), 0) // CHUNK
            qc = lax.broadcasted_iota(jnp.int32, (kt_rows, qb), 1) // CHUNK
            adm = kc <= qc
            sc_hi = jnp.where(adm, sc, -jnp.inf)
            sc_lo = jnp.where(adm, sc, jnp.inf)
            sc_0 = jnp.where(adm, sc, 0.0)
        ibuf[pl.ds(r0, kt_rows), :] = sc_hi
        mx8, mn8, s8, ss8 = stats
        return (jnp.maximum(mx8, fold(sc_hi, jnp.max)), jnp.minimum(mn8, fold(sc_lo, jnp.min)),
                s8 + fold(sc_0, jnp.sum), ss8 + fold(sc_0 * sc_0, jnp.sum))

    stats0 = (jnp.full((SUBLANES, qb), -jnp.inf, f32), jnp.full((SUBLANES, qb), jnp.inf, f32),
              jnp.zeros((SUBLANES, qb), f32), jnp.zeros((SUBLANES, qb), f32))
    stats = lax.fori_loop(0, i, lambda kt, st: idx_tile(kt, False, st), stats0)
    mx8, mn8, s8, ss8 = idx_tile(i, True, stats)
    col_max = jnp.max(mx8, axis=0, keepdims=True)
    col_min = jnp.min(mn8, axis=0, keepdims=True)
    col_sum = jnp.sum(s8, axis=0, keepdims=True)
    col_ssq = jnp.sum(ss8, axis=0, keepdims=True)

    def count(pred):
        def body(r, acc):
            r0 = pl.multiple_of(r * kt_rows, kt_rows)
            tile = ibuf[pl.ds(r0, kt_rows), :]
            rows = r0 + lax.broadcasted_iota(jnp.int32, (kt_rows, qb), 0)
            ind = jnp.where(pred(tile, rows), 1.0, 0.0)
            return acc + jnp.sum(ind.reshape(kt_rows // SUBLANES, SUBLANES, qb), axis=0)
        acc = lax.fori_loop(0, n_tiles, body, jnp.zeros((SUBLANES, qb), f32))
        return jnp.sum(acc, axis=0, keepdims=True)

    qpos = i * qb + lax.broadcasted_iota(jnp.int32, (1, qb), 1)
    n_adm = (qpos // CHUNK + 1) * CHUNK
    select_all = n_adm <= TOPK_MAX
    kf = float(TOPK_MAX)

    thr_ref[...] = jnp.full((1, qb), F32_LOWEST, f32)
    pending_ref[0] = jnp.int32(0)

    @pl.when(n_tiles * kt_rows > TOPK_MAX)
    def _():
        n_f = n_adm.astype(f32)
        mean = col_sum / n_f
        sigma = jnp.sqrt(jnp.maximum(col_ssq / n_f - mean * mean, 0.0))
        spacing = tab_ref[1:2, :] * sigma

        def inside(c, lo, hi):
            return (c > lo) & (c < hi)

        c0 = mean + tab_ref[0:1, :] * sigma
        c0 = jnp.where(inside(c0, col_min, col_max), c0, 0.5 * col_min + 0.5 * col_max)
        zero = jnp.zeros((1, qb), f32)
        one = jnp.ones((1, qb), f32)
        state0 = dict(it=jnp.int32(0), active=jnp.int32(1), c=c0, lo=col_min, hi=col_max,
                      glo=n_f, ghi=zero, tlo=zero, thi=zero, boost=one, side=zero,
                      thr=jnp.full((1, qb), F32_LOWEST, f32),
                      open=jnp.where(select_all, 0.0, 1.0))

        def search_cond(st):
            return (st["it"] < SEARCH_MAX_ITERS) & (st["active"] > 0)

        def search_body(st):
            c, lo, hi = st["c"], st["lo"], st["hi"]
            g = count(lambda tile, rows: tile >= c)
            hit = (g == kf) & (st["open"] > 0.0)
            thr = jnp.where(hit, c, st["thr"])
            still = jnp.where(hit, 0.0, st["open"])
            above = g > kf
            lo = jnp.where(above, c, lo)
            hi = jnp.where(above, hi, c)
            glo = jnp.where(above, g, st["glo"])
            ghi = jnp.where(above, st["ghi"], g)
            tlo = jnp.where(above, 1.0, st["tlo"])
            thi = jnp.where(above, st["thi"], 1.0)
            side = jnp.where(above, 1.0, -1.0)
            repeat = side == st["side"]
            boost = jnp.where(repeat, 2.0 * st["boost"], 1.0)
            mid = 0.5 * lo + 0.5 * hi
            frac = (glo - kf - 0.5) / jnp.maximum(glo - ghi, 1.0)
            c_two = jnp.where(repeat, mid, lo + frac * (hi - lo))
            step = 2.0 * boost * spacing
            c_one = jnp.where(tlo > 0.0, lo + jnp.maximum(glo - kf, 1.0) * step,
                              hi - jnp.maximum(kf - ghi, 1.0) * step)
            c_one = jnp.where(inside(c_one, lo, hi), c_one, mid)
            c_new = jnp.where((tlo > 0.0) & (thi > 0.0), c_two, c_one)
            movable = inside(c_new, lo, hi)
            searching = jnp.where(movable, still, 0.0)
            return dict(it=st["it"] + 1, active=(jnp.max(searching) > 0.0).astype(jnp.int32),
                        c=c_new, lo=lo, hi=hi, glo=glo, ghi=ghi, tlo=tlo, thi=thi,
                        boost=boost, side=side, thr=thr, open=still)

        final = lax.while_loop(search_cond, search_body, state0)
        thr_ref[...] = final["thr"]
        pending_ref[0] = (jnp.max(final["open"]) > 0.0).astype(jnp.int32)

    @pl.when(pending_ref[0] > 0)
    def _():
        def key_to_f32(key):
            bits = jnp.where(key < 0, key ^ jnp.int32(0x7FFFFFFF), key)
            return lax.bitcast_convert_type(bits, f32)

        def bit_body(t, prefix):
            step = lax.shift_left(jnp.int32(1), jnp.int32(31) - t)
            cand = prefix + step
            cand_f = key_to_f32(cand)
            cnt = count(lambda tile, rows: tile >= cand_f)
            return jnp.where(cnt >= kf, cand, prefix)

        prefix = lax.fori_loop(0, 32, bit_body, jnp.full((1, qb), INT32_MIN, jnp.int32))
        thr = jnp.where(select_all, F32_LOWEST, key_to_f32(prefix))
        thr_ref[...] = thr

        cnt_ge = count(lambda tile, rows: tile >= thr)
        excess = jnp.max(jnp.where(select_all, 0.0, cnt_ge - kf))

        @pl.when(excess > 0.0)
        def _():
            need = kf - count(lambda tile, rows: tile > thr)

            def tie_body(t, pos):
                cand = pos | lax.shift_left(jnp.int32(1), jnp.int32(idx_bits - 1) - t)
                below = count(lambda tile, rows: jnp.where(tile == thr, rows, seq_len) < cand)
                return jnp.where(below < need, cand, pos)

            last = lax.fori_loop(0, idx_bits, tie_body, jnp.zeros((1, qb), jnp.int32))

            def drop_body(r, carry):
                r0 = pl.multiple_of(r * kt_rows, kt_rows)
                tile = ibuf[pl.ds(r0, kt_rows), :]
                rows = r0 + lax.broadcasted_iota(jnp.int32, (kt_rows, qb), 0)
                dropped = jnp.where(tile == thr, rows, -1) > last
                ibuf[pl.ds(r0, kt_rows), :] = jnp.where(dropped, -jnp.inf, tile)
                return carry

            lax.fori_loop(0, n_tiles, drop_body, 0)

    thr = thr_ref[...]
    ones_rows = jnp.ones((2 * SUBLANES, kt_rows), bf16)

    def att_body(kt, carry):
        r0 = pl.multiple_of(kt * kt_rows, kt_rows)
        bias_ref[...] = jnp.where(ibuf[pl.ds(r0, kt_rows), :] >= thr, 0.0, NEG_BIG)
        for h in range(N_ATTN_HEADS):
            k_pair = k_ref[pl.ds(r0, kt_rows), (h // 2) * LANES:(h // 2 + 1) * LANES]
            s = jnp.dot(k_pair, qTm[h], preferred_element_type=f32) + bias_ref[...]
            s_ref[h] = s
            m_old = m_ref[h]
            m_new = jnp.maximum(m_old, jnp.max(s, axis=0, keepdims=True))
            alpha_ref[h] = jnp.exp2(m_old - m_new)
            m_ref[h] = m_new
        for h in range(N_ATTN_HEADS):
            p = jnp.exp2(s_ref[h] - m_ref[h]).astype(bf16)
            hs = slice(h * ATTN_HEAD_DIM, (h + 1) * ATTN_HEAD_DIM)
            lhs = jnp.concatenate([vT_ref[hs, pl.ds(r0, kt_rows)], ones_rows], axis=0)
            pv = jnp.dot(lhs, p, preferred_element_type=f32)
            alpha = alpha_ref[h]
            acc_ref[hs, :] = alpha * acc_ref[hs, :] + pv[:ATTN_HEAD_DIM]
            l_ref[h] = alpha * l_ref[h] + pv[ATTN_HEAD_DIM:ATTN_HEAD_DIM + 1]
        return carry

    lax.fori_loop(0, n_tiles, att_body, 0)

    for h in range(N_ATTN_HEADS):
        hs = slice(h * ATTN_HEAD_DIM, (h + 1) * ATTN_HEAD_DIM)
        o_ref[hs, :] = (acc_ref[hs, :] / l_ref[h]).astype(o_ref.dtype)


def _search_hints(seq_len):
    n = (np.arange(seq_len) // CHUNK + 1) * CHUNK
    frac = np.minimum(TOPK_MAX / n, 0.5)
    nd = statistics.NormalDist()
    z_of = {f: nd.inv_cdf(1.0 - f) for f in np.unique(frac)}
    z = np.array([z_of[f] for f in frac])
    dens = n * np.exp(-0.5 * z * z) / np.sqrt(2.0 * np.pi)
    return jnp.asarray(np.stack([z, 1.0 / dens]), f32)


def _dsa(qT, iqT, iwT, z3, vT, ik3):
    b, _, s = qT.shape
    qb = DSA_BLOCK
    return pl.pallas_call(
        functools.partial(_dsa_kernel, seq_len=s),
        grid=(b, s // qb),
        in_specs=[
            pl.BlockSpec((None, ATTN_W, qb), lambda bi, i: (bi, 0, i)),
            pl.BlockSpec((None, IDX_Q_W, qb), lambda bi, i: (bi, 0, i)),
            pl.BlockSpec((None, N_IDX_HEADS, qb), lambda bi, i: (bi, 0, i)),
            pl.BlockSpec((2, qb), lambda bi, i: (0, i)),
            pl.BlockSpec((None, s, COL_TILE), lambda bi, i: (bi, 0, Z_AK)),
            pl.BlockSpec((None, ATTN_W, s), lambda bi, i: (bi, 0, 0)),
            pl.BlockSpec((None, s, IDX_HEAD_DIM), lambda bi, i: (bi, 0, 0)),
        ],
        out_specs=pl.BlockSpec((None, ATTN_W, qb), lambda bi, i: (bi, 0, i)),
        out_shape=jax.ShapeDtypeStruct((b, ATTN_W, s), bf16),
        scratch_shapes=[
            pltpu.VMEM((s, qb), f32),
            pltpu.VMEM((N_ATTN_HEADS, LANES, qb), bf16),
            pltpu.VMEM((N_ATTN_HEADS, qb, qb), f32),
            pltpu.VMEM((qb, qb), f32),
            pltpu.VMEM((N_ATTN_HEADS, 1, qb), f32),
            pltpu.VMEM((N_ATTN_HEADS, 1, qb), f32),
            pltpu.VMEM((N_ATTN_HEADS, 1, qb), f32),
            pltpu.VMEM((ATTN_W, qb), f32),
            pltpu.VMEM((1, qb), f32),
            pltpu.SMEM((1,), jnp.int32),
        ],
        compiler_params=_cparams(("parallel", "arbitrary")),
        name="dsa",
    )(qT, iqT, iwT, _search_hints(s), z3, vT, ik3)


def _gla_kernel(qk_ref, v_ref, r_ref, sm_ref, w2_ref, gb_ref, gn_ref, o_ref, state_ref):
    c = CHUNK

    @pl.when(pl.program_id(1) == 0)
    def _():
        state_ref[...] = jnp.zeros_like(state_ref)

    row = lax.broadcasted_iota(jnp.int32, (c, c), 0)
    col = lax.broadcasted_iota(jnp.int32, (c, c), 1)
    causal = row >= col
    tril = jnp.where(causal, 1.0, 0.0)
    lane = lax.broadcasted_iota(jnp.int32, (c, LANES), 1)
    half = (lane < GLA_HEAD_K, lane >= GLA_HEAD_K)
    contract_last = (((1,), (1,)), ((), ()))
    contract_first = (((0,), (0,)), ((), ()))

    for ci in range(GLA_TILE // c):
        rows = slice(ci * c, (ci + 1) * c)
        qk = qk_ref[rows, :].astype(f32)
        q = qk[:, :GLA_K_W]
        k = qk[:, GLA_K_W:]
        ga = sm_ref[rows, :][:, SM_GA0:SM_GA0 + GLA_GATE_RANK]
        glin = jnp.dot(ga, w2_ref[...], preferred_element_type=f32,
                       precision=lax.Precision.HIGHEST) + gb_ref[...]
        logg = (jnp.minimum(glin, 0.0) - jnp.log(1.0 + jnp.exp(-jnp.abs(glin)))) / GLA_GATE_TAU
        bcum = jnp.dot(tril, logg, preferred_element_type=f32, precision=lax.Precision.HIGHEST)
        b_last = bcum[c - 1:c, :]
        b_mid = bcum[c // 2 - 1:c // 2, :]
        q_in = q * jnp.exp(bcum)
        q_mid = q * jnp.exp(bcum - b_mid)
        k_mid = k * jnp.exp(b_mid - bcum)
        k_out = k * jnp.exp(b_last - bcum)
        decay = jnp.exp(b_last)

        for pr in range(N_GLA_HEADS // 2):
            ps = slice(pr * LANES, (pr + 1) * LANES)
            st = state_ref[pr]
            st_b = st.astype(bf16)
            k_mid_p = k_mid[:, ps].astype(bf16)
            upd = jnp.zeros_like(st)
            for hh in range(2):
                h = pr * 2 + hh
                vs = slice(h * GLA_HEAD_V, (h + 1) * GLA_HEAD_V)
                v_h = v_ref[rows, vs]
                q_mid_m = jnp.where(half[hh], q_mid[:, ps], 0.0).astype(bf16)
                q_in_m = jnp.where(half[hh], q_in[:, ps], 0.0).astype(bf16)
                k_out_m = jnp.where(half[hh], k_out[:, ps], 0.0).astype(bf16)
                attn = lax.dot_general(q_mid_m, k_mid_p, contract_last,
                                       preferred_element_type=f32)
                attn = jnp.where(causal, attn, 0.0).astype(bf16)
                o = lax.dot_general(q_in_m, st_b, contract_last, preferred_element_type=f32)
                o = o + jnp.dot(attn, v_h, preferred_element_type=f32)
                upd = upd + lax.dot_general(v_h, k_out_m, contract_first,
                                            preferred_element_type=f32)
                y = _rms(o, gn_ref[...])
                r = r_ref[rows, vs].astype(f32)
                o_ref[rows, vs] = (y * r * _sigmoid(r)).astype(o_ref.dtype)
            state_ref[pr] = decay[:, ps] * st + upd


def _gla(z3, sm3, w2, gb, gn):
    b, s, _ = z3.shape
    tb = GLA_TILE
    return pl.pallas_call(
        _gla_kernel,
        grid=(b, s // tb),
        in_specs=[
            pl.BlockSpec((None, tb, COL_TILE), lambda bi, j: (bi, j, Z_GQK)),
            pl.BlockSpec((None, tb, COL_TILE), lambda bi, j: (bi, j, Z_GV)),
            pl.BlockSpec((None, tb, COL_TILE), lambda bi, j: (bi, j, Z_GR)),
            pl.BlockSpec((None, tb, LANES), lambda bi, j: (bi, j, 0)),
            pl.BlockSpec((GLA_GATE_RANK, GLA_K_W), lambda bi, j: (0, 0)),
            pl.BlockSpec((1, GLA_K_W), lambda bi, j: (0, 0)),
            pl.BlockSpec((1, GLA_HEAD_V), lambda bi, j: (0, 0)),
        ],
        out_specs=pl.BlockSpec((None, tb, GLA_V_W), lambda bi, j: (bi, j, 0)),
        out_shape=jax.ShapeDtypeStruct((b, s, GLA_V_W), bf16),
        scratch_shapes=[pltpu.VMEM((N_GLA_HEADS // 2, GLA_HEAD_V, LANES), f32)],
        compiler_params=_cparams(("parallel", "arbitrary")),
        name="gla",
    )(z3, z3, z3, sm3, w2, gb, gn)


def _merge_kernel(x_ref, oa_ref, ob_ref, ga_ref, gb_ref, wa_ref, wb_ref, wo_ref, o_ref):
    ya = jnp.dot(oa_ref[...], wa_ref[...], preferred_element_type=f32)
    yb = jnp.dot(ob_ref[...], wb_ref[...], preferred_element_type=f32)
    mg = _sigmoid(ga_ref[...].astype(f32)) * ya + _sigmoid(gb_ref[...].astype(f32)) * yb
    o_ref[...] = x_ref[...] + jnp.dot(mg.astype(bf16), wo_ref[...], preferred_element_type=f32)


def _merge(x, oa, ob, z, wa, wb, wo):
    t = x.shape[0]
    tm = TOKEN_TILE
    return pl.pallas_call(
        _merge_kernel,
        grid=(t // tm,),
        in_specs=[
            pl.BlockSpec((tm, D_MODEL), lambda i: (i, 0)),
            pl.BlockSpec((tm, ATTN_W), lambda i: (i, 0)),
            pl.BlockSpec((tm, GLA_V_W), lambda i: (i, 0)),
            pl.BlockSpec((tm, D_MODEL), lambda i: (i, Z_MGA)),
            pl.BlockSpec((tm, D_MODEL), lambda i: (i, Z_MGB)),
            pl.BlockSpec((ATTN_W, D_MODEL), lambda i: (0, 0)),
            pl.BlockSpec((GLA_V_W, D_MODEL), lambda i: (0, 0)),
            pl.BlockSpec((D_MODEL, D_MODEL), lambda i: (0, 0)),
        ],
        out_specs=pl.BlockSpec((tm, D_MODEL), lambda i: (i, 0)),
        out_shape=jax.ShapeDtypeStruct((t, D_MODEL), f32),
        compiler_params=_cparams(("parallel",)),
        name="merge",
    )(x, oa, ob, z, z, wa, wb, wo)


def _ple_kernel(x_ref, p_ref, g_ref, wg_ref, wp_ref, gf_ref, o_ref, *, final):
    x = x_ref[...]
    h = _rms(x, g_ref[...]).astype(bf16)
    gate = _sigmoid(jnp.dot(h, wg_ref[...], preferred_element_type=f32))
    e = jnp.dot(p_ref[...].astype(bf16), wp_ref[...], preferred_element_type=f32)
    y = x + gate * e
    if final:
        y = _rms(y, gf_ref[...])
    o_ref[...] = y


def _ple(x, p, g, wg, wp, gf, final):
    t = x.shape[0]
    tm = TOKEN_TILE
    return pl.pallas_call(
        functools.partial(_ple_kernel, final=final),
        grid=(t // tm,),
        in_specs=[
            pl.BlockSpec((tm, D_MODEL), lambda i: (i, 0)),
            pl.BlockSpec((tm, PLE_DIM), lambda i: (i, 0)),
            pl.BlockSpec((1, D_MODEL), lambda i: (0, 0)),
            pl.BlockSpec((D_MODEL, D_MODEL), lambda i: (0, 0)),
            pl.BlockSpec((PLE_DIM, D_MODEL), lambda i: (0, 0)),
            pl.BlockSpec((1, D_MODEL), lambda i: (0, 0)),
        ],
        out_specs=pl.BlockSpec((tm, D_MODEL), lambda i: (i, 0)),
        out_shape=jax.ShapeDtypeStruct((t, D_MODEL), f32),
        compiler_params=_cparams(("parallel",)),
        name="ple",
    )(x, p, g, wg, wp, gf)


def _split_w_in(w_in):
    cuts = np.cumsum(np.array(SPLIT_SIZES))[:-1].tolist()
    aq, ak, av, iq, ik, iw, gq, gk, gv, gr, ga, mga, mgb = jnp.split(w_in, cuts, axis=-1)
    att_scale = ATTN_HEAD_DIM ** -0.5 * float(np.log2(np.e))
    idx_scale = IDX_HEAD_DIM ** -0.5 * N_IDX_HEADS ** -0.5
    main = jnp.concatenate(
        [mga, mgb, aq * att_scale, ak, av, iq, gq * (GLA_HEAD_K ** -0.5), gk, gv, gr], axis=-1)
    zeros = lambda n: jnp.zeros(w_in.shape[:-1] + (n,), w_in.dtype)
    small = jnp.concatenate(
        [ik, zeros(LANES - IDX_HEAD_DIM), iw * idx_scale, ga,
         zeros(LANES - N_IDX_HEADS - GLA_GATE_RANK)], axis=-1)
    return main.astype(bf16), small.astype(bf16)


def kernel(x, p, w_in, gla_gate_w2, gla_gate_b, gla_norm, w_branch_a, w_branch_b, w_out,
           norm_ff1, norm_mix, norm_ff2, norm_ple, ff1_w_gate, ff1_w_up, ff1_w_down,
           ff2_w_gate, ff2_w_up, ff2_w_down, ple_w_proj, ple_w_gate, norm_final):
    b, s, d = x.shape
    t = b * s
    depth = w_in.shape[0]
    w_main, w_small = _split_w_in(w_in)
    cast = lambda w: w.astype(bf16)
    ff1 = (cast(ff1_w_gate), cast(ff1_w_up), cast(ff1_w_down))
    ff2 = (cast(ff2_w_gate), cast(ff2_w_up), cast(ff2_w_down))
    wa, wb, wo = cast(w_branch_a), cast(w_branch_b), cast(w_out)
    wpg, wpp = cast(ple_w_gate), cast(ple_w_proj)
    row = lambda v: v.reshape(1, -1)

    xf = x.reshape(t, d)
    for l in range(depth):
        xf = _ffn(xf, row(norm_ff1[l]), ff1[0][l], ff1[1][l], ff1[2][l])

        z, ik, sm = _proj(xf, row(norm_mix[l]), w_main[l], w_small[l])
        z3 = z.reshape(b, s, Z_WIDTH)
        sm3 = sm.reshape(b, s, LANES)
        feature_major = lambda a: jnp.swapaxes(a, 1, 2)
        qT = feature_major(z3[:, :, Z_AQ * COL_TILE:(Z_AQ + 1) * COL_TILE])
        vT = feature_major(z3[:, :, Z_AV * COL_TILE:(Z_AV + 1) * COL_TILE])
        iqT = feature_major(z3[:, :, Z_IQ * COL_TILE:(Z_IQ + 1) * COL_TILE])
        iwT = feature_major(sm3[:, :, SM_IW0:SM_IW0 + N_IDX_HEADS])
        oaT = _dsa(qT, iqT, iwT, z3, vT, ik.reshape(b, s, IDX_HEAD_DIM))
        oa = feature_major(oaT).reshape(t, ATTN_W)

        ob = _gla(z3, sm3, gla_gate_w2[l], row(gla_gate_b[l]), row(gla_norm[l]))
        xf = _merge(xf, oa, ob.reshape(t, GLA_V_W), z, wa[l], wb[l], wo[l])

        xf = _ffn(xf, row(norm_ff2[l]), ff2[0][l], ff2[1][l], ff2[2][l])
        xf = _ple(xf, p[l].reshape(t, PLE_DIM), row(norm_ple[l]), wpg[l], wpp[l],
                  row(norm_final), final=(l == depth - 1))
    return xf.reshape(b, s, d)
```

```python
import functools
import statistics

import jax
import jax.numpy as jnp
import numpy as np
from jax import lax
from jax.experimental import pallas as pl
from jax.experimental.pallas import tpu as pltpu

D_MODEL = 1024
D_FF = 2816
PLE_DIM = 256
DEPTH = 2
EPS = 1e-6

CHUNK = 64
N_ATTN_HEADS = 8
ATTN_HEAD_DIM = 64
N_IDX_HEADS = 8
IDX_HEAD_DIM = 64
TOPK_MAX = 256
N_GLA_HEADS = 4
GLA_HEAD_K = 64
GLA_HEAD_V = 128
GLA_GATE_RANK = 16
GLA_GATE_TAU = 16.0

ATTN_W = N_ATTN_HEADS * ATTN_HEAD_DIM
IDX_Q_W = N_IDX_HEADS * IDX_HEAD_DIM
GLA_K_W = N_GLA_HEADS * GLA_HEAD_K
GLA_V_W = N_GLA_HEADS * GLA_HEAD_V
SPLIT_SIZES = (ATTN_W, ATTN_W, ATTN_W, IDX_Q_W, IDX_HEAD_DIM, N_IDX_HEADS,
               GLA_K_W, GLA_K_W, GLA_V_W, GLA_V_W, GLA_GATE_RANK, D_MODEL, D_MODEL)

LANES = 128
SUBLANES = 8
VMEM_LIMIT_BYTES = 56 * 1024 * 1024

COL_TILE = 512
Z_MGA, Z_MGB = 0, 1
Z_AQ, Z_AK, Z_AV, Z_IQ, Z_GQK, Z_GV, Z_GR = 4, 5, 6, 7, 8, 9, 10
Z_WIDTH = 11 * COL_TILE
SM_IW0 = 0
SM_GA0 = N_IDX_HEADS

TOKEN_TILE = 512
FF_TILE = 1408
DSA_BLOCK = 256
GLA_TILE = 512
SEARCH_MAX_ITERS = 24

NEG_BIG = -1e30
F32_LOWEST = float(np.finfo(np.float32).min)
F32_TINY = float(np.finfo(np.float32).tiny)
INT32_MIN = int(np.iinfo(np.int32).min)

f32 = jnp.float32
bf16 = jnp.bfloat16


def _rms(x, g):
    return x * lax.rsqrt(jnp.mean(x * x, axis=-1, keepdims=True) + EPS) * g


def _sigmoid(x):
    return 1.0 / (1.0 + jnp.exp(-x))


def _cparams(sem):
    return pltpu.CompilerParams(dimension_semantics=sem, vmem_limit_bytes=VMEM_LIMIT_BYTES)


def _ffn_kernel(x_ref, g_ref, wg_ref, wu_ref, wd_ref, o_ref, h_ref, acc_ref):
    f = pl.program_id(1)

    @pl.when(f == 0)
    def _():
        h_ref[...] = _rms(x_ref[...], g_ref[...]).astype(bf16)
        acc_ref[...] = jnp.zeros_like(acc_ref)

    h = h_ref[...]
    gate = jnp.dot(h, wg_ref[...], preferred_element_type=f32)
    up = jnp.dot(h, wu_ref[...], preferred_element_type=f32)
    a = (gate * _sigmoid(gate) * up).astype(bf16)
    acc_ref[...] += jnp.dot(a, wd_ref[...], preferred_element_type=f32)

    @pl.when(f == pl.num_programs(1) - 1)
    def _():
        o_ref[...] = x_ref[...] + 0.5 * acc_ref[...]


def _ffn(x, g, wg, wu, wd):
    t = x.shape[0]
    return pl.pallas_call(
        _ffn_kernel,
        grid=(t // TOKEN_TILE, D_FF // FF_TILE),
        in_specs=[
            pl.BlockSpec((TOKEN_TILE, D_MODEL), lambda i, f: (i, 0)),
            pl.BlockSpec((1, D_MODEL), lambda i, f: (0, 0)),
            pl.BlockSpec((D_MODEL, FF_TILE), lambda i, f: (0, f)),
            pl.BlockSpec((D_MODEL, FF_TILE), lambda i, f: (0, f)),
            pl.BlockSpec((FF_TILE, D_MODEL), lambda i, f: (f, 0)),
        ],
        out_specs=pl.BlockSpec((TOKEN_TILE, D_MODEL), lambda i, f: (i, 0)),
        out_shape=jax.ShapeDtypeStruct((t, D_MODEL), f32),
        scratch_shapes=[pltpu.VMEM((TOKEN_TILE, D_MODEL), bf16),
                        pltpu.VMEM((TOKEN_TILE, D_MODEL), f32)],
        compiler_params=_cparams(("parallel", "arbitrary")),
        name="ffn",
    )(x, g, wg, wu, wd)


def _proj_kernel(x_ref, g_ref, w_ref, ws_ref, z_ref, ik_ref, sm_ref, h_ref):
    j = pl.program_id(1)

    @pl.when(j == 0)
    def _():
        h = _rms(x_ref[...], g_ref[...]).astype(bf16)
        h_ref[...] = h
        s = jnp.dot(h, ws_ref[...], preferred_element_type=f32)
        ik_ref[...] = s[:, :IDX_HEAD_DIM].astype(bf16)
        sm_ref[...] = s[:, LANES:]

    z_ref[...] = jnp.dot(h_ref[...], w_ref[...], preferred_element_type=f32).astype(bf16)


def _proj(x, g, w_main, w_small):
    t = x.shape[0]
    return pl.pallas_call(
        _proj_kernel,
        grid=(t // TOKEN_TILE, Z_WIDTH // COL_TILE),
        in_specs=[
            pl.BlockSpec((TOKEN_TILE, D_MODEL), lambda i, j: (i, 0)),
            pl.BlockSpec((1, D_MODEL), lambda i, j: (0, 0)),
            pl.BlockSpec((D_MODEL, COL_TILE), lambda i, j: (0, j)),
            pl.BlockSpec((D_MODEL, 2 * LANES), lambda i, j: (0, 0)),
        ],
        out_specs=[
            pl.BlockSpec((TOKEN_TILE, COL_TILE), lambda i, j: (i, j)),
            pl.BlockSpec((TOKEN_TILE, IDX_HEAD_DIM), lambda i, j: (i, 0)),
            pl.BlockSpec((TOKEN_TILE, LANES), lambda i, j: (i, 0)),
        ],
        out_shape=[
            jax.ShapeDtypeStruct((t, Z_WIDTH), bf16),
            jax.ShapeDtypeStruct((t, IDX_HEAD_DIM), bf16),
            jax.ShapeDtypeStruct((t, LANES), f32),
        ],
        scratch_shapes=[pltpu.VMEM((TOKEN_TILE, D_MODEL), bf16)],
        compiler_params=_cparams(("parallel", "arbitrary")),
        name="proj",
    )(x, g, w_main, w_small)


def _dsa_kernel(qT_ref, iqT_ref, iwT_ref, tab_ref, k_ref, vT_ref, ik_ref, o_ref,
                ibuf, qTm, s_ref, bias_ref, m_ref, l_ref, alpha_ref, acc_ref, thr_ref,
                pending_ref, *, seq_len):
    qb = DSA_BLOCK
    kt_rows = DSA_BLOCK
    i = pl.program_id(1)
    n_tiles = i + 1

    rowid = lax.broadcasted_iota(jnp.int32, (LANES, qb), 0)
    for h in range(N_ATTN_HEADS):
        pair = qT_ref[(h // 2) * LANES:(h // 2 + 1) * LANES, :]
        keep = (rowid < ATTN_HEAD_DIM) if h % 2 == 0 else (rowid >= ATTN_HEAD_DIM)
        qTm[h] = jnp.where(keep, pair, jnp.zeros_like(pair))
    m_ref[...] = jnp.full(m_ref.shape, NEG_BIG, f32)
    l_ref[...] = jnp.zeros_like(l_ref)
    acc_ref[...] = jnp.zeros_like(acc_ref)

    def fold(x, op):
        return op(x.reshape(kt_rows // SUBLANES, SUBLANES, qb), axis=0)

    def idx_tile(kt, diagonal, stats):
        r0 = pl.multiple_of(kt * kt_rows, kt_rows)
        ki_t = ik_ref[pl.ds(r0, kt_rows), :]
        sc = jnp.zeros((kt_rows, qb), f32)
        for h in range(N_IDX_HEADS):
            s = jnp.dot(ki_t, iqT_ref[h * IDX_HEAD_DIM:(h + 1) * IDX_HEAD_DIM, :],
                        preferred_element_type=f32)
            sc = sc + iwT_ref[h:h + 1, :] * jnp.maximum(s, 0.0)
        sc_hi = sc_lo = sc_0 = sc
        if diagonal:
            kc = lax.broadcasted_iota(jnp.int32, (kt_rows, qb), 0) // CHUNK
            qc = lax.broadcasted_iota(jnp.int32, (kt_rows, qb), 1) // CHUNK
            adm = kc <= qc
            sc_hi = jnp.where(adm, sc, -jnp.inf)
            sc_lo = jnp.where(adm, sc, jnp.inf)
            sc_0 = jnp.where(adm, sc, 0.0)
        ibuf[pl.ds(r0, kt_rows), :] = sc_hi
        mx8, mn8, s8, ss8 = stats
        return (jnp.maximum(mx8, fold(sc_hi, jnp.max)), jnp.minimum(mn8, fold(sc_lo, jnp.min)),
                s8 + fold(sc_0, jnp.sum), ss8 + fold(sc_0 * sc_0, jnp.sum))

    stats0 = (jnp.full((SUBLANES, qb), -jnp.inf, f32), jnp.full((SUBLANES, qb), jnp.inf, f32),
              jnp.zeros((SUBLANES, qb), f32), jnp.zeros((SUBLANES, qb), f32))
    stats = lax.fori_loop(0, i, lambda kt, st: idx_tile(kt, False, st), stats0)
    mx8, mn8, s8, ss8 = idx_tile(i, True, stats)
    col_max = jnp.max(mx8, axis=0, keepdims=True)
    col_min = jnp.min(mn8, axis=0, keepdims=True)
    col_sum = jnp.sum(s8, axis=0, keepdims=True)
    col_ssq = jnp.sum(ss8, axis=0, keepdims=True)

    def count(pred):
        def body(r, acc):
            r0 = pl.multiple_of(r * kt_rows, kt_rows)
            tile = ibuf[pl.ds(r0, kt_rows), :]
            rows = r0 + lax.broadcasted_iota(jnp.int32, (kt_rows, qb), 0)
            ind = jnp.where(pred(tile, rows), 1.0, 0.0)
            part = ind.reshape(kt_rows // SUBLANES, SUBLANES, qb)
            while part.shape[0] > 1:
                half = part.shape[0] // 2
                part = part[:half] + part[half:]
            return acc + part[0]
        acc = lax.fori_loop(0, n_tiles, body, jnp.zeros((SUBLANES, qb), f32))
        return jnp.sum(acc, axis=0, keepdims=True)

    def resolve_ties(tv, need, mult):
        is_tie = tv == tv
        n_keys = n_tiles * kt_rows

        def tie_cond(st):
            return st["active"] > 0

        def tie_body(st):
            plo, phi, flo, fhi = st["plo"], st["phi"], st["flo"], st["fhi"]
            span = phi - plo
            est = ((need - flo) / jnp.maximum(fhi - flo, 1.0) * span.astype(f32)).astype(jnp.int32)
            pick = jnp.where(st["bisect"] > 0, span // 2, est)
            cand = plo + jnp.clip(pick, 1, jnp.maximum(span - 1, 1))
            f = count(lambda tile, rows: jnp.where(tile == tv, rows, seq_len) < cand)
            lower = f < need
            plo = jnp.where(lower, cand, plo)
            phi = jnp.where(lower, phi, cand)
            flo = jnp.where(lower, f, flo)
            fhi = jnp.where(lower, fhi, f)
            unsplit = jnp.where(is_tie & (phi - plo > 1), 1, 0)
            return dict(active=jnp.max(unsplit), bisect=1 - st["bisect"],
                        plo=plo, phi=phi, flo=flo, fhi=fhi)

        split = lax.while_loop(tie_cond, tie_body, dict(
            active=jnp.int32(1), bisect=jnp.int32(0),
            plo=jnp.zeros((1, qb), jnp.int32), phi=jnp.zeros((1, qb), jnp.int32) + n_keys,
            flo=jnp.zeros((1, qb), f32), fhi=mult))["phi"]

        def drop_body(r, carry):
            r0 = pl.multiple_of(r * kt_rows, kt_rows)
            tile = ibuf[pl.ds(r0, kt_rows), :]
            rows = r0 + lax.broadcasted_iota(jnp.int32, (kt_rows, qb), 0)
            dropped = jnp.where(tile == tv, rows, -1) >= split
            ibuf[pl.ds(r0, kt_rows), :] = jnp.where(dropped, -jnp.inf, tile)
            return carry

        lax.fori_loop(0, n_tiles, drop_body, 0)

    qpos = i * qb + lax.broadcasted_iota(jnp.int32, (1, qb), 1)
    n_adm = (qpos // CHUNK + 1) * CHUNK
    select_all = n_adm <= TOPK_MAX
    kf = float(TOPK_MAX)

    thr_ref[...] = jnp.full((1, qb), F32_LOWEST, f32)
    pending_ref[0] = jnp.int32(0)

    @pl.when(n_tiles * kt_rows > TOPK_MAX)
    def _():
        n_f = n_adm.astype(f32)
        mean = col_sum / n_f
        sigma = jnp.sqrt(jnp.maximum(col_ssq / n_f - mean * mean, 0.0))
        spacing = tab_ref[1:2, :] * sigma

        def inside(c, lo, hi):
            return (c > lo) & (c < hi)

        c0 = mean + tab_ref[0:1, :] * sigma
        c0 = jnp.where(inside(c0, col_min, col_max), c0, 0.5 * col_min + 0.5 * col_max)
        zero = jnp.zeros((1, qb), f32)
        one = jnp.ones((1, qb), f32)
        state0 = dict(it=jnp.int32(0), active=jnp.int32(1), c=c0, lo=col_min, hi=col_max,
                      glo=n_f, ghi=zero, tlo=zero, thi=zero, boost=one, side=zero,
                      thr=jnp.full((1, qb), F32_LOWEST, f32),
                      open=jnp.where(select_all, 0.0, 1.0), tie=zero)

        def search_cond(st):
            return (st["it"] < SEARCH_MAX_ITERS) & (st["active"] > 0)

        def search_body(st):
            c, lo, hi = st["c"], st["lo"], st["hi"]
            g = count(lambda tile, rows: tile >= c)
            hit = (g == kf) & (st["open"] > 0.0)
            thr = jnp.where(hit, c, st["thr"])
            still = jnp.where(hit, 0.0, st["open"])
            above = g > kf
            lo = jnp.where(above, c, lo)
            hi = jnp.where(above, hi, c)
            glo = jnp.where(above, g, st["glo"])
            ghi = jnp.where(above, st["ghi"], g)
            tlo = jnp.where(above, 1.0, st["tlo"])
            thi = jnp.where(above, st["thi"], 1.0)
            side = jnp.where(above, 1.0, -1.0)
            repeat = side == st["side"]
            boost = jnp.where(repeat, 2.0 * st["boost"], 1.0)
            mid = 0.5 * lo + 0.5 * hi
            stale = 1.0 / jnp.minimum(boost, 256.0)
            w_lo = (glo - kf - 0.5) * jnp.where(above, 1.0, stale)
            w_hi = (kf + 0.5 - ghi) * jnp.where(above, stale, 1.0)
            c_two = lo + w_lo / (w_lo + w_hi) * (hi - lo)
            step = 2.0 * boost * spacing
            c_one = jnp.where(tlo > 0.0, lo + jnp.maximum(glo - kf, 1.0) * step,
                              hi - jnp.maximum(kf - ghi, 1.0) * step)
            c_one = jnp.where(inside(c_one, lo, hi), c_one, mid)
            both = (tlo > 0.0) & (thi > 0.0)
            c_new = jnp.where(both, c_two, c_one)
            c_new = jnp.where((lo < 0.0) & (hi > 0.0), 0.0, c_new)
            c_new = jnp.where((lo == 0.0) & (hi > F32_TINY), F32_TINY, c_new)
            movable = inside(c_new, lo, hi) & ~((lo == 0.0) & (hi <= F32_TINY))
            closed = ~movable & both & (still > 0.0)
            tie = jnp.where(closed, 1.0, st["tie"])
            still = jnp.where(closed, 0.0, still)
            searching = jnp.where(movable, still, 0.0)
            return dict(it=st["it"] + 1, active=(jnp.max(searching) > 0.0).astype(jnp.int32),
                        c=c_new, lo=lo, hi=hi, glo=glo, ghi=ghi, tlo=tlo, thi=thi,
                        boost=boost, side=side, thr=thr, open=still, tie=tie)

        final = lax.while_loop(search_cond, search_body, state0)
        tied = final["tie"] > 0.0
        thr_ref[...] = jnp.where(tied, final["lo"], final["thr"])
        pending_ref[0] = (jnp.max(final["open"]) > 0.0).astype(jnp.int32)

        @pl.when(jnp.max(final["tie"]) > 0.0)
        def _():
            resolve_ties(jnp.where(tied, final["lo"], jnp.nan), kf - final["ghi"],
                         final["glo"] - final["ghi"])

    @pl.when(pending_ref[0] > 0)
    def _():
        def key_to_f32(key):
            bits = jnp.where(key < 0, key ^ jnp.int32(0x7FFFFFFF), key)
            return lax.bitcast_convert_type(bits, f32)

        def bit_body(t, prefix):
            step = lax.shift_left(jnp.int32(1), jnp.int32(31) - t)
            cand = prefix + step
            cand_f = key_to_f32(cand)
            cnt = count(lambda tile, rows: tile >= cand_f)
            return jnp.where(cnt >= kf, cand, prefix)

        prefix = lax.fori_loop(0, 32, bit_body, jnp.full((1, qb), INT32_MIN, jnp.int32))
        thr = jnp.where(select_all, F32_LOWEST, key_to_f32(prefix))
        thr_ref[...] = thr

        cnt_ge = count(lambda tile, rows: tile >= thr)
        excess = jnp.where(select_all, 0.0, cnt_ge - kf)

        @pl.when(jnp.max(excess) > 0.0)
        def _():
            cnt_gt = count(lambda tile, rows: tile > thr)
            resolve_ties(jnp.where(excess > 0.0, thr, jnp.nan), kf - cnt_gt, cnt_ge - cnt_gt)

    thr = thr_ref[...]
    ones_rows = jnp.ones((2 * SUBLANES, kt_rows), bf16)

    def att_body(kt, carry):
        r0 = pl.multiple_of(kt * kt_rows, kt_rows)
        bias_ref[...] = jnp.where(ibuf[pl.ds(r0, kt_rows), :] >= thr, 0.0, NEG_BIG)
        for h in range(N_ATTN_HEADS):
            k_pair = k_ref[pl.ds(r0, kt_rows), (h // 2) * LANES:(h // 2 + 1) * LANES]
            s = jnp.dot(k_pair, qTm[h], preferred_element_type=f32) + bias_ref[...]
            s_ref[h] = s
            m_old = m_ref[h]
            m_new = jnp.maximum(m_old, jnp.max(s, axis=0, keepdims=True))
            alpha_ref[h] = jnp.exp2(m_old - m_new)
            m_ref[h] = m_new
        for h in range(N_ATTN_HEADS):
            p = jnp.exp2(s_ref[h] - m_ref[h]).astype(bf16)
            hs = slice(h * ATTN_HEAD_DIM, (h + 1) * ATTN_HEAD_DIM)
            lhs = jnp.concatenate([vT_ref[hs, pl.ds(r0, kt_rows)], ones_rows], axis=0)
            pv = jnp.dot(lhs, p, preferred_element_type=f32)
            alpha = alpha_ref[h]
            acc_ref[hs, :] = alpha * acc_ref[hs, :] + pv[:ATTN_HEAD_DIM]
            l_ref[h] = alpha * l_ref[h] + pv[ATTN_HEAD_DIM:ATTN_HEAD_DIM + 1]
        return carry

    lax.fori_loop(0, n_tiles, att_body, 0)

    for h in range(N_ATTN_HEADS):
        hs = slice(h * ATTN_HEAD_DIM, (h + 1) * ATTN_HEAD_DIM)
        o_ref[hs, :] = (acc_ref[hs, :] / l_ref[h]).astype(o_ref.dtype)


def _search_hints(seq_len):
    n = (np.arange(seq_len) // CHUNK + 1) * CHUNK
    frac = np.minimum(TOPK_MAX / n, 0.5)
    nd = statistics.NormalDist()
    z_of = {f: nd.inv_cdf(1.0 - f) for f in np.unique(frac)}
    z = np.array([z_of[f] for f in frac])
    dens = n * np.exp(-0.5 * z * z) / np.sqrt(2.0 * np.pi)
    return jnp.asarray(np.stack([z, 1.0 / dens]), f32)


def _dsa(qT, iqT, iwT, z3, vT, ik3):
    b, _, s = qT.shape
    qb = DSA_BLOCK
    return pl.pallas_call(
        functools.partial(_dsa_kernel, seq_len=s),
        grid=(b, s // qb),
        in_specs=[
            pl.BlockSpec((None, ATTN_W, qb), lambda bi, i: (bi, 0, i)),
            pl.BlockSpec((None, IDX_Q_W, qb), lambda bi, i: (bi, 0, i)),
            pl.BlockSpec((None, N_IDX_HEADS, qb), lambda bi, i: (bi, 0, i)),
            pl.BlockSpec((2, qb), lambda bi, i: (0, i)),
            pl.BlockSpec((None, s, COL_TILE), lambda bi, i: (bi, 0, Z_AK)),
            pl.BlockSpec((None, ATTN_W, s), lambda bi, i: (bi, 0, 0)),
            pl.BlockSpec((None, s, IDX_HEAD_DIM), lambda bi, i: (bi, 0, 0)),
        ],
        out_specs=pl.BlockSpec((None, ATTN_W, qb), lambda bi, i: (bi, 0, i)),
        out_shape=jax.ShapeDtypeStruct((b, ATTN_W, s), bf16),
        scratch_shapes=[
            pltpu.VMEM((s, qb), f32),
            pltpu.VMEM((N_ATTN_HEADS, LANES, qb), bf16),
            pltpu.VMEM((N_ATTN_HEADS, qb, qb), f32),
            pltpu.VMEM((qb, qb), f32),
            pltpu.VMEM((N_ATTN_HEADS, 1, qb), f32),
            pltpu.VMEM((N_ATTN_HEADS, 1, qb), f32),
            pltpu.VMEM((N_ATTN_HEADS, 1, qb), f32),
            pltpu.VMEM((ATTN_W, qb), f32),
            pltpu.VMEM((1, qb), f32),
            pltpu.SMEM((1,), jnp.int32),
        ],
        compiler_params=_cparams(("parallel", "arbitrary")),
        name="dsa",
    )(qT, iqT, iwT, _search_hints(s), z3, vT, ik3)


def _gla_kernel(qk_ref, v_ref, r_ref, sm_ref, w2_ref, gb_ref, gn_ref, o_ref, state_ref):
    c = CHUNK

    @pl.when(pl.program_id(1) == 0)
    def _():
        state_ref[...] = jnp.zeros_like(state_ref)

    row = lax.broadcasted_iota(jnp.int32, (c, c), 0)
    col = lax.broadcasted_iota(jnp.int32, (c, c), 1)
    causal = row >= col
    tril = jnp.where(causal, 1.0, 0.0)
    lane = lax.broadcasted_iota(jnp.int32, (c, LANES), 1)
    half = (lane < GLA_HEAD_K, lane >= GLA_HEAD_K)
    contract_last = (((1,), (1,)), ((), ()))
    contract_first = (((0,), (0,)), ((), ()))

    for ci in range(GLA_TILE // c):
        rows = slice(ci * c, (ci + 1) * c)
        qk = qk_ref[rows, :].astype(f32)
        q = qk[:, :GLA_K_W]
        k = qk[:, GLA_K_W:]
        ga = sm_ref[rows, :][:, SM_GA0:SM_GA0 + GLA_GATE_RANK]
        glin = jnp.dot(ga, w2_ref[...], preferred_element_type=f32,
                       precision=lax.Precision.HIGHEST) + gb_ref[...]
        logg = (jnp.minimum(glin, 0.0) - jnp.log(1.0 + jnp.exp(-jnp.abs(glin)))) / GLA_GATE_TAU
        bcum = jnp.dot(tril, logg, preferred_element_type=f32, precision=lax.Precision.HIGHEST)
        b_last = bcum[c - 1:c, :]
        b_mid = bcum[c // 2 - 1:c // 2, :]
        q_in = q * jnp.exp(bcum)
        q_mid = q * jnp.exp(bcum - b_mid)
        k_mid = k * jnp.exp(b_mid - bcum)
        k_out = k * jnp.exp(b_last - bcum)
        decay = jnp.exp(b_last)

        for pr in range(N_GLA_HEADS // 2):
            ps = slice(pr * LANES, (pr + 1) * LANES)
            st = state_ref[pr]
            st_b = st.astype(bf16)
            k_mid_p = k_mid[:, ps].astype(bf16)
            upd = jnp.zeros_like(st)
            for hh in range(2):
                h = pr * 2 + hh
                vs = slice(h * GLA_HEAD_V, (h + 1) * GLA_HEAD_V)
                v_h = v_ref[rows, vs]
                q_mid_m = jnp.where(half[hh], q_mid[:, ps], 0.0).astype(bf16)
                q_in_m = jnp.where(half[hh], q_in[:, ps], 0.0).astype(bf16)
                k_out_m = jnp.where(half[hh], k_out[:, ps], 0.0).astype(bf16)
                attn = lax.dot_general(q_mid_m, k_mid_p, contract_last,
                                       preferred_element_type=f32)
                attn = jnp.where(causal, attn, 0.0).astype(bf16)
                o = lax.dot_general(q_in_m, st_b, contract_last, preferred_element_type=f32)
                o = o + jnp.dot(attn, v_h, preferred_element_type=f32)
                upd = upd + lax.dot_general(v_h, k_out_m, contract_first,
                                            preferred_element_type=f32)
                y = _rms(o, gn_ref[...])
                r = r_ref[rows, vs].astype(f32)
                o_ref[rows, vs] = (y * r * _sigmoid(r)).astype(o_ref.dtype)
            state_ref[pr] = decay[:, ps] * st + upd


def _gla(z3, sm3, w2, gb, gn):
    b, s, _ = z3.shape
    tb = GLA_TILE
    return pl.pallas_call(
        _gla_kernel,
        grid=(b, s // tb),
        in_specs=[
            pl.BlockSpec((None, tb, COL_TILE), lambda bi, j: (bi, j, Z_GQK)),
            pl.BlockSpec((None, tb, COL_TILE), lambda bi, j: (bi, j, Z_GV)),
            pl.BlockSpec((None, tb, COL_TILE), lambda bi, j: (bi, j, Z_GR)),
            pl.BlockSpec((None, tb, LANES), lambda bi, j: (bi, j, 0)),
            pl.BlockSpec((GLA_GATE_RANK, GLA_K_W), lambda bi, j: (0, 0)),
            pl.BlockSpec((1, GLA_K_W), lambda bi, j: (0, 0)),
            pl.BlockSpec((1, GLA_HEAD_V), lambda bi, j: (0, 0)),
        ],
        out_specs=pl.BlockSpec((None, tb, GLA_V_W), lambda bi, j: (bi, j, 0)),
        out_shape=jax.ShapeDtypeStruct((b, s, GLA_V_W), bf16),
        scratch_shapes=[pltpu.VMEM((N_GLA_HEADS // 2, GLA_HEAD_V, LANES), f32)],
        compiler_params=_cparams(("parallel", "arbitrary")),
        name="gla",
    )(z3, z3, z3, sm3, w2, gb, gn)


def _merge_kernel(x_ref, oa_ref, ob_ref, ga_ref, gb_ref, wa_ref, wb_ref, wo_ref, o_ref):
    ya = jnp.dot(oa_ref[...], wa_ref[...], preferred_element_type=f32)
    yb = jnp.dot(ob_ref[...], wb_ref[...], preferred_element_type=f32)
    mg = _sigmoid(ga_ref[...].astype(f32)) * ya + _sigmoid(gb_ref[...].astype(f32)) * yb
    o_ref[...] = x_ref[...] + jnp.dot(mg.astype(bf16), wo_ref[...], preferred_element_type=f32)


def _merge(x, oa, ob, z, wa, wb, wo):
    t = x.shape[0]
    tm = TOKEN_TILE
    return pl.pallas_call(
        _merge_kernel,
        grid=(t // tm,),
        in_specs=[
            pl.BlockSpec((tm, D_MODEL), lambda i: (i, 0)),
            pl.BlockSpec((tm, ATTN_W), lambda i: (i, 0)),
            pl.BlockSpec((tm, GLA_V_W), lambda i: (i, 0)),
            pl.BlockSpec((tm, D_MODEL), lambda i: (i, Z_MGA)),
            pl.BlockSpec((tm, D_MODEL), lambda i: (i, Z_MGB)),
            pl.BlockSpec((ATTN_W, D_MODEL), lambda i: (0, 0)),
            pl.BlockSpec((GLA_V_W, D_MODEL), lambda i: (0, 0)),
            pl.BlockSpec((D_MODEL, D_MODEL), lambda i: (0, 0)),
        ],
        out_specs=pl.BlockSpec((tm, D_MODEL), lambda i: (i, 0)),
        out_shape=jax.ShapeDtypeStruct((t, D_MODEL), f32),
        compiler_params=_cparams(("parallel",)),
        name="merge",
    )(x, oa, ob, z, z, wa, wb, wo)


def _ple_kernel(x_ref, p_ref, g_ref, wg_ref, wp_ref, gf_ref, o_ref, *, final):
    x = x_ref[...]
    h = _rms(x, g_ref[...]).astype(bf16)
    gate = _sigmoid(jnp.dot(h, wg_ref[...], preferred_element_type=f32))
    e = jnp.dot(p_ref[...].astype(bf16), wp_ref[...], preferred_element_type=f32)
    y = x + gate * e
    if final:
        y = _rms(y, gf_ref[...])
    o_ref[...] = y


def _ple(x, p, g, wg, wp, gf, final):
    t = x.shape[0]
    tm = TOKEN_TILE
    return pl.pallas_call(
        functools.partial(_ple_kernel, final=final),
        grid=(t // tm,),
        in_specs=[
            pl.BlockSpec((tm, D_MODEL), lambda i: (i, 0)),
            pl.BlockSpec((tm, PLE_DIM), lambda i: (i, 0)),
            pl.BlockSpec((1, D_MODEL), lambda i: (0, 0)),
            pl.BlockSpec((D_MODEL, D_MODEL), lambda i: (0, 0)),
            pl.BlockSpec((PLE_DIM, D_MODEL), lambda i: (0, 0)),
            pl.BlockSpec((1, D_MODEL), lambda i: (0, 0)),
        ],
        out_specs=pl.BlockSpec((tm, D_MODEL), lambda i: (i, 0)),
        out_shape=jax.ShapeDtypeStruct((t, D_MODEL), f32),
        compiler_params=_cparams(("parallel",)),
        name="ple",
    )(x, p, g, wg, wp, gf)


def _split_w_in(w_in):
    cuts = np.cumsum(np.array(SPLIT_SIZES))[:-1].tolist()
    aq, ak, av, iq, ik, iw, gq, gk, gv, gr, ga, mga, mgb = jnp.split(w_in, cuts, axis=-1)
    att_scale = ATTN_HEAD_DIM ** -0.5 * float(np.log2(np.e))
    idx_scale = IDX_HEAD_DIM ** -0.5 * N_IDX_HEADS ** -0.5
    main = jnp.concatenate(
        [mga, mgb, aq * att_scale, ak, av, iq, gq * (GLA_HEAD_K ** -0.5), gk, gv, gr], axis=-1)
    zeros = lambda n: jnp.zeros(w_in.shape[:-1] + (n,), w_in.dtype)
    small = jnp.concatenate(
        [ik, zeros(LANES - IDX_HEAD_DIM), iw * idx_scale, ga,
         zeros(LANES - N_IDX_HEADS - GLA_GATE_RANK)], axis=-1)
    return main.astype(bf16), small.astype(bf16)


def kernel(x, p, w_in, gla_gate_w2, gla_gate_b, gla_norm, w_branch_a, w_branch_b, w_out,
           norm_ff1, norm_mix, norm_ff2, norm_ple, ff1_w_gate, ff1_w_up, ff1_w_down,
           ff2_w_gate, ff2_w_up, ff2_w_down, ple_w_proj, ple_w_gate, norm_final):
    b, s, d = x.shape
    t = b * s
    depth = w_in.shape[0]
    w_main, w_small = _split_w_in(w_in)
    cast = lambda w: w.astype(bf16)
    ff1 = (cast(ff1_w_gate), cast(ff1_w_up), cast(ff1_w_down))
    ff2 = (cast(ff2_w_gate), cast(ff2_w_up), cast(ff2_w_down))
    wa, wb, wo = cast(w_branch_a), cast(w_branch_b), cast(w_out)
    wpg, wpp = cast(ple_w_gate), cast(ple_w_proj)
    row = lambda v: v.reshape(1, -1)

    xf = x.reshape(t, d)
    for l in range(depth):
        xf = _ffn(xf, row(norm_ff1[l]), ff1[0][l], ff1[1][l], ff1[2][l])

        z, ik, sm = _proj(xf, row(norm_mix[l]), w_main[l], w_small[l])
        z3 = z.reshape(b, s, Z_WIDTH)
        sm3 = sm.reshape(b, s, LANES)
        feature_major = lambda a: jnp.swapaxes(a, 1, 2)
        qT = feature_major(z3[:, :, Z_AQ * COL_TILE:(Z_AQ + 1) * COL_TILE])
        vT = feature_major(z3[:, :, Z_AV * COL_TILE:(Z_AV + 1) * COL_TILE])
        iqT = feature_major(z3[:, :, Z_IQ * COL_TILE:(Z_IQ + 1) * COL_TILE])
        iwT = feature_major(sm3[:, :, SM_IW0:SM_IW0 + N_IDX_HEADS])
        oaT = _dsa(qT, iqT, iwT, z3, vT, ik.reshape(b, s, IDX_HEAD_DIM))
        oa = feature_major(oaT).reshape(t, ATTN_W)

        ob = _gla(z3, sm3, gla_gate_w2[l], row(gla_gate_b[l]), row(gla_norm[l]))
        xf = _merge(xf, oa, ob.reshape(t, GLA_V_W), z, wa[l], wb[l], wo[l])

        xf = _ffn(xf, row(norm_ff2[l]), ff2[0][l], ff2[1][l], ff2[2][l])
        xf = _ple(xf, p[l].reshape(t, PLE_DIM), row(norm_ple[l]), wpg[l], wpp[l],
                  row(norm_final), final=(l == depth - 1))
    return xf.reshape(b, s, d)
```

```python
import functools
import statistics

import jax
import jax.numpy as jnp
import numpy as np
from jax import lax
from jax.experimental import pallas as pl
from jax.experimental.pallas import tpu as pltpu

D_MODEL = 1024
D_FF = 2816
PLE_DIM = 256
DEPTH = 2
EPS = 1e-6

CHUNK = 64
N_ATTN_HEADS = 8
ATTN_HEAD_DIM = 64
N_IDX_HEADS = 8
IDX_HEAD_DIM = 64
TOPK_MAX = 256
N_GLA_HEADS = 4
GLA_HEAD_K = 64
GLA_HEAD_V = 128
GLA_GATE_RANK = 16
GLA_GATE_TAU = 16.0

ATTN_W = N_ATTN_HEADS * ATTN_HEAD_DIM
IDX_Q_W = N_IDX_HEADS * IDX_HEAD_DIM
GLA_K_W = N_GLA_HEADS * GLA_HEAD_K
GLA_V_W = N_GLA_HEADS * GLA_HEAD_V
SPLIT_SIZES = (ATTN_W, ATTN_W, ATTN_W, IDX_Q_W, IDX_HEAD_DIM, N_IDX_HEADS,
               GLA_K_W, GLA_K_W, GLA_V_W, GLA_V_W, GLA_GATE_RANK, D_MODEL, D_MODEL)

LANES = 128
SUBLANES = 8
VMEM_LIMIT_BYTES = 56 * 1024 * 1024

COL_TILE = 512
Z_MGA, Z_MGB = 0, 1
Z_AK, Z_GQK, Z_GV, Z_GR = 4, 5, 6, 7
Z_TILES = 8
Z_WIDTH = Z_TILES * COL_TILE
ZT_TILES = 3
SM_IW0 = 0
SM_GA0 = N_IDX_HEADS

TOKEN_TILE = 512
FF_TILE = 1408
DSA_BLOCK = 256
GLA_TILE = 256
SEARCH_MAX_ITERS = 24

NEG_BIG = -1e30
F32_LOWEST = float(np.finfo(np.float32).min)
F32_TINY = float(np.finfo(np.float32).tiny)
INT32_MIN = int(np.iinfo(np.int32).min)

f32 = jnp.float32
bf16 = jnp.bfloat16


def _rms(x, g):
    return x * lax.rsqrt(jnp.mean(x * x, axis=-1, keepdims=True) + EPS) * g


def _sigmoid(x):
    return 1.0 / (1.0 + jnp.exp(-x))


def _cparams(sem):
    return pltpu.CompilerParams(dimension_semantics=sem, vmem_limit_bytes=VMEM_LIMIT_BYTES)


def _ffn_kernel(x_ref, g_ref, wg_ref, wu_ref, wd_ref, o_ref, h_ref, acc_ref):
    f = pl.program_id(1)

    @pl.when(f == 0)
    def _():
        h_ref[...] = _rms(x_ref[...], g_ref[...]).astype(bf16)
        acc_ref[...] = jnp.zeros_like(acc_ref)

    h = h_ref[...]
    gate = jnp.dot(h, wg_ref[...], preferred_element_type=f32)
    up = jnp.dot(h, wu_ref[...], preferred_element_type=f32)
    a = (gate * _sigmoid(gate) * up).astype(bf16)
    acc_ref[...] += jnp.dot(a, wd_ref[...], preferred_element_type=f32)

    @pl.when(f == pl.num_programs(1) - 1)
    def _():
        o_ref[...] = x_ref[...] + 0.5 * acc_ref[...]


def _ffn(x, g, wg, wu, wd):
    t = x.shape[0]
    return pl.pallas_call(
        _ffn_kernel,
        grid=(t // TOKEN_TILE, D_FF // FF_TILE),
        in_specs=[
            pl.BlockSpec((TOKEN_TILE, D_MODEL), lambda i, f: (i, 0)),
            pl.BlockSpec((1, D_MODEL), lambda i, f: (0, 0)),
            pl.BlockSpec((D_MODEL, FF_TILE), lambda i, f: (0, f)),
            pl.BlockSpec((D_MODEL, FF_TILE), lambda i, f: (0, f)),
            pl.BlockSpec((FF_TILE, D_MODEL), lambda i, f: (f, 0)),
        ],
        out_specs=pl.BlockSpec((TOKEN_TILE, D_MODEL), lambda i, f: (i, 0)),
        out_shape=jax.ShapeDtypeStruct((t, D_MODEL), f32),
        scratch_shapes=[pltpu.VMEM((TOKEN_TILE, D_MODEL), bf16),
                        pltpu.VMEM((TOKEN_TILE, D_MODEL), f32)],
        compiler_params=_cparams(("parallel", "arbitrary")),
        name="ffn",
    )(x, g, wg, wu, wd)


def _proj_kernel(x_ref, g_ref, w_ref, wt_ref, ws_ref,
                 z_ref, qT_ref, vT_ref, iqT_ref, ik_ref, sm_ref, iwT_ref):
    h = _rms(x_ref[...], g_ref[...]).astype(bf16)
    for j in range(Z_TILES):
        cols = slice(j * COL_TILE, (j + 1) * COL_TILE)
        z_ref[:, cols] = jnp.dot(h, w_ref[:, cols], preferred_element_type=f32).astype(bf16)
    for j, out_ref in enumerate((qT_ref, vT_ref, iqT_ref)):
        cols = slice(j * COL_TILE, (j + 1) * COL_TILE)
        zt = jnp.dot(h, wt_ref[:, cols], preferred_element_type=f32)
        out_ref[...] = zt.T.astype(bf16)
    s = jnp.dot(h, ws_ref[...], preferred_element_type=f32)
    ik_ref[...] = s[:, :IDX_HEAD_DIM].astype(bf16)
    sm = s[:, LANES:]
    sm_ref[...] = sm
    iwT_ref[...] = sm.T[SM_IW0:SM_IW0 + N_IDX_HEADS, :]


def _proj(x, g, w_main, w_t, w_small, batch):
    t = x.shape[0]
    tm = TOKEN_TILE
    s = t // batch
    per_b = s // tm
    fm = lambda rows: pl.BlockSpec((None, rows, tm), lambda i: (i // per_b, 0, i % per_b))
    return pl.pallas_call(
        _proj_kernel,
        grid=(t // tm,),
        in_specs=[
            pl.BlockSpec((tm, D_MODEL), lambda i: (i, 0)),
            pl.BlockSpec((1, D_MODEL), lambda i: (0, 0)),
            pl.BlockSpec((D_MODEL, Z_WIDTH), lambda i: (0, 0)),
            pl.BlockSpec((D_MODEL, ZT_TILES * COL_TILE), lambda i: (0, 0)),
            pl.BlockSpec((D_MODEL, 2 * LANES), lambda i: (0, 0)),
        ],
        out_specs=[
            pl.BlockSpec((tm, Z_WIDTH), lambda i: (i, 0)),
            fm(ATTN_W), fm(ATTN_W), fm(IDX_Q_W),
            pl.BlockSpec((tm, IDX_HEAD_DIM), lambda i: (i, 0)),
            pl.BlockSpec((tm, LANES), lambda i: (i, 0)),
            fm(N_IDX_HEADS),
        ],
        out_shape=[
            jax.ShapeDtypeStruct((t, Z_WIDTH), bf16),
            jax.ShapeDtypeStruct((batch, ATTN_W, s), bf16),
            jax.ShapeDtypeStruct((batch, ATTN_W, s), bf16),
            jax.ShapeDtypeStruct((batch, IDX_Q_W, s), bf16),
            jax.ShapeDtypeStruct((t, IDX_HEAD_DIM), bf16),
            jax.ShapeDtypeStruct((t, LANES), f32),
            jax.ShapeDtypeStruct((batch, N_IDX_HEADS, s), f32),
        ],
        compiler_params=_cparams(("parallel",)),
        name="proj",
    )(x, g, w_main, w_t, w_small)


def _dsa_kernel(qT_ref, iqT_ref, iwT_ref, tab_ref, k_ref, vT_ref, ik_ref, o_ref,
                ibuf, qTm, s_ref, bias_ref, m_ref, l_ref, alpha_ref, acc_ref, thr_ref,
                pending_ref, *, seq_len):
    qb = DSA_BLOCK
    kt_rows = DSA_BLOCK
    i = pl.program_id(1)
    n_tiles = i + 1

    rowid = lax.broadcasted_iota(jnp.int32, (LANES, qb), 0)
    for h in range(N_ATTN_HEADS):
        pair = qT_ref[(h // 2) * LANES:(h // 2 + 1) * LANES, :]
        keep = (rowid < ATTN_HEAD_DIM) if h % 2 == 0 else (rowid >= ATTN_HEAD_DIM)
        qTm[h] = jnp.where(keep, pair, jnp.zeros_like(pair))
    m_ref[...] = jnp.full(m_ref.shape, NEG_BIG, f32)
    l_ref[...] = jnp.zeros_like(l_ref)
    acc_ref[...] = jnp.zeros_like(acc_ref)

    def fold(x, op):
        return op(x.reshape(kt_rows // SUBLANES, SUBLANES, qb), axis=0)

    def idx_tile(kt, diagonal, stats):
        r0 = pl.multiple_of(kt * kt_rows, kt_rows)
        ki_t = ik_ref[pl.ds(r0, kt_rows), :]
        sc = jnp.zeros((kt_rows, qb), f32)
        for h in range(N_IDX_HEADS):
            s = jnp.dot(ki_t, iqT_ref[h * IDX_HEAD_DIM:(h + 1) * IDX_HEAD_DIM, :],
                        preferred_element_type=f32)
            sc = sc + iwT_ref[h:h + 1, :] * jnp.maximum(s, 0.0)
        sc_hi = sc_lo = sc_0 = sc
        if diagonal:
            kc = lax.broadcasted_iota(jnp.int32, (kt_rows, qb), 0) // CHUNK
            qc = lax.broadcasted_iota(jnp.int32, (kt_rows, qb), 1) // CHUNK
            adm = kc <= qc
            sc_hi = jnp.where(adm, sc, -jnp.inf)
            sc_lo = jnp.where(adm, sc, jnp.inf)
            sc_0 = jnp.where(adm, sc, 0.0)
        ibuf[pl.ds(r0, kt_rows), :] = sc_hi
        mx8, mn8, s8, ss8 = stats
        return (jnp.maximum(mx8, fold(sc_hi, jnp.max)), jnp.minimum(mn8, fold(sc_lo, jnp.min)),
                s8 + fold(sc_0, jnp.sum), ss8 + fold(sc_0 * sc_0, jnp.sum))

    stats0 = (jnp.full((SUBLANES, qb), -jnp.inf, f32), jnp.full((SUBLANES, qb), jnp.inf, f32),
              jnp.zeros((SUBLANES, qb), f32), jnp.zeros((SUBLANES, qb), f32))
    stats = lax.fori_loop(0, i, lambda kt, st: idx_tile(kt, False, st), stats0)
    mx8, mn8, s8, ss8 = idx_tile(i, True, stats)
    col_max = jnp.max(mx8, axis=0, keepdims=True)
    col_min = jnp.min(mn8, axis=0, keepdims=True)
    col_sum = jnp.sum(s8, axis=0, keepdims=True)
    col_ssq = jnp.sum(ss8, axis=0, keepdims=True)

    def count(pred):
        def body(r, acc):
            r0 = pl.multiple_of(r * kt_rows, kt_rows)
            tile = ibuf[pl.ds(r0, kt_rows), :]
            rows = r0 + lax.broadcasted_iota(jnp.int32, (kt_rows, qb), 0)
            ind = jnp.where(pred(tile, rows), 1.0, 0.0)
            part = ind.reshape(kt_rows // SUBLANES, SUBLANES, qb)
            while part.shape[0] > 1:
                half = part.shape[0] // 2
                part = part[:half] + part[half:]
            return acc + part[0]
        acc = lax.fori_loop(0, n_tiles, body, jnp.zeros((SUBLANES, qb), f32))
        return jnp.sum(acc, axis=0, keepdims=True)

    def resolve_ties(tv, need, mult):
        is_tie = tv == tv
        n_keys = n_tiles * kt_rows

        def tie_cond(st):
            return st["active"] > 0

        def tie_body(st):
            plo, phi, flo, fhi = st["plo"], st["phi"], st["flo"], st["fhi"]
            span = phi - plo
            est = ((need - flo) / jnp.maximum(fhi - flo, 1.0) * span.astype(f32)).astype(jnp.int32)
            pick = jnp.where(st["bisect"] > 0, span // 2, est)
            cand = plo + jnp.clip(pick, 1, jnp.maximum(span - 1, 1))
            f = count(lambda tile, rows: jnp.where(tile == tv, rows, seq_len) < cand)
            lower = f < need
            exact = f == need
            plo = jnp.where(lower, cand, jnp.where(exact, cand - 1, plo))
            phi = jnp.where(lower, phi, cand)
            flo = jnp.where(lower, f, flo)
            fhi = jnp.where(lower, fhi, f)
            unsplit = jnp.where(is_tie & (phi - plo > 1), 1, 0)
            return dict(active=jnp.max(unsplit), bisect=1 - st["bisect"],
                        plo=plo, phi=phi, flo=flo, fhi=fhi)

        split = lax.while_loop(tie_cond, tie_body, dict(
            active=jnp.int32(1), bisect=jnp.int32(0),
            plo=jnp.zeros((1, qb), jnp.int32), phi=jnp.zeros((1, qb), jnp.int32) + n_keys,
            flo=jnp.zeros((1, qb), f32), fhi=mult))["phi"]

        def drop_body(r, carry):
            r0 = pl.multiple_of(r * kt_rows, kt_rows)
            tile = ibuf[pl.ds(r0, kt_rows), :]
            rows = r0 + lax.broadcasted_iota(jnp.int32, (kt_rows, qb), 0)
            dropped = jnp.where(tile == tv, rows, -1) >= split
            ibuf[pl.ds(r0, kt_rows), :] = jnp.where(dropped, -jnp.inf, tile)
            return carry

        lax.fori_loop(0, n_tiles, drop_body, 0)

    qpos = i * qb + lax.broadcasted_iota(jnp.int32, (1, qb), 1)
    n_adm = (qpos // CHUNK + 1) * CHUNK
    select_all = n_adm <= TOPK_MAX
    kf = float(TOPK_MAX)

    thr_ref[...] = jnp.full((1, qb), F32_LOWEST, f32)
    pending_ref[0] = jnp.int32(0)

    @pl.when(n_tiles * kt_rows > TOPK_MAX)
    def _():
        n_f = n_adm.astype(f32)
        mean = col_sum / n_f
        sigma = jnp.sqrt(jnp.maximum(col_ssq / n_f - mean * mean, 0.0))
        spacing = tab_ref[1:2, :] * sigma

        def inside(c, lo, hi):
            return (c > lo) & (c < hi)

        c0 = mean + tab_ref[0:1, :] * sigma
        c0 = jnp.where(inside(c0, col_min, col_max), c0, 0.5 * col_min + 0.5 * col_max)
        zero = jnp.zeros((1, qb), f32)
        one = jnp.ones((1, qb), f32)
        state0 = dict(it=jnp.int32(0), active=jnp.int32(1), c=c0, lo=col_min, hi=col_max,
                      glo=n_f, ghi=zero, tlo=zero, thi=zero, boost=one, side=zero,
                      thr=jnp.full((1, qb), F32_LOWEST, f32),
                      open=jnp.where(select_all, 0.0, 1.0), tie=zero)

        def search_cond(st):
            return (st["it"] < SEARCH_MAX_ITERS) & (st["active"] > 0)

        def search_body(st):
            c, lo, hi = st["c"], st["lo"], st["hi"]
            g = count(lambda tile, rows: tile >= c)
            hit = (g == kf) & (st["open"] > 0.0)
            thr = jnp.where(hit, c, st["thr"])
            still = jnp.where(hit, 0.0, st["open"])
            above = g > kf
            lo = jnp.where(above, c, lo)
            hi = jnp.where(above, hi, c)
            glo = jnp.where(above, g, st["glo"])
            ghi = jnp.where(above, st["ghi"], g)
            tlo = jnp.where(above, 1.0, st["tlo"])
            thi = jnp.where(above, st["thi"], 1.0)
            side = jnp.where(above, 1.0, -1.0)
            repeat = side == st["side"]
            boost = jnp.where(repeat, 2.0 * st["boost"], 1.0)
            mid = 0.5 * lo + 0.5 * hi
            stale = 1.0 / jnp.minimum(boost, 256.0)
            w_lo = (glo - kf - 0.5) * jnp.where(above, 1.0, stale)
            w_hi = (kf + 0.5 - ghi) * jnp.where(above, stale, 1.0)
            c_two = lo + w_lo / (w_lo + w_hi) * (hi - lo)
            step = 2.0 * boost * spacing
            c_one = jnp.where(tlo > 0.0, lo + jnp.maximum(glo - kf, 1.0) * step,
                              hi - jnp.maximum(kf - ghi, 1.0) * step)
            c_one = jnp.where(inside(c_one, lo, hi), c_one, mid)
            both = (tlo > 0.0) & (thi > 0.0)
            c_new = jnp.where(both, c_two, c_one)
            c_new = jnp.where(both & (lo < 0.0) & (hi > 0.0), 0.0, c_new)
            c_new = jnp.where((lo == 0.0) & (hi > F32_TINY), F32_TINY, c_new)
            movable = inside(c_new, lo, hi) & ~((lo == 0.0) & (hi <= F32_TINY))
            closed = ~movable & both & (still > 0.0)
            tie = jnp.where(closed, 1.0, st["tie"])
            still = jnp.where(closed, 0.0, still)
            searching = jnp.where(movable, still, 0.0)
            return dict(it=st["it"] + 1, active=(jnp.max(searching) > 0.0).astype(jnp.int32),
                        c=c_new, lo=lo, hi=hi, glo=glo, ghi=ghi, tlo=tlo, thi=thi,
                        boost=boost, side=side, thr=thr, open=still, tie=tie)

        final = lax.while_loop(search_cond, search_body, state0)
        tied = final["tie"] > 0.0
        thr_ref[...] = jnp.where(tied, final["lo"], final["thr"])
        pending_ref[0] = (jnp.max(final["open"]) > 0.0).astype(jnp.int32)

        @pl.when(jnp.max(final["tie"]) > 0.0)
        def _():
            resolve_ties(jnp.where(tied, final["lo"], jnp.nan), kf - final["ghi"],
                         final["glo"] - final["ghi"])

    @pl.when(pending_ref[0] > 0)
    def _():
        def key_to_f32(key):
            bits = jnp.where(key < 0, key ^ jnp.int32(0x7FFFFFFF), key)
            return lax.bitcast_convert_type(bits, f32)

        def bit_body(t, prefix):
            step = lax.shift_left(jnp.int32(1), jnp.int32(31) - t)
            cand = prefix + step
            cand_f = key_to_f32(cand)
            cnt = count(lambda tile, rows: tile >= cand_f)
            return jnp.where(cnt >= kf, cand, prefix)

        prefix = lax.fori_loop(0, 32, bit_body, jnp.full((1, qb), INT32_MIN, jnp.int32))
        thr = jnp.where(select_all, F32_LOWEST, key_to_f32(prefix))
        thr_ref[...] = thr

        cnt_ge = count(lambda tile, rows: tile >= thr)
        excess = jnp.where(select_all, 0.0, cnt_ge - kf)

        @pl.when(jnp.max(excess) > 0.0)
        def _():
            cnt_gt = count(lambda tile, rows: tile > thr)
            resolve_ties(jnp.where(excess > 0.0, thr, jnp.nan), kf - cnt_gt, cnt_ge - cnt_gt)

    thr = thr_ref[...]
    ones_rows = jnp.ones((2 * SUBLANES, kt_rows), bf16)

    def att_body(kt, carry):
        r0 = pl.multiple_of(kt * kt_rows, kt_rows)
        bias_ref[...] = jnp.where(ibuf[pl.ds(r0, kt_rows), :] >= thr, 0.0, NEG_BIG)
        for h in range(N_ATTN_HEADS):
            k_pair = k_ref[pl.ds(r0, kt_rows), (h // 2) * LANES:(h // 2 + 1) * LANES]
            s_ref[h] = jnp.dot(k_pair, qTm[h], preferred_element_type=f32) + bias_ref[...]
            m_old = m_ref[h]
            m_new = jnp.maximum(m_old, jnp.max(s_ref[h], axis=0, keepdims=True))
            alpha_ref[h] = jnp.exp2(m_old - m_new)
            m_ref[h] = m_new
        for h in range(N_ATTN_HEADS):
            p = jnp.exp2(s_ref[h] - m_ref[h]).astype(bf16)
            hs = slice(h * ATTN_HEAD_DIM, (h + 1) * ATTN_HEAD_DIM)
            lhs = jnp.concatenate([vT_ref[hs, pl.ds(r0, kt_rows)], ones_rows], axis=0)
            pv = jnp.dot(lhs, p, preferred_element_type=f32)
            alpha = alpha_ref[h]
            acc_ref[hs, :] = alpha * acc_ref[hs, :] + pv[:ATTN_HEAD_DIM]
            l_ref[h] = alpha * l_ref[h] + pv[ATTN_HEAD_DIM:ATTN_HEAD_DIM + 1]
        return carry

    lax.fori_loop(0, n_tiles, att_body, 0)

    for h in range(N_ATTN_HEADS):
        hs = slice(h * ATTN_HEAD_DIM, (h + 1) * ATTN_HEAD_DIM)
        acc_ref[hs, :] = acc_ref[hs, :] / l_ref[h]
    o_ref[...] = acc_ref[...].T.astype(o_ref.dtype)


def _search_hints(seq_len):
    n = (np.arange(seq_len) // CHUNK + 1) * CHUNK
    frac = np.minimum(TOPK_MAX / n, 0.5)
    nd = statistics.NormalDist()
    z_of = {f: nd.inv_cdf(1.0 - f) for f in np.unique(frac)}
    z = np.array([z_of[f] for f in frac])
    dens = n * np.exp(-0.5 * z * z) / np.sqrt(2.0 * np.pi)
    return jnp.asarray(np.stack([z, 1.0 / dens]), f32)


def _dsa(qT, iqT, iwT, z3, vT, ik3):
    b, _, s = qT.shape
    qb = DSA_BLOCK
    return pl.pallas_call(
        functools.partial(_dsa_kernel, seq_len=s),
        grid=(b, s // qb),
        in_specs=[
            pl.BlockSpec((None, ATTN_W, qb), lambda bi, i: (bi, 0, i)),
            pl.BlockSpec((None, IDX_Q_W, qb), lambda bi, i: (bi, 0, i)),
            pl.BlockSpec((None, N_IDX_HEADS, qb), lambda bi, i: (bi, 0, i)),
            pl.BlockSpec((2, qb), lambda bi, i: (0, i)),
            pl.BlockSpec((None, s, COL_TILE), lambda bi, i: (bi, 0, Z_AK)),
            pl.BlockSpec((None, ATTN_W, s), lambda bi, i: (bi, 0, 0)),
            pl.BlockSpec((None, s, IDX_HEAD_DIM), lambda bi, i: (bi, 0, 0)),
        ],
        out_specs=pl.BlockSpec((None, qb, ATTN_W), lambda bi, i: (bi, i, 0)),
        out_shape=jax.ShapeDtypeStruct((b, s, ATTN_W), bf16),
        scratch_shapes=[
            pltpu.VMEM((s, qb), f32),
            pltpu.VMEM((N_ATTN_HEADS, LANES, qb), bf16),
            pltpu.VMEM((N_ATTN_HEADS, qb, qb), f32),
            pltpu.VMEM((qb, qb), f32),
            pltpu.VMEM((N_ATTN_HEADS, 1, qb), f32),
            pltpu.VMEM((N_ATTN_HEADS, 1, qb), f32),
            pltpu.VMEM((N_ATTN_HEADS, 1, qb), f32),
            pltpu.VMEM((ATTN_W, qb), f32),
            pltpu.VMEM((1, qb), f32),
            pltpu.SMEM((1,), jnp.int32),
        ],
        compiler_params=_cparams(("parallel", "arbitrary")),
        name="dsa",
    )(qT, iqT, iwT, _search_hints(s), z3, vT, ik3)


def _gla_kernel(qk_ref, v_ref, r_ref, sm_ref, w2_ref, gb_ref, gn_ref, o_ref, state_ref):
    c = CHUNK
    n_batch = qk_ref.shape[0]

    @pl.when(pl.program_id(0) == 0)
    def _():
        state_ref[...] = jnp.zeros_like(state_ref)

    row = lax.broadcasted_iota(jnp.int32, (c, c), 0)
    col = lax.broadcasted_iota(jnp.int32, (c, c), 1)
    causal = row >= col
    tril = jnp.where(causal, 1.0, 0.0)
    lane = lax.broadcasted_iota(jnp.int32, (c, LANES), 1)
    half = (lane < GLA_HEAD_K, lane >= GLA_HEAD_K)
    contract_last = (((1,), (1,)), ((), ()))
    contract_first = (((0,), (0,)), ((), ()))

    for ci, bi in [(ci, bi) for ci in range(GLA_TILE // c) for bi in range(n_batch)]:
        rows = slice(ci * c, (ci + 1) * c)
        qk = qk_ref[bi, rows, :].astype(f32)
        q = qk[:, :GLA_K_W]
        k = qk[:, GLA_K_W:]
        ga = sm_ref[bi, rows, :][:, SM_GA0:SM_GA0 + GLA_GATE_RANK]
        glin = jnp.dot(ga, w2_ref[...], preferred_element_type=f32,
                       precision=lax.Precision.HIGHEST) + gb_ref[...]
        logg = (jnp.minimum(glin, 0.0) - jnp.log(1.0 + jnp.exp(-jnp.abs(glin)))) / GLA_GATE_TAU
        bcum = jnp.dot(tril, logg, preferred_element_type=f32, precision=lax.Precision.HIGHEST)
        b_last = bcum[c - 1:c, :]
        b_mid = bcum[c // 2 - 1:c // 2, :]
        q_in = q * jnp.exp(bcum)
        q_mid = q * jnp.exp(bcum - b_mid)
        k_mid = k * jnp.exp(b_mid - bcum)
        k_out = k * jnp.exp(b_last - bcum)
        decay = jnp.exp(b_last)

        for pr in range(N_GLA_HEADS // 2):
            ps = slice(pr * LANES, (pr + 1) * LANES)
            st = state_ref[bi, pr]
            st_b = st.astype(bf16)
            k_mid_p = k_mid[:, ps].astype(bf16)
            upd = jnp.zeros_like(st)
            for hh in range(2):
                h = pr * 2 + hh
                vs = slice(h * GLA_HEAD_V, (h + 1) * GLA_HEAD_V)
                v_h = v_ref[bi, rows, vs]
                q_mid_m = jnp.where(half[hh], q_mid[:, ps], 0.0).astype(bf16)
                q_in_m = jnp.where(half[hh], q_in[:, ps], 0.0).astype(bf16)
                k_out_m = jnp.where(half[hh], k_out[:, ps], 0.0).astype(bf16)
                attn = lax.dot_general(q_mid_m, k_mid_p, contract_last,
                                       preferred_element_type=f32)
                attn = jnp.where(causal, attn, 0.0).astype(bf16)
                o = lax.dot_general(q_in_m, st_b, contract_last, preferred_element_type=f32)
                o = o + jnp.dot(attn, v_h, preferred_element_type=f32)
                upd = upd + lax.dot_general(v_h, k_out_m, contract_first,
                                            preferred_element_type=f32)
                y = _rms(o, gn_ref[...])
                r = r_ref[bi, rows, vs].astype(f32)
                o_ref[bi, rows, vs] = (y * r * _sigmoid(r)).astype(o_ref.dtype)
            state_ref[bi, pr] = decay[:, ps] * st + upd


def _gla(z3, sm3, w2, gb, gn):
    b, s, _ = z3.shape
    tb = GLA_TILE
    return pl.pallas_call(
        _gla_kernel,
        grid=(s // tb,),
        in_specs=[
            pl.BlockSpec((b, tb, COL_TILE), lambda j: (0, j, Z_GQK)),
            pl.BlockSpec((b, tb, COL_TILE), lambda j: (0, j, Z_GV)),
            pl.BlockSpec((b, tb, COL_TILE), lambda j: (0, j, Z_GR)),
            pl.BlockSpec((b, tb, LANES), lambda j: (0, j, 0)),
            pl.BlockSpec((GLA_GATE_RANK, GLA_K_W), lambda j: (0, 0)),
            pl.BlockSpec((1, GLA_K_W), lambda j: (0, 0)),
            pl.BlockSpec((1, GLA_HEAD_V), lambda j: (0, 0)),
        ],
        out_specs=pl.BlockSpec((b, tb, GLA_V_W), lambda j: (0, j, 0)),
        out_shape=jax.ShapeDtypeStruct((b, s, GLA_V_W), bf16),
        scratch_shapes=[pltpu.VMEM((b, N_GLA_HEADS // 2, GLA_HEAD_V, LANES), f32)],
        compiler_params=_cparams(("arbitrary",)),
        name="gla",
    )(z3, z3, z3, sm3, w2, gb, gn)


def _merge_kernel(x_ref, oa_ref, ob_ref, ga_ref, gb_ref, wa_ref, wb_ref, wo_ref, o_ref):
    ya = jnp.dot(oa_ref[...], wa_ref[...], preferred_element_type=f32)
    yb = jnp.dot(ob_ref[...], wb_ref[...], preferred_element_type=f32)
    mg = _sigmoid(ga_ref[...].astype(f32)) * ya + _sigmoid(gb_ref[...].astype(f32)) * yb
    o_ref[...] = x_ref[...] + jnp.dot(mg.astype(bf16), wo_ref[...], preferred_element_type=f32)


def _merge(x, oa, ob, z, wa, wb, wo):
    t = x.shape[0]
    tm = TOKEN_TILE
    return pl.pallas_call(
        _merge_kernel,
        grid=(t // tm,),
        in_specs=[
            pl.BlockSpec((tm, D_MODEL), lambda i: (i, 0)),
            pl.BlockSpec((tm, ATTN_W), lambda i: (i, 0)),
            pl.BlockSpec((tm, GLA_V_W), lambda i: (i, 0)),
            pl.BlockSpec((tm, D_MODEL), lambda i: (i, Z_MGA)),
            pl.BlockSpec((tm, D_MODEL), lambda i: (i, Z_MGB)),
            pl.BlockSpec((ATTN_W, D_MODEL), lambda i: (0, 0)),
            pl.BlockSpec((GLA_V_W, D_MODEL), lambda i: (0, 0)),
            pl.BlockSpec((D_MODEL, D_MODEL), lambda i: (0, 0)),
        ],
        out_specs=pl.BlockSpec((tm, D_MODEL), lambda i: (i, 0)),
        out_shape=jax.ShapeDtypeStruct((t, D_MODEL), f32),
        compiler_params=_cparams(("parallel",)),
        name="merge",
    )(x, oa, ob, z, z, wa, wb, wo)


def _ple_kernel(x_ref, p_ref, g_ref, wg_ref, wp_ref, gf_ref, o_ref, *, final):
    x = x_ref[...]
    h = _rms(x, g_ref[...]).astype(bf16)
    gate = _sigmoid(jnp.dot(h, wg_ref[...], preferred_element_type=f32))
    e = jnp.dot(p_ref[...].astype(bf16), wp_ref[...], preferred_element_type=f32)
    y = x + gate * e
    if final:
        y = _rms(y, gf_ref[...])
    o_ref[...] = y


def _ple(x, p, g, wg, wp, gf, final):
    t = x.shape[0]
    tm = TOKEN_TILE
    return pl.pallas_call(
        functools.partial(_ple_kernel, final=final),
        grid=(t // tm,),
        in_specs=[
            pl.BlockSpec((tm, D_MODEL), lambda i: (i, 0)),
            pl.BlockSpec((tm, PLE_DIM), lambda i: (i, 0)),
            pl.BlockSpec((1, D_MODEL), lambda i: (0, 0)),
            pl.BlockSpec((D_MODEL, D_MODEL), lambda i: (0, 0)),
            pl.BlockSpec((PLE_DIM, D_MODEL), lambda i: (0, 0)),
            pl.BlockSpec((1, D_MODEL), lambda i: (0, 0)),
        ],
        out_specs=pl.BlockSpec((tm, D_MODEL), lambda i: (i, 0)),
        out_shape=jax.ShapeDtypeStruct((t, D_MODEL), f32),
        compiler_params=_cparams(("parallel",)),
        name="ple",
    )(x, p, g, wg, wp, gf)


def _split_w_in(w_in):
    cuts = np.cumsum(np.array(SPLIT_SIZES))[:-1].tolist()
    aq, ak, av, iq, ik, iw, gq, gk, gv, gr, ga, mga, mgb = jnp.split(w_in, cuts, axis=-1)
    att_scale = ATTN_HEAD_DIM ** -0.5 * float(np.log2(np.e))
    idx_scale = IDX_HEAD_DIM ** -0.5 * N_IDX_HEADS ** -0.5
    main = jnp.concatenate([mga, mgb, ak, gq * (GLA_HEAD_K ** -0.5), gk, gv, gr], axis=-1)
    feature_major = jnp.concatenate([aq * att_scale, av, iq], axis=-1)
    zeros = lambda n: jnp.zeros(w_in.shape[:-1] + (n,), w_in.dtype)
    small = jnp.concatenate(
        [ik, zeros(LANES - IDX_HEAD_DIM), iw * idx_scale, ga,
         zeros(LANES - N_IDX_HEADS - GLA_GATE_RANK)], axis=-1)
    return main.astype(bf16), feature_major.astype(bf16), small.astype(bf16)


def kernel(x, p, w_in, gla_gate_w2, gla_gate_b, gla_norm, w_branch_a, w_branch_b, w_out,
           norm_ff1, norm_mix, norm_ff2, norm_ple, ff1_w_gate, ff1_w_up, ff1_w_down,
           ff2_w_gate, ff2_w_up, ff2_w_down, ple_w_proj, ple_w_gate, norm_final):
    b, s, d = x.shape
    t = b * s
    depth = w_in.shape[0]
    w_main, w_fm, w_small = _split_w_in(w_in)
    cast = lambda w: w.astype(bf16)
    ff1 = (cast(ff1_w_gate), cast(ff1_w_up), cast(ff1_w_down))
    ff2 = (cast(ff2_w_gate), cast(ff2_w_up), cast(ff2_w_down))
    wa, wb, wo = cast(w_branch_a), cast(w_branch_b), cast(w_out)
    wpg, wpp = cast(ple_w_gate), cast(ple_w_proj)
    row = lambda v: v.reshape(1, -1)

    xf = x.reshape(t, d)
    for l in range(depth):
        xf = _ffn(xf, row(norm_ff1[l]), ff1[0][l], ff1[1][l], ff1[2][l])

        z, qT, vT, iqT, ik, sm, iwT = _proj(xf, row(norm_mix[l]), w_main[l], w_fm[l],
                                            w_small[l], b)
        z3 = z.reshape(b, s, Z_WIDTH)
        sm3 = sm.reshape(b, s, LANES)
        oa = _dsa(qT, iqT, iwT, z3, vT, ik.reshape(b, s, IDX_HEAD_DIM))
        ob = _gla(z3, sm3, gla_gate_w2[l], row(gla_gate_b[l]), row(gla_norm[l]))
        xf = _merge(xf, oa.reshape(t, ATTN_W), ob.reshape(t, GLA_V_W), z, wa[l], wb[l], wo[l])

        xf = _ffn(xf, row(norm_ff2[l]), ff2[0][l], ff2[1][l], ff2[2][l])
        xf = _ple(xf, p[l].reshape(t, PLE_DIM), row(norm_ple[l]), wpg[l], wpp[l],
                  row(norm_final), final=(l == depth - 1))
    return xf.reshape(b, s, d)
```

```python
import functools
import statistics

import jax
import jax.numpy as jnp
import numpy as np
from jax import lax
from jax.experimental import pallas as pl
from jax.experimental.pallas import tpu as pltpu

D_MODEL = 1024
D_FF = 2816
PLE_DIM = 256
DEPTH = 2
EPS = 1e-6

CHUNK = 64
N_ATTN_HEADS = 8
ATTN_HEAD_DIM = 64
N_IDX_HEADS = 8
IDX_HEAD_DIM = 64
TOPK_MAX = 256
N_GLA_HEADS = 4
GLA_HEAD_K = 64
GLA_HEAD_V = 128
GLA_GATE_RANK = 16
GLA_GATE_TAU = 16.0

ATTN_W = N_ATTN_HEADS * ATTN_HEAD_DIM
IDX_Q_W = N_IDX_HEADS * IDX_HEAD_DIM
GLA_K_W = N_GLA_HEADS * GLA_HEAD_K
GLA_V_W = N_GLA_HEADS * GLA_HEAD_V
SPLIT_SIZES = (ATTN_W, ATTN_W, ATTN_W, IDX_Q_W, IDX_HEAD_DIM, N_IDX_HEADS,
               GLA_K_W, GLA_K_W, GLA_V_W, GLA_V_W, GLA_GATE_RANK, D_MODEL, D_MODEL)

LANES = 128
SUBLANES = 8
VMEM_LIMIT_BYTES = 56 * 1024 * 1024

COL_TILE = 512
Z_MGA, Z_MGB = 0, 1
Z_AK, Z_GQK, Z_GV, Z_GR = 4, 5, 6, 7
Z_TILES = 8
Z_WIDTH = Z_TILES * COL_TILE
ZT_TILES = 3
SM_IW0 = 0
SM_GA0 = N_IDX_HEADS

TOKEN_TILE = 512
FF_TILE = 1408
DSA_BLOCK = 256
GLA_TILE = 256
SEARCH_MAX_ITERS = 24
SEARCH_FINISH_FROM = 4

NEG_BIG = -1e30
F32_LOWEST = float(np.finfo(np.float32).min)
F32_TINY = float(np.finfo(np.float32).tiny)
INT32_MIN = int(np.iinfo(np.int32).min)

f32 = jnp.float32
bf16 = jnp.bfloat16


def _rms(x, g):
    return x * lax.rsqrt(jnp.mean(x * x, axis=-1, keepdims=True) + EPS) * g


def _sigmoid(x):
    return 1.0 / (1.0 + jnp.exp(-x))


def _cparams(sem):
    return pltpu.CompilerParams(dimension_semantics=sem, vmem_limit_bytes=VMEM_LIMIT_BYTES)


def _ffn_kernel(x_ref, g_ref, wg_ref, wu_ref, wd_ref, o_ref, h_ref, acc_ref):
    f = pl.program_id(1)

    @pl.when(f == 0)
    def _():
        h_ref[...] = _rms(x_ref[...], g_ref[...]).astype(bf16)
        acc_ref[...] = jnp.zeros_like(acc_ref)

    h = h_ref[...]
    gate = jnp.dot(h, wg_ref[...], preferred_element_type=f32)
    up = jnp.dot(h, wu_ref[...], preferred_element_type=f32)
    a = (gate * _sigmoid(gate) * up).astype(bf16)
    acc_ref[...] += jnp.dot(a, wd_ref[...], preferred_element_type=f32)

    @pl.when(f == pl.num_programs(1) - 1)
    def _():
        o_ref[...] = x_ref[...] + 0.5 * acc_ref[...]


def _ffn(x, g, wg, wu, wd):
    t = x.shape[0]
    return pl.pallas_call(
        _ffn_kernel,
        grid=(t // TOKEN_TILE, D_FF // FF_TILE),
        in_specs=[
            pl.BlockSpec((TOKEN_TILE, D_MODEL), lambda i, f: (i, 0)),
            pl.BlockSpec((1, D_MODEL), lambda i, f: (0, 0)),
            pl.BlockSpec((D_MODEL, FF_TILE), lambda i, f: (0, f)),
            pl.BlockSpec((D_MODEL, FF_TILE), lambda i, f: (0, f)),
            pl.BlockSpec((FF_TILE, D_MODEL), lambda i, f: (f, 0)),
        ],
        out_specs=pl.BlockSpec((TOKEN_TILE, D_MODEL), lambda i, f: (i, 0)),
        out_shape=jax.ShapeDtypeStruct((t, D_MODEL), f32),
        scratch_shapes=[pltpu.VMEM((TOKEN_TILE, D_MODEL), bf16),
                        pltpu.VMEM((TOKEN_TILE, D_MODEL), f32)],
        compiler_params=_cparams(("parallel", "arbitrary")),
        name="ffn",
    )(x, g, wg, wu, wd)


def _proj_kernel(x_ref, g_ref, w_ref, wt_ref, ws_ref,
                 z_ref, qT_ref, vT_ref, iqT_ref, ik_ref, sm_ref, iwT_ref):
    h = _rms(x_ref[...], g_ref[...]).astype(bf16)
    for j in range(Z_TILES):
        cols = slice(j * COL_TILE, (j + 1) * COL_TILE)
        z_ref[:, cols] = jnp.dot(h, w_ref[:, cols], preferred_element_type=f32).astype(bf16)
    for j, out_ref in enumerate((qT_ref, vT_ref, iqT_ref)):
        cols = slice(j * COL_TILE, (j + 1) * COL_TILE)
        zt = jnp.dot(h, wt_ref[:, cols], preferred_element_type=f32)
        out_ref[...] = zt.T.astype(bf16)
    s = jnp.dot(h, ws_ref[...], preferred_element_type=f32)
    ik_ref[...] = s[:, :IDX_HEAD_DIM].astype(bf16)
    sm = s[:, LANES:]
    sm_ref[...] = sm
    iwT_ref[...] = sm.T[SM_IW0:SM_IW0 + N_IDX_HEADS, :]


def _proj(x, g, w_main, w_t, w_small, batch):
    t = x.shape[0]
    tm = TOKEN_TILE
    s = t // batch
    per_b = s // tm
    fm = lambda rows: pl.BlockSpec((None, rows, tm), lambda i: (i // per_b, 0, i % per_b))
    return pl.pallas_call(
        _proj_kernel,
        grid=(t // tm,),
        in_specs=[
            pl.BlockSpec((tm, D_MODEL), lambda i: (i, 0)),
            pl.BlockSpec((1, D_MODEL), lambda i: (0, 0)),
            pl.BlockSpec((D_MODEL, Z_WIDTH), lambda i: (0, 0)),
            pl.BlockSpec((D_MODEL, ZT_TILES * COL_TILE), lambda i: (0, 0)),
            pl.BlockSpec((D_MODEL, 2 * LANES), lambda i: (0, 0)),
        ],
        out_specs=[
            pl.BlockSpec((tm, Z_WIDTH), lambda i: (i, 0)),
            fm(ATTN_W), fm(ATTN_W), fm(IDX_Q_W),
            pl.BlockSpec((tm, IDX_HEAD_DIM), lambda i: (i, 0)),
            pl.BlockSpec((tm, LANES), lambda i: (i, 0)),
            fm(N_IDX_HEADS),
        ],
        out_shape=[
            jax.ShapeDtypeStruct((t, Z_WIDTH), bf16),
            jax.ShapeDtypeStruct((batch, ATTN_W, s), bf16),
            jax.ShapeDtypeStruct((batch, ATTN_W, s), bf16),
            jax.ShapeDtypeStruct((batch, IDX_Q_W, s), bf16),
            jax.ShapeDtypeStruct((t, IDX_HEAD_DIM), bf16),
            jax.ShapeDtypeStruct((t, LANES), f32),
            jax.ShapeDtypeStruct((batch, N_IDX_HEADS, s), f32),
        ],
        compiler_params=_cparams(("parallel",)),
        name="proj",
    )(x, g, w_main, w_t, w_small)


def _dsa_kernel(qT_ref, iqT_ref, iwT_ref, tab_ref, k_ref, vT_ref, ik_ref, o_ref,
                ibuf, qTm, s_ref, bias_ref, m_ref, l_ref, alpha_ref, acc_ref, thr_ref,
                ext_ref, pending_ref, finish_ref, *, seq_len):
    qb = DSA_BLOCK
    kt_rows = DSA_BLOCK
    i = pl.program_id(1)
    n_tiles = i + 1

    rowid = lax.broadcasted_iota(jnp.int32, (LANES, qb), 0)
    for h in range(N_ATTN_HEADS):
        pair = qT_ref[(h // 2) * LANES:(h // 2 + 1) * LANES, :]
        keep = (rowid < ATTN_HEAD_DIM) if h % 2 == 0 else (rowid >= ATTN_HEAD_DIM)
        qTm[h] = jnp.where(keep, pair, jnp.zeros_like(pair))
    m_ref[...] = jnp.full(m_ref.shape, NEG_BIG, f32)
    l_ref[...] = jnp.zeros_like(l_ref)
    acc_ref[...] = jnp.zeros_like(acc_ref)

    def fold(x, op):
        return op(x.reshape(kt_rows // SUBLANES, SUBLANES, qb), axis=0)

    def idx_tile(kt, diagonal, stats):
        r0 = pl.multiple_of(kt * kt_rows, kt_rows)
        ki_t = ik_ref[pl.ds(r0, kt_rows), :]
        sc = jnp.zeros((kt_rows, qb), f32)
        for h in range(N_IDX_HEADS):
            s = jnp.dot(ki_t, iqT_ref[h * IDX_HEAD_DIM:(h + 1) * IDX_HEAD_DIM, :],
                        preferred_element_type=f32)
            sc = sc + iwT_ref[h:h + 1, :] * jnp.maximum(s, 0.0)
        sc_hi = sc_lo = sc_0 = sc
        if diagonal:
            kc = lax.broadcasted_iota(jnp.int32, (kt_rows, qb), 0) // CHUNK
            qc = lax.broadcasted_iota(jnp.int32, (kt_rows, qb), 1) // CHUNK
            adm = kc <= qc
            sc_hi = jnp.where(adm, sc, -jnp.inf)
            sc_lo = jnp.where(adm, sc, jnp.inf)
            sc_0 = jnp.where(adm, sc, 0.0)
        ibuf[pl.ds(r0, kt_rows), :] = sc_hi
        mx8, mn8, s8, ss8 = stats
        return (jnp.maximum(mx8, fold(sc_hi, jnp.max)), jnp.minimum(mn8, fold(sc_lo, jnp.min)),
                s8 + fold(sc_0, jnp.sum), ss8 + fold(sc_0 * sc_0, jnp.sum))

    stats0 = (jnp.full((SUBLANES, qb), -jnp.inf, f32), jnp.full((SUBLANES, qb), jnp.inf, f32),
              jnp.zeros((SUBLANES, qb), f32), jnp.zeros((SUBLANES, qb), f32))
    stats = lax.fori_loop(0, i, lambda kt, st: idx_tile(kt, False, st), stats0)
    mx8, mn8, s8, ss8 = idx_tile(i, True, stats)
    col_max = jnp.max(mx8, axis=0, keepdims=True)
    col_min = jnp.min(mn8, axis=0, keepdims=True)
    col_sum = jnp.sum(s8, axis=0, keepdims=True)
    col_ssq = jnp.sum(ss8, axis=0, keepdims=True)

    def count(pred):
        def body(r, acc):
            r0 = pl.multiple_of(r * kt_rows, kt_rows)
            tile = ibuf[pl.ds(r0, kt_rows), :]
            rows = r0 + lax.broadcasted_iota(jnp.int32, (kt_rows, qb), 0)
            ind = jnp.where(pred(tile, rows), 1.0, 0.0)
            part = ind.reshape(kt_rows // SUBLANES, SUBLANES, qb)
            while part.shape[0] > 1:
                half = part.shape[0] // 2
                part = part[:half] + part[half:]
            return acc + part[0]
        acc = lax.fori_loop(0, n_tiles, body, jnp.zeros((SUBLANES, qb), f32))
        return jnp.sum(acc, axis=0, keepdims=True)

    def resolve_ties(tv, need, mult):
        is_tie = tv == tv
        n_keys = n_tiles * kt_rows

        def tie_cond(st):
            return st["active"] > 0

        def tie_body(st):
            plo, phi, flo, fhi = st["plo"], st["phi"], st["flo"], st["fhi"]
            span = phi - plo
            est = ((need - flo) / jnp.maximum(fhi - flo, 1.0) * span.astype(f32)).astype(jnp.int32)
            pick = jnp.where(st["bisect"] > 0, span // 2, est)
            cand = plo + jnp.clip(pick, 1, jnp.maximum(span - 1, 1))
            f = count(lambda tile, rows: jnp.where(tile == tv, rows, seq_len) < cand)
            lower = f < need
            exact = f == need
            plo = jnp.where(lower, cand, jnp.where(exact, cand - 1, plo))
            phi = jnp.where(lower, phi, cand)
            flo = jnp.where(lower, f, flo)
            fhi = jnp.where(lower, fhi, f)
            unsplit = jnp.where(is_tie & (phi - plo > 1), 1, 0)
            return dict(active=jnp.max(unsplit), bisect=1 - st["bisect"],
                        plo=plo, phi=phi, flo=flo, fhi=fhi)

        split = lax.while_loop(tie_cond, tie_body, dict(
            active=jnp.int32(1), bisect=jnp.int32(0),
            plo=jnp.zeros((1, qb), jnp.int32), phi=jnp.zeros((1, qb), jnp.int32) + n_keys,
            flo=jnp.zeros((1, qb), f32), fhi=mult))["phi"]

        def drop_body(r, carry):
            r0 = pl.multiple_of(r * kt_rows, kt_rows)
            tile = ibuf[pl.ds(r0, kt_rows), :]
            rows = r0 + lax.broadcasted_iota(jnp.int32, (kt_rows, qb), 0)
            dropped = jnp.where(tile == tv, rows, -1) >= split
            ibuf[pl.ds(r0, kt_rows), :] = jnp.where(dropped, -jnp.inf, tile)
            return carry

        lax.fori_loop(0, n_tiles, drop_body, 0)

    qpos = i * qb + lax.broadcasted_iota(jnp.int32, (1, qb), 1)
    n_adm = (qpos // CHUNK + 1) * CHUNK
    select_all = n_adm <= TOPK_MAX
    kf = float(TOPK_MAX)

    thr_ref[...] = jnp.full((1, qb), F32_LOWEST, f32)
    pending_ref[0] = jnp.int32(0)

    @pl.when(n_tiles * kt_rows > TOPK_MAX)
    def _():
        n_f = n_adm.astype(f32)
        mean = col_sum / n_f
        sigma = jnp.sqrt(jnp.maximum(col_ssq / n_f - mean * mean, 0.0))
        spacing = tab_ref[1:2, :] * sigma

        def inside(c, lo, hi):
            return (c > lo) & (c < hi)

        c0 = mean + tab_ref[0:1, :] * sigma
        c0 = jnp.where(inside(c0, col_min, col_max), c0, 0.5 * col_min + 0.5 * col_max)
        zero = jnp.zeros((1, qb), f32)
        one = jnp.ones((1, qb), f32)
        state0 = dict(it=jnp.int32(0), active=jnp.int32(1), c=c0, lo=col_min, hi=col_max,
                      glo=n_f, ghi=zero, tlo=zero, thi=zero, boost=one, side=zero,
                      thr=jnp.full((1, qb), F32_LOWEST, f32),
                      open=jnp.where(select_all, 0.0, 1.0), tie=zero, kind=zero, emin=zero)

        def search_cond(st):
            return (st["it"] < SEARCH_MAX_ITERS) & (st["active"] > 0)

        def search_body(st):
            c, lo, hi = st["c"], st["lo"], st["hi"]
            g = count(lambda tile, rows: tile >= c)
            hit = (g == kf) & (st["open"] > 0.0)
            thr = jnp.where(hit, c, st["thr"])
            still = jnp.where(hit, 0.0, st["open"])
            above = g > kf
            lo = jnp.where(above, c, lo)
            hi = jnp.where(above, hi, c)
            glo = jnp.where(above, g, st["glo"])
            ghi = jnp.where(above, st["ghi"], g)
            tlo = jnp.where(above, 1.0, st["tlo"])
            thi = jnp.where(above, st["thi"], 1.0)
            side = jnp.where(above, 1.0, -1.0)
            repeat = side == st["side"]
            boost = jnp.where(repeat, 2.0 * st["boost"], 1.0)
            mid = 0.5 * lo + 0.5 * hi
            stale = 1.0 / jnp.minimum(boost, 256.0)
            w_lo = (glo - kf - 0.5) * jnp.where(above, 1.0, stale)
            w_hi = (kf + 0.5 - ghi) * jnp.where(above, stale, 1.0)
            c_two = lo + w_lo / (w_lo + w_hi) * (hi - lo)
            step = 2.0 * boost * spacing
            c_one = jnp.where(tlo > 0.0, lo + jnp.maximum(glo - kf, 1.0) * step,
                              hi - jnp.maximum(kf - ghi, 1.0) * step)
            c_one = jnp.where(inside(c_one, lo, hi), c_one, mid)
            both = (tlo > 0.0) & (thi > 0.0)
            c_new = jnp.where(both, c_two, c_one)
            c_new = jnp.where(both & (lo < 0.0) & (hi > 0.0), 0.0, c_new)
            c_new = jnp.where((lo == 0.0) & (hi > F32_TINY), F32_TINY, c_new)
            movable = inside(c_new, lo, hi) & ~((lo == 0.0) & (hi <= F32_TINY))
            missed = (still > 0.0) & (((st["kind"] == 1.0) & above) |
                                      ((st["kind"] == 2.0) & (g < kf)))
            lo = jnp.where(missed & (st["kind"] == 2.0), st["emin"], lo)
            closed = (~movable & both & (still > 0.0)) | missed
            tie = jnp.where(closed, 1.0, st["tie"])
            still = jnp.where(closed, 0.0, still)
            searching = jnp.where(movable, still, 0.0)

            from_hi = both & (kf - ghi == 1.0)
            from_lo = both & (glo - kf == 1.0) & ~from_hi
            ready = jnp.where(searching > 0.0, jnp.where(from_hi | from_lo, 1.0, 2.0), 0.0)
            code = jnp.max(ready)
            ext_ref[...] = jnp.zeros_like(ext_ref)
            finish_ref[0] = jnp.int32(0)

            @pl.when((code == 1.0) & (st["it"] + 1 >= SEARCH_FINISH_FROM))
            def _():
                def ext_body(r, carry):
                    r0 = pl.multiple_of(r * kt_rows, kt_rows)
                    tile = ibuf[pl.ds(r0, kt_rows), :]
                    below = fold(jnp.where(tile < hi, tile, -jnp.inf), jnp.max)
                    above_lo = fold(jnp.where(tile >= lo, tile, jnp.inf), jnp.min)
                    return jnp.maximum(carry[0], below), jnp.minimum(carry[1], above_lo)
                mx8, mn8 = lax.fori_loop(0, n_tiles, ext_body,
                                         (jnp.full((SUBLANES, qb), -jnp.inf, f32),
                                          jnp.full((SUBLANES, qb), jnp.inf, f32)))
                ext_ref[0:1, :] = jnp.max(mx8, axis=0, keepdims=True)
                ext_ref[1:2, :] = jnp.min(mn8, axis=0, keepdims=True)
                finish_ref[0] = jnp.int32(1)

            finishing = finish_ref[0] > 0
            emax = ext_ref[0:1, :]
            emin = ext_ref[1:2, :]
            ebits = lax.bitcast_convert_type(emin, jnp.int32)
            next_up = lax.bitcast_convert_type(ebits + jnp.where(emin > 0.0, 1, -1), f32)
            next_up = jnp.where(emin == 0.0, F32_TINY, next_up)
            kind = jnp.where(finishing & (searching > 0.0),
                             jnp.where(from_hi, 1.0, jnp.where(from_lo, 2.0, 0.0)), 0.0)
            c_new = jnp.where(kind == 1.0, emax, jnp.where(kind == 2.0, next_up, c_new))
            return dict(it=st["it"] + 1, active=(code > 0.0).astype(jnp.int32),
                        c=c_new, lo=lo, hi=hi, glo=glo, ghi=ghi, tlo=tlo, thi=thi,
                        boost=boost, side=side, thr=thr, open=still, tie=tie,
                        kind=kind, emin=emin)

        final = lax.while_loop(search_cond, search_body, state0)
        tied = final["tie"] > 0.0
        thr_ref[...] = jnp.where(tied, final["lo"], final["thr"])
        pending_ref[0] = (jnp.max(final["open"]) > 0.0).astype(jnp.int32)

        @pl.when(jnp.max(final["tie"]) > 0.0)
        def _():
            resolve_ties(jnp.where(tied, final["lo"], jnp.nan), kf - final["ghi"],
                         final["glo"] - final["ghi"])

    @pl.when(pending_ref[0] > 0)
    def _():
        def key_to_f32(key):
            bits = jnp.where(key < 0, key ^ jnp.int32(0x7FFFFFFF), key)
            return lax.bitcast_convert_type(bits, f32)

        def bit_body(t, prefix):
            step = lax.shift_left(jnp.int32(1), jnp.int32(31) - t)
            cand = prefix + step
            cand_f = key_to_f32(cand)
            cnt = count(lambda tile, rows: tile >= cand_f)
            return jnp.where(cnt >= kf, cand, prefix)

        prefix = lax.fori_loop(0, 32, bit_body, jnp.full((1, qb), INT32_MIN, jnp.int32))
        thr = jnp.where(select_all, F32_LOWEST, key_to_f32(prefix))
        thr_ref[...] = thr

        cnt_ge = count(lambda tile, rows: tile >= thr)
        excess = jnp.where(select_all, 0.0, cnt_ge - kf)

        @pl.when(jnp.max(excess) > 0.0)
        def _():
            cnt_gt = count(lambda tile, rows: tile > thr)
            resolve_ties(jnp.where(excess > 0.0, thr, jnp.nan), kf - cnt_gt, cnt_ge - cnt_gt)

    thr = thr_ref[...]
    ones_rows = jnp.ones((2 * SUBLANES, kt_rows), bf16)

    def att_body(kt, carry):
        r0 = pl.multiple_of(kt * kt_rows, kt_rows)
        bias_ref[...] = jnp.where(ibuf[pl.ds(r0, kt_rows), :] >= thr, 0.0, NEG_BIG)
        for h in range(N_ATTN_HEADS):
            k_pair = k_ref[pl.ds(r0, kt_rows), (h // 2) * LANES:(h // 2 + 1) * LANES]
            s_ref[h] = jnp.dot(k_pair, qTm[h], preferred_element_type=f32) + bias_ref[...]
            m_old = m_ref[h]
            m_new = jnp.maximum(m_old, jnp.max(s_ref[h], axis=0, keepdims=True))
            alpha_ref[h] = jnp.exp2(m_old - m_new)
            m_ref[h] = m_new
        for h in range(N_ATTN_HEADS):
            p = jnp.exp2(s_ref[h] - m_ref[h]).astype(bf16)
            hs = slice(h * ATTN_HEAD_DIM, (h + 1) * ATTN_HEAD_DIM)
            lhs = jnp.concatenate([vT_ref[hs, pl.ds(r0, kt_rows)], ones_rows], axis=0)
            pv = jnp.dot(lhs, p, preferred_element_type=f32)
            alpha = alpha_ref[h]
            acc_ref[hs, :] = alpha * acc_ref[hs, :] + pv[:ATTN_HEAD_DIM]
            l_ref[h] = alpha * l_ref[h] + pv[ATTN_HEAD_DIM:ATTN_HEAD_DIM + 1]
        return carry

    lax.fori_loop(0, n_tiles, att_body, 0)

    for h in range(N_ATTN_HEADS):
        hs = slice(h * ATTN_HEAD_DIM, (h + 1) * ATTN_HEAD_DIM)
        acc_ref[hs, :] = acc_ref[hs, :] / l_ref[h]
    o_ref[...] = acc_ref[...].T.astype(o_ref.dtype)


def _search_hints(seq_len):
    n = (np.arange(seq_len) // CHUNK + 1) * CHUNK
    frac = np.minimum(TOPK_MAX / n, 0.5)
    nd = statistics.NormalDist()
    z_of = {f: nd.inv_cdf(1.0 - f) for f in np.unique(frac)}
    z = np.array([z_of[f] for f in frac])
    dens = n * np.exp(-0.5 * z * z) / np.sqrt(2.0 * np.pi)
    return jnp.asarray(np.stack([z, 1.0 / dens]), f32)


def _dsa(qT, iqT, iwT, z3, vT, ik3):
    b, _, s = qT.shape
    qb = DSA_BLOCK
    return pl.pallas_call(
        functools.partial(_dsa_kernel, seq_len=s),
        grid=(b, s // qb),
        in_specs=[
            pl.BlockSpec((None, ATTN_W, qb), lambda bi, i: (bi, 0, i)),
            pl.BlockSpec((None, IDX_Q_W, qb), lambda bi, i: (bi, 0, i)),
            pl.BlockSpec((None, N_IDX_HEADS, qb), lambda bi, i: (bi, 0, i)),
            pl.BlockSpec((2, qb), lambda bi, i: (0, i)),
            pl.BlockSpec((None, s, COL_TILE), lambda bi, i: (bi, 0, Z_AK)),
            pl.BlockSpec((None, ATTN_W, s), lambda bi, i: (bi, 0, 0)),
            pl.BlockSpec((None, s, IDX_HEAD_DIM), lambda bi, i: (bi, 0, 0)),
        ],
        out_specs=pl.BlockSpec((None, qb, ATTN_W), lambda bi, i: (bi, i, 0)),
        out_shape=jax.ShapeDtypeStruct((b, s, ATTN_W), bf16),
        scratch_shapes=[
            pltpu.VMEM((s, qb), f32),
            pltpu.VMEM((N_ATTN_HEADS, LANES, qb), bf16),
            pltpu.VMEM((N_ATTN_HEADS, qb, qb), f32),
            pltpu.VMEM((qb, qb), f32),
            pltpu.VMEM((N_ATTN_HEADS, 1, qb), f32),
            pltpu.VMEM((N_ATTN_HEADS, 1, qb), f32),
            pltpu.VMEM((N_ATTN_HEADS, 1, qb), f32),
            pltpu.VMEM((ATTN_W, qb), f32),
            pltpu.VMEM((1, qb), f32),
            pltpu.VMEM((2, qb), f32),
            pltpu.SMEM((1,), jnp.int32),
            pltpu.SMEM((1,), jnp.int32),
        ],
        compiler_params=_cparams(("parallel", "arbitrary")),
        name="dsa",
    )(qT, iqT, iwT, _search_hints(s), z3, vT, ik3)


def _gla_kernel(qk_ref, v_ref, r_ref, sm_ref, w2_ref, gb_ref, gn_ref, o_ref, state_ref):
    c = CHUNK
    n_batch = qk_ref.shape[0]

    @pl.when(pl.program_id(0) == 0)
    def _():
        state_ref[...] = jnp.zeros_like(state_ref)

    row = lax.broadcasted_iota(jnp.int32, (c, c), 0)
    col = lax.broadcasted_iota(jnp.int32, (c, c), 1)
    causal = row >= col
    tril = jnp.where(causal, 1.0, 0.0)
    lane = lax.broadcasted_iota(jnp.int32, (c, LANES), 1)
    half = (lane < GLA_HEAD_K, lane >= GLA_HEAD_K)
    contract_last = (((1,), (1,)), ((), ()))
    contract_first = (((0,), (0,)), ((), ()))

    for ci, bi in [(ci, bi) for ci in range(GLA_TILE // c) for bi in range(n_batch)]:
        rows = slice(ci * c, (ci + 1) * c)
        qk = qk_ref[bi, rows, :].astype(f32)
        q = qk[:, :GLA_K_W]
        k = qk[:, GLA_K_W:]
        ga = sm_ref[bi, rows, :][:, SM_GA0:SM_GA0 + GLA_GATE_RANK]
        glin = jnp.dot(ga, w2_ref[...], preferred_element_type=f32,
                       precision=lax.Precision.HIGHEST) + gb_ref[...]
        logg = (jnp.minimum(glin, 0.0) - jnp.log(1.0 + jnp.exp(-jnp.abs(glin)))) / GLA_GATE_TAU
        bcum = jnp.dot(tril, logg, preferred_element_type=f32, precision=lax.Precision.HIGHEST)
        b_last = bcum[c - 1:c, :]
        b_mid = bcum[c // 2 - 1:c // 2, :]
        q_in = q * jnp.exp(bcum)
        q_mid = q * jnp.exp(bcum - b_mid)
        k_mid = k * jnp.exp(b_mid - bcum)
        k_out = k * jnp.exp(b_last - bcum)
        decay = jnp.exp(b_last)

        for pr in range(N_GLA_HEADS // 2):
            ps = slice(pr * LANES, (pr + 1) * LANES)
            st = state_ref[bi, pr]
            st_b = st.astype(bf16)
            k_mid_p = k_mid[:, ps].astype(bf16)
            upd = jnp.zeros_like(st)
            for hh in range(2):
                h = pr * 2 + hh
                vs = slice(h * GLA_HEAD_V, (h + 1) * GLA_HEAD_V)
                v_h = v_ref[bi, rows, vs]
                q_mid_m = jnp.where(half[hh], q_mid[:, ps], 0.0).astype(bf16)
                q_in_m = jnp.where(half[hh], q_in[:, ps], 0.0).astype(bf16)
                k_out_m = jnp.where(half[hh], k_out[:, ps], 0.0).astype(bf16)
                attn = lax.dot_general(q_mid_m, k_mid_p, contract_last,
                                       preferred_element_type=f32)
                attn = jnp.where(causal, attn, 0.0).astype(bf16)
                o = lax.dot_general(q_in_m, st_b, contract_last, preferred_element_type=f32)
                o = o + jnp.dot(attn, v_h, preferred_element_type=f32)
                upd = upd + lax.dot_general(v_h, k_out_m, contract_first,
                                            preferred_element_type=f32)
                y = _rms(o, gn_ref[...])
                r = r_ref[bi, rows, vs].astype(f32)
                o_ref[bi, rows, vs] = (y * r * _sigmoid(r)).astype(o_ref.dtype)
            state_ref[bi, pr] = decay[:, ps] * st + upd


def _gla(z3, sm3, w2, gb, gn):
    b, s, _ = z3.shape
    tb = GLA_TILE
    return pl.pallas_call(
        _gla_kernel,
        grid=(s // tb,),
        in_specs=[
            pl.BlockSpec((b, tb, COL_TILE), lambda j: (0, j, Z_GQK)),
            pl.BlockSpec((b, tb, COL_TILE), lambda j: (0, j, Z_GV)),
            pl.BlockSpec((b, tb, COL_TILE), lambda j: (0, j, Z_GR)),
            pl.BlockSpec((b, tb, LANES), lambda j: (0, j, 0)),
            pl.BlockSpec((GLA_GATE_RANK, GLA_K_W), lambda j: (0, 0)),
            pl.BlockSpec((1, GLA_K_W), lambda j: (0, 0)),
            pl.BlockSpec((1, GLA_HEAD_V), lambda j: (0, 0)),
        ],
        out_specs=pl.BlockSpec((b, tb, GLA_V_W), lambda j: (0, j, 0)),
        out_shape=jax.ShapeDtypeStruct((b, s, GLA_V_W), bf16),
        scratch_shapes=[pltpu.VMEM((b, N_GLA_HEADS // 2, GLA_HEAD_V, LANES), f32)],
        compiler_params=_cparams(("arbitrary",)),
        name="gla",
    )(z3, z3, z3, sm3, w2, gb, gn)


def _merge_kernel(x_ref, oa_ref, ob_ref, ga_ref, gb_ref, wa_ref, wb_ref, wo_ref, o_ref):
    ya = jnp.dot(oa_ref[...], wa_ref[...], preferred_element_type=f32)
    yb = jnp.dot(ob_ref[...], wb_ref[...], preferred_element_type=f32)
    mg = _sigmoid(ga_ref[...].astype(f32)) * ya + _sigmoid(gb_ref[...].astype(f32)) * yb
    o_ref[...] = x_ref[...] + jnp.dot(mg.astype(bf16), wo_ref[...], preferred_element_type=f32)


def _merge(x, oa, ob, z, wa, wb, wo):
    t = x.shape[0]
    tm = TOKEN_TILE
    return pl.pallas_call(
        _merge_kernel,
        grid=(t // tm,),
        in_specs=[
            pl.BlockSpec((tm, D_MODEL), lambda i: (i, 0)),
            pl.BlockSpec((tm, ATTN_W), lambda i: (i, 0)),
            pl.BlockSpec((tm, GLA_V_W), lambda i: (i, 0)),
            pl.BlockSpec((tm, D_MODEL), lambda i: (i, Z_MGA)),
            pl.BlockSpec((tm, D_MODEL), lambda i: (i, Z_MGB)),
            pl.BlockSpec((ATTN_W, D_MODEL), lambda i: (0, 0)),
            pl.BlockSpec((GLA_V_W, D_MODEL), lambda i: (0, 0)),
            pl.BlockSpec((D_MODEL, D_MODEL), lambda i: (0, 0)),
        ],
        out_specs=pl.BlockSpec((tm, D_MODEL), lambda i: (i, 0)),
        out_shape=jax.ShapeDtypeStruct((t, D_MODEL), f32),
        compiler_params=_cparams(("parallel",)),
        name="merge",
    )(x, oa, ob, z, z, wa, wb, wo)


def _ple_kernel(x_ref, p_ref, g_ref, wg_ref, wp_ref, gf_ref, o_ref, *, final):
    x = x_ref[...]
    h = _rms(x, g_ref[...]).astype(bf16)
    gate = _sigmoid(jnp.dot(h, wg_ref[...], preferred_element_type=f32))
    e = jnp.dot(p_ref[...].astype(bf16), wp_ref[...], preferred_element_type=f32)
    y = x + gate * e
    if final:
        y = _rms(y, gf_ref[...])
    o_ref[...] = y


def _ple(x, p, g, wg, wp, gf, final):
    t = x.shape[0]
    tm = TOKEN_TILE
    return pl.pallas_call(
        functools.partial(_ple_kernel, final=final),
        grid=(t // tm,),
        in_specs=[
            pl.BlockSpec((tm, D_MODEL), lambda i: (i, 0)),
            pl.BlockSpec((tm, PLE_DIM), lambda i: (i, 0)),
            pl.BlockSpec((1, D_MODEL), lambda i: (0, 0)),
            pl.BlockSpec((D_MODEL, D_MODEL), lambda i: (0, 0)),
            pl.BlockSpec((PLE_DIM, D_MODEL), lambda i: (0, 0)),
            pl.BlockSpec((1, D_MODEL), lambda i: (0, 0)),
        ],
        out_specs=pl.BlockSpec((tm, D_MODEL), lambda i: (i, 0)),
        out_shape=jax.ShapeDtypeStruct((t, D_MODEL), f32),
        compiler_params=_cparams(("parallel",)),
        name="ple",
    )(x, p, g, wg, wp, gf)


def _split_w_in(w_in):
    cuts = np.cumsum(np.array(SPLIT_SIZES))[:-1].tolist()
    aq, ak, av, iq, ik, iw, gq, gk, gv, gr, ga, mga, mgb = jnp.split(w_in, cuts, axis=-1)
    att_scale = ATTN_HEAD_DIM ** -0.5 * float(np.log2(np.e))
    idx_scale = IDX_HEAD_DIM ** -0.5 * N_IDX_HEADS ** -0.5
    main = jnp.concatenate([mga, mgb, ak, gq * (GLA_HEAD_K ** -0.5), gk, gv, gr], axis=-1)
    feature_major = jnp.concatenate([aq * att_scale, av, iq], axis=-1)
    zeros = lambda n: jnp.zeros(w_in.shape[:-1] + (n,), w_in.dtype)
    small = jnp.concatenate(
        [ik, zeros(LANES - IDX_HEAD_DIM), iw * idx_scale, ga,
         zeros(LANES - N_IDX_HEADS - GLA_GATE_RANK)], axis=-1)
    return main.astype(bf16), feature_major.astype(bf16), small.astype(bf16)


def kernel(x, p, w_in, gla_gate_w2, gla_gate_b, gla_norm, w_branch_a, w_branch_b, w_out,
           norm_ff1, norm_mix, norm_ff2, norm_ple, ff1_w_gate, ff1_w_up, ff1_w_down,
           ff2_w_gate, ff2_w_up, ff2_w_down, ple_w_proj, ple_w_gate, norm_final):
    b, s, d = x.shape
    t = b * s
    depth = w_in.shape[0]
    w_main, w_fm, w_small = _split_w_in(w_in)
    cast = lambda w: w.astype(bf16)
    ff1 = (cast(ff1_w_gate), cast(ff1_w_up), cast(ff1_w_down))
    ff2 = (cast(ff2_w_gate), cast(ff2_w_up), cast(ff2_w_down))
    wa, wb, wo = cast(w_branch_a), cast(w_branch_b), cast(w_out)
    wpg, wpp = cast(ple_w_gate), cast(ple_w_proj)
    row = lambda v: v.reshape(1, -1)

    xf = x.reshape(t, d)
    for l in range(depth):
        xf = _ffn(xf, row(norm_ff1[l]), ff1[0][l], ff1[1][l], ff1[2][l])

        z, qT, vT, iqT, ik, sm, iwT = _proj(xf, row(norm_mix[l]), w_main[l], w_fm[l],
                                            w_small[l], b)
        z3 = z.reshape(b, s, Z_WIDTH)
        sm3 = sm.reshape(b, s, LANES)
        oa = _dsa(qT, iqT, iwT, z3, vT, ik.reshape(b, s, IDX_HEAD_DIM))
        ob = _gla(z3, sm3, gla_gate_w2[l], row(gla_gate_b[l]), row(gla_norm[l]))
        xf = _merge(xf, oa.reshape(t, ATTN_W), ob.reshape(t, GLA_V_W), z, wa[l], wb[l], wo[l])

        xf = _ffn(xf, row(norm_ff2[l]), ff2[0][l], ff2[1][l], ff2[2][l])
        xf = _ple(xf, p[l].reshape(t, PLE_DIM), row(norm_ple[l]), wpg[l], wpp[l],
                  row(norm_final), final=(l == depth - 1))
    return xf.reshape(b, s, d)
```

```python
import functools
import statistics

import jax
import jax.numpy as jnp
import numpy as np
from jax import lax
from jax.experimental import pallas as pl
from jax.experimental.pallas import tpu as pltpu

D_MODEL = 1024
D_FF = 2816
PLE_DIM = 256
DEPTH = 2
EPS = 1e-6

CHUNK = 64
N_ATTN_HEADS = 8
ATTN_HEAD_DIM = 64
N_IDX_HEADS = 8
IDX_HEAD_DIM = 64
TOPK_MAX = 256
N_GLA_HEADS = 4
GLA_HEAD_K = 64
GLA_HEAD_V = 128
GLA_GATE_RANK = 16
GLA_GATE_TAU = 16.0

ATTN_W = N_ATTN_HEADS * ATTN_HEAD_DIM
IDX_Q_W = N_IDX_HEADS * IDX_HEAD_DIM
GLA_K_W = N_GLA_HEADS * GLA_HEAD_K
GLA_V_W = N_GLA_HEADS * GLA_HEAD_V
SPLIT_SIZES = (ATTN_W, ATTN_W, ATTN_W, IDX_Q_W, IDX_HEAD_DIM, N_IDX_HEADS,
               GLA_K_W, GLA_K_W, GLA_V_W, GLA_V_W, GLA_GATE_RANK, D_MODEL, D_MODEL)

LANES = 128
SUBLANES = 8
VMEM_LIMIT_BYTES = 56 * 1024 * 1024

COL_TILE = 512
Z_MGA, Z_MGB = 0, 1
Z_AK, Z_GQK, Z_GV, Z_GR = 4, 5, 6, 7
Z_TILES = 8
Z_WIDTH = Z_TILES * COL_TILE
ZT_TILES = 3
SM_IW0 = 0
SM_GA0 = N_IDX_HEADS

TOKEN_TILE = 512
FF_TILE = 1408
DSA_BLOCK = 256
GLA_TILE = 256
SEARCH_MAX_ITERS = 24
SEARCH_FINISH_FROM = 4

NEG_BIG = -1e30
F32_LOWEST = float(np.finfo(np.float32).min)
F32_TINY = float(np.finfo(np.float32).tiny)
INT32_MIN = int(np.iinfo(np.int32).min)

f32 = jnp.float32
bf16 = jnp.bfloat16


def _rms(x, g):
    return x * lax.rsqrt(jnp.mean(x * x, axis=-1, keepdims=True) + EPS) * g


def _sigmoid(x):
    return 1.0 / (1.0 + jnp.exp(-x))


def _cparams(sem):
    return pltpu.CompilerParams(dimension_semantics=sem, vmem_limit_bytes=VMEM_LIMIT_BYTES)


def _ffn_kernel(x_ref, g_ref, wg_ref, wu_ref, wd_ref, o_ref, h_ref, acc_ref):
    f = pl.program_id(1)

    @pl.when(f == 0)
    def _():
        h_ref[...] = _rms(x_ref[...], g_ref[...]).astype(bf16)
        acc_ref[...] = jnp.zeros_like(acc_ref)

    h = h_ref[...]
    gate = jnp.dot(h, wg_ref[...], preferred_element_type=f32)
    up = jnp.dot(h, wu_ref[...], preferred_element_type=f32)
    a = (gate * _sigmoid(gate) * up).astype(bf16)
    acc_ref[...] += jnp.dot(a, wd_ref[...], preferred_element_type=f32)

    @pl.when(f == pl.num_programs(1) - 1)
    def _():
        o_ref[...] = x_ref[...] + 0.5 * acc_ref[...]


def _ffn(x, g, wg, wu, wd):
    t = x.shape[0]
    return pl.pallas_call(
        _ffn_kernel,
        grid=(t // TOKEN_TILE, D_FF // FF_TILE),
        in_specs=[
            pl.BlockSpec((TOKEN_TILE, D_MODEL), lambda i, f: (i, 0)),
            pl.BlockSpec((1, D_MODEL), lambda i, f: (0, 0)),
            pl.BlockSpec((D_MODEL, FF_TILE), lambda i, f: (0, f)),
            pl.BlockSpec((D_MODEL, FF_TILE), lambda i, f: (0, f)),
            pl.BlockSpec((FF_TILE, D_MODEL), lambda i, f: (f, 0)),
        ],
        out_specs=pl.BlockSpec((TOKEN_TILE, D_MODEL), lambda i, f: (i, 0)),
        out_shape=jax.ShapeDtypeStruct((t, D_MODEL), f32),
        scratch_shapes=[pltpu.VMEM((TOKEN_TILE, D_MODEL), bf16),
                        pltpu.VMEM((TOKEN_TILE, D_MODEL), f32)],
        compiler_params=_cparams(("parallel", "arbitrary")),
        name="ffn",
    )(x, g, wg, wu, wd)


def _proj_kernel(x_ref, g_ref, w_ref, wt_ref, ws_ref,
                 z_ref, qT_ref, vT_ref, iqT_ref, ik_ref, sm_ref, iwT_ref):
    h = _rms(x_ref[...], g_ref[...]).astype(bf16)
    for j in range(Z_TILES):
        cols = slice(j * COL_TILE, (j + 1) * COL_TILE)
        z_ref[:, cols] = jnp.dot(h, w_ref[:, cols], preferred_element_type=f32).astype(bf16)
    for j, out_ref in enumerate((qT_ref, vT_ref, iqT_ref)):
        cols = slice(j * COL_TILE, (j + 1) * COL_TILE)
        zt = jnp.dot(h, wt_ref[:, cols], preferred_element_type=f32)
        out_ref[...] = zt.T.astype(bf16)
    s = jnp.dot(h, ws_ref[...], preferred_element_type=f32)
    ik_ref[...] = s[:, :IDX_HEAD_DIM].astype(bf16)
    sm = s[:, LANES:]
    sm_ref[...] = sm
    iwT_ref[...] = sm.T[SM_IW0:SM_IW0 + N_IDX_HEADS, :]


def _proj(x, g, w_main, w_t, w_small, batch):
    t = x.shape[0]
    tm = TOKEN_TILE
    s = t // batch
    per_b = s // tm
    fm = lambda rows: pl.BlockSpec((None, rows, tm), lambda i: (i // per_b, 0, i % per_b))
    return pl.pallas_call(
        _proj_kernel,
        grid=(t // tm,),
        in_specs=[
            pl.BlockSpec((tm, D_MODEL), lambda i: (i, 0)),
            pl.BlockSpec((1, D_MODEL), lambda i: (0, 0)),
            pl.BlockSpec((D_MODEL, Z_WIDTH), lambda i: (0, 0)),
            pl.BlockSpec((D_MODEL, ZT_TILES * COL_TILE), lambda i: (0, 0)),
            pl.BlockSpec((D_MODEL, 2 * LANES), lambda i: (0, 0)),
        ],
        out_specs=[
            pl.BlockSpec((tm, Z_WIDTH), lambda i: (i, 0)),
            fm(ATTN_W), fm(ATTN_W), fm(IDX_Q_W),
            pl.BlockSpec((tm, IDX_HEAD_DIM), lambda i: (i, 0)),
            pl.BlockSpec((tm, LANES), lambda i: (i, 0)),
            fm(N_IDX_HEADS),
        ],
        out_shape=[
            jax.ShapeDtypeStruct((t, Z_WIDTH), bf16),
            jax.ShapeDtypeStruct((batch, ATTN_W, s), bf16),
            jax.ShapeDtypeStruct((batch, ATTN_W, s), bf16),
            jax.ShapeDtypeStruct((batch, IDX_Q_W, s), bf16),
            jax.ShapeDtypeStruct((t, IDX_HEAD_DIM), bf16),
            jax.ShapeDtypeStruct((t, LANES), f32),
            jax.ShapeDtypeStruct((batch, N_IDX_HEADS, s), f32),
        ],
        compiler_params=_cparams(("parallel",)),
        name="proj",
    )(x, g, w_main, w_t, w_small)


def _dsa_kernel(qT_ref, iqT_ref, iwT_ref, tab_ref, k_ref, vT_ref, ik_ref, o_ref,
                ibuf, qTm, s_ref, bias_ref, m_ref, l_ref, alpha_ref, acc_ref, thr_ref,
                ext_ref, pending_ref, finish_ref, *, seq_len):
    qb = DSA_BLOCK
    kt_rows = DSA_BLOCK
    i = pl.program_id(1)
    n_tiles = i + 1

    rowid = lax.broadcasted_iota(jnp.int32, (LANES, qb), 0)
    for h in range(N_ATTN_HEADS):
        pair = qT_ref[(h // 2) * LANES:(h // 2 + 1) * LANES, :]
        keep = (rowid < ATTN_HEAD_DIM) if h % 2 == 0 else (rowid >= ATTN_HEAD_DIM)
        qTm[h] = jnp.where(keep, pair, jnp.zeros_like(pair))
    l_ref[...] = jnp.zeros_like(l_ref)
    acc_ref[...] = jnp.zeros_like(acc_ref)

    def fold(x, op):
        return op(x.reshape(kt_rows // SUBLANES, SUBLANES, qb), axis=0)

    def idx_tile(kt, diagonal, stats):
        r0 = pl.multiple_of(kt * kt_rows, kt_rows)
        ki_t = ik_ref[pl.ds(r0, kt_rows), :]
        sc = jnp.zeros((kt_rows, qb), f32)
        for h in range(N_IDX_HEADS):
            s = jnp.dot(ki_t, iqT_ref[h * IDX_HEAD_DIM:(h + 1) * IDX_HEAD_DIM, :],
                        preferred_element_type=f32)
            sc = sc + iwT_ref[h:h + 1, :] * jnp.maximum(s, 0.0)
        sc_hi = sc_lo = sc_0 = sc
        if diagonal:
            kc = lax.broadcasted_iota(jnp.int32, (kt_rows, qb), 0) // CHUNK
            qc = lax.broadcasted_iota(jnp.int32, (kt_rows, qb), 1) // CHUNK
            adm = kc <= qc
            sc_hi = jnp.where(adm, sc, -jnp.inf)
            sc_lo = jnp.where(adm, sc, jnp.inf)
            sc_0 = jnp.where(adm, sc, 0.0)
        ibuf[pl.ds(r0, kt_rows), :] = sc_hi
        mx8, mn8, s8, ss8 = stats
        return (jnp.maximum(mx8, fold(sc_hi, jnp.max)), jnp.minimum(mn8, fold(sc_lo, jnp.min)),
                s8 + fold(sc_0, jnp.sum), ss8 + fold(sc_0 * sc_0, jnp.sum))

    stats0 = (jnp.full((SUBLANES, qb), -jnp.inf, f32), jnp.full((SUBLANES, qb), jnp.inf, f32),
              jnp.zeros((SUBLANES, qb), f32), jnp.zeros((SUBLANES, qb), f32))
    stats = lax.fori_loop(0, i, lambda kt, st: idx_tile(kt, False, st), stats0)
    mx8, mn8, s8, ss8 = idx_tile(i, True, stats)
    col_max = jnp.max(mx8, axis=0, keepdims=True)
    col_min = jnp.min(mn8, axis=0, keepdims=True)
    col_sum = jnp.sum(s8, axis=0, keepdims=True)
    col_ssq = jnp.sum(ss8, axis=0, keepdims=True)

    def count(pred):
        def body(r, acc):
            r0 = pl.multiple_of(r * kt_rows, kt_rows)
            tile = ibuf[pl.ds(r0, kt_rows), :]
            rows = r0 + lax.broadcasted_iota(jnp.int32, (kt_rows, qb), 0)
            ind = jnp.where(pred(tile, rows), 1.0, 0.0)
            part = ind.reshape(kt_rows // SUBLANES, SUBLANES, qb)
            while part.shape[0] > 1:
                half = part.shape[0] // 2
                part = part[:half] + part[half:]
            return acc + part[0]
        acc = lax.fori_loop(0, n_tiles, body, jnp.zeros((SUBLANES, qb), f32))
        return jnp.sum(acc, axis=0, keepdims=True)

    def resolve_ties(tv, need, mult):
        is_tie = tv == tv
        n_keys = n_tiles * kt_rows

        def tie_cond(st):
            return st["active"] > 0

        def tie_body(st):
            plo, phi, flo, fhi = st["plo"], st["phi"], st["flo"], st["fhi"]
            span = phi - plo
            est = ((need - flo) / jnp.maximum(fhi - flo, 1.0) * span.astype(f32)).astype(jnp.int32)
            pick = jnp.where(st["bisect"] > 0, span // 2, est)
            cand = plo + jnp.clip(pick, 1, jnp.maximum(span - 1, 1))
            f = count(lambda tile, rows: jnp.where(tile == tv, rows, seq_len) < cand)
            lower = f < need
            exact = f == need
            plo = jnp.where(lower, cand, jnp.where(exact, cand - 1, plo))
            phi = jnp.where(lower, phi, cand)
            flo = jnp.where(lower, f, flo)
            fhi = jnp.where(lower, fhi, f)
            unsplit = jnp.where(is_tie & (phi - plo > 1), 1, 0)
            return dict(active=jnp.max(unsplit), bisect=1 - st["bisect"],
                        plo=plo, phi=phi, flo=flo, fhi=fhi)

        split = lax.while_loop(tie_cond, tie_body, dict(
            active=jnp.int32(1), bisect=jnp.int32(0),
            plo=jnp.zeros((1, qb), jnp.int32), phi=jnp.zeros((1, qb), jnp.int32) + n_keys,
            flo=jnp.zeros((1, qb), f32), fhi=mult))["phi"]

        def drop_body(r, carry):
            r0 = pl.multiple_of(r * kt_rows, kt_rows)
            tile = ibuf[pl.ds(r0, kt_rows), :]
            rows = r0 + lax.broadcasted_iota(jnp.int32, (kt_rows, qb), 0)
            dropped = jnp.where(tile == tv, rows, -1) >= split
            ibuf[pl.ds(r0, kt_rows), :] = jnp.where(dropped, -jnp.inf, tile)
            return carry

        lax.fori_loop(0, n_tiles, drop_body, 0)

    qpos = i * qb + lax.broadcasted_iota(jnp.int32, (1, qb), 1)
    n_adm = (qpos // CHUNK + 1) * CHUNK
    select_all = n_adm <= TOPK_MAX
    kf = float(TOPK_MAX)

    thr_ref[...] = jnp.full((1, qb), F32_LOWEST, f32)
    pending_ref[0] = jnp.int32(0)

    @pl.when(n_tiles * kt_rows > TOPK_MAX)
    def _():
        n_f = n_adm.astype(f32)
        mean = col_sum / n_f
        sigma = jnp.sqrt(jnp.maximum(col_ssq / n_f - mean * mean, 0.0))
        spacing = tab_ref[1:2, :] * sigma

        def inside(c, lo, hi):
            return (c > lo) & (c < hi)

        c0 = mean + tab_ref[0:1, :] * sigma
        c0 = jnp.where(inside(c0, col_min, col_max), c0, 0.5 * col_min + 0.5 * col_max)
        zero = jnp.zeros((1, qb), f32)
        one = jnp.ones((1, qb), f32)
        state0 = dict(it=jnp.int32(0), active=jnp.int32(1), c=c0, lo=col_min, hi=col_max,
                      glo=n_f, ghi=zero, tlo=zero, thi=zero, boost=one, side=zero,
                      thr=jnp.full((1, qb), F32_LOWEST, f32),
                      open=jnp.where(select_all, 0.0, 1.0), tie=zero, kind=zero, emin=zero)

        def search_cond(st):
            return (st["it"] < SEARCH_MAX_ITERS) & (st["active"] > 0)

        def search_body(st):
            c, lo, hi = st["c"], st["lo"], st["hi"]
            g = count(lambda tile, rows: tile >= c)
            hit = (g == kf) & (st["open"] > 0.0)
            thr = jnp.where(hit, c, st["thr"])
            still = jnp.where(hit, 0.0, st["open"])
            above = g > kf
            lo = jnp.where(above, c, lo)
            hi = jnp.where(above, hi, c)
            glo = jnp.where(above, g, st["glo"])
            ghi = jnp.where(above, st["ghi"], g)
            tlo = jnp.where(above, 1.0, st["tlo"])
            thi = jnp.where(above, st["thi"], 1.0)
            side = jnp.where(above, 1.0, -1.0)
            repeat = side == st["side"]
            boost = jnp.where(repeat, 2.0 * st["boost"], 1.0)
            mid = 0.5 * lo + 0.5 * hi
            stale = 1.0 / jnp.minimum(boost, 256.0)
            w_lo = (glo - kf - 0.5) * jnp.where(above, 1.0, stale)
            w_hi = (kf + 0.5 - ghi) * jnp.where(above, stale, 1.0)
            c_two = lo + w_lo / (w_lo + w_hi) * (hi - lo)
            step = 2.0 * boost * spacing
            c_one = jnp.where(tlo > 0.0, lo + jnp.maximum(glo - kf, 1.0) * step,
                              hi - jnp.maximum(kf - ghi, 1.0) * step)
            c_one = jnp.where(inside(c_one, lo, hi), c_one, mid)
            both = (tlo > 0.0) & (thi > 0.0)
            c_new = jnp.where(both, c_two, c_one)
            c_new = jnp.where(both & (lo < 0.0) & (hi > 0.0), 0.0, c_new)
            c_new = jnp.where((lo == 0.0) & (hi > F32_TINY), F32_TINY, c_new)
            movable = inside(c_new, lo, hi) & ~((lo == 0.0) & (hi <= F32_TINY))
            missed = (still > 0.0) & (((st["kind"] == 1.0) & above) |
                                      ((st["kind"] == 2.0) & (g < kf)))
            lo = jnp.where(missed & (st["kind"] == 2.0), st["emin"], lo)
            closed = (~movable & both & (still > 0.0)) | missed
            tie = jnp.where(closed, 1.0, st["tie"])
            still = jnp.where(closed, 0.0, still)
            searching = jnp.where(movable, still, 0.0)

            from_hi = both & (kf - ghi == 1.0)
            from_lo = both & (glo - kf == 1.0) & ~from_hi
            ready = jnp.where(searching > 0.0, jnp.where(from_hi | from_lo, 1.0, 2.0), 0.0)
            code = jnp.max(ready)
            ext_ref[...] = jnp.zeros_like(ext_ref)
            finish_ref[0] = jnp.int32(0)

            @pl.when((code == 1.0) & (st["it"] + 1 >= SEARCH_FINISH_FROM))
            def _():
                def ext_body(r, carry):
                    r0 = pl.multiple_of(r * kt_rows, kt_rows)
                    tile = ibuf[pl.ds(r0, kt_rows), :]
                    below = fold(jnp.where(tile < hi, tile, -jnp.inf), jnp.max)
                    above_lo = fold(jnp.where(tile >= lo, tile, jnp.inf), jnp.min)
                    return jnp.maximum(carry[0], below), jnp.minimum(carry[1], above_lo)
                mx8, mn8 = lax.fori_loop(0, n_tiles, ext_body,
                                         (jnp.full((SUBLANES, qb), -jnp.inf, f32),
                                          jnp.full((SUBLANES, qb), jnp.inf, f32)))
                ext_ref[0:1, :] = jnp.max(mx8, axis=0, keepdims=True)
                ext_ref[1:2, :] = jnp.min(mn8, axis=0, keepdims=True)
                finish_ref[0] = jnp.int32(1)

            finishing = finish_ref[0] > 0
            emax = ext_ref[0:1, :]
            emin = ext_ref[1:2, :]
            ebits = lax.bitcast_convert_type(emin, jnp.int32)
            next_up = lax.bitcast_convert_type(ebits + jnp.where(emin > 0.0, 1, -1), f32)
            next_up = jnp.where(emin == 0.0, F32_TINY, next_up)
            kind = jnp.where(finishing & (searching > 0.0),
                             jnp.where(from_hi, 1.0, jnp.where(from_lo, 2.0, 0.0)), 0.0)
            c_new = jnp.where(kind == 1.0, emax, jnp.where(kind == 2.0, next_up, c_new))
            return dict(it=st["it"] + 1, active=(code > 0.0).astype(jnp.int32),
                        c=c_new, lo=lo, hi=hi, glo=glo, ghi=ghi, tlo=tlo, thi=thi,
                        boost=boost, side=side, thr=thr, open=still, tie=tie,
                        kind=kind, emin=emin)

        final = lax.while_loop(search_cond, search_body, state0)
        tied = final["tie"] > 0.0
        thr_ref[...] = jnp.where(tied, final["lo"], final["thr"])
        pending_ref[0] = (jnp.max(final["open"]) > 0.0).astype(jnp.int32)

        @pl.when(jnp.max(final["tie"]) > 0.0)
        def _():
            resolve_ties(jnp.where(tied, final["lo"], jnp.nan), kf - final["ghi"],
                         final["glo"] - final["ghi"])

    @pl.when(pending_ref[0] > 0)
    def _():
        def key_to_f32(key):
            bits = jnp.where(key < 0, key ^ jnp.int32(0x7FFFFFFF), key)
            return lax.bitcast_convert_type(bits, f32)

        def bit_body(t, prefix):
            step = lax.shift_left(jnp.int32(1), jnp.int32(31) - t)
            cand = prefix + step
            cand_f = key_to_f32(cand)
            cnt = count(lambda tile, rows: tile >= cand_f)
            return jnp.where(cnt >= kf, cand, prefix)

        prefix = lax.fori_loop(0, 32, bit_body, jnp.full((1, qb), INT32_MIN, jnp.int32))
        thr = jnp.where(select_all, F32_LOWEST, key_to_f32(prefix))
        thr_ref[...] = thr

        cnt_ge = count(lambda tile, rows: tile >= thr)
        excess = jnp.where(select_all, 0.0, cnt_ge - kf)

        @pl.when(jnp.max(excess) > 0.0)
        def _():
            cnt_gt = count(lambda tile, rows: tile > thr)
            resolve_ties(jnp.where(excess > 0.0, thr, jnp.nan), kf - cnt_gt, cnt_ge - cnt_gt)

    thr = thr_ref[...]
    ones_rows = jnp.ones((2 * SUBLANES, kt_rows), bf16)

    m_ref[1] = jnp.full(m_ref.shape[1:], NEG_BIG, f32)

    def logits_stage(kt, slot):
        r0 = pl.multiple_of(kt * kt_rows, kt_rows)
        bias_ref[...] = jnp.where(ibuf[pl.ds(r0, kt_rows), :] >= thr, 0.0, NEG_BIG)
        for h in range(N_ATTN_HEADS):
            k_pair = k_ref[pl.ds(r0, kt_rows), (h // 2) * LANES:(h // 2 + 1) * LANES]
            s_ref[slot, h] = jnp.dot(k_pair, qTm[h], preferred_element_type=f32) + bias_ref[...]
            m_old = m_ref[1 - slot, h]
            m_new = jnp.maximum(m_old, jnp.max(s_ref[slot, h], axis=0, keepdims=True))
            alpha_ref[slot, h] = jnp.exp2(m_old - m_new)
            m_ref[slot, h] = m_new

    def values_stage(kt, slot):
        r0 = pl.multiple_of(kt * kt_rows, kt_rows)
        for h in range(N_ATTN_HEADS):
            p = jnp.exp2(s_ref[slot, h] - m_ref[slot, h]).astype(bf16)
            hs = slice(h * ATTN_HEAD_DIM, (h + 1) * ATTN_HEAD_DIM)
            lhs = jnp.concatenate([vT_ref[hs, pl.ds(r0, kt_rows)], ones_rows], axis=0)
            pv = jnp.dot(lhs, p, preferred_element_type=f32)
            alpha = alpha_ref[slot, h]
            acc_ref[hs, :] = alpha * acc_ref[hs, :] + pv[:ATTN_HEAD_DIM]
            l_ref[h] = alpha * l_ref[h] + pv[ATTN_HEAD_DIM:ATTN_HEAD_DIM + 1]

    def att_body(j, carry):
        kt = 2 * j
        logits_stage(kt + 1, 1)
        values_stage(kt, 0)
        logits_stage(kt + 2, 0)
        values_stage(kt + 1, 1)
        return carry

    logits_stage(0, 0)
    lax.fori_loop(0, i // 2, att_body, 0)

    @pl.when(i % 2 == 1)
    def _():
        logits_stage(i, 1)
        values_stage(i - 1, 0)
        values_stage(i, 1)

    @pl.when(i % 2 == 0)
    def _():
        values_stage(i, 0)

    for h in range(N_ATTN_HEADS):
        hs = slice(h * ATTN_HEAD_DIM, (h + 1) * ATTN_HEAD_DIM)
        acc_ref[hs, :] = acc_ref[hs, :] / l_ref[h]
    o_ref[...] = acc_ref[...].T.astype(o_ref.dtype)


def _search_hints(seq_len):
    n = (np.arange(seq_len) // CHUNK + 1) * CHUNK
    frac = np.minimum(TOPK_MAX / n, 0.5)
    nd = statistics.NormalDist()
    z_of = {f: nd.inv_cdf(1.0 - f) for f in np.unique(frac)}
    z = np.array([z_of[f] for f in frac])
    dens = n * np.exp(-0.5 * z * z) / np.sqrt(2.0 * np.pi)
    return jnp.asarray(np.stack([z, 1.0 / dens]), f32)


def _dsa(qT, iqT, iwT, z3, vT, ik3):
    b, _, s = qT.shape
    qb = DSA_BLOCK
    return pl.pallas_call(
        functools.partial(_dsa_kernel, seq_len=s),
        grid=(b, s // qb),
        in_specs=[
            pl.BlockSpec((None, ATTN_W, qb), lambda bi, i: (bi, 0, i)),
            pl.BlockSpec((None, IDX_Q_W, qb), lambda bi, i: (bi, 0, i)),
            pl.BlockSpec((None, N_IDX_HEADS, qb), lambda bi, i: (bi, 0, i)),
            pl.BlockSpec((2, qb), lambda bi, i: (0, i)),
            pl.BlockSpec((None, s, COL_TILE), lambda bi, i: (bi, 0, Z_AK)),
            pl.BlockSpec((None, ATTN_W, s), lambda bi, i: (bi, 0, 0)),
            pl.BlockSpec((None, s, IDX_HEAD_DIM), lambda bi, i: (bi, 0, 0)),
        ],
        out_specs=pl.BlockSpec((None, qb, ATTN_W), lambda bi, i: (bi, i, 0)),
        out_shape=jax.ShapeDtypeStruct((b, s, ATTN_W), bf16),
        scratch_shapes=[
            pltpu.VMEM((s, qb), f32),
            pltpu.VMEM((N_ATTN_HEADS, LANES, qb), bf16),
            pltpu.VMEM((2, N_ATTN_HEADS, qb, qb), f32),
            pltpu.VMEM((qb, qb), f32),
            pltpu.VMEM((2, N_ATTN_HEADS, 1, qb), f32),
            pltpu.VMEM((N_ATTN_HEADS, 1, qb), f32),
            pltpu.VMEM((2, N_ATTN_HEADS, 1, qb), f32),
            pltpu.VMEM((ATTN_W, qb), f32),
            pltpu.VMEM((1, qb), f32),
            pltpu.VMEM((2, qb), f32),
            pltpu.SMEM((1,), jnp.int32),
            pltpu.SMEM((1,), jnp.int32),
        ],
        compiler_params=_cparams(("parallel", "arbitrary")),
        name="dsa",
    )(qT, iqT, iwT, _search_hints(s), z3, vT, ik3)


def _gla_kernel(qk_ref, v_ref, r_ref, sm_ref, w2_ref, gb_ref, gn_ref, o_ref, state_ref):
    c = CHUNK
    n_batch = qk_ref.shape[0]

    @pl.when(pl.program_id(0) == 0)
    def _():
        state_ref[...] = jnp.zeros_like(state_ref)

    row = lax.broadcasted_iota(jnp.int32, (c, c), 0)
    col = lax.broadcasted_iota(jnp.int32, (c, c), 1)
    causal = row >= col
    tril = jnp.where(causal, 1.0, 0.0)
    lane = lax.broadcasted_iota(jnp.int32, (c, LANES), 1)
    half = (lane < GLA_HEAD_K, lane >= GLA_HEAD_K)
    contract_last = (((1,), (1,)), ((), ()))
    contract_first = (((0,), (0,)), ((), ()))

    for ci, bi in [(ci, bi) for ci in range(GLA_TILE // c) for bi in range(n_batch)]:
        rows = slice(ci * c, (ci + 1) * c)
        qk = qk_ref[bi, rows, :].astype(f32)
        q = qk[:, :GLA_K_W]
        k = qk[:, GLA_K_W:]
        ga = sm_ref[bi, rows, :][:, SM_GA0:SM_GA0 + GLA_GATE_RANK]
        glin = jnp.dot(ga, w2_ref[...], preferred_element_type=f32,
                       precision=lax.Precision.HIGHEST) + gb_ref[...]
        logg = (jnp.minimum(glin, 0.0) - jnp.log(1.0 + jnp.exp(-jnp.abs(glin)))) / GLA_GATE_TAU
        bcum = jnp.dot(tril, logg, preferred_element_type=f32, precision=lax.Precision.HIGHEST)
        b_last = bcum[c - 1:c, :]
        b_mid = bcum[c // 2 - 1:c // 2, :]
        q_in = q * jnp.exp(bcum)
        q_mid = q * jnp.exp(bcum - b_mid)
        k_mid = k * jnp.exp(b_mid - bcum)
        k_out = k * jnp.exp(b_last - bcum)
        decay = jnp.exp(b_last)

        for pr in range(N_GLA_HEADS // 2):
            ps = slice(pr * LANES, (pr + 1) * LANES)
            st = state_ref[bi, pr]
            st_b = st.astype(bf16)
            k_mid_p = k_mid[:, ps].astype(bf16)
            upd = jnp.zeros_like(st)
            for hh in range(2):
                h = pr * 2 + hh
                vs = slice(h * GLA_HEAD_V, (h + 1) * GLA_HEAD_V)
                v_h = v_ref[bi, rows, vs]
                q_mid_m = jnp.where(half[hh], q_mid[:, ps], 0.0).astype(bf16)
                q_in_m = jnp.where(half[hh], q_in[:, ps], 0.0).astype(bf16)
                k_out_m = jnp.where(half[hh], k_out[:, ps], 0.0).astype(bf16)
                attn = lax.dot_general(q_mid_m, k_mid_p, contract_last,
                                       preferred_element_type=f32)
                attn = jnp.where(causal, attn, 0.0).astype(bf16)
                o = lax.dot_general(q_in_m, st_b, contract_last, preferred_element_type=f32)
                o = o + jnp.dot(attn, v_h, preferred_element_type=f32)
                upd = upd + lax.dot_general(v_h, k_out_m, contract_first,
                                            preferred_element_type=f32)
                y = _rms(o, gn_ref[...])
                r = r_ref[bi, rows, vs].astype(f32)
                o_ref[bi, rows, vs] = (y * r * _sigmoid(r)).astype(o_ref.dtype)
            state_ref[bi, pr] = decay[:, ps] * st + upd


def _gla(z3, sm3, w2, gb, gn):
    b, s, _ = z3.shape
    tb = GLA_TILE
    return pl.pallas_call(
        _gla_kernel,
        grid=(s // tb,),
        in_specs=[
            pl.BlockSpec((b, tb, COL_TILE), lambda j: (0, j, Z_GQK)),
            pl.BlockSpec((b, tb, COL_TILE), lambda j: (0, j, Z_GV)),
            pl.BlockSpec((b, tb, COL_TILE), lambda j: (0, j, Z_GR)),
            pl.BlockSpec((b, tb, LANES), lambda j: (0, j, 0)),
            pl.BlockSpec((GLA_GATE_RANK, GLA_K_W), lambda j: (0, 0)),
            pl.BlockSpec((1, GLA_K_W), lambda j: (0, 0)),
            pl.BlockSpec((1, GLA_HEAD_V), lambda j: (0, 0)),
        ],
        out_specs=pl.BlockSpec((b, tb, GLA_V_W), lambda j: (0, j, 0)),
        out_shape=jax.ShapeDtypeStruct((b, s, GLA_V_W), bf16),
        scratch_shapes=[pltpu.VMEM((b, N_GLA_HEADS // 2, GLA_HEAD_V, LANES), f32)],
        compiler_params=_cparams(("arbitrary",)),
        name="gla",
    )(z3, z3, z3, sm3, w2, gb, gn)


def _merge_kernel(x_ref, oa_ref, ob_ref, ga_ref, gb_ref, wa_ref, wb_ref, wo_ref, o_ref):
    ya = jnp.dot(oa_ref[...], wa_ref[...], preferred_element_type=f32)
    yb = jnp.dot(ob_ref[...], wb_ref[...], preferred_element_type=f32)
    mg = _sigmoid(ga_ref[...].astype(f32)) * ya + _sigmoid(gb_ref[...].astype(f32)) * yb
    o_ref[...] = x_ref[...] + jnp.dot(mg.astype(bf16), wo_ref[...], preferred_element_type=f32)


def _merge(x, oa, ob, z, wa, wb, wo):
    t = x.shape[0]
    tm = TOKEN_TILE
    return pl.pallas_call(
        _merge_kernel,
        grid=(t // tm,),
        in_specs=[
            pl.BlockSpec((tm, D_MODEL), lambda i: (i, 0)),
            pl.BlockSpec((tm, ATTN_W), lambda i: (i, 0)),
            pl.BlockSpec((tm, GLA_V_W), lambda i: (i, 0)),
            pl.BlockSpec((tm, D_MODEL), lambda i: (i, Z_MGA)),
            pl.BlockSpec((tm, D_MODEL), lambda i: (i, Z_MGB)),
            pl.BlockSpec((ATTN_W, D_MODEL), lambda i: (0, 0)),
            pl.BlockSpec((GLA_V_W, D_MODEL), lambda i: (0, 0)),
            pl.BlockSpec((D_MODEL, D_MODEL), lambda i: (0, 0)),
        ],
        out_specs=pl.BlockSpec((tm, D_MODEL), lambda i: (i, 0)),
        out_shape=jax.ShapeDtypeStruct((t, D_MODEL), f32),
        compiler_params=_cparams(("parallel",)),
        name="merge",
    )(x, oa, ob, z, z, wa, wb, wo)


def _ple_kernel(x_ref, p_ref, g_ref, wg_ref, wp_ref, gf_ref, o_ref, *, final):
    x = x_ref[...]
    h = _rms(x, g_ref[...]).astype(bf16)
    gate = _sigmoid(jnp.dot(h, wg_ref[...], preferred_element_type=f32))
    e = jnp.dot(p_ref[...].astype(bf16), wp_ref[...], preferred_element_type=f32)
    y = x + gate * e
    if final:
        y = _rms(y, gf_ref[...])
    o_ref[...] = y


def _ple(x, p, g, wg, wp, gf, final):
    t = x.shape[0]
    tm = TOKEN_TILE
    return pl.pallas_call(
        functools.partial(_ple_kernel, final=final),
        grid=(t // tm,),
        in_specs=[
            pl.BlockSpec((tm, D_MODEL), lambda i: (i, 0)),
            pl.BlockSpec((tm, PLE_DIM), lambda i: (i, 0)),
            pl.BlockSpec((1, D_MODEL), lambda i: (0, 0)),
            pl.BlockSpec((D_MODEL, D_MODEL), lambda i: (0, 0)),
            pl.BlockSpec((PLE_DIM, D_MODEL), lambda i: (0, 0)),
            pl.BlockSpec((1, D_MODEL), lambda i: (0, 0)),
        ],
        out_specs=pl.BlockSpec((tm, D_MODEL), lambda i: (i, 0)),
        out_shape=jax.ShapeDtypeStruct((t, D_MODEL), f32),
        compiler_params=_cparams(("parallel",)),
        name="ple",
    )(x, p, g, wg, wp, gf)


def _split_w_in(w_in):
    cuts = np.cumsum(np.array(SPLIT_SIZES))[:-1].tolist()
    aq, ak, av, iq, ik, iw, gq, gk, gv, gr, ga, mga, mgb = jnp.split(w_in, cuts, axis=-1)
    att_scale = ATTN_HEAD_DIM ** -0.5 * float(np.log2(np.e))
    idx_scale = IDX_HEAD_DIM ** -0.5 * N_IDX_HEADS ** -0.5
    main = jnp.concatenate([mga, mgb, ak, gq * (GLA_HEAD_K ** -0.5), gk, gv, gr], axis=-1)
    feature_major = jnp.concatenate([aq * att_scale, av, iq], axis=-1)
    zeros = lambda n: jnp.zeros(w_in.shape[:-1] + (n,), w_in.dtype)
    small = jnp.concatenate(
        [ik, zeros(LANES - IDX_HEAD_DIM), iw * idx_scale, ga,
         zeros(LANES - N_IDX_HEADS - GLA_GATE_RANK)], axis=-1)
    return main.astype(bf16), feature_major.astype(bf16), small.astype(bf16)


def kernel(x, p, w_in, gla_gate_w2, gla_gate_b, gla_norm, w_branch_a, w_branch_b, w_out,
           norm_ff1, norm_mix, norm_ff2, norm_ple, ff1_w_gate, ff1_w_up, ff1_w_down,
           ff2_w_gate, ff2_w_up, ff2_w_down, ple_w_proj, ple_w_gate, norm_final):
    b, s, d = x.shape
    t = b * s
    depth = w_in.shape[0]
    w_main, w_fm, w_small = _split_w_in(w_in)
    cast = lambda w: w.astype(bf16)
    ff1 = (cast(ff1_w_gate), cast(ff1_w_up), cast(ff1_w_down))
    ff2 = (cast(ff2_w_gate), cast(ff2_w_up), cast(ff2_w_down))
    wa, wb, wo = cast(w_branch_a), cast(w_branch_b), cast(w_out)
    wpg, wpp = cast(ple_w_gate), cast(ple_w_proj)
    row = lambda v: v.reshape(1, -1)

    xf = x.reshape(t, d)
    for l in range(depth):
        xf = _ffn(xf, row(norm_ff1[l]), ff1[0][l], ff1[1][l], ff1[2][l])

        z, qT, vT, iqT, ik, sm, iwT = _proj(xf, row(norm_mix[l]), w_main[l], w_fm[l],
                                            w_small[l], b)
        z3 = z.reshape(b, s, Z_WIDTH)
        sm3 = sm.reshape(b, s, LANES)
        oa = _dsa(qT, iqT, iwT, z3, vT, ik.reshape(b, s, IDX_HEAD_DIM))
        ob = _gla(z3, sm3, gla_gate_w2[l], row(gla_gate_b[l]), row(gla_norm[l]))
        xf = _merge(xf, oa.reshape(t, ATTN_W), ob.reshape(t, GLA_V_W), z, wa[l], wb[l], wo[l])

        xf = _ffn(xf, row(norm_ff2[l]), ff2[0][l], ff2[1][l], ff2[2][l])
        xf = _ple(xf, p[l].reshape(t, PLE_DIM), row(norm_ple[l]), wpg[l], wpp[l],
                  row(norm_final), final=(l == depth - 1))
    return xf.reshape(b, s, d)
```

```python
import functools
import statistics

import jax
import jax.numpy as jnp
import numpy as np
from jax import lax
from jax.experimental import pallas as pl
from jax.experimental.pallas import tpu as pltpu

D_MODEL = 1024
D_FF = 2816
PLE_DIM = 256
DEPTH = 2
EPS = 1e-6

CHUNK = 64
N_ATTN_HEADS = 8
ATTN_HEAD_DIM = 64
N_IDX_HEADS = 8
IDX_HEAD_DIM = 64
TOPK_MAX = 256
N_GLA_HEADS = 4
GLA_HEAD_K = 64
GLA_HEAD_V = 128
GLA_GATE_RANK = 16
GLA_GATE_TAU = 16.0

ATTN_W = N_ATTN_HEADS * ATTN_HEAD_DIM
IDX_Q_W = N_IDX_HEADS * IDX_HEAD_DIM
GLA_K_W = N_GLA_HEADS * GLA_HEAD_K
GLA_V_W = N_GLA_HEADS * GLA_HEAD_V
SPLIT_SIZES = (ATTN_W, ATTN_W, ATTN_W, IDX_Q_W, IDX_HEAD_DIM, N_IDX_HEADS,
               GLA_K_W, GLA_K_W, GLA_V_W, GLA_V_W, GLA_GATE_RANK, D_MODEL, D_MODEL)

LANES = 128
SUBLANES = 8
VMEM_LIMIT_BYTES = 56 * 1024 * 1024

COL_TILE = 512
Z_MGA, Z_MGB = 0, 1
Z_AK, Z_GQK, Z_GV, Z_GR = 4, 5, 6, 7
Z_TILES = 8
Z_WIDTH = Z_TILES * COL_TILE
ZT_TILES = 3
SM_IW0 = 0
SM_GA0 = N_IDX_HEADS

TOKEN_TILE = 512
FF_TILE = 1408
DSA_BLOCK = 256
GLA_TILE = 256
SEARCH_MAX_ITERS = 24
SEARCH_FINISH_FROM = 4

NEG_BIG = -1e30
F32_LOWEST = float(np.finfo(np.float32).min)
F32_TINY = float(np.finfo(np.float32).tiny)
INT32_MIN = int(np.iinfo(np.int32).min)

f32 = jnp.float32
bf16 = jnp.bfloat16


def _rms(x, g):
    return x * lax.rsqrt(jnp.mean(x * x, axis=-1, keepdims=True) + EPS) * g


def _sigmoid(x):
    return 1.0 / (1.0 + jnp.exp(-x))


def _cparams(sem):
    return pltpu.CompilerParams(dimension_semantics=sem, vmem_limit_bytes=VMEM_LIMIT_BYTES)


def _ffn_kernel(x_ref, g_ref, wg_ref, wu_ref, wd_ref, o_ref, h_ref, acc_ref):
    f = pl.program_id(1)

    @pl.when(f == 0)
    def _():
        h_ref[...] = _rms(x_ref[...], g_ref[...]).astype(bf16)
        acc_ref[...] = jnp.zeros_like(acc_ref)

    h = h_ref[...]
    gate = jnp.dot(h, wg_ref[...], preferred_element_type=f32)
    up = jnp.dot(h, wu_ref[...], preferred_element_type=f32)
    a = (gate * _sigmoid(gate) * up).astype(bf16)
    acc_ref[...] += jnp.dot(a, wd_ref[...], preferred_element_type=f32)

    @pl.when(f == pl.num_programs(1) - 1)
    def _():
        o_ref[...] = x_ref[...] + 0.5 * acc_ref[...]


def _ffn(x, g, wg, wu, wd):
    t = x.shape[0]
    return pl.pallas_call(
        _ffn_kernel,
        grid=(t // TOKEN_TILE, D_FF // FF_TILE),
        in_specs=[
            pl.BlockSpec((TOKEN_TILE, D_MODEL), lambda i, f: (i, 0)),
            pl.BlockSpec((1, D_MODEL), lambda i, f: (0, 0)),
            pl.BlockSpec((D_MODEL, FF_TILE), lambda i, f: (0, f)),
            pl.BlockSpec((D_MODEL, FF_TILE), lambda i, f: (0, f)),
            pl.BlockSpec((FF_TILE, D_MODEL), lambda i, f: (f, 0)),
        ],
        out_specs=pl.BlockSpec((TOKEN_TILE, D_MODEL), lambda i, f: (i, 0)),
        out_shape=jax.ShapeDtypeStruct((t, D_MODEL), f32),
        scratch_shapes=[pltpu.VMEM((TOKEN_TILE, D_MODEL), bf16),
                        pltpu.VMEM((TOKEN_TILE, D_MODEL), f32)],
        compiler_params=_cparams(("parallel", "arbitrary")),
        name="ffn",
    )(x, g, wg, wu, wd)


def _proj_kernel(x_ref, g_ref, w_ref, wt_ref, ws_ref,
                 z_ref, qT_ref, vT_ref, iqT_ref, ik_ref, sm_ref, iwT_ref):
    h = _rms(x_ref[...], g_ref[...]).astype(bf16)
    for j in range(Z_TILES):
        cols = slice(j * COL_TILE, (j + 1) * COL_TILE)
        z_ref[:, cols] = jnp.dot(h, w_ref[:, cols], preferred_element_type=f32).astype(bf16)
    for j, out_ref in enumerate((qT_ref, vT_ref, iqT_ref)):
        cols = slice(j * COL_TILE, (j + 1) * COL_TILE)
        zt = jnp.dot(h, wt_ref[:, cols], preferred_element_type=f32)
        out_ref[...] = zt.T.astype(bf16)
    s = jnp.dot(h, ws_ref[...], preferred_element_type=f32)
    ik_ref[...] = s[:, :IDX_HEAD_DIM].astype(bf16)
    sm = s[:, LANES:]
    sm_ref[...] = sm
    iwT_ref[...] = sm.T[SM_IW0:SM_IW0 + N_IDX_HEADS, :]


def _proj(x, g, w_main, w_t, w_small, batch):
    t = x.shape[0]
    tm = TOKEN_TILE
    s = t // batch
    per_b = s // tm
    fm = lambda rows: pl.BlockSpec((None, rows, tm), lambda i: (i // per_b, 0, i % per_b))
    return pl.pallas_call(
        _proj_kernel,
        grid=(t // tm,),
        in_specs=[
            pl.BlockSpec((tm, D_MODEL), lambda i: (i, 0)),
            pl.BlockSpec((1, D_MODEL), lambda i: (0, 0)),
            pl.BlockSpec((D_MODEL, Z_WIDTH), lambda i: (0, 0)),
            pl.BlockSpec((D_MODEL, ZT_TILES * COL_TILE), lambda i: (0, 0)),
            pl.BlockSpec((D_MODEL, 2 * LANES), lambda i: (0, 0)),
        ],
        out_specs=[
            pl.BlockSpec((tm, Z_WIDTH), lambda i: (i, 0)),
            fm(ATTN_W), fm(ATTN_W), fm(IDX_Q_W),
            pl.BlockSpec((tm, IDX_HEAD_DIM), lambda i: (i, 0)),
            pl.BlockSpec((tm, LANES), lambda i: (i, 0)),
            fm(N_IDX_HEADS),
        ],
        out_shape=[
            jax.ShapeDtypeStruct((t, Z_WIDTH), bf16),
            jax.ShapeDtypeStruct((batch, ATTN_W, s), bf16),
            jax.ShapeDtypeStruct((batch, ATTN_W, s), bf16),
            jax.ShapeDtypeStruct((batch, IDX_Q_W, s), bf16),
            jax.ShapeDtypeStruct((t, IDX_HEAD_DIM), bf16),
            jax.ShapeDtypeStruct((t, LANES), f32),
            jax.ShapeDtypeStruct((batch, N_IDX_HEADS, s), f32),
        ],
        compiler_params=_cparams(("parallel",)),
        name="proj",
    )(x, g, w_main, w_t, w_small)


def _dsa_kernel(qT_ref, iqT_ref, iwT_ref, tab_ref, k_ref, vT_ref, ik_ref, o_ref,
                ibuf, qTm, s_ref, bias_ref, m_ref, l_ref, alpha_ref, acc_ref, thr_ref,
                ext_ref, pending_ref, finish_ref, *, seq_len):
    qb = DSA_BLOCK
    kt_rows = DSA_BLOCK
    i = pl.program_id(1)
    n_tiles = i + 1

    rowid = lax.broadcasted_iota(jnp.int32, (LANES, qb), 0)
    for h in range(N_ATTN_HEADS):
        pair = qT_ref[(h // 2) * LANES:(h // 2 + 1) * LANES, :]
        keep = (rowid < ATTN_HEAD_DIM) if h % 2 == 0 else (rowid >= ATTN_HEAD_DIM)
        qTm[h] = jnp.where(keep, pair, jnp.zeros_like(pair))
    l_ref[...] = jnp.zeros_like(l_ref)
    acc_ref[...] = jnp.zeros_like(acc_ref)

    def fold(x, op):
        return op(x.reshape(kt_rows // SUBLANES, SUBLANES, qb), axis=0)

    def idx_tile(kt, diagonal, stats):
        r0 = pl.multiple_of(kt * kt_rows, kt_rows)
        ki_t = ik_ref[pl.ds(r0, kt_rows), :]
        sc = jnp.zeros((kt_rows, qb), f32)
        for h in range(N_IDX_HEADS):
            s = jnp.dot(ki_t, iqT_ref[h * IDX_HEAD_DIM:(h + 1) * IDX_HEAD_DIM, :],
                        preferred_element_type=f32)
            sc = sc + iwT_ref[h:h + 1, :] * jnp.maximum(s, 0.0)
        sc_hi = sc_lo = sc_0 = sc
        if diagonal:
            kc = lax.broadcasted_iota(jnp.int32, (kt_rows, qb), 0) // CHUNK
            qc = lax.broadcasted_iota(jnp.int32, (kt_rows, qb), 1) // CHUNK
            adm = kc <= qc
            sc_hi = jnp.where(adm, sc, -jnp.inf)
            sc_lo = jnp.where(adm, sc, jnp.inf)
            sc_0 = jnp.where(adm, sc, 0.0)
        ibuf[pl.ds(r0, kt_rows), :] = sc_hi
        mx8, mn8, s8, ss8 = stats
        return (jnp.maximum(mx8, fold(sc_hi, jnp.max)), jnp.minimum(mn8, fold(sc_lo, jnp.min)),
                s8 + fold(sc_0, jnp.sum), ss8 + fold(sc_0 * sc_0, jnp.sum))

    stats0 = (jnp.full((SUBLANES, qb), -jnp.inf, f32), jnp.full((SUBLANES, qb), jnp.inf, f32),
              jnp.zeros((SUBLANES, qb), f32), jnp.zeros((SUBLANES, qb), f32))
    odd = i % 2
    stats = lax.cond(odd == 1, lambda st: idx_tile(0, False, st), lambda st: st, stats0)
    stats = lax.fori_loop(
        0, i // 2,
        lambda j, st: idx_tile(odd + 2 * j + 1, False, idx_tile(odd + 2 * j, False, st)),
        stats)
    mx8, mn8, s8, ss8 = idx_tile(i, True, stats)
    col_max = jnp.max(mx8, axis=0, keepdims=True)
    col_min = jnp.min(mn8, axis=0, keepdims=True)
    col_sum = jnp.sum(s8, axis=0, keepdims=True)
    col_ssq = jnp.sum(ss8, axis=0, keepdims=True)

    def count(pred):
        def body(r, acc):
            r0 = pl.multiple_of(r * kt_rows, kt_rows)
            tile = ibuf[pl.ds(r0, kt_rows), :]
            rows = r0 + lax.broadcasted_iota(jnp.int32, (kt_rows, qb), 0)
            ind = jnp.where(pred(tile, rows), 1.0, 0.0)
            part = ind.reshape(kt_rows // SUBLANES, SUBLANES, qb)
            while part.shape[0] > 1:
                half = part.shape[0] // 2
                part = part[:half] + part[half:]
            return acc + part[0]
        acc = jnp.zeros((SUBLANES, qb), f32)
        acc = lax.cond(i % 2 == 0, lambda a: body(0, a), lambda a: a, acc)
        first = 1 - i % 2
        acc = lax.fori_loop(
            0, (i + 1) // 2,
            lambda j, a: body(first + 2 * j + 1, body(first + 2 * j, a)), acc)
        return jnp.sum(acc, axis=0, keepdims=True)

    def resolve_ties(tv, need, mult):
        is_tie = tv == tv
        n_keys = n_tiles * kt_rows

        def tie_cond(st):
            return st["active"] > 0

        def tie_body(st):
            plo, phi, flo, fhi = st["plo"], st["phi"], st["flo"], st["fhi"]
            span = phi - plo
            est = ((need - flo) / jnp.maximum(fhi - flo, 1.0) * span.astype(f32)).astype(jnp.int32)
            pick = jnp.where(st["bisect"] > 0, span // 2, est)
            cand = plo + jnp.clip(pick, 1, jnp.maximum(span - 1, 1))
            f = count(lambda tile, rows: jnp.where(tile == tv, rows, seq_len) < cand)
            lower = f < need
            exact = f == need
            plo = jnp.where(lower, cand, jnp.where(exact, cand - 1, plo))
            phi = jnp.where(lower, phi, cand)
            flo = jnp.where(lower, f, flo)
            fhi = jnp.where(lower, fhi, f)
            unsplit = jnp.where(is_tie & (phi - plo > 1), 1, 0)
            return dict(active=jnp.max(unsplit), bisect=1 - st["bisect"],
                        plo=plo, phi=phi, flo=flo, fhi=fhi)

        split = lax.while_loop(tie_cond, tie_body, dict(
            active=jnp.int32(1), bisect=jnp.int32(0),
            plo=jnp.zeros((1, qb), jnp.int32), phi=jnp.zeros((1, qb), jnp.int32) + n_keys,
            flo=jnp.zeros((1, qb), f32), fhi=mult))["phi"]

        def drop_body(r, carry):
            r0 = pl.multiple_of(r * kt_rows, kt_rows)
            tile = ibuf[pl.ds(r0, kt_rows), :]
            rows = r0 + lax.broadcasted_iota(jnp.int32, (kt_rows, qb), 0)
            dropped = jnp.where(tile == tv, rows, -1) >= split
            ibuf[pl.ds(r0, kt_rows), :] = jnp.where(dropped, -jnp.inf, tile)
            return carry

        lax.fori_loop(0, n_tiles, drop_body, 0)

    qpos = i * qb + lax.broadcasted_iota(jnp.int32, (1, qb), 1)
    n_adm = (qpos // CHUNK + 1) * CHUNK
    select_all = n_adm <= TOPK_MAX
    kf = float(TOPK_MAX)

    thr_ref[...] = jnp.full((1, qb), F32_LOWEST, f32)
    pending_ref[0] = jnp.int32(0)

    @pl.when(n_tiles * kt_rows > TOPK_MAX)
    def _():
        n_f = n_adm.astype(f32)
        mean = col_sum / n_f
        sigma = jnp.sqrt(jnp.maximum(col_ssq / n_f - mean * mean, 0.0))
        spacing = tab_ref[1:2, :] * sigma

        def inside(c, lo, hi):
            return (c > lo) & (c < hi)

        c0 = mean + tab_ref[0:1, :] * sigma
        c0 = jnp.where(inside(c0, col_min, col_max), c0, 0.5 * col_min + 0.5 * col_max)
        zero = jnp.zeros((1, qb), f32)
        one = jnp.ones((1, qb), f32)
        state0 = dict(it=jnp.int32(0), active=jnp.int32(1), c=c0, lo=col_min, hi=col_max,
                      glo=n_f, ghi=zero, tlo=zero, thi=zero, boost=one, side=zero,
                      thr=jnp.full((1, qb), F32_LOWEST, f32),
                      open=jnp.where(select_all, 0.0, 1.0), tie=zero, kind=zero, emin=zero)

        def search_cond(st):
            return (st["it"] < SEARCH_MAX_ITERS) & (st["active"] > 0)

        def search_body(st):
            c, lo, hi = st["c"], st["lo"], st["hi"]
            g = count(lambda tile, rows: tile >= c)
            hit = (g == kf) & (st["open"] > 0.0)
            thr = jnp.where(hit, c, st["thr"])
            still = jnp.where(hit, 0.0, st["open"])
            above = g > kf
            lo = jnp.where(above, c, lo)
            hi = jnp.where(above, hi, c)
            glo = jnp.where(above, g, st["glo"])
            ghi = jnp.where(above, st["ghi"], g)
            tlo = jnp.where(above, 1.0, st["tlo"])
            thi = jnp.where(above, st["thi"], 1.0)
            side = jnp.where(above, 1.0, -1.0)
            repeat = side == st["side"]
            boost = jnp.where(repeat, 2.0 * st["boost"], 1.0)
            mid = 0.5 * lo + 0.5 * hi
            stale = 1.0 / jnp.minimum(boost, 256.0)
            w_lo = (glo - kf - 0.5) * jnp.where(above, 1.0, stale)
            w_hi = (kf + 0.5 - ghi) * jnp.where(above, stale, 1.0)
            c_two = lo + w_lo / (w_lo + w_hi) * (hi - lo)
            step = 2.0 * boost * spacing
            c_one = jnp.where(tlo > 0.0, lo + jnp.maximum(glo - kf, 1.0) * step,
                              hi - jnp.maximum(kf - ghi, 1.0) * step)
            c_one = jnp.where(inside(c_one, lo, hi), c_one, mid)
            both = (tlo > 0.0) & (thi > 0.0)
            c_new = jnp.where(both, c_two, c_one)
            c_new = jnp.where(both & (lo < 0.0) & (hi > 0.0), 0.0, c_new)
            c_new = jnp.where((lo == 0.0) & (hi > F32_TINY), F32_TINY, c_new)
            movable = inside(c_new, lo, hi) & ~((lo == 0.0) & (hi <= F32_TINY))
            missed = (still > 0.0) & (((st["kind"] == 1.0) & above) |
                                      ((st["kind"] == 2.0) & (g < kf)))
            lo = jnp.where(missed & (st["kind"] == 2.0), st["emin"], lo)
            closed = (~movable & both & (still > 0.0)) | missed
            tie = jnp.where(closed, 1.0, st["tie"])
            still = jnp.where(closed, 0.0, still)
            searching = jnp.where(movable, still, 0.0)

            from_hi = both & (kf - ghi == 1.0)
            from_lo = both & (glo - kf == 1.0) & ~from_hi
            ready = jnp.where(searching > 0.0, jnp.where(from_hi | from_lo, 1.0, 2.0), 0.0)
            code = jnp.max(ready)
            ext_ref[...] = jnp.zeros_like(ext_ref)
            finish_ref[0] = jnp.int32(0)

            @pl.when((code == 1.0) & (st["it"] + 1 >= SEARCH_FINISH_FROM))
            def _():
                def ext_body(r, carry):
                    r0 = pl.multiple_of(r * kt_rows, kt_rows)
                    tile = ibuf[pl.ds(r0, kt_rows), :]
                    below = fold(jnp.where(tile < hi, tile, -jnp.inf), jnp.max)
                    above_lo = fold(jnp.where(tile >= lo, tile, jnp.inf), jnp.min)
                    return jnp.maximum(carry[0], below), jnp.minimum(carry[1], above_lo)
                mx8, mn8 = lax.fori_loop(0, n_tiles, ext_body,
                                         (jnp.full((SUBLANES, qb), -jnp.inf, f32),
                                          jnp.full((SUBLANES, qb), jnp.inf, f32)))
                ext_ref[0:1, :] = jnp.max(mx8, axis=0, keepdims=True)
                ext_ref[1:2, :] = jnp.min(mn8, axis=0, keepdims=True)
                finish_ref[0] = jnp.int32(1)

            finishing = finish_ref[0] > 0
            emax = ext_ref[0:1, :]
            emin = ext_ref[1:2, :]
            ebits = lax.bitcast_convert_type(emin, jnp.int32)
            next_up = lax.bitcast_convert_type(ebits + jnp.where(emin > 0.0, 1, -1), f32)
            next_up = jnp.where(emin == 0.0, F32_TINY, next_up)
            kind = jnp.where(finishing & (searching > 0.0),
                             jnp.where(from_hi, 1.0, jnp.where(from_lo, 2.0, 0.0)), 0.0)
            c_new = jnp.where(kind == 1.0, emax, jnp.where(kind == 2.0, next_up, c_new))
            return dict(it=st["it"] + 1, active=(code > 0.0).astype(jnp.int32),
                        c=c_new, lo=lo, hi=hi, glo=glo, ghi=ghi, tlo=tlo, thi=thi,
                        boost=boost, side=side, thr=thr, open=still, tie=tie,
                        kind=kind, emin=emin)

        final = lax.while_loop(search_cond, search_body, state0)
        tied = final["tie"] > 0.0
        thr_ref[...] = jnp.where(tied, final["lo"], final["thr"])
        pending_ref[0] = (jnp.max(final["open"]) > 0.0).astype(jnp.int32)

        @pl.when(jnp.max(final["tie"]) > 0.0)
        def _():
            resolve_ties(jnp.where(tied, final["lo"], jnp.nan), kf - final["ghi"],
                         final["glo"] - final["ghi"])

    @pl.when(pending_ref[0] > 0)
    def _():
        def key_to_f32(key):
            bits = jnp.where(key < 0, key ^ jnp.int32(0x7FFFFFFF), key)
            return lax.bitcast_convert_type(bits, f32)

        def bit_body(t, prefix):
            step = lax.shift_left(jnp.int32(1), jnp.int32(31) - t)
            cand = prefix + step
            cand_f = key_to_f32(cand)
            cnt = count(lambda tile, rows: tile >= cand_f)
            return jnp.where(cnt >= kf, cand, prefix)

        prefix = lax.fori_loop(0, 32, bit_body, jnp.full((1, qb), INT32_MIN, jnp.int32))
        thr = jnp.where(select_all, F32_LOWEST, key_to_f32(prefix))
        thr_ref[...] = thr

        cnt_ge = count(lambda tile, rows: tile >= thr)
        excess = jnp.where(select_all, 0.0, cnt_ge - kf)

        @pl.when(jnp.max(excess) > 0.0)
        def _():
            cnt_gt = count(lambda tile, rows: tile > thr)
            resolve_ties(jnp.where(excess > 0.0, thr, jnp.nan), kf - cnt_gt, cnt_ge - cnt_gt)

    thr = thr_ref[...]
    ones_rows = jnp.ones((2 * SUBLANES, kt_rows), bf16)

    m_ref[1] = jnp.full(m_ref.shape[1:], NEG_BIG, f32)

    def logits_stage(kt, slot):
        r0 = pl.multiple_of(kt * kt_rows, kt_rows)
        bias_ref[...] = jnp.where(ibuf[pl.ds(r0, kt_rows), :] >= thr, 0.0, NEG_BIG)
        for h in range(N_ATTN_HEADS):
            k_pair = k_ref[pl.ds(r0, kt_rows), (h // 2) * LANES:(h // 2 + 1) * LANES]
            s_ref[slot, h] = jnp.dot(k_pair, qTm[h], preferred_element_type=f32) + bias_ref[...]
            m_old = m_ref[1 - slot, h]
            m_new = jnp.maximum(m_old, jnp.max(s_ref[slot, h], axis=0, keepdims=True))
            alpha_ref[slot, h] = jnp.exp2(m_old - m_new)
            m_ref[slot, h] = m_new

    def values_stage(kt, slot):
        r0 = pl.multiple_of(kt * kt_rows, kt_rows)
        for h in range(N_ATTN_HEADS):
            p = jnp.exp2(s_ref[slot, h] - m_ref[slot, h]).astype(bf16)
            hs = slice(h * ATTN_HEAD_DIM, (h + 1) * ATTN_HEAD_DIM)
            lhs = jnp.concatenate([vT_ref[hs, pl.ds(r0, kt_rows)], ones_rows], axis=0)
            pv = jnp.dot(lhs, p, preferred_element_type=f32)
            alpha = alpha_ref[slot, h]
            acc_ref[hs, :] = alpha * acc_ref[hs, :] + pv[:ATTN_HEAD_DIM]
            l_ref[h] = alpha * l_ref[h] + pv[ATTN_HEAD_DIM:ATTN_HEAD_DIM + 1]

    def att_body(j, carry):
        kt = 2 * j
        logits_stage(kt + 1, 1)
        values_stage(kt, 0)
        logits_stage(kt + 2, 0)
        values_stage(kt + 1, 1)
        return carry

    logits_stage(0, 0)
    lax.fori_loop(0, i // 2, att_body, 0)

    @pl.when(i % 2 == 1)
    def _():
        logits_stage(i, 1)
        values_stage(i - 1, 0)
        values_stage(i, 1)

    @pl.when(i % 2 == 0)
    def _():
        values_stage(i, 0)

    for h in range(N_ATTN_HEADS):
        hs = slice(h * ATTN_HEAD_DIM, (h + 1) * ATTN_HEAD_DIM)
        acc_ref[hs, :] = acc_ref[hs, :] / l_ref[h]
    o_ref[...] = acc_ref[...].T.astype(o_ref.dtype)


def _search_hints(seq_len):
    n = (np.arange(seq_len) // CHUNK + 1) * CHUNK
    frac = np.minimum(TOPK_MAX / n, 0.5)
    nd = statistics.NormalDist()
    z_of = {f: nd.inv_cdf(1.0 - f) for f in np.unique(frac)}
    z = np.array([z_of[f] for f in frac])
    dens = n * np.exp(-0.5 * z * z) / np.sqrt(2.0 * np.pi)
    return jnp.asarray(np.stack([z, 1.0 / dens]), f32)


def _dsa(qT, iqT, iwT, z3, vT, ik3):
    b, _, s = qT.shape
    qb = DSA_BLOCK
    return pl.pallas_call(
        functools.partial(_dsa_kernel, seq_len=s),
        grid=(b, s // qb),
        in_specs=[
            pl.BlockSpec((None, ATTN_W, qb), lambda bi, i: (bi, 0, i)),
            pl.BlockSpec((None, IDX_Q_W, qb), lambda bi, i: (bi, 0, i)),
            pl.BlockSpec((None, N_IDX_HEADS, qb), lambda bi, i: (bi, 0, i)),
            pl.BlockSpec((2, qb), lambda bi, i: (0, i)),
            pl.BlockSpec((None, s, COL_TILE), lambda bi, i: (bi, 0, Z_AK)),
            pl.BlockSpec((None, ATTN_W, s), lambda bi, i: (bi, 0, 0)),
            pl.BlockSpec((None, s, IDX_HEAD_DIM), lambda bi, i: (bi, 0, 0)),
        ],
        out_specs=pl.BlockSpec((None, qb, ATTN_W), lambda bi, i: (bi, i, 0)),
        out_shape=jax.ShapeDtypeStruct((b, s, ATTN_W), bf16),
        scratch_shapes=[
            pltpu.VMEM((s, qb), f32),
            pltpu.VMEM((N_ATTN_HEADS, LANES, qb), bf16),
            pltpu.VMEM((2, N_ATTN_HEADS, qb, qb), f32),
            pltpu.VMEM((qb, qb), f32),
            pltpu.VMEM((2, N_ATTN_HEADS, 1, qb), f32),
            pltpu.VMEM((N_ATTN_HEADS, 1, qb), f32),
            pltpu.VMEM((2, N_ATTN_HEADS, 1, qb), f32),
            pltpu.VMEM((ATTN_W, qb), f32),
            pltpu.VMEM((1, qb), f32),
            pltpu.VMEM((2, qb), f32),
            pltpu.SMEM((1,), jnp.int32),
            pltpu.SMEM((1,), jnp.int32),
        ],
        compiler_params=_cparams(("parallel", "arbitrary")),
        name="dsa",
    )(qT, iqT, iwT, _search_hints(s), z3, vT, ik3)


def _gla_kernel(qk_ref, v_ref, r_ref, sm_ref, w2_ref, gb_ref, gn_ref, o_ref, state_ref):
    c = CHUNK
    n_batch = qk_ref.shape[0]

    @pl.when(pl.program_id(0) == 0)
    def _():
        state_ref[...] = jnp.zeros_like(state_ref)

    row = lax.broadcasted_iota(jnp.int32, (c, c), 0)
    col = lax.broadcasted_iota(jnp.int32, (c, c), 1)
    causal = row >= col
    tril = jnp.where(causal, 1.0, 0.0)
    lane = lax.broadcasted_iota(jnp.int32, (c, LANES), 1)
    half = (lane < GLA_HEAD_K, lane >= GLA_HEAD_K)
    contract_last = (((1,), (1,)), ((), ()))
    contract_first = (((0,), (0,)), ((), ()))

    for ci, bi in [(ci, bi) for ci in range(GLA_TILE // c) for bi in range(n_batch)]:
        rows = slice(ci * c, (ci + 1) * c)
        qk = qk_ref[bi, rows, :].astype(f32)
        q = qk[:, :GLA_K_W]
        k = qk[:, GLA_K_W:]
        ga = sm_ref[bi, rows, :][:, SM_GA0:SM_GA0 + GLA_GATE_RANK]
        glin = jnp.dot(ga, w2_ref[...], preferred_element_type=f32,
                       precision=lax.Precision.HIGHEST) + gb_ref[...]
        logg = (jnp.minimum(glin, 0.0) - jnp.log(1.0 + jnp.exp(-jnp.abs(glin)))) / GLA_GATE_TAU
        bcum = jnp.dot(tril, logg, preferred_element_type=f32, precision=lax.Precision.HIGHEST)
        b_last = bcum[c - 1:c, :]
        b_mid = bcum[c // 2 - 1:c // 2, :]
        q_in = q * jnp.exp(bcum)
        q_mid = q * jnp.exp(bcum - b_mid)
        k_mid = k * jnp.exp(b_mid - bcum)
        k_out = k * jnp.exp(b_last - bcum)
        decay = jnp.exp(b_last)

        for pr in range(N_GLA_HEADS // 2):
            ps = slice(pr * LANES, (pr + 1) * LANES)
            st = state_ref[bi, pr]
            st_b = st.astype(bf16)
            k_mid_p = k_mid[:, ps].astype(bf16)
            upd = jnp.zeros_like(st)
            for hh in range(2):
                h = pr * 2 + hh
                vs = slice(h * GLA_HEAD_V, (h + 1) * GLA_HEAD_V)
                v_h = v_ref[bi, rows, vs]
                q_mid_m = jnp.where(half[hh], q_mid[:, ps], 0.0).astype(bf16)
                q_in_m = jnp.where(half[hh], q_in[:, ps], 0.0).astype(bf16)
                k_out_m = jnp.where(half[hh], k_out[:, ps], 0.0).astype(bf16)
                attn = lax.dot_general(q_mid_m, k_mid_p, contract_last,
                                       preferred_element_type=f32)
                attn = jnp.where(causal, attn, 0.0).astype(bf16)
                o = lax.dot_general(q_in_m, st_b, contract_last, preferred_element_type=f32)
                o = o + jnp.dot(attn, v_h, preferred_element_type=f32)
                upd = upd + lax.dot_general(v_h, k_out_m, contract_first,
                                            preferred_element_type=f32)
                y = _rms(o, gn_ref[...])
                r = r_ref[bi, rows, vs].astype(f32)
                o_ref[bi, rows, vs] = (y * r * _sigmoid(r)).astype(o_ref.dtype)
            state_ref[bi, pr] = decay[:, ps] * st + upd


def _gla(z3, sm3, w2, gb, gn):
    b, s, _ = z3.shape
    tb = GLA_TILE
    return pl.pallas_call(
        _gla_kernel,
        grid=(s // tb,),
        in_specs=[
            pl.BlockSpec((b, tb, COL_TILE), lambda j: (0, j, Z_GQK)),
            pl.BlockSpec((b, tb, COL_TILE), lambda j: (0, j, Z_GV)),
            pl.BlockSpec((b, tb, COL_TILE), lambda j: (0, j, Z_GR)),
            pl.BlockSpec((b, tb, LANES), lambda j: (0, j, 0)),
            pl.BlockSpec((GLA_GATE_RANK, GLA_K_W), lambda j: (0, 0)),
            pl.BlockSpec((1, GLA_K_W), lambda j: (0, 0)),
            pl.BlockSpec((1, GLA_HEAD_V), lambda j: (0, 0)),
        ],
        out_specs=pl.BlockSpec((b, tb, GLA_V_W), lambda j: (0, j, 0)),
        out_shape=jax.ShapeDtypeStruct((b, s, GLA_V_W), bf16),
        scratch_shapes=[pltpu.VMEM((b, N_GLA_HEADS // 2, GLA_HEAD_V, LANES), f32)],
        compiler_params=_cparams(("arbitrary",)),
        name="gla",
    )(z3, z3, z3, sm3, w2, gb, gn)


def _merge_kernel(x_ref, oa_ref, ob_ref, ga_ref, gb_ref, wa_ref, wb_ref, wo_ref, o_ref):
    ya = jnp.dot(oa_ref[...], wa_ref[...], preferred_element_type=f32)
    yb = jnp.dot(ob_ref[...], wb_ref[...], preferred_element_type=f32)
    mg = _sigmoid(ga_ref[...].astype(f32)) * ya + _sigmoid(gb_ref[...].astype(f32)) * yb
    o_ref[...] = x_ref[...] + jnp.dot(mg.astype(bf16), wo_ref[...], preferred_element_type=f32)


def _merge(x, oa, ob, z, wa, wb, wo):
    t = x.shape[0]
    tm = TOKEN_TILE
    return pl.pallas_call(
        _merge_kernel,
        grid=(t // tm,),
        in_specs=[
            pl.BlockSpec((tm, D_MODEL), lambda i: (i, 0)),
            pl.BlockSpec((tm, ATTN_W), lambda i: (i, 0)),
            pl.BlockSpec((tm, GLA_V_W), lambda i: (i, 0)),
            pl.BlockSpec((tm, D_MODEL), lambda i: (i, Z_MGA)),
            pl.BlockSpec((tm, D_MODEL), lambda i: (i, Z_MGB)),
            pl.BlockSpec((ATTN_W, D_MODEL), lambda i: (0, 0)),
            pl.BlockSpec((GLA_V_W, D_MODEL), lambda i: (0, 0)),
            pl.BlockSpec((D_MODEL, D_MODEL), lambda i: (0, 0)),
        ],
        out_specs=pl.BlockSpec((tm, D_MODEL), lambda i: (i, 0)),
        out_shape=jax.ShapeDtypeStruct((t, D_MODEL), f32),
        compiler_params=_cparams(("parallel",)),
        name="merge",
    )(x, oa, ob, z, z, wa, wb, wo)


def _ple_kernel(x_ref, p_ref, g_ref, wg_ref, wp_ref, gf_ref, o_ref, *, final):
    x = x_ref[...]
    h = _rms(x, g_ref[...]).astype(bf16)
    gate = _sigmoid(jnp.dot(h, wg_ref[...], preferred_element_type=f32))
    e = jnp.dot(p_ref[...].astype(bf16), wp_ref[...], preferred_element_type=f32)
    y = x + gate * e
    if final:
        y = _rms(y, gf_ref[...])
    o_ref[...] = y


def _ple(x, p, g, wg, wp, gf, final):
    t = x.shape[0]
    tm = TOKEN_TILE
    return pl.pallas_call(
        functools.partial(_ple_kernel, final=final),
        grid=(t // tm,),
        in_specs=[
            pl.BlockSpec((tm, D_MODEL), lambda i: (i, 0)),
            pl.BlockSpec((tm, PLE_DIM), lambda i: (i, 0)),
            pl.BlockSpec((1, D_MODEL), lambda i: (0, 0)),
            pl.BlockSpec((D_MODEL, D_MODEL), lambda i: (0, 0)),
            pl.BlockSpec((PLE_DIM, D_MODEL), lambda i: (0, 0)),
            pl.BlockSpec((1, D_MODEL), lambda i: (0, 0)),
        ],
        out_specs=pl.BlockSpec((tm, D_MODEL), lambda i: (i, 0)),
        out_shape=jax.ShapeDtypeStruct((t, D_MODEL), f32),
        compiler_params=_cparams(("parallel",)),
        name="ple",
    )(x, p, g, wg, wp, gf)


def _split_w_in(w_in):
    cuts = np.cumsum(np.array(SPLIT_SIZES))[:-1].tolist()
    aq, ak, av, iq, ik, iw, gq, gk, gv, gr, ga, mga, mgb = jnp.split(w_in, cuts, axis=-1)
    att_scale = ATTN_HEAD_DIM ** -0.5 * float(np.log2(np.e))
    idx_scale = IDX_HEAD_DIM ** -0.5 * N_IDX_HEADS ** -0.5
    main = jnp.concatenate([mga, mgb, ak, gq * (GLA_HEAD_K ** -0.5), gk, gv, gr], axis=-1)
    feature_major = jnp.concatenate([aq * att_scale, av, iq], axis=-1)
    zeros = lambda n: jnp.zeros(w_in.shape[:-1] + (n,), w_in.dtype)
    small = jnp.concatenate(
        [ik, zeros(LANES - IDX_HEAD_DIM), iw * idx_scale, ga,
         zeros(LANES - N_IDX_HEADS - GLA_GATE_RANK)], axis=-1)
    return main.astype(bf16), feature_major.astype(bf16), small.astype(bf16)


def kernel(x, p, w_in, gla_gate_w2, gla_gate_b, gla_norm, w_branch_a, w_branch_b, w_out,
           norm_ff1, norm_mix, norm_ff2, norm_ple, ff1_w_gate, ff1_w_up, ff1_w_down,
           ff2_w_gate, ff2_w_up, ff2_w_down, ple_w_proj, ple_w_gate, norm_final):
    b, s, d = x.shape
    t = b * s
    depth = w_in.shape[0]
    w_main, w_fm, w_small = _split_w_in(w_in)
    cast = lambda w: w.astype(bf16)
    ff1 = (cast(ff1_w_gate), cast(ff1_w_up), cast(ff1_w_down))
    ff2 = (cast(ff2_w_gate), cast(ff2_w_up), cast(ff2_w_down))
    wa, wb, wo = cast(w_branch_a), cast(w_branch_b), cast(w_out)
    wpg, wpp = cast(ple_w_gate), cast(ple_w_proj)
    row = lambda v: v.reshape(1, -1)

    xf = x.reshape(t, d)
    for l in range(depth):
        xf = _ffn(xf, row(norm_ff1[l]), ff1[0][l], ff1[1][l], ff1[2][l])

        z, qT, vT, iqT, ik, sm, iwT = _proj(xf, row(norm_mix[l]), w_main[l], w_fm[l],
                                            w_small[l], b)
        z3 = z.reshape(b, s, Z_WIDTH)
        sm3 = sm.reshape(b, s, LANES)
        oa = _dsa(qT, iqT, iwT, z3, vT, ik.reshape(b, s, IDX_HEAD_DIM))
        ob = _gla(z3, sm3, gla_gate_w2[l], row(gla_gate_b[l]), row(gla_norm[l]))
        xf = _merge(xf, oa.reshape(t, ATTN_W), ob.reshape(t, GLA_V_W), z, wa[l], wb[l], wo[l])

        xf = _ffn(xf, row(norm_ff2[l]), ff2[0][l], ff2[1][l], ff2[2][l])
        xf = _ple(xf, p[l].reshape(t, PLE_DIM), row(norm_ple[l]), wpg[l], wpp[l],
                  row(norm_final), final=(l == depth - 1))
    return xf.reshape(b, s, d)
```

```python
import functools
import statistics

import jax
import jax.numpy as jnp
import numpy as np
from jax import lax
from jax.experimental import pallas as pl
from jax.experimental.pallas import tpu as pltpu

D_MODEL = 1024
D_FF = 2816
PLE_DIM = 256
DEPTH = 2
EPS = 1e-6

CHUNK = 64
N_ATTN_HEADS = 8
ATTN_HEAD_DIM = 64
N_IDX_HEADS = 8
IDX_HEAD_DIM = 64
TOPK_MAX = 256
N_GLA_HEADS = 4
GLA_HEAD_K = 64
GLA_HEAD_V = 128
GLA_GATE_RANK = 16
GLA_GATE_TAU = 16.0

ATTN_W = N_ATTN_HEADS * ATTN_HEAD_DIM
IDX_Q_W = N_IDX_HEADS * IDX_HEAD_DIM
GLA_K_W = N_GLA_HEADS * GLA_HEAD_K
GLA_V_W = N_GLA_HEADS * GLA_HEAD_V
SPLIT_SIZES = (ATTN_W, ATTN_W, ATTN_W, IDX_Q_W, IDX_HEAD_DIM, N_IDX_HEADS,
               GLA_K_W, GLA_K_W, GLA_V_W, GLA_V_W, GLA_GATE_RANK, D_MODEL, D_MODEL)

LANES = 128
SUBLANES = 8
VMEM_LIMIT_BYTES = 56 * 1024 * 1024

COL_TILE = 512
Z_MGA, Z_MGB = 0, 1
Z_AK, Z_GV, Z_GR = 4, 5, 6
Z_TILES = 7
Z_WIDTH = Z_TILES * COL_TILE
ZT_TILES = 3
SM_IW0 = 0
SM_GA0 = N_IDX_HEADS

TOKEN_TILE = 512
FF_TILE = 1408
DSA_BLOCK = 256
GLA_TILE = 512
GLA_IN_W = 4 * GLA_K_W
SEARCH_MAX_ITERS = 24
SEARCH_FINISH_FROM = 4

NEG_BIG = -1e30
F32_LOWEST = float(np.finfo(np.float32).min)
F32_TINY = float(np.finfo(np.float32).tiny)
INT32_MIN = int(np.iinfo(np.int32).min)

f32 = jnp.float32
bf16 = jnp.bfloat16


def _rms(x, g):
    return x * lax.rsqrt(jnp.mean(x * x, axis=-1, keepdims=True) + EPS) * g


def _sigmoid(x):
    return 1.0 / (1.0 + jnp.exp(-x))


def _cparams(sem):
    return pltpu.CompilerParams(dimension_semantics=sem, vmem_limit_bytes=VMEM_LIMIT_BYTES)


def _ffn_kernel(x_ref, g_ref, wg_ref, wu_ref, wd_ref, o_ref, h_ref, acc_ref):
    f = pl.program_id(1)

    @pl.when(f == 0)
    def _():
        h_ref[...] = _rms(x_ref[...], g_ref[...]).astype(bf16)
        acc_ref[...] = jnp.zeros_like(acc_ref)

    h = h_ref[...]
    gate = jnp.dot(h, wg_ref[...], preferred_element_type=f32)
    up = jnp.dot(h, wu_ref[...], preferred_element_type=f32)
    a = (gate * _sigmoid(gate) * up).astype(bf16)
    acc_ref[...] += jnp.dot(a, wd_ref[...], preferred_element_type=f32)

    @pl.when(f == pl.num_programs(1) - 1)
    def _():
        o_ref[...] = x_ref[...] + 0.5 * acc_ref[...]


def _ffn(x, g, wg, wu, wd):
    t = x.shape[0]
    return pl.pallas_call(
        _ffn_kernel,
        grid=(t // TOKEN_TILE, D_FF // FF_TILE),
        in_specs=[
            pl.BlockSpec((TOKEN_TILE, D_MODEL), lambda i, f: (i, 0)),
            pl.BlockSpec((1, D_MODEL), lambda i, f: (0, 0)),
            pl.BlockSpec((D_MODEL, FF_TILE), lambda i, f: (0, f)),
            pl.BlockSpec((D_MODEL, FF_TILE), lambda i, f: (0, f)),
            pl.BlockSpec((FF_TILE, D_MODEL), lambda i, f: (f, 0)),
        ],
        out_specs=pl.BlockSpec((TOKEN_TILE, D_MODEL), lambda i, f: (i, 0)),
        out_shape=jax.ShapeDtypeStruct((t, D_MODEL), f32),
        scratch_shapes=[pltpu.VMEM((TOKEN_TILE, D_MODEL), bf16),
                        pltpu.VMEM((TOKEN_TILE, D_MODEL), f32)],
        compiler_params=_cparams(("parallel", "arbitrary")),
        name="ffn",
    )(x, g, wg, wu, wd)


def _gla_operands(q, k, ga, w2, gb):
    n = q.shape[0]
    c = CHUNK
    glin = jnp.dot(ga, w2, preferred_element_type=f32, precision=lax.Precision.HIGHEST) + gb
    logg = (jnp.minimum(glin, 0.0) - jnp.log(1.0 + jnp.exp(-jnp.abs(glin)))) / GLA_GATE_TAU
    row = lax.broadcasted_iota(jnp.int32, (n, n), 0)
    col = lax.broadcasted_iota(jnp.int32, (n, n), 1)
    tril = jnp.where((row // c == col // c) & (col <= row), 1.0, 0.0).astype(bf16)
    hi = logg.astype(bf16)
    lo = (logg - hi.astype(f32)).astype(bf16)
    bcum = (jnp.dot(tril, hi, preferred_element_type=f32)
            + jnp.dot(tril, lo, preferred_element_type=f32))
    by_chunk = bcum.reshape(n // c, c, GLA_K_W)
    spread = lambda r: jnp.broadcast_to(r, (n // c, c, GLA_K_W)).reshape(n, GLA_K_W)
    b_last = by_chunk[:, c - 1:c, :]
    b_mid = spread(by_chunk[:, c // 2 - 1:c // 2, :])
    operands = jnp.concatenate(
        [q * jnp.exp(bcum), q * jnp.exp(bcum - b_mid), k * jnp.exp(b_mid - bcum),
         k * jnp.exp(spread(b_last) - bcum)], axis=1)
    return operands, jnp.exp(b_last.reshape(n // c, GLA_K_W))


def _proj_kernel(x_ref, g_ref, w_ref, wt_ref, wg_ref, ws_ref, w2_ref, gb_ref,
                 z_ref, qT_ref, vT_ref, iqT_ref, ik_ref, iwT_ref, gin_ref, dec_ref):
    h = _rms(x_ref[...], g_ref[...]).astype(bf16)
    for j in range(Z_TILES):
        cols = slice(j * COL_TILE, (j + 1) * COL_TILE)
        z_ref[:, cols] = jnp.dot(h, w_ref[:, cols], preferred_element_type=f32).astype(bf16)
    for j, out_ref in enumerate((qT_ref, vT_ref, iqT_ref)):
        cols = slice(j * COL_TILE, (j + 1) * COL_TILE)
        zt = jnp.dot(h, wt_ref[:, cols], preferred_element_type=f32)
        out_ref[...] = zt.T.astype(bf16)
    s = jnp.dot(h, ws_ref[...], preferred_element_type=f32)
    ik_ref[...] = s[:, :IDX_HEAD_DIM].astype(bf16)
    sm = s[:, LANES:]
    iwT_ref[...] = sm.T[SM_IW0:SM_IW0 + N_IDX_HEADS, :]
    gqk = jnp.dot(h, wg_ref[...], preferred_element_type=f32)
    operands, decay = _gla_operands(gqk[:, :GLA_K_W], gqk[:, GLA_K_W:],
                                    sm[:, SM_GA0:SM_GA0 + GLA_GATE_RANK],
                                    w2_ref[...], gb_ref[...])
    gin_ref[...] = operands.astype(bf16)
    dec_ref[...] = decay


def _proj(x, g, w_main, w_t, w_g, w_small, w2, gb, batch):
    t = x.shape[0]
    tm = TOKEN_TILE
    s = t // batch
    per_b = s // tm
    fm = lambda rows: pl.BlockSpec((None, rows, tm), lambda i: (i // per_b, 0, i % per_b))
    whole = lambda a: pl.BlockSpec(a.shape, lambda i: (0, 0))
    return pl.pallas_call(
        _proj_kernel,
        grid=(t // tm,),
        in_specs=[
            pl.BlockSpec((tm, D_MODEL), lambda i: (i, 0)),
            whole(g), whole(w_main), whole(w_t), whole(w_g), whole(w_small), whole(w2), whole(gb),
        ],
        out_specs=[
            pl.BlockSpec((tm, Z_WIDTH), lambda i: (i, 0)),
            fm(ATTN_W), fm(ATTN_W), fm(IDX_Q_W),
            pl.BlockSpec((tm, IDX_HEAD_DIM), lambda i: (i, 0)),
            fm(N_IDX_HEADS),
            pl.BlockSpec((tm, GLA_IN_W), lambda i: (i, 0)),
            pl.BlockSpec((tm // CHUNK, GLA_K_W), lambda i: (i, 0)),
        ],
        out_shape=[
            jax.ShapeDtypeStruct((t, Z_WIDTH), bf16),
            jax.ShapeDtypeStruct((batch, ATTN_W, s), bf16),
            jax.ShapeDtypeStruct((batch, ATTN_W, s), bf16),
            jax.ShapeDtypeStruct((batch, IDX_Q_W, s), bf16),
            jax.ShapeDtypeStruct((t, IDX_HEAD_DIM), bf16),
            jax.ShapeDtypeStruct((batch, N_IDX_HEADS, s), f32),
            jax.ShapeDtypeStruct((t, GLA_IN_W), bf16),
            jax.ShapeDtypeStruct((t // CHUNK, GLA_K_W), f32),
        ],
        compiler_params=_cparams(("parallel",)),
        name="proj",
    )(x, g, w_main, w_t, w_g, w_small, w2, gb)


def _dsa_kernel(qT_ref, iqT_ref, iwT_ref, tab_ref, k_ref, vT_ref, ik_ref, o_ref,
                ibuf, qTm, s_ref, bias_ref, m_ref, l_ref, alpha_ref, acc_ref, thr_ref,
                ext_ref, pending_ref, finish_ref, *, seq_len):
    qb = DSA_BLOCK
    kt_rows = DSA_BLOCK
    i = pl.program_id(1)
    n_tiles = i + 1

    rowid = lax.broadcasted_iota(jnp.int32, (LANES, qb), 0)
    for h in range(N_ATTN_HEADS):
        pair = qT_ref[(h // 2) * LANES:(h // 2 + 1) * LANES, :]
        keep = (rowid < ATTN_HEAD_DIM) if h % 2 == 0 else (rowid >= ATTN_HEAD_DIM)
        qTm[h] = jnp.where(keep, pair, jnp.zeros_like(pair))
    l_ref[...] = jnp.zeros_like(l_ref)
    acc_ref[...] = jnp.zeros_like(acc_ref)

    def fold(x, op):
        return op(x.reshape(kt_rows // SUBLANES, SUBLANES, qb), axis=0)

    def idx_tile(kt, diagonal, stats):
        r0 = pl.multiple_of(kt * kt_rows, kt_rows)
        ki_t = ik_ref[pl.ds(r0, kt_rows), :]
        sc = jnp.zeros((kt_rows, qb), f32)
        for h in range(N_IDX_HEADS):
            s = jnp.dot(ki_t, iqT_ref[h * IDX_HEAD_DIM:(h + 1) * IDX_HEAD_DIM, :],
                        preferred_element_type=f32)
            sc = sc + iwT_ref[h:h + 1, :] * jnp.maximum(s, 0.0)
        sc_hi = sc_lo = sc_0 = sc
        if diagonal:
            kc = lax.broadcasted_iota(jnp.int32, (kt_rows, qb), 0) // CHUNK
            qc = lax.broadcasted_iota(jnp.int32, (kt_rows, qb), 1) // CHUNK
            adm = kc <= qc
            sc_hi = jnp.where(adm, sc, -jnp.inf)
            sc_lo = jnp.where(adm, sc, jnp.inf)
            sc_0 = jnp.where(adm, sc, 0.0)
        ibuf[pl.ds(r0, kt_rows), :] = sc_hi
        mx8, mn8, s8, ss8 = stats
        return (jnp.maximum(mx8, fold(sc_hi, jnp.max)), jnp.minimum(mn8, fold(sc_lo, jnp.min)),
                s8 + fold(sc_0, jnp.sum), ss8 + fold(sc_0 * sc_0, jnp.sum))

    stats0 = (jnp.full((SUBLANES, qb), -jnp.inf, f32), jnp.full((SUBLANES, qb), jnp.inf, f32),
              jnp.zeros((SUBLANES, qb), f32), jnp.zeros((SUBLANES, qb), f32))
    odd = i % 2
    stats = lax.cond(odd == 1, lambda st: idx_tile(0, False, st), lambda st: st, stats0)
    stats = lax.fori_loop(
        0, i // 2,
        lambda j, st: idx_tile(odd + 2 * j + 1, False, idx_tile(odd + 2 * j, False, st)),
        stats)
    mx8, mn8, s8, ss8 = idx_tile(i, True, stats)
    col_max = jnp.max(mx8, axis=0, keepdims=True)
    col_min = jnp.min(mn8, axis=0, keepdims=True)
    col_sum = jnp.sum(s8, axis=0, keepdims=True)
    col_ssq = jnp.sum(ss8, axis=0, keepdims=True)

    def count(pred):
        def body(r, acc):
            r0 = pl.multiple_of(r * kt_rows, kt_rows)
            tile = ibuf[pl.ds(r0, kt_rows), :]
            rows = r0 + lax.broadcasted_iota(jnp.int32, (kt_rows, qb), 0)
            ind = jnp.where(pred(tile, rows), 1.0, 0.0)
            part = ind.reshape(kt_rows // SUBLANES, SUBLANES, qb)
            while part.shape[0] > 1:
                half = part.shape[0] // 2
                part = part[:half] + part[half:]
            return acc + part[0]
        acc = jnp.zeros((SUBLANES, qb), f32)
        acc = lax.cond(i % 2 == 0, lambda a: body(0, a), lambda a: a, acc)
        first = 1 - i % 2
        acc = lax.fori_loop(
            0, (i + 1) // 2,
            lambda j, a: body(first + 2 * j + 1, body(first + 2 * j, a)), acc)
        return jnp.sum(acc, axis=0, keepdims=True)

    def resolve_ties(tv, need, mult):
        is_tie = tv == tv
        n_keys = n_tiles * kt_rows

        def tie_cond(st):
            return st["active"] > 0

        def tie_body(st):
            plo, phi, flo, fhi = st["plo"], st["phi"], st["flo"], st["fhi"]
            span = phi - plo
            est = ((need - flo) / jnp.maximum(fhi - flo, 1.0) * span.astype(f32)).astype(jnp.int32)
            pick = jnp.where(st["bisect"] == 2, span // 2, est)
            cand = plo + jnp.clip(pick, 1, jnp.maximum(span - 1, 1))
            f = count(lambda tile, rows: jnp.where(tile == tv, rows, seq_len) < cand)
            lower = f < need
            exact = f == need
            plo = jnp.where(lower, cand, jnp.where(exact, cand - 1, plo))
            phi = jnp.where(lower, phi, cand)
            flo = jnp.where(lower, f, flo)
            fhi = jnp.where(lower, fhi, f)
            unsplit = jnp.where(is_tie & (phi - plo > 1), 1, 0)
            return dict(active=jnp.max(unsplit), bisect=(st["bisect"] + 1) % 3,
                        plo=plo, phi=phi, flo=flo, fhi=fhi)

        split = lax.while_loop(tie_cond, tie_body, dict(
            active=jnp.int32(1), bisect=jnp.int32(0),
            plo=jnp.zeros((1, qb), jnp.int32), phi=jnp.zeros((1, qb), jnp.int32) + n_keys,
            flo=jnp.zeros((1, qb), f32), fhi=mult))["phi"]

        def drop_body(r, carry):
            r0 = pl.multiple_of(r * kt_rows, kt_rows)
            tile = ibuf[pl.ds(r0, kt_rows), :]
            rows = r0 + lax.broadcasted_iota(jnp.int32, (kt_rows, qb), 0)
            dropped = jnp.where(tile == tv, rows, -1) >= split
            ibuf[pl.ds(r0, kt_rows), :] = jnp.where(dropped, -jnp.inf, tile)
            return carry

        lax.fori_loop(0, n_tiles, drop_body, 0)

    qpos = i * qb + lax.broadcasted_iota(jnp.int32, (1, qb), 1)
    n_adm = (qpos // CHUNK + 1) * CHUNK
    select_all = n_adm <= TOPK_MAX
    kf = float(TOPK_MAX)

    thr_ref[...] = jnp.full((1, qb), F32_LOWEST, f32)
    pending_ref[0] = jnp.int32(0)

    @pl.when(n_tiles * kt_rows > TOPK_MAX)
    def _():
        n_f = n_adm.astype(f32)
        mean = col_sum / n_f
        sigma = jnp.sqrt(jnp.maximum(col_ssq / n_f - mean * mean, 0.0))
        spacing = tab_ref[1:2, :] * sigma

        def inside(c, lo, hi):
            return (c > lo) & (c < hi)

        c0 = mean + tab_ref[0:1, :] * sigma
        c0 = jnp.where(inside(c0, col_min, col_max), c0, 0.5 * col_min + 0.5 * col_max)
        zero = jnp.zeros((1, qb), f32)
        one = jnp.ones((1, qb), f32)
        state0 = dict(it=jnp.int32(0), active=jnp.int32(1), c=c0, lo=col_min, hi=col_max,
                      glo=n_f, ghi=zero, tlo=zero, thi=zero, boost=one, side=zero,
                      thr=jnp.full((1, qb), F32_LOWEST, f32),
                      open=jnp.where(select_all, 0.0, 1.0), tie=zero, kind=zero, emin=zero)

        def search_cond(st):
            return (st["it"] < SEARCH_MAX_ITERS) & (st["active"] > 0)

        def search_body(st):
            c, lo, hi = st["c"], st["lo"], st["hi"]
            g = count(lambda tile, rows: tile >= c)
            hit = (g == kf) & (st["open"] > 0.0)
            thr = jnp.where(hit, c, st["thr"])
            still = jnp.where(hit, 0.0, st["open"])
            above = g > kf
            lo = jnp.where(above, c, lo)
            hi = jnp.where(above, hi, c)
            glo = jnp.where(above, g, st["glo"])
            ghi = jnp.where(above, st["ghi"], g)
            tlo = jnp.where(above, 1.0, st["tlo"])
            thi = jnp.where(above, st["thi"], 1.0)
            side = jnp.where(above, 1.0, -1.0)
            repeat = side == st["side"]
            boost = jnp.where(repeat, 2.0 * st["boost"], 1.0)
            mid = 0.5 * lo + 0.5 * hi
            stale = 1.0 / jnp.minimum(boost, 256.0)
            w_lo = (glo - kf - 0.5) * jnp.where(above, 1.0, stale)
            w_hi = (kf + 0.5 - ghi) * jnp.where(above, stale, 1.0)
            c_two = lo + w_lo / (w_lo + w_hi) * (hi - lo)
            step = 2.0 * boost * spacing
            c_one = jnp.where(tlo > 0.0, lo + jnp.maximum(glo - kf, 1.0) * step,
                              hi - jnp.maximum(kf - ghi, 1.0) * step)
            c_one = jnp.where(inside(c_one, lo, hi), c_one, mid)
            both = (tlo > 0.0) & (thi > 0.0)
            c_new = jnp.where(both, c_two, c_one)
            c_new = jnp.where(both & (lo < 0.0) & (hi > 0.0), 0.0, c_new)
            c_new = jnp.where((lo == 0.0) & (hi > F32_TINY), F32_TINY, c_new)
            movable = inside(c_new, lo, hi) & ~((lo == 0.0) & (hi <= F32_TINY))
            missed = (still > 0.0) & (((st["kind"] == 1.0) & above) |
                                      ((st["kind"] == 2.0) & (g < kf)))
            lo = jnp.where(missed & (st["kind"] == 2.0), st["emin"], lo)
            closed = (~movable & both & (still > 0.0)) | missed
            tie = jnp.where(closed, 1.0, st["tie"])
            still = jnp.where(closed, 0.0, still)
            searching = jnp.where(movable, still, 0.0)

            from_hi = both & (kf - ghi == 1.0)
            from_lo = both & (glo - kf == 1.0) & ~from_hi
            ready = jnp.where(searching > 0.0, jnp.where(from_hi | from_lo, 1.0, 2.0), 0.0)
            code = jnp.max(ready)
            ext_ref[...] = jnp.zeros_like(ext_ref)
            finish_ref[0] = jnp.int32(0)

            @pl.when((code == 1.0) & (st["it"] + 1 >= SEARCH_FINISH_FROM))
            def _():
                def ext_body(r, carry):
                    r0 = pl.multiple_of(r * kt_rows, kt_rows)
                    tile = ibuf[pl.ds(r0, kt_rows), :]
                    below = fold(jnp.where(tile < hi, tile, -jnp.inf), jnp.max)
                    above_lo = fold(jnp.where(tile >= lo, tile, jnp.inf), jnp.min)
                    return jnp.maximum(carry[0], below), jnp.minimum(carry[1], above_lo)
                mx8, mn8 = lax.fori_loop(0, n_tiles, ext_body,
                                         (jnp.full((SUBLANES, qb), -jnp.inf, f32),
                                          jnp.full((SUBLANES, qb), jnp.inf, f32)))
                ext_ref[0:1, :] = jnp.max(mx8, axis=0, keepdims=True)
                ext_ref[1:2, :] = jnp.min(mn8, axis=0, keepdims=True)
                finish_ref[0] = jnp.int32(1)

            finishing = finish_ref[0] > 0
            emax = ext_ref[0:1, :]
            emin = ext_ref[1:2, :]
            ebits = lax.bitcast_convert_type(emin, jnp.int32)
            next_up = lax.bitcast_convert_type(ebits + jnp.where(emin > 0.0, 1, -1), f32)
            next_up = jnp.where(emin == 0.0, F32_TINY, next_up)
            kind = jnp.where(finishing & (searching > 0.0),
                             jnp.where(from_hi, 1.0, jnp.where(from_lo, 2.0, 0.0)), 0.0)
            c_new = jnp.where(kind == 1.0, emax, jnp.where(kind == 2.0, next_up, c_new))
            return dict(it=st["it"] + 1, active=(code > 0.0).astype(jnp.int32),
                        c=c_new, lo=lo, hi=hi, glo=glo, ghi=ghi, tlo=tlo, thi=thi,
                        boost=boost, side=side, thr=thr, open=still, tie=tie,
                        kind=kind, emin=emin)

        final = lax.while_loop(search_cond, search_body, state0)
        tied = final["tie"] > 0.0
        thr_ref[...] = jnp.where(tied, final["lo"], final["thr"])
        pending_ref[0] = (jnp.max(final["open"]) > 0.0).astype(jnp.int32)

        @pl.when(jnp.max(final["tie"]) > 0.0)
        def _():
            resolve_ties(jnp.where(tied, final["lo"], jnp.nan), kf - final["ghi"],
                         final["glo"] - final["ghi"])

    @pl.when(pending_ref[0] > 0)
    def _():
        def key_to_f32(key):
            bits = jnp.where(key < 0, key ^ jnp.int32(0x7FFFFFFF), key)
            return lax.bitcast_convert_type(bits, f32)

        def bit_body(t, prefix):
            step = lax.shift_left(jnp.int32(1), jnp.int32(31) - t)
            cand = prefix + step
            cand_f = key_to_f32(cand)
            cnt = count(lambda tile, rows: tile >= cand_f)
            return jnp.where(cnt >= kf, cand, prefix)

        prefix = lax.fori_loop(0, 32, bit_body, jnp.full((1, qb), INT32_MIN, jnp.int32))
        thr = jnp.where(select_all, F32_LOWEST, key_to_f32(prefix))
        thr_ref[...] = thr

        cnt_ge = count(lambda tile, rows: tile >= thr)
        excess = jnp.where(select_all, 0.0, cnt_ge - kf)

        @pl.when(jnp.max(excess) > 0.0)
        def _():
            cnt_gt = count(lambda tile, rows: tile > thr)
            resolve_ties(jnp.where(excess > 0.0, thr, jnp.nan), kf - cnt_gt, cnt_ge - cnt_gt)

    thr = thr_ref[...]
    ones_rows = jnp.ones((2 * SUBLANES, kt_rows), bf16)

    m_ref[1] = jnp.full(m_ref.shape[1:], NEG_BIG, f32)

    def logits_stage(kt, slot):
        r0 = pl.multiple_of(kt * kt_rows, kt_rows)
        bias_ref[...] = jnp.where(ibuf[pl.ds(r0, kt_rows), :] >= thr, 0.0, NEG_BIG)
        for h in range(N_ATTN_HEADS):
            k_pair = k_ref[pl.ds(r0, kt_rows), (h // 2) * LANES:(h // 2 + 1) * LANES]
            s_ref[slot, h] = jnp.dot(k_pair, qTm[h], preferred_element_type=f32) + bias_ref[...]
            m_old = m_ref[1 - slot, h]
            m_new = jnp.maximum(m_old, jnp.max(s_ref[slot, h], axis=0, keepdims=True))
            alpha_ref[slot, h] = jnp.exp2(m_old - m_new)
            m_ref[slot, h] = m_new

    def values_stage(kt, slot):
        r0 = pl.multiple_of(kt * kt_rows, kt_rows)
        for h in range(N_ATTN_HEADS):
            p = jnp.exp2(s_ref[slot, h] - m_ref[slot, h]).astype(bf16)
            hs = slice(h * ATTN_HEAD_DIM, (h + 1) * ATTN_HEAD_DIM)
            lhs = jnp.concatenate([vT_ref[hs, pl.ds(r0, kt_rows)], ones_rows], axis=0)
            pv = jnp.dot(lhs, p, preferred_element_type=f32)
            alpha = alpha_ref[slot, h]
            acc_ref[hs, :] = alpha * acc_ref[hs, :] + pv[:ATTN_HEAD_DIM]
            l_ref[h] = alpha * l_ref[h] + pv[ATTN_HEAD_DIM:ATTN_HEAD_DIM + 1]

    def att_body(j, carry):
        kt = 2 * j
        logits_stage(kt + 1, 1)
        values_stage(kt, 0)
        logits_stage(kt + 2, 0)
        values_stage(kt + 1, 1)
        return carry

    logits_stage(0, 0)
    lax.fori_loop(0, i // 2, att_body, 0)

    @pl.when(i % 2 == 1)
    def _():
        logits_stage(i, 1)
        values_stage(i - 1, 0)
        values_stage(i, 1)

    @pl.when(i % 2 == 0)
    def _():
        values_stage(i, 0)

    for h in range(N_ATTN_HEADS):
        hs = slice(h * ATTN_HEAD_DIM, (h + 1) * ATTN_HEAD_DIM)
        acc_ref[hs, :] = acc_ref[hs, :] / l_ref[h]
    o_ref[...] = acc_ref[...].T.astype(o_ref.dtype)


def _search_hints(seq_len):
    n = (np.arange(seq_len) // CHUNK + 1) * CHUNK
    frac = np.minimum(TOPK_MAX / n, 0.5)
    nd = statistics.NormalDist()
    z_of = {f: nd.inv_cdf(1.0 - f) for f in np.unique(frac)}
    z = np.array([z_of[f] for f in frac])
    dens = n * np.exp(-0.5 * z * z) / np.sqrt(2.0 * np.pi)
    return jnp.asarray(np.stack([z, 1.0 / dens]), f32)


def _dsa(qT, iqT, iwT, z3, vT, ik3):
    b, _, s = qT.shape
    qb = DSA_BLOCK
    return pl.pallas_call(
        functools.partial(_dsa_kernel, seq_len=s),
        grid=(b, s // qb),
        in_specs=[
            pl.BlockSpec((None, ATTN_W, qb), lambda bi, i: (bi, 0, i)),
            pl.BlockSpec((None, IDX_Q_W, qb), lambda bi, i: (bi, 0, i)),
            pl.BlockSpec((None, N_IDX_HEADS, qb), lambda bi, i: (bi, 0, i)),
            pl.BlockSpec((2, qb), lambda bi, i: (0, i)),
            pl.BlockSpec((None, s, COL_TILE), lambda bi, i: (bi, 0, Z_AK)),
            pl.BlockSpec((None, ATTN_W, s), lambda bi, i: (bi, 0, 0)),
            pl.BlockSpec((None, s, IDX_HEAD_DIM), lambda bi, i: (bi, 0, 0)),
        ],
        out_specs=pl.BlockSpec((None, qb, ATTN_W), lambda bi, i: (bi, i, 0)),
        out_shape=jax.ShapeDtypeStruct((b, s, ATTN_W), bf16),
        scratch_shapes=[
            pltpu.VMEM((s, qb), f32),
            pltpu.VMEM((N_ATTN_HEADS, LANES, qb), bf16),
            pltpu.VMEM((2, N_ATTN_HEADS, qb, qb), f32),
            pltpu.VMEM((qb, qb), f32),
            pltpu.VMEM((2, N_ATTN_HEADS, 1, qb), f32),
            pltpu.VMEM((N_ATTN_HEADS, 1, qb), f32),
            pltpu.VMEM((2, N_ATTN_HEADS, 1, qb), f32),
            pltpu.VMEM((ATTN_W, qb), f32),
            pltpu.VMEM((1, qb), f32),
            pltpu.VMEM((2, qb), f32),
            pltpu.SMEM((1,), jnp.int32),
            pltpu.SMEM((1,), jnp.int32),
        ],
        compiler_params=_cparams(("parallel", "arbitrary")),
        name="dsa",
    )(qT, iqT, iwT, _search_hints(s), z3, vT, ik3)


def _gla_kernel(gin_ref, dec_ref, v_ref, r_ref, gn_ref, o_ref, state_ref):
    c = CHUNK
    n_batch = gin_ref.shape[0]

    @pl.when(pl.program_id(0) == 0)
    def _():
        state_ref[...] = jnp.zeros_like(state_ref)

    row = lax.broadcasted_iota(jnp.int32, (c, c), 0)
    col = lax.broadcasted_iota(jnp.int32, (c, c), 1)
    causal = row >= col
    lane = lax.broadcasted_iota(jnp.int32, (c, LANES), 1)
    half = (lane < GLA_HEAD_K, lane >= GLA_HEAD_K)
    contract_last = (((1,), (1,)), ((), ()))
    contract_first = (((0,), (0,)), ((), ()))
    zero = jnp.zeros((c, LANES), bf16)

    for ci, bi in [(ci, bi) for ci in range(GLA_TILE // c) for bi in range(n_batch)]:
        rows = slice(ci * c, (ci + 1) * c)
        decay = dec_ref[bi, ci:ci + 1, :]
        for pr in range(N_GLA_HEADS // 2):
            ps = slice(pr * LANES, (pr + 1) * LANES)
            part = lambda g: gin_ref[bi, rows, g * GLA_K_W + pr * LANES:
                                     g * GLA_K_W + (pr + 1) * LANES]
            q_in, q_mid, k_mid, k_out = part(0), part(1), part(2), part(3)
            st = state_ref[bi, pr]
            st_b = st.astype(bf16)
            upd = jnp.zeros_like(st)
            for hh in range(2):
                h = pr * 2 + hh
                vs = slice(h * GLA_HEAD_V, (h + 1) * GLA_HEAD_V)
                v_h = v_ref[bi, rows, vs]
                attn = lax.dot_general(jnp.where(half[hh], q_mid, zero), k_mid, contract_last,
                                       preferred_element_type=f32)
                attn = jnp.where(causal, attn, 0.0).astype(bf16)
                o = lax.dot_general(jnp.where(half[hh], q_in, zero), st_b, contract_last,
                                    preferred_element_type=f32)
                o = o + jnp.dot(attn, v_h, preferred_element_type=f32)
                upd = upd + lax.dot_general(v_h, jnp.where(half[hh], k_out, zero),
                                            contract_first, preferred_element_type=f32)
                y = _rms(o, gn_ref[...])
                r = r_ref[bi, rows, vs].astype(f32)
                o_ref[bi, rows, vs] = (y * r * _sigmoid(r)).astype(o_ref.dtype)
            state_ref[bi, pr] = decay[:, ps] * st + upd


def _gla(gin3, dec3, z3, gn):
    b, s, _ = z3.shape
    tb = GLA_TILE
    return pl.pallas_call(
        _gla_kernel,
        grid=(s // tb,),
        in_specs=[
            pl.BlockSpec((b, tb, GLA_IN_W), lambda j: (0, j, 0)),
            pl.BlockSpec((b, tb // CHUNK, GLA_K_W), lambda j: (0, j, 0)),
            pl.BlockSpec((b, tb, COL_TILE), lambda j: (0, j, Z_GV)),
            pl.BlockSpec((b, tb, COL_TILE), lambda j: (0, j, Z_GR)),
            pl.BlockSpec((1, GLA_HEAD_V), lambda j: (0, 0)),
        ],
        out_specs=pl.BlockSpec((b, tb, GLA_V_W), lambda j: (0, j, 0)),
        out_shape=jax.ShapeDtypeStruct((b, s, GLA_V_W), bf16),
        scratch_shapes=[pltpu.VMEM((b, N_GLA_HEADS // 2, GLA_HEAD_V, LANES), f32)],
        compiler_params=_cparams(("arbitrary",)),
        name="gla",
    )(gin3, dec3, z3, z3, gn)


def _merge_kernel(x_ref, oa_ref, ob_ref, ga_ref, gb_ref, wa_ref, wb_ref, wo_ref, o_ref):
    ya = jnp.dot(oa_ref[...], wa_ref[...], preferred_element_type=f32)
    yb = jnp.dot(ob_ref[...], wb_ref[...], preferred_element_type=f32)
    mg = _sigmoid(ga_ref[...].astype(f32)) * ya + _sigmoid(gb_ref[...].astype(f32)) * yb
    o_ref[...] = x_ref[...] + jnp.dot(mg.astype(bf16), wo_ref[...], preferred_element_type=f32)


def _merge(x, oa, ob, z, wa, wb, wo):
    t = x.shape[0]
    tm = TOKEN_TILE
    return pl.pallas_call(
        _merge_kernel,
        grid=(t // tm,),
        in_specs=[
            pl.BlockSpec((tm, D_MODEL), lambda i: (i, 0)),
            pl.BlockSpec((tm, ATTN_W), lambda i: (i, 0)),
            pl.BlockSpec((tm, GLA_V_W), lambda i: (i, 0)),
            pl.BlockSpec((tm, D_MODEL), lambda i: (i, Z_MGA)),
            pl.BlockSpec((tm, D_MODEL), lambda i: (i, Z_MGB)),
            pl.BlockSpec((ATTN_W, D_MODEL), lambda i: (0, 0)),
            pl.BlockSpec((GLA_V_W, D_MODEL), lambda i: (0, 0)),
            pl.BlockSpec((D_MODEL, D_MODEL), lambda i: (0, 0)),
        ],
        out_specs=pl.BlockSpec((tm, D_MODEL), lambda i: (i, 0)),
        out_shape=jax.ShapeDtypeStruct((t, D_MODEL), f32),
        compiler_params=_cparams(("parallel",)),
        name="merge",
    )(x, oa, ob, z, z, wa, wb, wo)


def _ple_kernel(x_ref, p_ref, g_ref, wg_ref, wp_ref, gf_ref, o_ref, *, final):
    x = x_ref[...]
    h = _rms(x, g_ref[...]).astype(bf16)
    gate = _sigmoid(jnp.dot(h, wg_ref[...], preferred_element_type=f32))
    e = jnp.dot(p_ref[...].astype(bf16), wp_ref[...], preferred_element_type=f32)
    y = x + gate * e
    if final:
        y = _rms(y, gf_ref[...])
    o_ref[...] = y


def _ple(x, p, g, wg, wp, gf, final):
    t = x.shape[0]
    tm = TOKEN_TILE
    return pl.pallas_call(
        functools.partial(_ple_kernel, final=final),
        grid=(t // tm,),
        in_specs=[
            pl.BlockSpec((tm, D_MODEL), lambda i: (i, 0)),
            pl.BlockSpec((tm, PLE_DIM), lambda i: (i, 0)),
            pl.BlockSpec((1, D_MODEL), lambda i: (0, 0)),
            pl.BlockSpec((D_MODEL, D_MODEL), lambda i: (0, 0)),
            pl.BlockSpec((PLE_DIM, D_MODEL), lambda i: (0, 0)),
            pl.BlockSpec((1, D_MODEL), lambda i: (0, 0)),
        ],
        out_specs=pl.BlockSpec((tm, D_MODEL), lambda i: (i, 0)),
        out_shape=jax.ShapeDtypeStruct((t, D_MODEL), f32),
        compiler_params=_cparams(("parallel",)),
        name="ple",
    )(x, p, g, wg, wp, gf)


def _split_w_in(w_in):
    cuts = np.cumsum(np.array(SPLIT_SIZES))[:-1].tolist()
    aq, ak, av, iq, ik, iw, gq, gk, gv, gr, ga, mga, mgb = jnp.split(w_in, cuts, axis=-1)
    att_scale = ATTN_HEAD_DIM ** -0.5 * float(np.log2(np.e))
    idx_scale = IDX_HEAD_DIM ** -0.5 * N_IDX_HEADS ** -0.5
    main = jnp.concatenate([mga, mgb, ak, gv, gr], axis=-1)
    feature_major = jnp.concatenate([aq * att_scale, av, iq], axis=-1)
    gla_qk = jnp.concatenate([gq * (GLA_HEAD_K ** -0.5), gk], axis=-1)
    zeros = lambda n: jnp.zeros(w_in.shape[:-1] + (n,), w_in.dtype)
    small = jnp.concatenate(
        [ik, zeros(LANES - IDX_HEAD_DIM), iw * idx_scale, ga,
         zeros(LANES - N_IDX_HEADS - GLA_GATE_RANK)], axis=-1)
    return main.astype(bf16), feature_major.astype(bf16), gla_qk.astype(bf16), small.astype(bf16)


def kernel(x, p, w_in, gla_gate_w2, gla_gate_b, gla_norm, w_branch_a, w_branch_b, w_out,
           norm_ff1, norm_mix, norm_ff2, norm_ple, ff1_w_gate, ff1_w_up, ff1_w_down,
           ff2_w_gate, ff2_w_up, ff2_w_down, ple_w_proj, ple_w_gate, norm_final):
    b, s, d = x.shape
    t = b * s
    depth = w_in.shape[0]
    w_main, w_fm, w_gqk, w_small = _split_w_in(w_in)
    cast = lambda w: w.astype(bf16)
    ff1 = (cast(ff1_w_gate), cast(ff1_w_up), cast(ff1_w_down))
    ff2 = (cast(ff2_w_gate), cast(ff2_w_up), cast(ff2_w_down))
    wa, wb, wo = cast(w_branch_a), cast(w_branch_b), cast(w_out)
    wpg, wpp = cast(ple_w_gate), cast(ple_w_proj)
    row = lambda v: v.reshape(1, -1)

    xf = x.reshape(t, d)
    for l in range(depth):
        xf = _ffn(xf, row(norm_ff1[l]), ff1[0][l], ff1[1][l], ff1[2][l])

        z, qT, vT, iqT, ik, iwT, gin, dec = _proj(
            xf, row(norm_mix[l]), w_main[l], w_fm[l], w_gqk[l], w_small[l],
            gla_gate_w2[l], row(gla_gate_b[l]), b)
        z3 = z.reshape(b, s, Z_WIDTH)
        oa = _dsa(qT, iqT, iwT, z3, vT, ik.reshape(b, s, IDX_HEAD_DIM))
        ob = _gla(gin.reshape(b, s, GLA_IN_W), dec.reshape(b, s // CHUNK, GLA_K_W), z3,
                  row(gla_norm[l]))
        xf = _merge(xf, oa.reshape(t, ATTN_W), ob.reshape(t, GLA_V_W), z, wa[l], wb[l], wo[l])

        xf = _ffn(xf, row(norm_ff2[l]), ff2[0][l], ff2[1][l], ff2[2][l])
        xf = _ple(xf, p[l].reshape(t, PLE_DIM), row(norm_ple[l]), wpg[l], wpp[l],
                  row(norm_final), final=(l == depth - 1))
    return xf.reshape(b, s, d)
```

```python
import functools
import statistics

import jax
import jax.numpy as jnp
import numpy as np
from jax import lax
from jax.experimental import pallas as pl
from jax.experimental.pallas import tpu as pltpu

D_MODEL = 1024
D_FF = 2816
PLE_DIM = 256
DEPTH = 2
EPS = 1e-6

CHUNK = 64
N_ATTN_HEADS = 8
ATTN_HEAD_DIM = 64
N_IDX_HEADS = 8
IDX_HEAD_DIM = 64
TOPK_MAX = 256
N_GLA_HEADS = 4
GLA_HEAD_K = 64
GLA_HEAD_V = 128
GLA_GATE_RANK = 16
GLA_GATE_TAU = 16.0

ATTN_W = N_ATTN_HEADS * ATTN_HEAD_DIM
IDX_Q_W = N_IDX_HEADS * IDX_HEAD_DIM
GLA_K_W = N_GLA_HEADS * GLA_HEAD_K
GLA_V_W = N_GLA_HEADS * GLA_HEAD_V
SPLIT_SIZES = (ATTN_W, ATTN_W, ATTN_W, IDX_Q_W, IDX_HEAD_DIM, N_IDX_HEADS,
               GLA_K_W, GLA_K_W, GLA_V_W, GLA_V_W, GLA_GATE_RANK, D_MODEL, D_MODEL)

LANES = 128
SUBLANES = 8
VMEM_LIMIT_BYTES = 56 * 1024 * 1024

COL_TILE = 512
Z_MGA, Z_MGB = 0, 1
Z_AK, Z_GV, Z_GR = 4, 5, 6
Z_TILES = 7
Z_WIDTH = Z_TILES * COL_TILE
ZT_TILES = 3
SM_IW0 = 0
SM_GA0 = N_IDX_HEADS

TOKEN_TILE = 512
FF_TILE = 1408
DSA_BLOCK = 256
GLA_TILE = 512
GLA_IN_W = 4 * GLA_K_W
SEARCH_MAX_ITERS = 24
SEARCH_FINISH_FROM = 4

NEG_BIG = -1e30
F32_LOWEST = float(np.finfo(np.float32).min)
F32_TINY = float(np.finfo(np.float32).tiny)
INT32_MIN = int(np.iinfo(np.int32).min)

f32 = jnp.float32
bf16 = jnp.bfloat16


def _rms(x, g):
    return x * lax.rsqrt(jnp.mean(x * x, axis=-1, keepdims=True) + EPS) * g


def _sigmoid(x):
    return 1.0 / (1.0 + jnp.exp(-x))


def _cparams(sem):
    return pltpu.CompilerParams(dimension_semantics=sem, vmem_limit_bytes=VMEM_LIMIT_BYTES)


def _ffn_kernel(x_ref, g_ref, wg_ref, wu_ref, wd_ref, o_ref, h_ref, acc_ref):
    f = pl.program_id(1)

    @pl.when(f == 0)
    def _():
        h_ref[...] = _rms(x_ref[...], g_ref[...]).astype(bf16)
        acc_ref[...] = jnp.zeros_like(acc_ref)

    h = h_ref[...]
    gate = jnp.dot(h, wg_ref[...], preferred_element_type=f32)
    up = jnp.dot(h, wu_ref[...], preferred_element_type=f32)
    a = (gate * _sigmoid(gate) * up).astype(bf16)
    acc_ref[...] += jnp.dot(a, wd_ref[...], preferred_element_type=f32)

    @pl.when(f == pl.num_programs(1) - 1)
    def _():
        o_ref[...] = x_ref[...] + 0.5 * acc_ref[...]


def _ffn(x, g, wg, wu, wd):
    t = x.shape[0]
    return pl.pallas_call(
        _ffn_kernel,
        grid=(t // TOKEN_TILE, D_FF // FF_TILE),
        in_specs=[
            pl.BlockSpec((TOKEN_TILE, D_MODEL), lambda i, f: (i, 0)),
            pl.BlockSpec((1, D_MODEL), lambda i, f: (0, 0)),
            pl.BlockSpec((D_MODEL, FF_TILE), lambda i, f: (0, f)),
            pl.BlockSpec((D_MODEL, FF_TILE), lambda i, f: (0, f)),
            pl.BlockSpec((FF_TILE, D_MODEL), lambda i, f: (f, 0)),
        ],
        out_specs=pl.BlockSpec((TOKEN_TILE, D_MODEL), lambda i, f: (i, 0)),
        out_shape=jax.ShapeDtypeStruct((t, D_MODEL), f32),
        scratch_shapes=[pltpu.VMEM((TOKEN_TILE, D_MODEL), bf16),
                        pltpu.VMEM((TOKEN_TILE, D_MODEL), f32)],
        compiler_params=_cparams(("parallel", "arbitrary")),
        name="ffn",
    )(x, g, wg, wu, wd)


def _gla_operands(q, k, ga, w2, gb):
    n = q.shape[0]
    c = CHUNK
    glin = jnp.dot(ga, w2, preferred_element_type=f32, precision=lax.Precision.HIGHEST) + gb
    logg = (jnp.minimum(glin, 0.0) - jnp.log(1.0 + jnp.exp(-jnp.abs(glin)))) / GLA_GATE_TAU
    row = lax.broadcasted_iota(jnp.int32, (n, n), 0)
    col = lax.broadcasted_iota(jnp.int32, (n, n), 1)
    tril = jnp.where((row // c == col // c) & (col <= row), 1.0, 0.0).astype(bf16)
    hi = logg.astype(bf16)
    lo = (logg - hi.astype(f32)).astype(bf16)
    bcum = (jnp.dot(tril, hi, preferred_element_type=f32)
            + jnp.dot(tril, lo, preferred_element_type=f32))
    by_chunk = bcum.reshape(n // c, c, GLA_K_W)
    spread = lambda r: jnp.broadcast_to(r, (n // c, c, GLA_K_W)).reshape(n, GLA_K_W)
    b_last = by_chunk[:, c - 1:c, :]
    b_mid = spread(by_chunk[:, c // 2 - 1:c // 2, :])
    operands = jnp.concatenate(
        [q * jnp.exp(bcum), q * jnp.exp(bcum - b_mid), k * jnp.exp(b_mid - bcum),
         k * jnp.exp(spread(b_last) - bcum)], axis=1)
    return operands, jnp.exp(b_last.reshape(n // c, GLA_K_W))


def _proj_kernel(x_ref, g_ref, w_ref, wt_ref, wg_ref, ws_ref, w2_ref, gb_ref,
                 z_ref, qT_ref, vT_ref, iqT_ref, ik_ref, iwT_ref, gin_ref, dec_ref):
    h = _rms(x_ref[...], g_ref[...]).astype(bf16)
    for j in range(Z_TILES):
        cols = slice(j * COL_TILE, (j + 1) * COL_TILE)
        z_ref[:, cols] = jnp.dot(h, w_ref[:, cols], preferred_element_type=f32).astype(bf16)
    for j, out_ref in enumerate((qT_ref, vT_ref, iqT_ref)):
        cols = slice(j * COL_TILE, (j + 1) * COL_TILE)
        zt = jnp.dot(h, wt_ref[:, cols], preferred_element_type=f32)
        out_ref[...] = zt.T.astype(bf16)
    s = jnp.dot(h, ws_ref[...], preferred_element_type=f32)
    ik_ref[...] = s[:, :IDX_HEAD_DIM].astype(bf16)
    sm = s[:, LANES:]
    iwT_ref[...] = sm.T[SM_IW0:SM_IW0 + N_IDX_HEADS, :]
    gqk = jnp.dot(h, wg_ref[...], preferred_element_type=f32)
    operands, decay = _gla_operands(gqk[:, :GLA_K_W], gqk[:, GLA_K_W:],
                                    sm[:, SM_GA0:SM_GA0 + GLA_GATE_RANK],
                                    w2_ref[...], gb_ref[...])
    gin_ref[...] = operands.astype(bf16)
    dec_ref[...] = decay


def _proj(x, g, w_main, w_t, w_g, w_small, w2, gb, batch):
    t = x.shape[0]
    tm = TOKEN_TILE
    s = t // batch
    per_b = s // tm
    fm = lambda rows: pl.BlockSpec((None, rows, tm), lambda i: (i // per_b, 0, i % per_b))
    whole = lambda a: pl.BlockSpec(a.shape, lambda i: (0, 0))
    return pl.pallas_call(
        _proj_kernel,
        grid=(t // tm,),
        in_specs=[
            pl.BlockSpec((tm, D_MODEL), lambda i: (i, 0)),
            whole(g), whole(w_main), whole(w_t), whole(w_g), whole(w_small), whole(w2), whole(gb),
        ],
        out_specs=[
            pl.BlockSpec((tm, Z_WIDTH), lambda i: (i, 0)),
            fm(ATTN_W), fm(ATTN_W), fm(IDX_Q_W),
            pl.BlockSpec((tm, IDX_HEAD_DIM), lambda i: (i, 0)),
            fm(N_IDX_HEADS),
            pl.BlockSpec((tm, GLA_IN_W), lambda i: (i, 0)),
            pl.BlockSpec((tm // CHUNK, GLA_K_W), lambda i: (i, 0)),
        ],
        out_shape=[
            jax.ShapeDtypeStruct((t, Z_WIDTH), bf16),
            jax.ShapeDtypeStruct((batch, ATTN_W, s), bf16),
            jax.ShapeDtypeStruct((batch, ATTN_W, s), bf16),
            jax.ShapeDtypeStruct((batch, IDX_Q_W, s), bf16),
            jax.ShapeDtypeStruct((t, IDX_HEAD_DIM), bf16),
            jax.ShapeDtypeStruct((batch, N_IDX_HEADS, s), f32),
            jax.ShapeDtypeStruct((t, GLA_IN_W), bf16),
            jax.ShapeDtypeStruct((t // CHUNK, GLA_K_W), f32),
        ],
        compiler_params=_cparams(("parallel",)),
        name="proj",
    )(x, g, w_main, w_t, w_g, w_small, w2, gb)


def _dsa_kernel(qT_ref, iqT_ref, iwT_ref, tab_ref, k_ref, vT_ref, ik_ref, o_ref,
                ibuf, qTm, s_ref, bias_ref, m_ref, l_ref, alpha_ref, acc_ref, thr_ref,
                ext_ref, pending_ref, finish_ref, *, seq_len):
    qb = DSA_BLOCK
    kt_rows = DSA_BLOCK
    i = pl.program_id(1)
    n_tiles = i + 1

    rowid = lax.broadcasted_iota(jnp.int32, (LANES, qb), 0)
    for h in range(N_ATTN_HEADS):
        pair = qT_ref[(h // 2) * LANES:(h // 2 + 1) * LANES, :]
        keep = (rowid < ATTN_HEAD_DIM) if h % 2 == 0 else (rowid >= ATTN_HEAD_DIM)
        qTm[h] = jnp.where(keep, pair, jnp.zeros_like(pair))
    l_ref[...] = jnp.zeros_like(l_ref)
    acc_ref[...] = jnp.zeros_like(acc_ref)

    def fold(x, op):
        return op(x.reshape(kt_rows // SUBLANES, SUBLANES, qb), axis=0)

    def idx_tile(kt, diagonal, stats):
        r0 = pl.multiple_of(kt * kt_rows, kt_rows)
        ki_t = ik_ref[pl.ds(r0, kt_rows), :]
        sc = jnp.zeros((kt_rows, qb), f32)
        for h in range(N_IDX_HEADS):
            s = jnp.dot(ki_t, iqT_ref[h * IDX_HEAD_DIM:(h + 1) * IDX_HEAD_DIM, :],
                        preferred_element_type=f32)
            sc = sc + iwT_ref[h:h + 1, :] * jnp.maximum(s, 0.0)
        sc_hi = sc_lo = sc_0 = sc
        if diagonal:
            kc = lax.broadcasted_iota(jnp.int32, (kt_rows, qb), 0) // CHUNK
            qc = lax.broadcasted_iota(jnp.int32, (kt_rows, qb), 1) // CHUNK
            adm = kc <= qc
            sc_hi = jnp.where(adm, sc, -jnp.inf)
            sc_lo = jnp.where(adm, sc, jnp.inf)
            sc_0 = jnp.where(adm, sc, 0.0)
        ibuf[pl.ds(r0, kt_rows), :] = sc_hi
        mx8, mn8, s8, ss8 = stats
        return (jnp.maximum(mx8, fold(sc_hi, jnp.max)), jnp.minimum(mn8, fold(sc_lo, jnp.min)),
                s8 + fold(sc_0, jnp.sum), ss8 + fold(sc_0 * sc_0, jnp.sum))

    stats0 = (jnp.full((SUBLANES, qb), -jnp.inf, f32), jnp.full((SUBLANES, qb), jnp.inf, f32),
              jnp.zeros((SUBLANES, qb), f32), jnp.zeros((SUBLANES, qb), f32))
    odd = i % 2
    stats = lax.cond(odd == 1, lambda st: idx_tile(0, False, st), lambda st: st, stats0)
    stats = lax.fori_loop(
        0, i // 2,
        lambda j, st: idx_tile(odd + 2 * j + 1, False, idx_tile(odd + 2 * j, False, st)),
        stats)
    mx8, mn8, s8, ss8 = idx_tile(i, True, stats)
    col_max = jnp.max(mx8, axis=0, keepdims=True)
    col_min = jnp.min(mn8, axis=0, keepdims=True)
    col_sum = jnp.sum(s8, axis=0, keepdims=True)
    col_ssq = jnp.sum(ss8, axis=0, keepdims=True)

    def count(pred):
        def body(r, acc):
            r0 = pl.multiple_of(r * kt_rows, kt_rows)
            tile = ibuf[pl.ds(r0, kt_rows), :]
            rows = r0 + lax.broadcasted_iota(jnp.int32, (kt_rows, qb), 0)
            ind = jnp.where(pred(tile, rows), 1.0, 0.0)
            part = ind.reshape(kt_rows // SUBLANES, SUBLANES, qb)
            while part.shape[0] > 1:
                half = part.shape[0] // 2
                part = part[:half] + part[half:]
            return acc + part[0]
        acc = jnp.zeros((SUBLANES, qb), f32)
        acc = lax.cond(i % 2 == 0, lambda a: body(0, a), lambda a: a, acc)
        first = 1 - i % 2
        acc = lax.fori_loop(
            0, (i + 1) // 2,
            lambda j, a: body(first + 2 * j + 1, body(first + 2 * j, a)), acc)
        return jnp.sum(acc, axis=0, keepdims=True)

    def resolve_ties(tv, need, mult):
        is_tie = tv == tv
        n_keys = n_tiles * kt_rows

        def tie_cond(st):
            return st["active"] > 0

        def tie_body(st):
            plo, phi, flo, fhi = st["plo"], st["phi"], st["flo"], st["fhi"]
            span = phi - plo
            est = ((need - flo) / jnp.maximum(fhi - flo, 1.0) * span.astype(f32)).astype(jnp.int32)
            pick = jnp.where(st["bisect"] == 2, span // 2, est)
            cand = plo + jnp.clip(pick, 1, jnp.maximum(span - 1, 1))
            f = count(lambda tile, rows: jnp.where(tile == tv, rows, seq_len) < cand)
            lower = f < need
            exact = f == need
            plo = jnp.where(lower, cand, jnp.where(exact, cand - 1, plo))
            phi = jnp.where(lower, phi, cand)
            flo = jnp.where(lower, f, flo)
            fhi = jnp.where(lower, fhi, f)
            unsplit = jnp.where(is_tie & (phi - plo > 1), 1, 0)
            return dict(active=jnp.max(unsplit), bisect=(st["bisect"] + 1) % 3,
                        plo=plo, phi=phi, flo=flo, fhi=fhi)

        split = lax.while_loop(tie_cond, tie_body, dict(
            active=jnp.int32(1), bisect=jnp.int32(0),
            plo=jnp.zeros((1, qb), jnp.int32), phi=jnp.zeros((1, qb), jnp.int32) + n_keys,
            flo=jnp.zeros((1, qb), f32), fhi=mult))["phi"]

        def drop_body(r, carry):
            r0 = pl.multiple_of(r * kt_rows, kt_rows)
            tile = ibuf[pl.ds(r0, kt_rows), :]
            rows = r0 + lax.broadcasted_iota(jnp.int32, (kt_rows, qb), 0)
            dropped = jnp.where(tile == tv, rows, -1) >= split
            ibuf[pl.ds(r0, kt_rows), :] = jnp.where(dropped, -jnp.inf, tile)
            return carry

        lax.fori_loop(0, n_tiles, drop_body, 0)

    qpos = i * qb + lax.broadcasted_iota(jnp.int32, (1, qb), 1)
    n_adm = (qpos // CHUNK + 1) * CHUNK
    select_all = n_adm <= TOPK_MAX
    kf = float(TOPK_MAX)

    thr_ref[...] = jnp.full((1, qb), F32_LOWEST, f32)
    pending_ref[0] = jnp.int32(0)

    @pl.when(n_tiles * kt_rows > TOPK_MAX)
    def _():
        n_f = n_adm.astype(f32)
        mean = col_sum / n_f
        sigma = jnp.sqrt(jnp.maximum(col_ssq / n_f - mean * mean, 0.0))
        spacing = tab_ref[1:2, :] * sigma

        def inside(c, lo, hi):
            return (c > lo) & (c < hi)

        c0 = mean + tab_ref[0:1, :] * sigma
        c0 = jnp.where(inside(c0, col_min, col_max), c0, 0.5 * col_min + 0.5 * col_max)
        zero = jnp.zeros((1, qb), f32)
        one = jnp.ones((1, qb), f32)
        state0 = dict(it=jnp.int32(0), active=jnp.int32(1), c=c0, lo=col_min, hi=col_max,
                      glo=n_f, ghi=zero, tlo=zero, thi=zero, boost=one, side=zero,
                      thr=jnp.full((1, qb), F32_LOWEST, f32),
                      open=jnp.where(select_all, 0.0, 1.0), tie=zero, kind=zero, emin=zero)

        def search_cond(st):
            return (st["it"] < SEARCH_MAX_ITERS) & (st["active"] > 0)

        def search_body(st):
            c, lo, hi = st["c"], st["lo"], st["hi"]
            g = count(lambda tile, rows: tile >= c)
            hit = (g == kf) & (st["open"] > 0.0)
            thr = jnp.where(hit, c, st["thr"])
            still = jnp.where(hit, 0.0, st["open"])
            above = g > kf
            lo = jnp.where(above, c, lo)
            hi = jnp.where(above, hi, c)
            glo = jnp.where(above, g, st["glo"])
            ghi = jnp.where(above, st["ghi"], g)
            tlo = jnp.where(above, 1.0, st["tlo"])
            thi = jnp.where(above, st["thi"], 1.0)
            side = jnp.where(above, 1.0, -1.0)
            repeat = side == st["side"]
            boost = jnp.where(repeat, 2.0 * st["boost"], 1.0)
            mid = 0.5 * lo + 0.5 * hi
            stale = 1.0 / jnp.minimum(boost, 256.0)
            w_lo = (glo - kf - 0.5) * jnp.where(above, 1.0, stale)
            w_hi = (kf + 0.5 - ghi) * jnp.where(above, stale, 1.0)
            c_two = lo + w_lo / (w_lo + w_hi) * (hi - lo)
            step = 2.0 * boost * spacing
            c_one = jnp.where(tlo > 0.0, lo + jnp.maximum(glo - kf, 1.0) * step,
                              hi - jnp.maximum(kf - ghi, 1.0) * step)
            c_one = jnp.where(inside(c_one, lo, hi), c_one, mid)
            both = (tlo > 0.0) & (thi > 0.0)
            c_new = jnp.where(both, c_two, c_one)
            c_new = jnp.where(both & (lo < 0.0) & (hi > 0.0), 0.0, c_new)
            c_new = jnp.where((lo == 0.0) & (hi > F32_TINY), F32_TINY, c_new)
            movable = inside(c_new, lo, hi) & ~((lo == 0.0) & (hi <= F32_TINY))
            missed = (still > 0.0) & (((st["kind"] == 1.0) & above) |
                                      ((st["kind"] == 2.0) & (g < kf)))
            lo = jnp.where(missed & (st["kind"] == 2.0), st["emin"], lo)
            closed = (~movable & both & (still > 0.0)) | missed
            tie = jnp.where(closed, 1.0, st["tie"])
            still = jnp.where(closed, 0.0, still)
            searching = jnp.where(movable, still, 0.0)

            from_hi = both & (kf - ghi == 1.0)
            from_lo = both & (glo - kf == 1.0) & ~from_hi
            ready = jnp.where(searching > 0.0, jnp.where(from_hi | from_lo, 1.0, 2.0), 0.0)
            code = jnp.max(ready)
            ext_ref[...] = jnp.zeros_like(ext_ref)
            finish_ref[0] = jnp.int32(0)

            @pl.when((code == 1.0) & (st["it"] + 1 >= SEARCH_FINISH_FROM))
            def _():
                def ext_body(r, carry):
                    r0 = pl.multiple_of(r * kt_rows, kt_rows)
                    tile = ibuf[pl.ds(r0, kt_rows), :]
                    below = fold(jnp.where(tile < hi, tile, -jnp.inf), jnp.max)
                    above_lo = fold(jnp.where(tile >= lo, tile, jnp.inf), jnp.min)
                    return jnp.maximum(carry[0], below), jnp.minimum(carry[1], above_lo)
                mx8, mn8 = lax.fori_loop(0, n_tiles, ext_body,
                                         (jnp.full((SUBLANES, qb), -jnp.inf, f32),
                                          jnp.full((SUBLANES, qb), jnp.inf, f32)))
                ext_ref[0:1, :] = jnp.max(mx8, axis=0, keepdims=True)
                ext_ref[1:2, :] = jnp.min(mn8, axis=0, keepdims=True)
                finish_ref[0] = jnp.int32(1)

            finishing = finish_ref[0] > 0
            emax = ext_ref[0:1, :]
            emin = ext_ref[1:2, :]
            ebits = lax.bitcast_convert_type(emin, jnp.int32)
            next_up = lax.bitcast_convert_type(ebits + jnp.where(emin > 0.0, 1, -1), f32)
            next_up = jnp.where(emin == 0.0, F32_TINY, next_up)
            kind = jnp.where(finishing & (searching > 0.0),
                             jnp.where(from_hi, 1.0, jnp.where(from_lo, 2.0, 0.0)), 0.0)
            c_new = jnp.where(kind == 1.0, emax, jnp.where(kind == 2.0, next_up, c_new))
            return dict(it=st["it"] + 1, active=(code > 0.0).astype(jnp.int32),
                        c=c_new, lo=lo, hi=hi, glo=glo, ghi=ghi, tlo=tlo, thi=thi,
                        boost=boost, side=side, thr=thr, open=still, tie=tie,
                        kind=kind, emin=emin)

        final = lax.while_loop(search_cond, search_body, state0)
        tied = final["tie"] > 0.0
        thr_ref[...] = jnp.where(tied, final["lo"], final["thr"])
        pending_ref[0] = (jnp.max(final["open"]) > 0.0).astype(jnp.int32)

        @pl.when(jnp.max(final["tie"]) > 0.0)
        def _():
            resolve_ties(jnp.where(tied, final["lo"], jnp.nan), kf - final["ghi"],
                         final["glo"] - final["ghi"])

    @pl.when(pending_ref[0] > 0)
    def _():
        def key_to_f32(key):
            bits = jnp.where(key < 0, key ^ jnp.int32(0x7FFFFFFF), key)
            return lax.bitcast_convert_type(bits, f32)

        def bit_body(t, prefix):
            step = lax.shift_left(jnp.int32(1), jnp.int32(31) - t)
            cand = prefix + step
            cand_f = key_to_f32(cand)
            cnt = count(lambda tile, rows: tile >= cand_f)
            return jnp.where(cnt >= kf, cand, prefix)

        prefix = lax.fori_loop(0, 32, bit_body, jnp.full((1, qb), INT32_MIN, jnp.int32))
        thr = jnp.where(select_all, F32_LOWEST, key_to_f32(prefix))
        thr_ref[...] = thr

        cnt_ge = count(lambda tile, rows: tile >= thr)
        excess = jnp.where(select_all, 0.0, cnt_ge - kf)

        @pl.when(jnp.max(excess) > 0.0)
        def _():
            cnt_gt = count(lambda tile, rows: tile > thr)
            resolve_ties(jnp.where(excess > 0.0, thr, jnp.nan), kf - cnt_gt, cnt_ge - cnt_gt)

    thr = thr_ref[...]
    ones_rows = jnp.ones((2 * SUBLANES, kt_rows), bf16)

    m_ref[1] = jnp.full(m_ref.shape[1:], NEG_BIG, bf16).astype(f32)

    def logits_stage(kt, slot):
        r0 = pl.multiple_of(kt * kt_rows, kt_rows)
        bias_ref[...] = jnp.where(ibuf[pl.ds(r0, kt_rows), :] >= thr, 0.0, NEG_BIG).astype(bf16)
        for h in range(N_ATTN_HEADS):
            k_pair = k_ref[pl.ds(r0, kt_rows), (h // 2) * LANES:(h // 2 + 1) * LANES]
            s = jnp.dot(k_pair, qTm[h], preferred_element_type=f32).astype(bf16)
            s_ref[slot, h] = s + bias_ref[...]
            m_old = m_ref[1 - slot, h]
            tile_max = jnp.max(s_ref[slot, h], axis=0, keepdims=True).astype(f32)
            m_new = jnp.maximum(m_old, tile_max)
            alpha_ref[slot, h] = jnp.exp2(m_old - m_new)
            m_ref[slot, h] = m_new

    def values_stage(kt, slot):
        r0 = pl.multiple_of(kt * kt_rows, kt_rows)
        for h in range(N_ATTN_HEADS):
            p = jnp.exp2(s_ref[slot, h] - m_ref[slot, h].astype(bf16))
            hs = slice(h * ATTN_HEAD_DIM, (h + 1) * ATTN_HEAD_DIM)
            lhs = jnp.concatenate([vT_ref[hs, pl.ds(r0, kt_rows)], ones_rows], axis=0)
            pv = jnp.dot(lhs, p, preferred_element_type=f32)
            alpha = alpha_ref[slot, h]
            acc_ref[hs, :] = alpha * acc_ref[hs, :] + pv[:ATTN_HEAD_DIM]
            l_ref[h] = alpha * l_ref[h] + pv[ATTN_HEAD_DIM:ATTN_HEAD_DIM + 1]

    def att_body(j, carry):
        kt = 2 * j
        logits_stage(kt + 1, 1)
        values_stage(kt, 0)
        logits_stage(kt + 2, 0)
        values_stage(kt + 1, 1)
        return carry

    logits_stage(0, 0)
    lax.fori_loop(0, i // 2, att_body, 0)

    @pl.when(i % 2 == 1)
    def _():
        logits_stage(i, 1)
        values_stage(i - 1, 0)
        values_stage(i, 1)

    @pl.when(i % 2 == 0)
    def _():
        values_stage(i, 0)

    for h in range(N_ATTN_HEADS):
        hs = slice(h * ATTN_HEAD_DIM, (h + 1) * ATTN_HEAD_DIM)
        acc_ref[hs, :] = acc_ref[hs, :] / l_ref[h]
    o_ref[...] = acc_ref[...].T.astype(o_ref.dtype)


def _search_hints(seq_len):
    n = (np.arange(seq_len) // CHUNK + 1) * CHUNK
    frac = np.minimum(TOPK_MAX / n, 0.5)
    nd = statistics.NormalDist()
    z_of = {f: nd.inv_cdf(1.0 - f) for f in np.unique(frac)}
    z = np.array([z_of[f] for f in frac])
    dens = n * np.exp(-0.5 * z * z) / np.sqrt(2.0 * np.pi)
    return jnp.asarray(np.stack([z, 1.0 / dens]), f32)


def _dsa(qT, iqT, iwT, z3, vT, ik3):
    b, _, s = qT.shape
    qb = DSA_BLOCK
    return pl.pallas_call(
        functools.partial(_dsa_kernel, seq_len=s),
        grid=(b, s // qb),
        in_specs=[
            pl.BlockSpec((None, ATTN_W, qb), lambda bi, i: (bi, 0, i)),
            pl.BlockSpec((None, IDX_Q_W, qb), lambda bi, i: (bi, 0, i)),
            pl.BlockSpec((None, N_IDX_HEADS, qb), lambda bi, i: (bi, 0, i)),
            pl.BlockSpec((2, qb), lambda bi, i: (0, i)),
            pl.BlockSpec((None, s, COL_TILE), lambda bi, i: (bi, 0, Z_AK)),
            pl.BlockSpec((None, ATTN_W, s), lambda bi, i: (bi, 0, 0)),
            pl.BlockSpec((None, s, IDX_HEAD_DIM), lambda bi, i: (bi, 0, 0)),
        ],
        out_specs=pl.BlockSpec((None, qb, ATTN_W), lambda bi, i: (bi, i, 0)),
        out_shape=jax.ShapeDtypeStruct((b, s, ATTN_W), bf16),
        scratch_shapes=[
            pltpu.VMEM((s, qb), f32),
            pltpu.VMEM((N_ATTN_HEADS, LANES, qb), bf16),
            pltpu.VMEM((2, N_ATTN_HEADS, qb, qb), bf16),
            pltpu.VMEM((qb, qb), bf16),
            pltpu.VMEM((2, N_ATTN_HEADS, 1, qb), f32),
            pltpu.VMEM((N_ATTN_HEADS, 1, qb), f32),
            pltpu.VMEM((2, N_ATTN_HEADS, 1, qb), f32),
            pltpu.VMEM((ATTN_W, qb), f32),
            pltpu.VMEM((1, qb), f32),
            pltpu.VMEM((2, qb), f32),
            pltpu.SMEM((1,), jnp.int32),
            pltpu.SMEM((1,), jnp.int32),
        ],
        compiler_params=_cparams(("parallel", "arbitrary")),
        name="dsa",
    )(qT, iqT, iwT, _search_hints(s), z3, vT, ik3)


def _gla_kernel(gin_ref, dec_ref, v_ref, r_ref, gn_ref, o_ref, state_ref):
    c = CHUNK
    n_batch = gin_ref.shape[0]

    @pl.when(pl.program_id(0) == 0)
    def _():
        state_ref[...] = jnp.zeros_like(state_ref)

    row = lax.broadcasted_iota(jnp.int32, (c, c), 0)
    col = lax.broadcasted_iota(jnp.int32, (c, c), 1)
    causal = row >= col
    lane = lax.broadcasted_iota(jnp.int32, (c, LANES), 1)
    half = (lane < GLA_HEAD_K, lane >= GLA_HEAD_K)
    contract_last = (((1,), (1,)), ((), ()))
    contract_first = (((0,), (0,)), ((), ()))
    zero = jnp.zeros((c, LANES), bf16)

    for ci, bi in [(ci, bi) for ci in range(GLA_TILE // c) for bi in range(n_batch)]:
        rows = slice(ci * c, (ci + 1) * c)
        decay = dec_ref[bi, ci:ci + 1, :]
        for pr in range(N_GLA_HEADS // 2):
            ps = slice(pr * LANES, (pr + 1) * LANES)
            part = lambda g: gin_ref[bi, rows, g * GLA_K_W + pr * LANES:
                                     g * GLA_K_W + (pr + 1) * LANES]
            q_in, q_mid, k_mid, k_out = part(0), part(1), part(2), part(3)
            st = state_ref[bi, pr]
            st_b = st.astype(bf16)
            upd = jnp.zeros_like(st)
            for hh in range(2):
                h = pr * 2 + hh
                vs = slice(h * GLA_HEAD_V, (h + 1) * GLA_HEAD_V)
                v_h = v_ref[bi, rows, vs]
                attn = lax.dot_general(jnp.where(half[hh], q_mid, zero), k_mid, contract_last,
                                       preferred_element_type=f32)
                attn = jnp.where(causal, attn, 0.0).astype(bf16)
                o = lax.dot_general(jnp.where(half[hh], q_in, zero), st_b, contract_last,
                                    preferred_element_type=f32)
                o = o + jnp.dot(attn, v_h, preferred_element_type=f32)
                upd = upd + lax.dot_general(v_h, jnp.where(half[hh], k_out, zero),
                                            contract_first, preferred_element_type=f32)
                y = _rms(o, gn_ref[...])
                r = r_ref[bi, rows, vs].astype(f32)
                o_ref[bi, rows, vs] = (y * r * _sigmoid(r)).astype(o_ref.dtype)
            state_ref[bi, pr] = decay[:, ps] * st + upd


def _gla(gin3, dec3, z3, gn):
    b, s, _ = z3.shape
    tb = GLA_TILE
    return pl.pallas_call(
        _gla_kernel,
        grid=(s // tb,),
        in_specs=[
            pl.BlockSpec((b, tb, GLA_IN_W), lambda j: (0, j, 0)),
            pl.BlockSpec((b, tb // CHUNK, GLA_K_W), lambda j: (0, j, 0)),
            pl.BlockSpec((b, tb, COL_TILE), lambda j: (0, j, Z_GV)),
            pl.BlockSpec((b, tb, COL_TILE), lambda j: (0, j, Z_GR)),
            pl.BlockSpec((1, GLA_HEAD_V), lambda j: (0, 0)),
        ],
        out_specs=pl.BlockSpec((b, tb, GLA_V_W), lambda j: (0, j, 0)),
        out_shape=jax.ShapeDtypeStruct((b, s, GLA_V_W), bf16),
        scratch_shapes=[pltpu.VMEM((b, N_GLA_HEADS // 2, GLA_HEAD_V, LANES), f32)],
        compiler_params=_cparams(("arbitrary",)),
        name="gla",
    )(gin3, dec3, z3, z3, gn)


def _merge_kernel(x_ref, oa_ref, ob_ref, ga_ref, gb_ref, wa_ref, wb_ref, wo_ref, o_ref):
    ya = jnp.dot(oa_ref[...], wa_ref[...], preferred_element_type=f32)
    yb = jnp.dot(ob_ref[...], wb_ref[...], preferred_element_type=f32)
    mg = _sigmoid(ga_ref[...].astype(f32)) * ya + _sigmoid(gb_ref[...].astype(f32)) * yb
    o_ref[...] = x_ref[...] + jnp.dot(mg.astype(bf16), wo_ref[...], preferred_element_type=f32)


def _merge(x, oa, ob, z, wa, wb, wo):
    t = x.shape[0]
    tm = TOKEN_TILE
    return pl.pallas_call(
        _merge_kernel,
        grid=(t // tm,),
        in_specs=[
            pl.BlockSpec((tm, D_MODEL), lambda i: (i, 0)),
            pl.BlockSpec((tm, ATTN_W), lambda i: (i, 0)),
            pl.BlockSpec((tm, GLA_V_W), lambda i: (i, 0)),
            pl.BlockSpec((tm, D_MODEL), lambda i: (i, Z_MGA)),
            pl.BlockSpec((tm, D_MODEL), lambda i: (i, Z_MGB)),
            pl.BlockSpec((ATTN_W, D_MODEL), lambda i: (0, 0)),
            pl.BlockSpec((GLA_V_W, D_MODEL), lambda i: (0, 0)),
            pl.BlockSpec((D_MODEL, D_MODEL), lambda i: (0, 0)),
        ],
        out_specs=pl.BlockSpec((tm, D_MODEL), lambda i: (i, 0)),
        out_shape=jax.ShapeDtypeStruct((t, D_MODEL), f32),
        compiler_params=_cparams(("parallel",)),
        name="merge",
    )(x, oa, ob, z, z, wa, wb, wo)


def _ple_kernel(x_ref, p_ref, g_ref, wg_ref, wp_ref, gf_ref, o_ref, *, final):
    x = x_ref[...]
    h = _rms(x, g_ref[...]).astype(bf16)
    gate = _sigmoid(jnp.dot(h, wg_ref[...], preferred_element_type=f32))
    e = jnp.dot(p_ref[...].astype(bf16), wp_ref[...], preferred_element_type=f32)
    y = x + gate * e
    if final:
        y = _rms(y, gf_ref[...])
    o_ref[...] = y


def _ple(x, p, g, wg, wp, gf, final):
    t = x.shape[0]
    tm = TOKEN_TILE
    return pl.pallas_call(
        functools.partial(_ple_kernel, final=final),
        grid=(t // tm,),
        in_specs=[
            pl.BlockSpec((tm, D_MODEL), lambda i: (i, 0)),
            pl.BlockSpec((tm, PLE_DIM), lambda i: (i, 0)),
            pl.BlockSpec((1, D_MODEL), lambda i: (0, 0)),
            pl.BlockSpec((D_MODEL, D_MODEL), lambda i: (0, 0)),
            pl.BlockSpec((PLE_DIM, D_MODEL), lambda i: (0, 0)),
            pl.BlockSpec((1, D_MODEL), lambda i: (0, 0)),
        ],
        out_specs=pl.BlockSpec((tm, D_MODEL), lambda i: (i, 0)),
        out_shape=jax.ShapeDtypeStruct((t, D_MODEL), f32),
        compiler_params=_cparams(("parallel",)),
        name="ple",
    )(x, p, g, wg, wp, gf)


def _split_w_in(w_in):
    cuts = np.cumsum(np.array(SPLIT_SIZES))[:-1].tolist()
    aq, ak, av, iq, ik, iw, gq, gk, gv, gr, ga, mga, mgb = jnp.split(w_in, cuts, axis=-1)
    att_scale = ATTN_HEAD_DIM ** -0.5 * float(np.log2(np.e))
    idx_scale = IDX_HEAD_DIM ** -0.5 * N_IDX_HEADS ** -0.5
    main = jnp.concatenate([mga, mgb, ak, gv, gr], axis=-1)
    feature_major = jnp.concatenate([aq * att_scale, av, iq], axis=-1)
    gla_qk = jnp.concatenate([gq * (GLA_HEAD_K ** -0.5), gk], axis=-1)
    zeros = lambda n: jnp.zeros(w_in.shape[:-1] + (n,), w_in.dtype)
    small = jnp.concatenate(
        [ik, zeros(LANES - IDX_HEAD_DIM), iw * idx_scale, ga,
         zeros(LANES - N_IDX_HEADS - GLA_GATE_RANK)], axis=-1)
    return main.astype(bf16), feature_major.astype(bf16), gla_qk.astype(bf16), small.astype(bf16)


def kernel(x, p, w_in, gla_gate_w2, gla_gate_b, gla_norm, w_branch_a, w_branch_b, w_out,
           norm_ff1, norm_mix, norm_ff2, norm_ple, ff1_w_gate, ff1_w_up, ff1_w_down,
           ff2_w_gate, ff2_w_up, ff2_w_down, ple_w_proj, ple_w_gate, norm_final):
    b, s, d = x.shape
    t = b * s
    depth = w_in.shape[0]
    w_main, w_fm, w_gqk, w_small = _split_w_in(w_in)
    cast = lambda w: w.astype(bf16)
    ff1 = (cast(ff1_w_gate), cast(ff1_w_up), cast(ff1_w_down))
    ff2 = (cast(ff2_w_gate), cast(ff2_w_up), cast(ff2_w_down))
    wa, wb, wo = cast(w_branch_a), cast(w_branch_b), cast(w_out)
    wpg, wpp = cast(ple_w_gate), cast(ple_w_proj)
    row = lambda v: v.reshape(1, -1)

    xf = x.reshape(t, d)
    for l in range(depth):
        xf = _ffn(xf, row(norm_ff1[l]), ff1[0][l], ff1[1][l], ff1[2][l])

        z, qT, vT, iqT, ik, iwT, gin, dec = _proj(
            xf, row(norm_mix[l]), w_main[l], w_fm[l], w_gqk[l], w_small[l],
            gla_gate_w2[l], row(gla_gate_b[l]), b)
        z3 = z.reshape(b, s, Z_WIDTH)
        oa = _dsa(qT, iqT, iwT, z3, vT, ik.reshape(b, s, IDX_HEAD_DIM))
        ob = _gla(gin.reshape(b, s, GLA_IN_W), dec.reshape(b, s // CHUNK, GLA_K_W), z3,
                  row(gla_norm[l]))
        xf = _merge(xf, oa.reshape(t, ATTN_W), ob.reshape(t, GLA_V_W), z, wa[l], wb[l], wo[l])

        xf = _ffn(xf, row(norm_ff2[l]), ff2[0][l], ff2[1][l], ff2[2][l])
        xf = _ple(xf, p[l].reshape(t, PLE_DIM), row(norm_ple[l]), wpg[l], wpp[l],
                  row(norm_final), final=(l == depth - 1))
    return xf.reshape(b, s, d)
```

```python
import functools
import statistics

import jax
import jax.numpy as jnp
import numpy as np
from jax import lax
from jax.experimental import pallas as pl
from jax.experimental.pallas import tpu as pltpu

D_MODEL = 1024
D_FF = 2816
PLE_DIM = 256
DEPTH = 2
EPS = 1e-6

CHUNK = 64
N_ATTN_HEADS = 8
ATTN_HEAD_DIM = 64
N_IDX_HEADS = 8
IDX_HEAD_DIM = 64
TOPK_MAX = 256
N_GLA_HEADS = 4
GLA_HEAD_K = 64
GLA_HEAD_V = 128
GLA_GATE_RANK = 16
GLA_GATE_TAU = 16.0

ATTN_W = N_ATTN_HEADS * ATTN_HEAD_DIM
IDX_Q_W = N_IDX_HEADS * IDX_HEAD_DIM
GLA_K_W = N_GLA_HEADS * GLA_HEAD_K
GLA_V_W = N_GLA_HEADS * GLA_HEAD_V
SPLIT_SIZES = (ATTN_W, ATTN_W, ATTN_W, IDX_Q_W, IDX_HEAD_DIM, N_IDX_HEADS,
               GLA_K_W, GLA_K_W, GLA_V_W, GLA_V_W, GLA_GATE_RANK, D_MODEL, D_MODEL)

LANES = 128
SUBLANES = 8
VMEM_LIMIT_BYTES = 56 * 1024 * 1024

COL_TILE = 512
Z_MGA, Z_MGB = 0, 1
Z_AK, Z_GV, Z_GR = 4, 5, 6
Z_TILES = 7
Z_WIDTH = Z_TILES * COL_TILE
ZT_TILES = 3
SM_IW0 = 0
SM_GA0 = N_IDX_HEADS

TOKEN_TILE = 512
FFN_TOKEN_TILE = 512
FF_TILE = 1408
DSA_BLOCK = 256
GLA_TILE = 512
GLA_IN_W = 4 * GLA_K_W
SEARCH_MAX_ITERS = 24
SEARCH_FINISH_FROM = 4

NEG_BIG = -1e30
F32_LOWEST = float(np.finfo(np.float32).min)
F32_TINY = float(np.finfo(np.float32).tiny)
INT32_MIN = int(np.iinfo(np.int32).min)

f32 = jnp.float32
bf16 = jnp.bfloat16


def _rms(x, g):
    return x * lax.rsqrt(jnp.mean(x * x, axis=-1, keepdims=True) + EPS) * g


def _sigmoid(x):
    return 1.0 / (1.0 + jnp.exp(-x))


def _cparams(sem):
    return pltpu.CompilerParams(dimension_semantics=sem, vmem_limit_bytes=VMEM_LIMIT_BYTES)


def _ffn_kernel(x_ref, g_ref, wg_ref, wu_ref, wd_ref, o_ref):
    x = x_ref[...]
    h = _rms(x, g_ref[...]).astype(bf16)
    y = x
    for f in range(D_FF // FF_TILE):
        cols = slice(f * FF_TILE, (f + 1) * FF_TILE)
        gate = jnp.dot(h, wg_ref[:, cols], preferred_element_type=f32)
        up = jnp.dot(h, wu_ref[:, cols], preferred_element_type=f32)
        a = (gate * _sigmoid(gate) * up).astype(bf16)
        y = y + 0.5 * jnp.dot(a, wd_ref[cols, :], preferred_element_type=f32)
    o_ref[...] = y


def _ffn(x, g, wg, wu, wd):
    t = x.shape[0]
    resident = lambda a: pl.BlockSpec(a.shape, lambda i: (0, 0), pipeline_mode=pl.Buffered(1))
    return pl.pallas_call(
        _ffn_kernel,
        grid=(t // FFN_TOKEN_TILE,),
        in_specs=[
            pl.BlockSpec((FFN_TOKEN_TILE, D_MODEL), lambda i: (i, 0)),
            pl.BlockSpec((1, D_MODEL), lambda i: (0, 0)),
            resident(wg), resident(wu), resident(wd),
        ],
        out_specs=pl.BlockSpec((FFN_TOKEN_TILE, D_MODEL), lambda i: (i, 0)),
        out_shape=jax.ShapeDtypeStruct((t, D_MODEL), f32),
        compiler_params=_cparams(("parallel",)),
        name="ffn",
    )(x, g, wg, wu, wd)


def _gla_operands(q, k, ga, w2, gb):
    n = q.shape[0]
    c = CHUNK
    glin = jnp.dot(ga, w2, preferred_element_type=f32, precision=lax.Precision.HIGHEST) + gb
    logg = (jnp.minimum(glin, 0.0) - jnp.log(1.0 + jnp.exp(-jnp.abs(glin)))) / GLA_GATE_TAU
    row = lax.broadcasted_iota(jnp.int32, (n, n), 0)
    col = lax.broadcasted_iota(jnp.int32, (n, n), 1)
    tril = jnp.where((row // c == col // c) & (col <= row), 1.0, 0.0).astype(bf16)
    hi = logg.astype(bf16)
    lo = (logg - hi.astype(f32)).astype(bf16)
    bcum = (jnp.dot(tril, hi, preferred_element_type=f32)
            + jnp.dot(tril, lo, preferred_element_type=f32))
    by_chunk = bcum.reshape(n // c, c, GLA_K_W)
    spread = lambda r: jnp.broadcast_to(r, (n // c, c, GLA_K_W)).reshape(n, GLA_K_W)
    b_last = by_chunk[:, c - 1:c, :]
    b_mid = spread(by_chunk[:, c // 2 - 1:c // 2, :])
    operands = jnp.concatenate(
        [q * jnp.exp(bcum), q * jnp.exp(bcum - b_mid), k * jnp.exp(b_mid - bcum),
         k * jnp.exp(spread(b_last) - bcum)], axis=1)
    return operands, jnp.exp(b_last.reshape(n // c, GLA_K_W))


def _proj_kernel(x_ref, g_ref, w_ref, wt_ref, wg_ref, ws_ref, w2_ref, gb_ref,
                 z_ref, qT_ref, vT_ref, iqT_ref, ik_ref, iwT_ref, gin_ref, dec_ref):
    h = _rms(x_ref[...], g_ref[...]).astype(bf16)
    for j in range(Z_TILES):
        cols = slice(j * COL_TILE, (j + 1) * COL_TILE)
        z_ref[:, cols] = jnp.dot(h, w_ref[:, cols], preferred_element_type=f32).astype(bf16)
    for j, out_ref in enumerate((qT_ref, vT_ref, iqT_ref)):
        cols = slice(j * COL_TILE, (j + 1) * COL_TILE)
        zt = jnp.dot(h, wt_ref[:, cols], preferred_element_type=f32)
        out_ref[...] = zt.T.astype(bf16)
    s = jnp.dot(h, ws_ref[...], preferred_element_type=f32)
    ik_ref[...] = s[:, :IDX_HEAD_DIM].astype(bf16)
    sm = s[:, LANES:]
    iwT_ref[...] = sm.T[SM_IW0:SM_IW0 + N_IDX_HEADS, :]
    gqk = jnp.dot(h, wg_ref[...], preferred_element_type=f32)
    operands, decay = _gla_operands(gqk[:, :GLA_K_W], gqk[:, GLA_K_W:],
                                    sm[:, SM_GA0:SM_GA0 + GLA_GATE_RANK],
                                    w2_ref[...], gb_ref[...])
    gin_ref[...] = operands.astype(bf16)
    dec_ref[...] = decay


def _proj(x, g, w_main, w_t, w_g, w_small, w2, gb, batch):
    t = x.shape[0]
    tm = TOKEN_TILE
    s = t // batch
    per_b = s // tm
    fm = lambda rows: pl.BlockSpec((None, rows, tm), lambda i: (i // per_b, 0, i % per_b))
    whole = lambda a: pl.BlockSpec(a.shape, lambda i: (0, 0))
    return pl.pallas_call(
        _proj_kernel,
        grid=(t // tm,),
        in_specs=[
            pl.BlockSpec((tm, D_MODEL), lambda i: (i, 0)),
            whole(g), whole(w_main), whole(w_t), whole(w_g), whole(w_small), whole(w2), whole(gb),
        ],
        out_specs=[
            pl.BlockSpec((tm, Z_WIDTH), lambda i: (i, 0)),
            fm(ATTN_W), fm(ATTN_W), fm(IDX_Q_W),
            pl.BlockSpec((tm, IDX_HEAD_DIM), lambda i: (i, 0)),
            fm(N_IDX_HEADS),
            pl.BlockSpec((tm, GLA_IN_W), lambda i: (i, 0)),
            pl.BlockSpec((tm // CHUNK, GLA_K_W), lambda i: (i, 0)),
        ],
        out_shape=[
            jax.ShapeDtypeStruct((t, Z_WIDTH), bf16),
            jax.ShapeDtypeStruct((batch, ATTN_W, s), bf16),
            jax.ShapeDtypeStruct((batch, ATTN_W, s), bf16),
            jax.ShapeDtypeStruct((batch, IDX_Q_W, s), bf16),
            jax.ShapeDtypeStruct((t, IDX_HEAD_DIM), bf16),
            jax.ShapeDtypeStruct((batch, N_IDX_HEADS, s), f32),
            jax.ShapeDtypeStruct((t, GLA_IN_W), bf16),
            jax.ShapeDtypeStruct((t // CHUNK, GLA_K_W), f32),
        ],
        compiler_params=_cparams(("parallel",)),
        name="proj",
    )(x, g, w_main, w_t, w_g, w_small, w2, gb)


def _dsa_kernel(qT_ref, iqT_ref, iwT_ref, tab_ref, k_ref, vT_ref, ik_ref, o_ref,
                ibuf, qTm, s_ref, bias_ref, m_ref, l_ref, alpha_ref, acc_ref, thr_ref,
                ext_ref, pending_ref, finish_ref, *, seq_len):
    qb = DSA_BLOCK
    kt_rows = DSA_BLOCK
    i = pl.program_id(1)
    n_tiles = i + 1

    rowid = lax.broadcasted_iota(jnp.int32, (LANES, qb), 0)
    for h in range(N_ATTN_HEADS):
        pair = qT_ref[(h // 2) * LANES:(h // 2 + 1) * LANES, :]
        keep = (rowid < ATTN_HEAD_DIM) if h % 2 == 0 else (rowid >= ATTN_HEAD_DIM)
        qTm[h] = jnp.where(keep, pair, jnp.zeros_like(pair))
    l_ref[...] = jnp.zeros_like(l_ref)
    acc_ref[...] = jnp.zeros_like(acc_ref)

    def fold(x, op):
        return op(x.reshape(kt_rows // SUBLANES, SUBLANES, qb), axis=0)

    def idx_tile(kt, diagonal, stats):
        r0 = pl.multiple_of(kt * kt_rows, kt_rows)
        ki_t = ik_ref[pl.ds(r0, kt_rows), :]
        sc = jnp.zeros((kt_rows, qb), f32)
        for h in range(N_IDX_HEADS):
            s = jnp.dot(ki_t, iqT_ref[h * IDX_HEAD_DIM:(h + 1) * IDX_HEAD_DIM, :],
                        preferred_element_type=f32)
            sc = sc + iwT_ref[h:h + 1, :] * jnp.maximum(s, 0.0)
        sc_hi = sc_lo = sc_0 = sc
        if diagonal:
            kc = lax.broadcasted_iota(jnp.int32, (kt_rows, qb), 0) // CHUNK
            qc = lax.broadcasted_iota(jnp.int32, (kt_rows, qb), 1) // CHUNK
            adm = kc <= qc
            sc_hi = jnp.where(adm, sc, -jnp.inf)
            sc_lo = jnp.where(adm, sc, jnp.inf)
            sc_0 = jnp.where(adm, sc, 0.0)
        ibuf[pl.ds(r0, kt_rows), :] = sc_hi
        mx8, mn8, s8, ss8 = stats
        return (jnp.maximum(mx8, fold(sc_hi, jnp.max)), jnp.minimum(mn8, fold(sc_lo, jnp.min)),
                s8 + fold(sc_0, jnp.sum), ss8 + fold(sc_0 * sc_0, jnp.sum))

    stats0 = (jnp.full((SUBLANES, qb), -jnp.inf, f32), jnp.full((SUBLANES, qb), jnp.inf, f32),
              jnp.zeros((SUBLANES, qb), f32), jnp.zeros((SUBLANES, qb), f32))
    odd = i % 2
    stats = lax.cond(odd == 1, lambda st: idx_tile(0, False, st), lambda st: st, stats0)
    stats = lax.fori_loop(
        0, i // 2,
        lambda j, st: idx_tile(odd + 2 * j + 1, False, idx_tile(odd + 2 * j, False, st)),
        stats)
    mx8, mn8, s8, ss8 = idx_tile(i, True, stats)
    col_max = jnp.max(mx8, axis=0, keepdims=True)
    col_min = jnp.min(mn8, axis=0, keepdims=True)
    col_sum = jnp.sum(s8, axis=0, keepdims=True)
    col_ssq = jnp.sum(ss8, axis=0, keepdims=True)

    def count(pred):
        def body(r, acc):
            r0 = pl.multiple_of(r * kt_rows, kt_rows)
            tile = ibuf[pl.ds(r0, kt_rows), :]
            rows = r0 + lax.broadcasted_iota(jnp.int32, (kt_rows, qb), 0)
            ind = jnp.where(pred(tile, rows), 1.0, 0.0)
            part = ind.reshape(kt_rows // SUBLANES, SUBLANES, qb)
            while part.shape[0] > 1:
                half = part.shape[0] // 2
                part = part[:half] + part[half:]
            return acc + part[0]
        acc = jnp.zeros((SUBLANES, qb), f32)
        acc = lax.cond(i % 2 == 0, lambda a: body(0, a), lambda a: a, acc)
        first = 1 - i % 2
        acc = lax.fori_loop(
            0, (i + 1) // 2,
            lambda j, a: body(first + 2 * j + 1, body(first + 2 * j, a)), acc)
        return jnp.sum(acc, axis=0, keepdims=True)

    def resolve_ties(tv, need, mult):
        is_tie = tv == tv
        n_keys = n_tiles * kt_rows

        def tie_cond(st):
            return st["active"] > 0

        def tie_body(st):
            plo, phi, flo, fhi = st["plo"], st["phi"], st["flo"], st["fhi"]
            span = phi - plo
            est = ((need - flo) / jnp.maximum(fhi - flo, 1.0) * span.astype(f32)).astype(jnp.int32)
            pick = jnp.where(st["bisect"] == 2, span // 2, est)
            cand = plo + jnp.clip(pick, 1, jnp.maximum(span - 1, 1))
            f = count(lambda tile, rows: jnp.where(tile == tv, rows, seq_len) < cand)
            lower = f < need
            exact = f == need
            plo = jnp.where(lower, cand, jnp.where(exact, cand - 1, plo))
            phi = jnp.where(lower, phi, cand)
            flo = jnp.where(lower, f, flo)
            fhi = jnp.where(lower, fhi, f)
            unsplit = jnp.where(is_tie & (phi - plo > 1), 1, 0)
            return dict(active=jnp.max(unsplit), bisect=(st["bisect"] + 1) % 3,
                        plo=plo, phi=phi, flo=flo, fhi=fhi)

        split = lax.while_loop(tie_cond, tie_body, dict(
            active=jnp.int32(1), bisect=jnp.int32(0),
            plo=jnp.zeros((1, qb), jnp.int32), phi=jnp.zeros((1, qb), jnp.int32) + n_keys,
            flo=jnp.zeros((1, qb), f32), fhi=mult))["phi"]

        def drop_body(r, carry):
            r0 = pl.multiple_of(r * kt_rows, kt_rows)
            tile = ibuf[pl.ds(r0, kt_rows), :]
            rows = r0 + lax.broadcasted_iota(jnp.int32, (kt_rows, qb), 0)
            dropped = jnp.where(tile == tv, rows, -1) >= split
            ibuf[pl.ds(r0, kt_rows), :] = jnp.where(dropped, -jnp.inf, tile)
            return carry

        lax.fori_loop(0, n_tiles, drop_body, 0)

    qpos = i * qb + lax.broadcasted_iota(jnp.int32, (1, qb), 1)
    n_adm = (qpos // CHUNK + 1) * CHUNK
    select_all = n_adm <= TOPK_MAX
    kf = float(TOPK_MAX)

    thr_ref[...] = jnp.full((1, qb), F32_LOWEST, f32)
    pending_ref[0] = jnp.int32(0)

    @pl.when(n_tiles * kt_rows > TOPK_MAX)
    def _():
        n_f = n_adm.astype(f32)
        mean = col_sum / n_f
        sigma = jnp.sqrt(jnp.maximum(col_ssq / n_f - mean * mean, 0.0))
        spacing = tab_ref[1:2, :] * sigma

        def inside(c, lo, hi):
            return (c > lo) & (c < hi)

        c0 = mean + tab_ref[0:1, :] * sigma
        c0 = jnp.where(inside(c0, col_min, col_max), c0, 0.5 * col_min + 0.5 * col_max)
        zero = jnp.zeros((1, qb), f32)
        one = jnp.ones((1, qb), f32)
        state0 = dict(it=jnp.int32(0), active=jnp.int32(1), c=c0, lo=col_min, hi=col_max,
                      glo=n_f, ghi=zero, tlo=zero, thi=zero, boost=one, side=zero,
                      thr=jnp.full((1, qb), F32_LOWEST, f32),
                      open=jnp.where(select_all, 0.0, 1.0), tie=zero, kind=zero, emin=zero)

        def search_cond(st):
            return (st["it"] < SEARCH_MAX_ITERS) & (st["active"] > 0)

        def search_body(st):
            c, lo, hi = st["c"], st["lo"], st["hi"]
            g = count(lambda tile, rows: tile >= c)
            hit = (g == kf) & (st["open"] > 0.0)
            thr = jnp.where(hit, c, st["thr"])
            still = jnp.where(hit, 0.0, st["open"])
            above = g > kf
            lo = jnp.where(above, c, lo)
            hi = jnp.where(above, hi, c)
            glo = jnp.where(above, g, st["glo"])
            ghi = jnp.where(above, st["ghi"], g)
            tlo = jnp.where(above, 1.0, st["tlo"])
            thi = jnp.where(above, st["thi"], 1.0)
            side = jnp.where(above, 1.0, -1.0)
            repeat = side == st["side"]
            boost = jnp.where(repeat, 2.0 * st["boost"], 1.0)
            mid = 0.5 * lo + 0.5 * hi
            stale = 1.0 / jnp.minimum(boost, 256.0)
            w_lo = (glo - kf - 0.5) * jnp.where(above, 1.0, stale)
            w_hi = (kf + 0.5 - ghi) * jnp.where(above, stale, 1.0)
            c_two = lo + w_lo / (w_lo + w_hi) * (hi - lo)
            step = 2.0 * boost * spacing
            c_one = jnp.where(tlo > 0.0, lo + jnp.maximum(glo - kf, 1.0) * step,
                              hi - jnp.maximum(kf - ghi, 1.0) * step)
            c_one = jnp.where(inside(c_one, lo, hi), c_one, mid)
            both = (tlo > 0.0) & (thi > 0.0)
            c_new = jnp.where(both, c_two, c_one)
            c_new = jnp.where(both & (lo < 0.0) & (hi > 0.0), 0.0, c_new)
            c_new = jnp.where((lo == 0.0) & (hi > F32_TINY), F32_TINY, c_new)
            movable = inside(c_new, lo, hi) & ~((lo == 0.0) & (hi <= F32_TINY))
            missed = (still > 0.0) & (((st["kind"] == 1.0) & above) |
                                      ((st["kind"] == 2.0) & (g < kf)))
            lo = jnp.where(missed & (st["kind"] == 2.0), st["emin"], lo)
            closed = (~movable & both & (still > 0.0)) | missed
            tie = jnp.where(closed, 1.0, st["tie"])
            still = jnp.where(closed, 0.0, still)
            searching = jnp.where(movable, still, 0.0)

            from_hi = both & (kf - ghi == 1.0)
            from_lo = both & (glo - kf == 1.0) & ~from_hi
            ready = jnp.where(searching > 0.0, jnp.where(from_hi | from_lo, 1.0, 2.0), 0.0)
            code = jnp.max(ready)
            ext_ref[...] = jnp.zeros_like(ext_ref)
            finish_ref[0] = jnp.int32(0)

            @pl.when((code == 1.0) & (st["it"] + 1 >= SEARCH_FINISH_FROM))
            def _():
                def ext_body(r, carry):
                    r0 = pl.multiple_of(r * kt_rows, kt_rows)
                    tile = ibuf[pl.ds(r0, kt_rows), :]
                    below = fold(jnp.where(tile < hi, tile, -jnp.inf), jnp.max)
                    above_lo = fold(jnp.where(tile >= lo, tile, jnp.inf), jnp.min)
                    return jnp.maximum(carry[0], below), jnp.minimum(carry[1], above_lo)
                mx8, mn8 = lax.fori_loop(0, n_tiles, ext_body,
                                         (jnp.full((SUBLANES, qb), -jnp.inf, f32),
                                          jnp.full((SUBLANES, qb), jnp.inf, f32)))
                ext_ref[0:1, :] = jnp.max(mx8, axis=0, keepdims=True)
                ext_ref[1:2, :] = jnp.min(mn8, axis=0, keepdims=True)
                finish_ref[0] = jnp.int32(1)

            finishing = finish_ref[0] > 0
            emax = ext_ref[0:1, :]
            emin = ext_ref[1:2, :]
            ebits = lax.bitcast_convert_type(emin, jnp.int32)
            next_up = lax.bitcast_convert_type(ebits + jnp.where(emin > 0.0, 1, -1), f32)
            next_up = jnp.where(emin == 0.0, F32_TINY, next_up)
            kind = jnp.where(finishing & (searching > 0.0),
                             jnp.where(from_hi, 1.0, jnp.where(from_lo, 2.0, 0.0)), 0.0)
            c_new = jnp.where(kind == 1.0, emax, jnp.where(kind == 2.0, next_up, c_new))
            return dict(it=st["it"] + 1, active=(code > 0.0).astype(jnp.int32),
                        c=c_new, lo=lo, hi=hi, glo=glo, ghi=ghi, tlo=tlo, thi=thi,
                        boost=boost, side=side, thr=thr, open=still, tie=tie,
                        kind=kind, emin=emin)

        final = lax.while_loop(search_cond, search_body, state0)
        tied = final["tie"] > 0.0
        thr_ref[...] = jnp.where(tied, final["lo"], final["thr"])
        pending_ref[0] = (jnp.max(final["open"]) > 0.0).astype(jnp.int32)

        @pl.when(jnp.max(final["tie"]) > 0.0)
        def _():
            resolve_ties(jnp.where(tied, final["lo"], jnp.nan), kf - final["ghi"],
                         final["glo"] - final["ghi"])

    @pl.when(pending_ref[0] > 0)
    def _():
        def key_to_f32(key):
            bits = jnp.where(key < 0, key ^ jnp.int32(0x7FFFFFFF), key)
            return lax.bitcast_convert_type(bits, f32)

        def bit_body(t, prefix):
            step = lax.shift_left(jnp.int32(1), jnp.int32(31) - t)
            cand = prefix + step
            cand_f = key_to_f32(cand)
            cnt = count(lambda tile, rows: tile >= cand_f)
            return jnp.where(cnt >= kf, cand, prefix)

        prefix = lax.fori_loop(0, 32, bit_body, jnp.full((1, qb), INT32_MIN, jnp.int32))
        thr = jnp.where(select_all, F32_LOWEST, key_to_f32(prefix))
        thr_ref[...] = thr

        cnt_ge = count(lambda tile, rows: tile >= thr)
        excess = jnp.where(select_all, 0.0, cnt_ge - kf)

        @pl.when(jnp.max(excess) > 0.0)
        def _():
            cnt_gt = count(lambda tile, rows: tile > thr)
            resolve_ties(jnp.where(excess > 0.0, thr, jnp.nan), kf - cnt_gt, cnt_ge - cnt_gt)

    thr = thr_ref[...]
    ones_rows = jnp.ones((2 * SUBLANES, kt_rows), bf16)

    m_ref[1] = jnp.full(m_ref.shape[1:], NEG_BIG, f32)

    def logits_stage(kt, slot):
        r0 = pl.multiple_of(kt * kt_rows, kt_rows)
        bias_ref[...] = jnp.where(ibuf[pl.ds(r0, kt_rows), :] >= thr, 0.0, NEG_BIG)
        for h in range(N_ATTN_HEADS):
            k_pair = k_ref[pl.ds(r0, kt_rows), (h // 2) * LANES:(h // 2 + 1) * LANES]
            s_ref[slot, h] = jnp.dot(k_pair, qTm[h], preferred_element_type=f32) + bias_ref[...]
            m_old = m_ref[1 - slot, h]
            m_new = jnp.maximum(m_old, jnp.max(s_ref[slot, h], axis=0, keepdims=True))
            alpha_ref[slot, h] = jnp.exp2(m_old - m_new)
            m_ref[slot, h] = m_new

    def values_stage(kt, slot):
        r0 = pl.multiple_of(kt * kt_rows, kt_rows)
        for h in range(N_ATTN_HEADS):
            p = jnp.exp2(s_ref[slot, h] - m_ref[slot, h]).astype(bf16)
            hs = slice(h * ATTN_HEAD_DIM, (h + 1) * ATTN_HEAD_DIM)
            lhs = jnp.concatenate([vT_ref[hs, pl.ds(r0, kt_rows)], ones_rows], axis=0)
            pv = jnp.dot(lhs, p, preferred_element_type=f32)
            alpha = alpha_ref[slot, h]
            acc_ref[hs, :] = alpha * acc_ref[hs, :] + pv[:ATTN_HEAD_DIM]
            l_ref[h] = alpha * l_ref[h] + pv[ATTN_HEAD_DIM:ATTN_HEAD_DIM + 1]

    def att_body(j, carry):
        kt = 2 * j
        logits_stage(kt + 1, 1)
        values_stage(kt, 0)
        logits_stage(kt + 2, 0)
        values_stage(kt + 1, 1)
        return carry

    logits_stage(0, 0)
    lax.fori_loop(0, i // 2, att_body, 0)

    @pl.when(i % 2 == 1)
    def _():
        logits_stage(i, 1)
        values_stage(i - 1, 0)
        values_stage(i, 1)

    @pl.when(i % 2 == 0)
    def _():
        values_stage(i, 0)

    for h in range(N_ATTN_HEADS):
        hs = slice(h * ATTN_HEAD_DIM, (h + 1) * ATTN_HEAD_DIM)
        acc_ref[hs, :] = acc_ref[hs, :] / l_ref[h]
    o_ref[...] = acc_ref[...].T.astype(o_ref.dtype)


def _search_hints(seq_len):
    n = (np.arange(seq_len) // CHUNK + 1) * CHUNK
    frac = np.minimum(TOPK_MAX / n, 0.5)
    nd = statistics.NormalDist()
    z_of = {f: nd.inv_cdf(1.0 - f) for f in np.unique(frac)}
    z = np.array([z_of[f] for f in frac])
    dens = n * np.exp(-0.5 * z * z) / np.sqrt(2.0 * np.pi)
    return jnp.asarray(np.stack([z, 1.0 / dens]), f32)


def _dsa(qT, iqT, iwT, z3, vT, ik3):
    b, _, s = qT.shape
    qb = DSA_BLOCK
    return pl.pallas_call(
        functools.partial(_dsa_kernel, seq_len=s),
        grid=(b, s // qb),
        in_specs=[
            pl.BlockSpec((None, ATTN_W, qb), lambda bi, i: (bi, 0, i)),
            pl.BlockSpec((None, IDX_Q_W, qb), lambda bi, i: (bi, 0, i)),
            pl.BlockSpec((None, N_IDX_HEADS, qb), lambda bi, i: (bi, 0, i)),
            pl.BlockSpec((2, qb), lambda bi, i: (0, i)),
            pl.BlockSpec((None, s, COL_TILE), lambda bi, i: (bi, 0, Z_AK)),
            pl.BlockSpec((None, ATTN_W, s), lambda bi, i: (bi, 0, 0)),
            pl.BlockSpec((None, s, IDX_HEAD_DIM), lambda bi, i: (bi, 0, 0)),
        ],
        out_specs=pl.BlockSpec((None, qb, ATTN_W), lambda bi, i: (bi, i, 0)),
        out_shape=jax.ShapeDtypeStruct((b, s, ATTN_W), bf16),
        scratch_shapes=[
            pltpu.VMEM((s, qb), f32),
            pltpu.VMEM((N_ATTN_HEADS, LANES, qb), bf16),
            pltpu.VMEM((2, N_ATTN_HEADS, qb, qb), f32),
            pltpu.VMEM((qb, qb), f32),
            pltpu.VMEM((2, N_ATTN_HEADS, 1, qb), f32),
            pltpu.VMEM((N_ATTN_HEADS, 1, qb), f32),
            pltpu.VMEM((2, N_ATTN_HEADS, 1, qb), f32),
            pltpu.VMEM((ATTN_W, qb), f32),
            pltpu.VMEM((1, qb), f32),
            pltpu.VMEM((2, qb), f32),
            pltpu.SMEM((1,), jnp.int32),
            pltpu.SMEM((1,), jnp.int32),
        ],
        compiler_params=_cparams(("parallel", "arbitrary")),
        name="dsa",
    )(qT, iqT, iwT, _search_hints(s), z3, vT, ik3)


def _gla_kernel(gin_ref, dec_ref, v_ref, r_ref, gn_ref, o_ref, state_ref):
    c = CHUNK
    n_batch = gin_ref.shape[0]

    @pl.when(pl.program_id(0) == 0)
    def _():
        state_ref[...] = jnp.zeros_like(state_ref)

    row = lax.broadcasted_iota(jnp.int32, (c, c), 0)
    col = lax.broadcasted_iota(jnp.int32, (c, c), 1)
    causal = row >= col
    lane = lax.broadcasted_iota(jnp.int32, (c, LANES), 1)
    half = (lane < GLA_HEAD_K, lane >= GLA_HEAD_K)
    contract_last = (((1,), (1,)), ((), ()))
    contract_first = (((0,), (0,)), ((), ()))
    zero = jnp.zeros((c, LANES), bf16)

    for ci, bi in [(ci, bi) for ci in range(GLA_TILE // c) for bi in range(n_batch)]:
        rows = slice(ci * c, (ci + 1) * c)
        decay = dec_ref[bi, ci:ci + 1, :]
        for pr in range(N_GLA_HEADS // 2):
            ps = slice(pr * LANES, (pr + 1) * LANES)
            part = lambda g: gin_ref[bi, rows, g * GLA_K_W + pr * LANES:
                                     g * GLA_K_W + (pr + 1) * LANES]
            q_in, q_mid, k_mid, k_out = part(0), part(1), part(2), part(3)
            st = state_ref[bi, pr]
            st_b = st.astype(bf16)
            upd = jnp.zeros_like(st)
            for hh in range(2):
                h = pr * 2 + hh
                vs = slice(h * GLA_HEAD_V, (h + 1) * GLA_HEAD_V)
                v_h = v_ref[bi, rows, vs]
                attn = lax.dot_general(jnp.where(half[hh], q_mid, zero), k_mid, contract_last,
                                       preferred_element_type=f32)
                attn = jnp.where(causal, attn, 0.0).astype(bf16)
                o = lax.dot_general(jnp.where(half[hh], q_in, zero), st_b, contract_last,
                                    preferred_element_type=f32)
                o = o + jnp.dot(attn, v_h, preferred_element_type=f32)
                upd = upd + lax.dot_general(v_h, jnp.where(half[hh], k_out, zero),
                                            contract_first, preferred_element_type=f32)
                y = _rms(o, gn_ref[...])
                r = r_ref[bi, rows, vs].astype(f32)
                o_ref[bi, rows, vs] = (y * r * _sigmoid(r)).astype(o_ref.dtype)
            state_ref[bi, pr] = decay[:, ps] * st + upd


def _gla(gin3, dec3, z3, gn):
    b, s, _ = z3.shape
    tb = GLA_TILE
    return pl.pallas_call(
        _gla_kernel,
        grid=(s // tb,),
        in_specs=[
            pl.BlockSpec((b, tb, GLA_IN_W), lambda j: (0, j, 0)),
            pl.BlockSpec((b, tb // CHUNK, GLA_K_W), lambda j: (0, j, 0)),
            pl.BlockSpec((b, tb, COL_TILE), lambda j: (0, j, Z_GV)),
            pl.BlockSpec((b, tb, COL_TILE), lambda j: (0, j, Z_GR)),
            pl.BlockSpec((1, GLA_HEAD_V), lambda j: (0, 0)),
        ],
        out_specs=pl.BlockSpec((b, tb, GLA_V_W), lambda j: (0, j, 0)),
        out_shape=jax.ShapeDtypeStruct((b, s, GLA_V_W), bf16),
        scratch_shapes=[pltpu.VMEM((b, N_GLA_HEADS // 2, GLA_HEAD_V, LANES), f32)],
        compiler_params=_cparams(("arbitrary",)),
        name="gla",
    )(gin3, dec3, z3, z3, gn)


def _merge_kernel(x_ref, oa_ref, ob_ref, ga_ref, gb_ref, wa_ref, wb_ref, wo_ref, o_ref):
    ya = jnp.dot(oa_ref[...], wa_ref[...], preferred_element_type=f32)
    yb = jnp.dot(ob_ref[...], wb_ref[...], preferred_element_type=f32)
    mg = _sigmoid(ga_ref[...].astype(f32)) * ya + _sigmoid(gb_ref[...].astype(f32)) * yb
    o_ref[...] = x_ref[...] + jnp.dot(mg.astype(bf16), wo_ref[...], preferred_element_type=f32)


def _merge(x, oa, ob, z, wa, wb, wo):
    t = x.shape[0]
    tm = TOKEN_TILE
    return pl.pallas_call(
        _merge_kernel,
        grid=(t // tm,),
        in_specs=[
            pl.BlockSpec((tm, D_MODEL), lambda i: (i, 0)),
            pl.BlockSpec((tm, ATTN_W), lambda i: (i, 0)),
            pl.BlockSpec((tm, GLA_V_W), lambda i: (i, 0)),
            pl.BlockSpec((tm, D_MODEL), lambda i: (i, Z_MGA)),
            pl.BlockSpec((tm, D_MODEL), lambda i: (i, Z_MGB)),
            pl.BlockSpec((ATTN_W, D_MODEL), lambda i: (0, 0)),
            pl.BlockSpec((GLA_V_W, D_MODEL), lambda i: (0, 0)),
            pl.BlockSpec((D_MODEL, D_MODEL), lambda i: (0, 0)),
        ],
        out_specs=pl.BlockSpec((tm, D_MODEL), lambda i: (i, 0)),
        out_shape=jax.ShapeDtypeStruct((t, D_MODEL), f32),
        compiler_params=_cparams(("parallel",)),
        name="merge",
    )(x, oa, ob, z, z, wa, wb, wo)


def _ple_kernel(x_ref, p_ref, g_ref, wg_ref, wp_ref, gf_ref, o_ref, *, final):
    x = x_ref[...]
    h = _rms(x, g_ref[...]).astype(bf16)
    gate = _sigmoid(jnp.dot(h, wg_ref[...], preferred_element_type=f32))
    e = jnp.dot(p_ref[...].astype(bf16), wp_ref[...], preferred_element_type=f32)
    y = x + gate * e
    if final:
        y = _rms(y, gf_ref[...])
    o_ref[...] = y


def _ple(x, p, g, wg, wp, gf, final):
    t = x.shape[0]
    tm = TOKEN_TILE
    return pl.pallas_call(
        functools.partial(_ple_kernel, final=final),
        grid=(t // tm,),
        in_specs=[
            pl.BlockSpec((tm, D_MODEL), lambda i: (i, 0)),
            pl.BlockSpec((tm, PLE_DIM), lambda i: (i, 0)),
            pl.BlockSpec((1, D_MODEL), lambda i: (0, 0)),
            pl.BlockSpec((D_MODEL, D_MODEL), lambda i: (0, 0)),
            pl.BlockSpec((PLE_DIM, D_MODEL), lambda i: (0, 0)),
            pl.BlockSpec((1, D_MODEL), lambda i: (0, 0)),
        ],
        out_specs=pl.BlockSpec((tm, D_MODEL), lambda i: (i, 0)),
        out_shape=jax.ShapeDtypeStruct((t, D_MODEL), f32),
        compiler_params=_cparams(("parallel",)),
        name="ple",
    )(x, p, g, wg, wp, gf)


def _split_w_in(w_in):
    cuts = np.cumsum(np.array(SPLIT_SIZES))[:-1].tolist()
    aq, ak, av, iq, ik, iw, gq, gk, gv, gr, ga, mga, mgb = jnp.split(w_in, cuts, axis=-1)
    att_scale = ATTN_HEAD_DIM ** -0.5 * float(np.log2(np.e))
    idx_scale = IDX_HEAD_DIM ** -0.5 * N_IDX_HEADS ** -0.5
    main = jnp.concatenate([mga, mgb, ak, gv, gr], axis=-1)
    feature_major = jnp.concatenate([aq * att_scale, av, iq], axis=-1)
    gla_qk = jnp.concatenate([gq * (GLA_HEAD_K ** -0.5), gk], axis=-1)
    zeros = lambda n: jnp.zeros(w_in.shape[:-1] + (n,), w_in.dtype)
    small = jnp.concatenate(
        [ik, zeros(LANES - IDX_HEAD_DIM), iw * idx_scale, ga,
         zeros(LANES - N_IDX_HEADS - GLA_GATE_RANK)], axis=-1)
    return main.astype(bf16), feature_major.astype(bf16), gla_qk.astype(bf16), small.astype(bf16)


def kernel(x, p, w_in, gla_gate_w2, gla_gate_b, gla_norm, w_branch_a, w_branch_b, w_out,
           norm_ff1, norm_mix, norm_ff2, norm_ple, ff1_w_gate, ff1_w_up, ff1_w_down,
           ff2_w_gate, ff2_w_up, ff2_w_down, ple_w_proj, ple_w_gate, norm_final):
    b, s, d = x.shape
    t = b * s
    depth = w_in.shape[0]
    w_main, w_fm, w_gqk, w_small = _split_w_in(w_in)
    cast = lambda w: w.astype(bf16)
    ff1 = (cast(ff1_w_gate), cast(ff1_w_up), cast(ff1_w_down))
    ff2 = (cast(ff2_w_gate), cast(ff2_w_up), cast(ff2_w_down))
    wa, wb, wo = cast(w_branch_a), cast(w_branch_b), cast(w_out)
    wpg, wpp = cast(ple_w_gate), cast(ple_w_proj)
    row = lambda v: v.reshape(1, -1)

    xf = x.reshape(t, d)
    for l in range(depth):
        xf = _ffn(xf, row(norm_ff1[l]), ff1[0][l], ff1[1][l], ff1[2][l])

        z, qT, vT, iqT, ik, iwT, gin, dec = _proj(
            xf, row(norm_mix[l]), w_main[l], w_fm[l], w_gqk[l], w_small[l],
            gla_gate_w2[l], row(gla_gate_b[l]), b)
        z3 = z.reshape(b, s, Z_WIDTH)
        oa = _dsa(qT, iqT, iwT, z3, vT, ik.reshape(b, s, IDX_HEAD_DIM))
        ob = _gla(gin.reshape(b, s, GLA_IN_W), dec.reshape(b, s // CHUNK, GLA_K_W), z3,
                  row(gla_norm[l]))
        xf = _merge(xf, oa.reshape(t, ATTN_W), ob.reshape(t, GLA_V_W), z, wa[l], wb[l], wo[l])

        xf = _ffn(xf, row(norm_ff2[l]), ff2[0][l], ff2[1][l], ff2[2][l])
        xf = _ple(xf, p[l].reshape(t, PLE_DIM), row(norm_ple[l]), wpg[l], wpp[l],
                  row(norm_final), final=(l == depth - 1))
    return xf.reshape(b, s, d)
```

```python
import functools
import statistics

import jax
import jax.numpy as jnp
import numpy as np
from jax import lax
from jax.experimental import pallas as pl
from jax.experimental.pallas import tpu as pltpu

D_MODEL = 1024
D_FF = 2816
PLE_DIM = 256
DEPTH = 2
EPS = 1e-6

CHUNK = 64
N_ATTN_HEADS = 8
ATTN_HEAD_DIM = 64
N_IDX_HEADS = 8
IDX_HEAD_DIM = 64
TOPK_MAX = 256
N_GLA_HEADS = 4
GLA_HEAD_K = 64
GLA_HEAD_V = 128
GLA_GATE_RANK = 16
GLA_GATE_TAU = 16.0

ATTN_W = N_ATTN_HEADS * ATTN_HEAD_DIM
IDX_Q_W = N_IDX_HEADS * IDX_HEAD_DIM
GLA_K_W = N_GLA_HEADS * GLA_HEAD_K
GLA_V_W = N_GLA_HEADS * GLA_HEAD_V
SPLIT_SIZES = (ATTN_W, ATTN_W, ATTN_W, IDX_Q_W, IDX_HEAD_DIM, N_IDX_HEADS,
               GLA_K_W, GLA_K_W, GLA_V_W, GLA_V_W, GLA_GATE_RANK, D_MODEL, D_MODEL)

LANES = 128
SUBLANES = 8
VMEM_LIMIT_BYTES = 56 * 1024 * 1024

COL_TILE = 512
Z_MGA, Z_MGB = 0, 1
Z_AK, Z_GV, Z_GR = 4, 5, 6
Z_TILES = 7
Z_WIDTH = Z_TILES * COL_TILE
ZT_TILES = 3
SM_IW0 = 0
SM_GA0 = N_IDX_HEADS

TOKEN_TILE = 512
FFN_TOKEN_TILE = 512
FF_TILE = 1408
DSA_BLOCK = 512
DSA_KEY_TILE = 256
GLA_TILE = 512
GLA_IN_W = 4 * GLA_K_W
SEARCH_MAX_ITERS = 24
SEARCH_FINISH_FROM = 4

NEG_BIG = -1e30
F32_LOWEST = float(np.finfo(np.float32).min)
F32_TINY = float(np.finfo(np.float32).tiny)
INT32_MIN = int(np.iinfo(np.int32).min)

f32 = jnp.float32
bf16 = jnp.bfloat16


def _rms(x, g):
    return x * lax.rsqrt(jnp.mean(x * x, axis=-1, keepdims=True) + EPS) * g


def _sigmoid(x):
    return 1.0 / (1.0 + jnp.exp(-x))


def _cparams(sem):
    return pltpu.CompilerParams(dimension_semantics=sem, vmem_limit_bytes=VMEM_LIMIT_BYTES)


def _ffn_kernel(x_ref, g_ref, wg_ref, wu_ref, wd_ref, o_ref):
    x = x_ref[...]
    h = _rms(x, g_ref[...]).astype(bf16)
    y = x
    for f in range(D_FF // FF_TILE):
        cols = slice(f * FF_TILE, (f + 1) * FF_TILE)
        gate = jnp.dot(h, wg_ref[:, cols], preferred_element_type=f32)
        up = jnp.dot(h, wu_ref[:, cols], preferred_element_type=f32)
        a = (gate * _sigmoid(gate) * up).astype(bf16)
        y = y + 0.5 * jnp.dot(a, wd_ref[cols, :], preferred_element_type=f32)
    o_ref[...] = y


def _ffn(x, g, wg, wu, wd):
    t = x.shape[0]
    resident = lambda a: pl.BlockSpec(a.shape, lambda i: (0, 0), pipeline_mode=pl.Buffered(1))
    return pl.pallas_call(
        _ffn_kernel,
        grid=(t // FFN_TOKEN_TILE,),
        in_specs=[
            pl.BlockSpec((FFN_TOKEN_TILE, D_MODEL), lambda i: (i, 0)),
            pl.BlockSpec((1, D_MODEL), lambda i: (0, 0)),
            resident(wg), resident(wu), resident(wd),
        ],
        out_specs=pl.BlockSpec((FFN_TOKEN_TILE, D_MODEL), lambda i: (i, 0)),
        out_shape=jax.ShapeDtypeStruct((t, D_MODEL), f32),
        compiler_params=_cparams(("parallel",)),
        name="ffn",
    )(x, g, wg, wu, wd)


def _gla_operands(q, k, ga, w2, gb):
    n = q.shape[0]
    c = CHUNK
    glin = jnp.dot(ga, w2, preferred_element_type=f32, precision=lax.Precision.HIGHEST) + gb
    logg = (jnp.minimum(glin, 0.0) - jnp.log(1.0 + jnp.exp(-jnp.abs(glin)))) / GLA_GATE_TAU
    row = lax.broadcasted_iota(jnp.int32, (n, n), 0)
    col = lax.broadcasted_iota(jnp.int32, (n, n), 1)
    tril = jnp.where((row // c == col // c) & (col <= row), 1.0, 0.0).astype(bf16)
    hi = logg.astype(bf16)
    lo = (logg - hi.astype(f32)).astype(bf16)
    bcum = (jnp.dot(tril, hi, preferred_element_type=f32)
            + jnp.dot(tril, lo, preferred_element_type=f32))
    by_chunk = bcum.reshape(n // c, c, GLA_K_W)
    spread = lambda r: jnp.broadcast_to(r, (n // c, c, GLA_K_W)).reshape(n, GLA_K_W)
    b_last = by_chunk[:, c - 1:c, :]
    b_mid = spread(by_chunk[:, c // 2 - 1:c // 2, :])
    operands = jnp.concatenate(
        [q * jnp.exp(bcum), q * jnp.exp(bcum - b_mid), k * jnp.exp(b_mid - bcum),
         k * jnp.exp(spread(b_last) - bcum)], axis=1)
    return operands, jnp.exp(b_last.reshape(n // c, GLA_K_W))


def _proj_kernel(x_ref, g_ref, w_ref, wt_ref, wg_ref, ws_ref, w2_ref, gb_ref,
                 z_ref, qT_ref, vT_ref, iqT_ref, ik_ref, iwT_ref, gin_ref, dec_ref):
    h = _rms(x_ref[...], g_ref[...]).astype(bf16)
    for j in range(Z_TILES):
        cols = slice(j * COL_TILE, (j + 1) * COL_TILE)
        z_ref[:, cols] = jnp.dot(h, w_ref[:, cols], preferred_element_type=f32).astype(bf16)
    for j, out_ref in enumerate((qT_ref, vT_ref, iqT_ref)):
        cols = slice(j * COL_TILE, (j + 1) * COL_TILE)
        zt = jnp.dot(h, wt_ref[:, cols], preferred_element_type=f32)
        out_ref[...] = zt.T.astype(bf16)
    s = jnp.dot(h, ws_ref[...], preferred_element_type=f32)
    ik_ref[...] = s[:, :IDX_HEAD_DIM].astype(bf16)
    sm = s[:, LANES:]
    iwT_ref[...] = sm.T[SM_IW0:SM_IW0 + N_IDX_HEADS, :]
    gqk = jnp.dot(h, wg_ref[...], preferred_element_type=f32)
    operands, decay = _gla_operands(gqk[:, :GLA_K_W], gqk[:, GLA_K_W:],
                                    sm[:, SM_GA0:SM_GA0 + GLA_GATE_RANK],
                                    w2_ref[...], gb_ref[...])
    gin_ref[...] = operands.astype(bf16)
    dec_ref[...] = decay


def _proj(x, g, w_main, w_t, w_g, w_small, w2, gb, batch):
    t = x.shape[0]
    tm = TOKEN_TILE
    s = t // batch
    per_b = s // tm
    fm = lambda rows: pl.BlockSpec((None, rows, tm), lambda i: (i // per_b, 0, i % per_b))
    whole = lambda a: pl.BlockSpec(a.shape, lambda i: (0, 0))
    return pl.pallas_call(
        _proj_kernel,
        grid=(t // tm,),
        in_specs=[
            pl.BlockSpec((tm, D_MODEL), lambda i: (i, 0)),
            whole(g), whole(w_main), whole(w_t), whole(w_g), whole(w_small), whole(w2), whole(gb),
        ],
        out_specs=[
            pl.BlockSpec((tm, Z_WIDTH), lambda i: (i, 0)),
            fm(ATTN_W), fm(ATTN_W), fm(IDX_Q_W),
            pl.BlockSpec((tm, IDX_HEAD_DIM), lambda i: (i, 0)),
            fm(N_IDX_HEADS),
            pl.BlockSpec((tm, GLA_IN_W), lambda i: (i, 0)),
            pl.BlockSpec((tm // CHUNK, GLA_K_W), lambda i: (i, 0)),
        ],
        out_shape=[
            jax.ShapeDtypeStruct((t, Z_WIDTH), bf16),
            jax.ShapeDtypeStruct((batch, ATTN_W, s), bf16),
            jax.ShapeDtypeStruct((batch, ATTN_W, s), bf16),
            jax.ShapeDtypeStruct((batch, IDX_Q_W, s), bf16),
            jax.ShapeDtypeStruct((t, IDX_HEAD_DIM), bf16),
            jax.ShapeDtypeStruct((batch, N_IDX_HEADS, s), f32),
            jax.ShapeDtypeStruct((t, GLA_IN_W), bf16),
            jax.ShapeDtypeStruct((t // CHUNK, GLA_K_W), f32),
        ],
        compiler_params=_cparams(("parallel",)),
        name="proj",
    )(x, g, w_main, w_t, w_g, w_small, w2, gb)


def _dsa_kernel(qT_ref, iqT_ref, iwT_ref, tab_ref, k_ref, vT_ref, ik_ref, o_ref,
                ibuf, qTm, s_ref, bias_ref, m_ref, l_ref, alpha_ref, acc_ref, thr_ref,
                ext_ref, pending_ref, finish_ref, *, seq_len):
    qb = DSA_BLOCK
    kt_rows = DSA_KEY_TILE
    diag_tiles = qb // kt_rows
    i = pl.program_id(1)
    n_tiles = (i + 1) * diag_tiles
    n_pairs = n_tiles // 2

    rowid = lax.broadcasted_iota(jnp.int32, (LANES, qb), 0)
    for h in range(N_ATTN_HEADS):
        pair = qT_ref[(h // 2) * LANES:(h // 2 + 1) * LANES, :]
        keep = (rowid < ATTN_HEAD_DIM) if h % 2 == 0 else (rowid >= ATTN_HEAD_DIM)
        qTm[h] = jnp.where(keep, pair, jnp.zeros_like(pair))
    l_ref[...] = jnp.zeros_like(l_ref)
    acc_ref[...] = jnp.zeros_like(acc_ref)

    def fold(x, op):
        return op(x.reshape(kt_rows // SUBLANES, SUBLANES, qb), axis=0)

    def idx_tile(kt, diagonal, stats):
        r0 = pl.multiple_of(kt * kt_rows, kt_rows)
        ki_t = ik_ref[pl.ds(r0, kt_rows), :]
        sc = jnp.zeros((kt_rows, qb), f32)
        for h in range(N_IDX_HEADS):
            s = jnp.dot(ki_t, iqT_ref[h * IDX_HEAD_DIM:(h + 1) * IDX_HEAD_DIM, :],
                        preferred_element_type=f32)
            sc = sc + iwT_ref[h:h + 1, :] * jnp.maximum(s, 0.0)
        sc_hi = sc_lo = sc_0 = sc
        if diagonal:
            kc = (r0 - i * qb + lax.broadcasted_iota(jnp.int32, (kt_rows, qb), 0)) // CHUNK
            qc = lax.broadcasted_iota(jnp.int32, (kt_rows, qb), 1) // CHUNK
            adm = kc <= qc
            sc_hi = jnp.where(adm, sc, -jnp.inf)
            sc_lo = jnp.where(adm, sc, jnp.inf)
            sc_0 = jnp.where(adm, sc, 0.0)
        ibuf[pl.ds(r0, kt_rows), :] = sc_hi
        mx8, mn8, s8, ss8 = stats
        return (jnp.maximum(mx8, fold(sc_hi, jnp.max)), jnp.minimum(mn8, fold(sc_lo, jnp.min)),
                s8 + fold(sc_0, jnp.sum), ss8 + fold(sc_0 * sc_0, jnp.sum))

    stats0 = (jnp.full((SUBLANES, qb), -jnp.inf, f32), jnp.full((SUBLANES, qb), jnp.inf, f32),
              jnp.zeros((SUBLANES, qb), f32), jnp.zeros((SUBLANES, qb), f32))
    stats = lax.fori_loop(
        0, n_pairs - diag_tiles // 2,
        lambda j, st: idx_tile(2 * j + 1, False, idx_tile(2 * j, False, st)), stats0)
    for d in range(diag_tiles):
        stats = idx_tile(n_tiles - diag_tiles + d, True, stats)
    mx8, mn8, s8, ss8 = stats
    col_max = jnp.max(mx8, axis=0, keepdims=True)
    col_min = jnp.min(mn8, axis=0, keepdims=True)
    col_sum = jnp.sum(s8, axis=0, keepdims=True)
    col_ssq = jnp.sum(ss8, axis=0, keepdims=True)

    def count(pred):
        def body(r, acc):
            r0 = pl.multiple_of(r * kt_rows, kt_rows)
            tile = ibuf[pl.ds(r0, kt_rows), :]
            rows = r0 + lax.broadcasted_iota(jnp.int32, (kt_rows, qb), 0)
            ind = jnp.where(pred(tile, rows), 1.0, 0.0)
            part = ind.reshape(kt_rows // SUBLANES, SUBLANES, qb)
            while part.shape[0] > 1:
                half = part.shape[0] // 2
                part = part[:half] + part[half:]
            return acc + part[0]
        acc = lax.fori_loop(0, n_pairs, lambda j, a: body(2 * j + 1, body(2 * j, a)),
                            jnp.zeros((SUBLANES, qb), f32))
        return jnp.sum(acc, axis=0, keepdims=True)

    def resolve_ties(tv, need, mult):
        is_tie = tv == tv
        n_keys = n_tiles * kt_rows

        def tie_cond(st):
            return st["active"] > 0

        def tie_body(st):
            plo, phi, flo, fhi = st["plo"], st["phi"], st["flo"], st["fhi"]
            span = phi - plo
            est = ((need - flo) / jnp.maximum(fhi - flo, 1.0) * span.astype(f32)).astype(jnp.int32)
            pick = jnp.where(st["bisect"] == 2, span // 2, est)
            cand = plo + jnp.clip(pick, 1, jnp.maximum(span - 1, 1))
            f = count(lambda tile, rows: jnp.where(tile == tv, rows, seq_len) < cand)
            lower = f < need
            exact = f == need
            plo = jnp.where(lower, cand, jnp.where(exact, cand - 1, plo))
            phi = jnp.where(lower, phi, cand)
            flo = jnp.where(lower, f, flo)
            fhi = jnp.where(lower, fhi, f)
            unsplit = jnp.where(is_tie & (phi - plo > 1), 1, 0)
            return dict(active=jnp.max(unsplit), bisect=(st["bisect"] + 1) % 3,
                        plo=plo, phi=phi, flo=flo, fhi=fhi)

        split = lax.while_loop(tie_cond, tie_body, dict(
            active=jnp.int32(1), bisect=jnp.int32(0),
            plo=jnp.zeros((1, qb), jnp.int32), phi=jnp.zeros((1, qb), jnp.int32) + n_keys,
            flo=jnp.zeros((1, qb), f32), fhi=mult))["phi"]

        def drop_body(r, carry):
            r0 = pl.multiple_of(r * kt_rows, kt_rows)
            tile = ibuf[pl.ds(r0, kt_rows), :]
            rows = r0 + lax.broadcasted_iota(jnp.int32, (kt_rows, qb), 0)
            dropped = jnp.where(tile == tv, rows, -1) >= split
            ibuf[pl.ds(r0, kt_rows), :] = jnp.where(dropped, -jnp.inf, tile)
            return carry

        lax.fori_loop(0, n_tiles, drop_body, 0)

    qpos = i * qb + lax.broadcasted_iota(jnp.int32, (1, qb), 1)
    n_adm = (qpos // CHUNK + 1) * CHUNK
    select_all = n_adm <= TOPK_MAX
    kf = float(TOPK_MAX)

    thr_ref[...] = jnp.full((1, qb), F32_LOWEST, f32)
    pending_ref[0] = jnp.int32(0)

    @pl.when(n_tiles * kt_rows > TOPK_MAX)
    def _():
        n_f = n_adm.astype(f32)
        mean = col_sum / n_f
        sigma = jnp.sqrt(jnp.maximum(col_ssq / n_f - mean * mean, 0.0))
        spacing = tab_ref[1:2, :] * sigma

        def inside(c, lo, hi):
            return (c > lo) & (c < hi)

        c0 = mean + tab_ref[0:1, :] * sigma
        c0 = jnp.where(inside(c0, col_min, col_max), c0, 0.5 * col_min + 0.5 * col_max)
        zero = jnp.zeros((1, qb), f32)
        one = jnp.ones((1, qb), f32)
        state0 = dict(it=jnp.int32(0), active=jnp.int32(1), c=c0, lo=col_min, hi=col_max,
                      glo=n_f, ghi=zero, tlo=zero, thi=zero, boost=one, side=zero,
                      thr=jnp.full((1, qb), F32_LOWEST, f32),
                      open=jnp.where(select_all, 0.0, 1.0), tie=zero, kind=zero, emin=zero)

        def search_cond(st):
            return (st["it"] < SEARCH_MAX_ITERS) & (st["active"] > 0)

        def search_body(st):
            c, lo, hi = st["c"], st["lo"], st["hi"]
            g = count(lambda tile, rows: tile >= c)
            hit = (g == kf) & (st["open"] > 0.0)
            thr = jnp.where(hit, c, st["thr"])
            still = jnp.where(hit, 0.0, st["open"])
            above = g > kf
            lo = jnp.where(above, c, lo)
            hi = jnp.where(above, hi, c)
            glo = jnp.where(above, g, st["glo"])
            ghi = jnp.where(above, st["ghi"], g)
            tlo = jnp.where(above, 1.0, st["tlo"])
            thi = jnp.where(above, st["thi"], 1.0)
            side = jnp.where(above, 1.0, -1.0)
            repeat = side == st["side"]
            boost = jnp.where(repeat, 2.0 * st["boost"], 1.0)
            mid = 0.5 * lo + 0.5 * hi
            stale = 1.0 / jnp.minimum(boost, 256.0)
            w_lo = (glo - kf - 0.5) * jnp.where(above, 1.0, stale)
            w_hi = (kf + 0.5 - ghi) * jnp.where(above, stale, 1.0)
            c_two = lo + w_lo / (w_lo + w_hi) * (hi - lo)
            step = 2.0 * boost * spacing
            c_one = jnp.where(tlo > 0.0, lo + jnp.maximum(glo - kf, 1.0) * step,
                              hi - jnp.maximum(kf - ghi, 1.0) * step)
            c_one = jnp.where(inside(c_one, lo, hi), c_one, mid)
            both = (tlo > 0.0) & (thi > 0.0)
            c_new = jnp.where(both, c_two, c_one)
            c_new = jnp.where(both & (lo < 0.0) & (hi > 0.0), 0.0, c_new)
            c_new = jnp.where((lo == 0.0) & (hi > F32_TINY), F32_TINY, c_new)
            movable = inside(c_new, lo, hi) & ~((lo == 0.0) & (hi <= F32_TINY))
            missed = (still > 0.0) & (((st["kind"] == 1.0) & above) |
                                      ((st["kind"] == 2.0) & (g < kf)))
            lo = jnp.where(missed & (st["kind"] == 2.0), st["emin"], lo)
            closed = (~movable & both & (still > 0.0)) | missed
            tie = jnp.where(closed, 1.0, st["tie"])
            still = jnp.where(closed, 0.0, still)
            searching = jnp.where(movable, still, 0.0)

            from_hi = both & (kf - ghi == 1.0)
            from_lo = both & (glo - kf == 1.0) & ~from_hi
            ready = jnp.where(searching > 0.0, jnp.where(from_hi | from_lo, 1.0, 2.0), 0.0)
            code = jnp.max(ready)
            ext_ref[...] = jnp.zeros_like(ext_ref)
            finish_ref[0] = jnp.int32(0)

            @pl.when((code == 1.0) & (st["it"] + 1 >= SEARCH_FINISH_FROM))
            def _():
                def ext_body(r, carry):
                    r0 = pl.multiple_of(r * kt_rows, kt_rows)
                    tile = ibuf[pl.ds(r0, kt_rows), :]
                    below = fold(jnp.where(tile < hi, tile, -jnp.inf), jnp.max)
                    above_lo = fold(jnp.where(tile >= lo, tile, jnp.inf), jnp.min)
                    return jnp.maximum(carry[0], below), jnp.minimum(carry[1], above_lo)
                mx8, mn8 = lax.fori_loop(0, n_tiles, ext_body,
                                         (jnp.full((SUBLANES, qb), -jnp.inf, f32),
                                          jnp.full((SUBLANES, qb), jnp.inf, f32)))
                ext_ref[0:1, :] = jnp.max(mx8, axis=0, keepdims=True)
                ext_ref[1:2, :] = jnp.min(mn8, axis=0, keepdims=True)
                finish_ref[0] = jnp.int32(1)

            finishing = finish_ref[0] > 0
            emax = ext_ref[0:1, :]
            emin = ext_ref[1:2, :]
            ebits = lax.bitcast_convert_type(emin, jnp.int32)
            next_up = lax.bitcast_convert_type(ebits + jnp.where(emin > 0.0, 1, -1), f32)
            next_up = jnp.where(emin == 0.0, F32_TINY, next_up)
            kind = jnp.where(finishing & (searching > 0.0),
                             jnp.where(from_hi, 1.0, jnp.where(from_lo, 2.0, 0.0)), 0.0)
            c_new = jnp.where(kind == 1.0, emax, jnp.where(kind == 2.0, next_up, c_new))
            return dict(it=st["it"] + 1, active=(code > 0.0).astype(jnp.int32),
                        c=c_new, lo=lo, hi=hi, glo=glo, ghi=ghi, tlo=tlo, thi=thi,
                        boost=boost, side=side, thr=thr, open=still, tie=tie,
                        kind=kind, emin=emin)

        final = lax.while_loop(search_cond, search_body, state0)
        tied = final["tie"] > 0.0
        thr_ref[...] = jnp.where(tied, final["lo"], final["thr"])
        pending_ref[0] = (jnp.max(final["open"]) > 0.0).astype(jnp.int32)

        @pl.when(jnp.max(final["tie"]) > 0.0)
        def _():
            resolve_ties(jnp.where(tied, final["lo"], jnp.nan), kf - final["ghi"],
                         final["glo"] - final["ghi"])

    @pl.when(pending_ref[0] > 0)
    def _():
        def key_to_f32(key):
            bits = jnp.where(key < 0, key ^ jnp.int32(0x7FFFFFFF), key)
            return lax.bitcast_convert_type(bits, f32)

        def bit_body(t, prefix):
            step = lax.shift_left(jnp.int32(1), jnp.int32(31) - t)
            cand = prefix + step
            cand_f = key_to_f32(cand)
            cnt = count(lambda tile, rows: tile >= cand_f)
            return jnp.where(cnt >= kf, cand, prefix)

        prefix = lax.fori_loop(0, 32, bit_body, jnp.full((1, qb), INT32_MIN, jnp.int32))
        thr = jnp.where(select_all, F32_LOWEST, key_to_f32(prefix))
        thr_ref[...] = thr

        cnt_ge = count(lambda tile, rows: tile >= thr)
        excess = jnp.where(select_all, 0.0, cnt_ge - kf)

        @pl.when(jnp.max(excess) > 0.0)
        def _():
            cnt_gt = count(lambda tile, rows: tile > thr)
            resolve_ties(jnp.where(excess > 0.0, thr, jnp.nan), kf - cnt_gt, cnt_ge - cnt_gt)

    thr = thr_ref[...]
    ones_rows = jnp.ones((2 * SUBLANES, kt_rows), bf16)

    m_ref[1] = jnp.full(m_ref.shape[1:], NEG_BIG, f32)

    def logits_stage(kt, slot):
        r0 = pl.multiple_of(kt * kt_rows, kt_rows)
        bias_ref[...] = jnp.where(ibuf[pl.ds(r0, kt_rows), :] >= thr, 0.0, NEG_BIG)
        for h in range(N_ATTN_HEADS):
            k_pair = k_ref[pl.ds(r0, kt_rows), (h // 2) * LANES:(h // 2 + 1) * LANES]
            s_ref[slot, h] = jnp.dot(k_pair, qTm[h], preferred_element_type=f32) + bias_ref[...]
            m_old = m_ref[1 - slot, h]
            m_new = jnp.maximum(m_old, jnp.max(s_ref[slot, h], axis=0, keepdims=True))
            alpha_ref[slot, h] = jnp.exp2(m_old - m_new)
            m_ref[slot, h] = m_new

    def values_stage(kt, slot):
        r0 = pl.multiple_of(kt * kt_rows, kt_rows)
        for h in range(N_ATTN_HEADS):
            p = jnp.exp2(s_ref[slot, h] - m_ref[slot, h]).astype(bf16)
            hs = slice(h * ATTN_HEAD_DIM, (h + 1) * ATTN_HEAD_DIM)
            lhs = jnp.concatenate([vT_ref[hs, pl.ds(r0, kt_rows)], ones_rows], axis=0)
            pv = jnp.dot(lhs, p, preferred_element_type=f32)
            alpha = alpha_ref[slot, h]
            acc_ref[hs, :] = alpha * acc_ref[hs, :] + pv[:ATTN_HEAD_DIM]
            l_ref[h] = alpha * l_ref[h] + pv[ATTN_HEAD_DIM:ATTN_HEAD_DIM + 1]

    def att_body(j, carry):
        kt = 2 * j
        logits_stage(kt + 1, 1)
        values_stage(kt, 0)
        logits_stage(kt + 2, 0)
        values_stage(kt + 1, 1)
        return carry

    logits_stage(0, 0)
    lax.fori_loop(0, n_pairs - 1, att_body, 0)
    logits_stage(n_tiles - 1, 1)
    values_stage(n_tiles - 2, 0)
    values_stage(n_tiles - 1, 1)

    for h in range(N_ATTN_HEADS):
        hs = slice(h * ATTN_HEAD_DIM, (h + 1) * ATTN_HEAD_DIM)
        acc_ref[hs, :] = acc_ref[hs, :] / l_ref[h]
    o_ref[...] = acc_ref[...].T.astype(o_ref.dtype)


def _search_hints(seq_len):
    n = (np.arange(seq_len) // CHUNK + 1) * CHUNK
    frac = np.minimum(TOPK_MAX / n, 0.5)
    nd = statistics.NormalDist()
    z_of = {f: nd.inv_cdf(1.0 - f) for f in np.unique(frac)}
    z = np.array([z_of[f] for f in frac])
    dens = n * np.exp(-0.5 * z * z) / np.sqrt(2.0 * np.pi)
    return jnp.asarray(np.stack([z, 1.0 / dens]), f32)


def _dsa(qT, iqT, iwT, z3, vT, ik3):
    b, _, s = qT.shape
    qb = DSA_BLOCK
    kt = DSA_KEY_TILE
    per_row = lambda shape, col: pl.BlockSpec(shape, lambda bi, i: (bi, 0, col),
                                              pipeline_mode=pl.Buffered(1))
    return pl.pallas_call(
        functools.partial(_dsa_kernel, seq_len=s),
        grid=(b, s // qb),
        in_specs=[
            pl.BlockSpec((None, ATTN_W, qb), lambda bi, i: (bi, 0, i)),
            pl.BlockSpec((None, IDX_Q_W, qb), lambda bi, i: (bi, 0, i)),
            pl.BlockSpec((None, N_IDX_HEADS, qb), lambda bi, i: (bi, 0, i)),
            pl.BlockSpec((2, qb), lambda bi, i: (0, i)),
            per_row((None, s, COL_TILE), Z_AK),
            per_row((None, ATTN_W, s), 0),
            per_row((None, s, IDX_HEAD_DIM), 0),
        ],
        out_specs=pl.BlockSpec((None, qb, ATTN_W), lambda bi, i: (bi, i, 0)),
        out_shape=jax.ShapeDtypeStruct((b, s, ATTN_W), bf16),
        scratch_shapes=[
            pltpu.VMEM((s, qb), f32),
            pltpu.VMEM((N_ATTN_HEADS, LANES, qb), bf16),
            pltpu.VMEM((2, N_ATTN_HEADS, kt, qb), f32),
            pltpu.VMEM((kt, qb), f32),
            pltpu.VMEM((2, N_ATTN_HEADS, 1, qb), f32),
            pltpu.VMEM((N_ATTN_HEADS, 1, qb), f32),
            pltpu.VMEM((2, N_ATTN_HEADS, 1, qb), f32),
            pltpu.VMEM((ATTN_W, qb), f32),
            pltpu.VMEM((1, qb), f32),
            pltpu.VMEM((2, qb), f32),
            pltpu.SMEM((1,), jnp.int32),
            pltpu.SMEM((1,), jnp.int32),
        ],
        compiler_params=_cparams(("parallel", "arbitrary")),
        name="dsa",
    )(qT, iqT, iwT, _search_hints(s), z3, vT, ik3)


def _gla_kernel(gin_ref, dec_ref, v_ref, r_ref, gn_ref, o_ref, state_ref):
    c = CHUNK
    n_batch = gin_ref.shape[0]

    @pl.when(pl.program_id(0) == 0)
    def _():
        state_ref[...] = jnp.zeros_like(state_ref)

    row = lax.broadcasted_iota(jnp.int32, (c, c), 0)
    col = lax.broadcasted_iota(jnp.int32, (c, c), 1)
    causal = row >= col
    lane = lax.broadcasted_iota(jnp.int32, (c, LANES), 1)
    half = (lane < GLA_HEAD_K, lane >= GLA_HEAD_K)
    contract_last = (((1,), (1,)), ((), ()))
    contract_first = (((0,), (0,)), ((), ()))
    zero = jnp.zeros((c, LANES), bf16)

    for ci, bi in [(ci, bi) for ci in range(GLA_TILE // c) for bi in range(n_batch)]:
        rows = slice(ci * c, (ci + 1) * c)
        decay = dec_ref[bi, ci:ci + 1, :]
        for pr in range(N_GLA_HEADS // 2):
            ps = slice(pr * LANES, (pr + 1) * LANES)
            part = lambda g: gin_ref[bi, rows, g * GLA_K_W + pr * LANES:
                                     g * GLA_K_W + (pr + 1) * LANES]
            q_in, q_mid, k_mid, k_out = part(0), part(1), part(2), part(3)
            st = state_ref[bi, pr]
            st_b = st.astype(bf16)
            upd = jnp.zeros_like(st)
            for hh in range(2):
                h = pr * 2 + hh
                vs = slice(h * GLA_HEAD_V, (h + 1) * GLA_HEAD_V)
                v_h = v_ref[bi, rows, vs]
                attn = lax.dot_general(jnp.where(half[hh], q_mid, zero), k_mid, contract_last,
                                       preferred_element_type=f32)
                attn = jnp.where(causal, attn, 0.0).astype(bf16)
                o = lax.dot_general(jnp.where(half[hh], q_in, zero), st_b, contract_last,
                                    preferred_element_type=f32)
                o = o + jnp.dot(attn, v_h, preferred_element_type=f32)
                upd = upd + lax.dot_general(v_h, jnp.where(half[hh], k_out, zero),
                                            contract_first, preferred_element_type=f32)
                y = _rms(o, gn_ref[...])
                r = r_ref[bi, rows, vs].astype(f32)
                o_ref[bi, rows, vs] = (y * r * _sigmoid(r)).astype(o_ref.dtype)
            state_ref[bi, pr] = decay[:, ps] * st + upd


def _gla(gin3, dec3, z3, gn):
    b, s, _ = z3.shape
    tb = GLA_TILE
    return pl.pallas_call(
        _gla_kernel,
        grid=(s // tb,),
        in_specs=[
            pl.BlockSpec((b, tb, GLA_IN_W), lambda j: (0, j, 0)),
            pl.BlockSpec((b, tb // CHUNK, GLA_K_W), lambda j: (0, j, 0)),
            pl.BlockSpec((b, tb, COL_TILE), lambda j: (0, j, Z_GV)),
            pl.BlockSpec((b, tb, COL_TILE), lambda j: (0, j, Z_GR)),
            pl.BlockSpec((1, GLA_HEAD_V), lambda j: (0, 0)),
        ],
        out_specs=pl.BlockSpec((b, tb, GLA_V_W), lambda j: (0, j, 0)),
        out_shape=jax.ShapeDtypeStruct((b, s, GLA_V_W), bf16),
        scratch_shapes=[pltpu.VMEM((b, N_GLA_HEADS // 2, GLA_HEAD_V, LANES), f32)],
        compiler_params=_cparams(("arbitrary",)),
        name="gla",
    )(gin3, dec3, z3, z3, gn)


def _merge_kernel(x_ref, oa_ref, ob_ref, ga_ref, gb_ref, wa_ref, wb_ref, wo_ref, o_ref):
    ya = jnp.dot(oa_ref[...], wa_ref[...], preferred_element_type=f32)
    yb = jnp.dot(ob_ref[...], wb_ref[...], preferred_element_type=f32)
    mg = _sigmoid(ga_ref[...].astype(f32)) * ya + _sigmoid(gb_ref[...].astype(f32)) * yb
    o_ref[...] = x_ref[...] + jnp.dot(mg.astype(bf16), wo_ref[...], preferred_element_type=f32)


def _merge(x, oa, ob, z, wa, wb, wo):
    t = x.shape[0]
    tm = TOKEN_TILE
    return pl.pallas_call(
        _merge_kernel,
        grid=(t // tm,),
        in_specs=[
            pl.BlockSpec((tm, D_MODEL), lambda i: (i, 0)),
            pl.BlockSpec((tm, ATTN_W), lambda i: (i, 0)),
            pl.BlockSpec((tm, GLA_V_W), lambda i: (i, 0)),
            pl.BlockSpec((tm, D_MODEL), lambda i: (i, Z_MGA)),
            pl.BlockSpec((tm, D_MODEL), lambda i: (i, Z_MGB)),
            pl.BlockSpec((ATTN_W, D_MODEL), lambda i: (0, 0)),
            pl.BlockSpec((GLA_V_W, D_MODEL), lambda i: (0, 0)),
            pl.BlockSpec((D_MODEL, D_MODEL), lambda i: (0, 0)),
        ],
        out_specs=pl.BlockSpec((tm, D_MODEL), lambda i: (i, 0)),
        out_shape=jax.ShapeDtypeStruct((t, D_MODEL), f32),
        compiler_params=_cparams(("parallel",)),
        name="merge",
    )(x, oa, ob, z, z, wa, wb, wo)


def _ple_kernel(x_ref, p_ref, g_ref, wg_ref, wp_ref, gf_ref, o_ref, *, final):
    x = x_ref[...]
    h = _rms(x, g_ref[...]).astype(bf16)
    gate = _sigmoid(jnp.dot(h, wg_ref[...], preferred_element_type=f32))
    e = jnp.dot(p_ref[...].astype(bf16), wp_ref[...], preferred_element_type=f32)
    y = x + gate * e
    if final:
        y = _rms(y, gf_ref[...])
    o_ref[...] = y


def _ple(x, p, g, wg, wp, gf, final):
    t = x.shape[0]
    tm = TOKEN_TILE
    return pl.pallas_call(
        functools.partial(_ple_kernel, final=final),
        grid=(t // tm,),
        in_specs=[
            pl.BlockSpec((tm, D_MODEL), lambda i: (i, 0)),
            pl.BlockSpec((tm, PLE_DIM), lambda i: (i, 0)),
            pl.BlockSpec((1, D_MODEL), lambda i: (0, 0)),
            pl.BlockSpec((D_MODEL, D_MODEL), lambda i: (0, 0)),
            pl.BlockSpec((PLE_DIM, D_MODEL), lambda i: (0, 0)),
            pl.BlockSpec((1, D_MODEL), lambda i: (0, 0)),
        ],
        out_specs=pl.BlockSpec((tm, D_MODEL), lambda i: (i, 0)),
        out_shape=jax.ShapeDtypeStruct((t, D_MODEL), f32),
        compiler_params=_cparams(("parallel",)),
        name="ple",
    )(x, p, g, wg, wp, gf)


def _split_w_in(w_in):
    cuts = np.cumsum(np.array(SPLIT_SIZES))[:-1].tolist()
    aq, ak, av, iq, ik, iw, gq, gk, gv, gr, ga, mga, mgb = jnp.split(w_in, cuts, axis=-1)
    att_scale = ATTN_HEAD_DIM ** -0.5 * float(np.log2(np.e))
    idx_scale = IDX_HEAD_DIM ** -0.5 * N_IDX_HEADS ** -0.5
    main = jnp.concatenate([mga, mgb, ak, gv, gr], axis=-1)
    feature_major = jnp.concatenate([aq * att_scale, av, iq], axis=-1)
    gla_qk = jnp.concatenate([gq * (GLA_HEAD_K ** -0.5), gk], axis=-1)
    zeros = lambda n: jnp.zeros(w_in.shape[:-1] + (n,), w_in.dtype)
    small = jnp.concatenate(
        [ik, zeros(LANES - IDX_HEAD_DIM), iw * idx_scale, ga,
         zeros(LANES - N_IDX_HEADS - GLA_GATE_RANK)], axis=-1)
    return main.astype(bf16), feature_major.astype(bf16), gla_qk.astype(bf16), small.astype(bf16)


def kernel(x, p, w_in, gla_gate_w2, gla_gate_b, gla_norm, w_branch_a, w_branch_b, w_out,
           norm_ff1, norm_mix, norm_ff2, norm_ple, ff1_w_gate, ff1_w_up, ff1_w_down,
           ff2_w_gate, ff2_w_up, ff2_w_down, ple_w_proj, ple_w_gate, norm_final):
    b, s, d = x.shape
    t = b * s
    depth = w_in.shape[0]
    w_main, w_fm, w_gqk, w_small = _split_w_in(w_in)
    cast = lambda w: w.astype(bf16)
    ff1 = (cast(ff1_w_gate), cast(ff1_w_up), cast(ff1_w_down))
    ff2 = (cast(ff2_w_gate), cast(ff2_w_up), cast(ff2_w_down))
    wa, wb, wo = cast(w_branch_a), cast(w_branch_b), cast(w_out)
    wpg, wpp = cast(ple_w_gate), cast(ple_w_proj)
    row = lambda v: v.reshape(1, -1)

    xf = x.reshape(t, d)
    for l in range(depth):
        xf = _ffn(xf, row(norm_ff1[l]), ff1[0][l], ff1[1][l], ff1[2][l])

        z, qT, vT, iqT, ik, iwT, gin, dec = _proj(
            xf, row(norm_mix[l]), w_main[l], w_fm[l], w_gqk[l], w_small[l],
            gla_gate_w2[l], row(gla_gate_b[l]), b)
        z3 = z.reshape(b, s, Z_WIDTH)
        oa = _dsa(qT, iqT, iwT, z3, vT, ik.reshape(b, s, IDX_HEAD_DIM))
        ob = _gla(gin.reshape(b, s, GLA_IN_W), dec.reshape(b, s // CHUNK, GLA_K_W), z3,
                  row(gla_norm[l]))
        xf = _merge(xf, oa.reshape(t, ATTN_W), ob.reshape(t, GLA_V_W), z, wa[l], wb[l], wo[l])

        xf = _ffn(xf, row(norm_ff2[l]), ff2[0][l], ff2[1][l], ff2[2][l])
        xf = _ple(xf, p[l].reshape(t, PLE_DIM), row(norm_ple[l]), wpg[l], wpp[l],
                  row(norm_final), final=(l == depth - 1))
    return xf.reshape(b, s, d)
```

```python
import functools
import statistics

import jax
import jax.numpy as jnp
import numpy as np
from jax import lax
from jax.experimental import pallas as pl
from jax.experimental.pallas import tpu as pltpu

D_MODEL = 1024
D_FF = 2816
PLE_DIM = 256
DEPTH = 2
EPS = 1e-6

CHUNK = 64
N_ATTN_HEADS = 8
ATTN_HEAD_DIM = 64
N_IDX_HEADS = 8
IDX_HEAD_DIM = 64
TOPK_MAX = 256
N_GLA_HEADS = 4
GLA_HEAD_K = 64
GLA_HEAD_V = 128
GLA_GATE_RANK = 16
GLA_GATE_TAU = 16.0

ATTN_W = N_ATTN_HEADS * ATTN_HEAD_DIM
IDX_Q_W = N_IDX_HEADS * IDX_HEAD_DIM
GLA_K_W = N_GLA_HEADS * GLA_HEAD_K
GLA_V_W = N_GLA_HEADS * GLA_HEAD_V
SPLIT_SIZES = (ATTN_W, ATTN_W, ATTN_W, IDX_Q_W, IDX_HEAD_DIM, N_IDX_HEADS,
               GLA_K_W, GLA_K_W, GLA_V_W, GLA_V_W, GLA_GATE_RANK, D_MODEL, D_MODEL)

LANES = 128
SUBLANES = 8
VMEM_LIMIT_BYTES = 56 * 1024 * 1024

COL_TILE = 512
Z_MGA, Z_MGB = 0, 1
Z_AK, Z_GV, Z_GR = 4, 5, 6
Z_TILES = 7
Z_WIDTH = Z_TILES * COL_TILE
ZT_TILES = 3
SM_IW0 = 0
SM_GA0 = N_IDX_HEADS

TOKEN_TILE = 512
FF_TILE = 1408
DSA_BLOCK = 256
GLA_TILE = 512
GLA_IN_W = 4 * GLA_K_W
SEARCH_MAX_ITERS = 24
SEARCH_FINISH_FROM = 4

NEG_BIG = -1e30
F32_LOWEST = float(np.finfo(np.float32).min)
F32_TINY = float(np.finfo(np.float32).tiny)
INT32_MIN = int(np.iinfo(np.int32).min)

f32 = jnp.float32
bf16 = jnp.bfloat16


def _rms(x, g):
    return x * lax.rsqrt(jnp.mean(x * x, axis=-1, keepdims=True) + EPS) * g


def _sigmoid(x):
    return 1.0 / (1.0 + jnp.exp(-x))


def _cparams(sem):
    return pltpu.CompilerParams(dimension_semantics=sem, vmem_limit_bytes=VMEM_LIMIT_BYTES)


def _resident(a):
    return pl.BlockSpec(a.shape, lambda i: (0, 0), pipeline_mode=pl.Buffered(1))


def _swiglu_half_step(x, g_ref, wg_ref, wu_ref, wd_ref):
    h = _rms(x, g_ref[...]).astype(bf16)
    y = x
    for f in range(D_FF // FF_TILE):
        cols = slice(f * FF_TILE, (f + 1) * FF_TILE)
        gate = jnp.dot(h, wg_ref[:, cols], preferred_element_type=f32)
        up = jnp.dot(h, wu_ref[:, cols], preferred_element_type=f32)
        a = (gate * _sigmoid(gate) * up).astype(bf16)
        y = y + 0.5 * jnp.dot(a, wd_ref[cols, :], preferred_element_type=f32)
    return y


def _ffn_kernel(x_ref, g_ref, wg_ref, wu_ref, wd_ref, o_ref):
    o_ref[...] = _swiglu_half_step(x_ref[...], g_ref, wg_ref, wu_ref, wd_ref)


def _ffn(x, g, wg, wu, wd):
    t = x.shape[0]
    return pl.pallas_call(
        _ffn_kernel,
        grid=(t // TOKEN_TILE,),
        in_specs=[
            pl.BlockSpec((TOKEN_TILE, D_MODEL), lambda i: (i, 0)),
            _resident(g), _resident(wg), _resident(wu), _resident(wd),
        ],
        out_specs=pl.BlockSpec((TOKEN_TILE, D_MODEL), lambda i: (i, 0)),
        out_shape=jax.ShapeDtypeStruct((t, D_MODEL), f32),
        compiler_params=_cparams(("parallel",)),
        name="ffn",
    )(x, g, wg, wu, wd)


def _gla_operands(q, k, ga, w2, gb):
    n = q.shape[0]
    c = CHUNK
    glin = jnp.dot(ga, w2, preferred_element_type=f32, precision=lax.Precision.HIGHEST) + gb
    logg = (jnp.minimum(glin, 0.0) - jnp.log(1.0 + jnp.exp(-jnp.abs(glin)))) / GLA_GATE_TAU
    row = lax.broadcasted_iota(jnp.int32, (n, n), 0)
    col = lax.broadcasted_iota(jnp.int32, (n, n), 1)
    tril = jnp.where((row // c == col // c) & (col <= row), 1.0, 0.0).astype(bf16)
    hi = logg.astype(bf16)
    lo = (logg - hi.astype(f32)).astype(bf16)
    bcum = (jnp.dot(tril, hi, preferred_element_type=f32)
            + jnp.dot(tril, lo, preferred_element_type=f32))
    by_chunk = bcum.reshape(n // c, c, GLA_K_W)
    spread = lambda r: jnp.broadcast_to(r, (n // c, c, GLA_K_W)).reshape(n, GLA_K_W)
    b_last = by_chunk[:, c - 1:c, :]
    b_mid = spread(by_chunk[:, c // 2 - 1:c // 2, :])
    operands = jnp.concatenate(
        [q * jnp.exp(bcum), q * jnp.exp(bcum - b_mid), k * jnp.exp(b_mid - bcum),
         k * jnp.exp(spread(b_last) - bcum)], axis=1)
    return operands, jnp.exp(b_last.reshape(n // c, GLA_K_W))


def _proj_kernel(x_ref, g_ref, w_ref, wt_ref, wg_ref, ws_ref, w2_ref, gb_ref,
                 z_ref, qT_ref, vT_ref, iqT_ref, ik_ref, iwT_ref, gin_ref, dec_ref):
    h = _rms(x_ref[...], g_ref[...]).astype(bf16)
    for j in range(Z_TILES):
        cols = slice(j * COL_TILE, (j + 1) * COL_TILE)
        z_ref[:, cols] = jnp.dot(h, w_ref[:, cols], preferred_element_type=f32).astype(bf16)
    for j, out_ref in enumerate((qT_ref, vT_ref, iqT_ref)):
        cols = slice(j * COL_TILE, (j + 1) * COL_TILE)
        zt = jnp.dot(h, wt_ref[:, cols], preferred_element_type=f32)
        out_ref[...] = zt.T.astype(bf16)
    s = jnp.dot(h, ws_ref[...], preferred_element_type=f32)
    ik_ref[...] = s[:, :IDX_HEAD_DIM].astype(bf16)
    sm = s[:, LANES:]
    iwT_ref[...] = sm.T[SM_IW0:SM_IW0 + N_IDX_HEADS, :]
    gqk = jnp.dot(h, wg_ref[...], preferred_element_type=f32)
    operands, decay = _gla_operands(gqk[:, :GLA_K_W], gqk[:, GLA_K_W:],
                                    sm[:, SM_GA0:SM_GA0 + GLA_GATE_RANK],
                                    w2_ref[...], gb_ref[...])
    gin_ref[...] = operands.astype(bf16)
    dec_ref[...] = decay


def _proj(x, g, w_main, w_t, w_g, w_small, w2, gb, batch):
    t = x.shape[0]
    tm = TOKEN_TILE
    s = t // batch
    per_b = s // tm
    fm = lambda rows: pl.BlockSpec((None, rows, tm), lambda i: (i // per_b, 0, i % per_b))
    whole = lambda a: pl.BlockSpec(a.shape, lambda i: (0, 0))
    return pl.pallas_call(
        _proj_kernel,
        grid=(t // tm,),
        in_specs=[
            pl.BlockSpec((tm, D_MODEL), lambda i: (i, 0)),
            whole(g), whole(w_main), whole(w_t), whole(w_g), whole(w_small), whole(w2), whole(gb),
        ],
        out_specs=[
            pl.BlockSpec((tm, Z_WIDTH), lambda i: (i, 0)),
            fm(ATTN_W), fm(ATTN_W), fm(IDX_Q_W),
            pl.BlockSpec((tm, IDX_HEAD_DIM), lambda i: (i, 0)),
            fm(N_IDX_HEADS),
            pl.BlockSpec((tm, GLA_IN_W), lambda i: (i, 0)),
            pl.BlockSpec((tm // CHUNK, GLA_K_W), lambda i: (i, 0)),
        ],
        out_shape=[
            jax.ShapeDtypeStruct((t, Z_WIDTH), bf16),
            jax.ShapeDtypeStruct((batch, ATTN_W, s), bf16),
            jax.ShapeDtypeStruct((batch, ATTN_W, s), bf16),
            jax.ShapeDtypeStruct((batch, IDX_Q_W, s), bf16),
            jax.ShapeDtypeStruct((t, IDX_HEAD_DIM), bf16),
            jax.ShapeDtypeStruct((batch, N_IDX_HEADS, s), f32),
            jax.ShapeDtypeStruct((t, GLA_IN_W), bf16),
            jax.ShapeDtypeStruct((t // CHUNK, GLA_K_W), f32),
        ],
        compiler_params=_cparams(("parallel",)),
        name="proj",
    )(x, g, w_main, w_t, w_g, w_small, w2, gb)


def _dsa_kernel(qT_ref, iqT_ref, iwT_ref, tab_ref, k_ref, vT_ref, ik_ref, o_ref,
                ibuf, qTm, s_ref, bias_ref, m_ref, l_ref, alpha_ref, acc_ref, thr_ref,
                ext_ref, pending_ref, finish_ref, *, seq_len):
    qb = DSA_BLOCK
    kt_rows = DSA_BLOCK
    i = pl.program_id(1)
    n_tiles = i + 1

    rowid = lax.broadcasted_iota(jnp.int32, (LANES, qb), 0)
    for h in range(N_ATTN_HEADS):
        pair = qT_ref[(h // 2) * LANES:(h // 2 + 1) * LANES, :]
        keep = (rowid < ATTN_HEAD_DIM) if h % 2 == 0 else (rowid >= ATTN_HEAD_DIM)
        qTm[h] = jnp.where(keep, pair, jnp.zeros_like(pair))
    l_ref[...] = jnp.zeros_like(l_ref)
    acc_ref[...] = jnp.zeros_like(acc_ref)

    def fold(x, op):
        return op(x.reshape(kt_rows // SUBLANES, SUBLANES, qb), axis=0)

    def idx_tile(kt, diagonal, stats):
        r0 = pl.multiple_of(kt * kt_rows, kt_rows)
        ki_t = ik_ref[pl.ds(r0, kt_rows), :]
        sc = jnp.zeros((kt_rows, qb), f32)
        for h in range(N_IDX_HEADS):
            s = jnp.dot(ki_t, iqT_ref[h * IDX_HEAD_DIM:(h + 1) * IDX_HEAD_DIM, :],
                        preferred_element_type=f32)
            sc = sc + iwT_ref[h:h + 1, :] * jnp.maximum(s, 0.0)
        sc_hi = sc_lo = sc_0 = sc
        if diagonal:
            kc = lax.broadcasted_iota(jnp.int32, (kt_rows, qb), 0) // CHUNK
            qc = lax.broadcasted_iota(jnp.int32, (kt_rows, qb), 1) // CHUNK
            adm = kc <= qc
            sc_hi = jnp.where(adm, sc, -jnp.inf)
            sc_lo = jnp.where(adm, sc, jnp.inf)
            sc_0 = jnp.where(adm, sc, 0.0)
        ibuf[pl.ds(r0, kt_rows), :] = sc_hi
        mx8, mn8, s8, ss8 = stats
        return (jnp.maximum(mx8, fold(sc_hi, jnp.max)), jnp.minimum(mn8, fold(sc_lo, jnp.min)),
                s8 + fold(sc_0, jnp.sum), ss8 + fold(sc_0 * sc_0, jnp.sum))

    stats0 = (jnp.full((SUBLANES, qb), -jnp.inf, f32), jnp.full((SUBLANES, qb), jnp.inf, f32),
              jnp.zeros((SUBLANES, qb), f32), jnp.zeros((SUBLANES, qb), f32))
    odd = i % 2
    stats = lax.cond(odd == 1, lambda st: idx_tile(0, False, st), lambda st: st, stats0)
    stats = lax.fori_loop(
        0, i // 2,
        lambda j, st: idx_tile(odd + 2 * j + 1, False, idx_tile(odd + 2 * j, False, st)),
        stats)
    mx8, mn8, s8, ss8 = idx_tile(i, True, stats)
    col_max = jnp.max(mx8, axis=0, keepdims=True)
    col_min = jnp.min(mn8, axis=0, keepdims=True)
    col_sum = jnp.sum(s8, axis=0, keepdims=True)
    col_ssq = jnp.sum(ss8, axis=0, keepdims=True)

    def count(pred):
        def body(r, acc):
            r0 = pl.multiple_of(r * kt_rows, kt_rows)
            tile = ibuf[pl.ds(r0, kt_rows), :]
            rows = r0 + lax.broadcasted_iota(jnp.int32, (kt_rows, qb), 0)
            ind = jnp.where(pred(tile, rows), 1.0, 0.0)
            part = ind.reshape(kt_rows // SUBLANES, SUBLANES, qb)
            while part.shape[0] > 1:
                half = part.shape[0] // 2
                part = part[:half] + part[half:]
            return acc + part[0]
        acc = jnp.zeros((SUBLANES, qb), f32)
        acc = lax.cond(i % 2 == 0, lambda a: body(0, a), lambda a: a, acc)
        first = 1 - i % 2
        acc = lax.fori_loop(
            0, (i + 1) // 2,
            lambda j, a: body(first + 2 * j + 1, body(first + 2 * j, a)), acc)
        return jnp.sum(acc, axis=0, keepdims=True)

    def resolve_ties(tv, need, mult):
        is_tie = tv == tv
        n_keys = n_tiles * kt_rows

        def tie_cond(st):
            return st["active"] > 0

        def tie_body(st):
            plo, phi, flo, fhi = st["plo"], st["phi"], st["flo"], st["fhi"]
            span = phi - plo
            est = ((need - flo) / jnp.maximum(fhi - flo, 1.0) * span.astype(f32)).astype(jnp.int32)
            pick = jnp.where(st["bisect"] == 2, span // 2, est)
            cand = plo + jnp.clip(pick, 1, jnp.maximum(span - 1, 1))
            f = count(lambda tile, rows: jnp.where(tile == tv, rows, seq_len) < cand)
            lower = f < need
            exact = f == need
            plo = jnp.where(lower, cand, jnp.where(exact, cand - 1, plo))
            phi = jnp.where(lower, phi, cand)
            flo = jnp.where(lower, f, flo)
            fhi = jnp.where(lower, fhi, f)
            unsplit = jnp.where(is_tie & (phi - plo > 1), 1, 0)
            return dict(active=jnp.max(unsplit), bisect=(st["bisect"] + 1) % 3,
                        plo=plo, phi=phi, flo=flo, fhi=fhi)

        split = lax.while_loop(tie_cond, tie_body, dict(
            active=jnp.int32(1), bisect=jnp.int32(0),
            plo=jnp.zeros((1, qb), jnp.int32), phi=jnp.zeros((1, qb), jnp.int32) + n_keys,
            flo=jnp.zeros((1, qb), f32), fhi=mult))["phi"]

        def drop_body(r, carry):
            r0 = pl.multiple_of(r * kt_rows, kt_rows)
            tile = ibuf[pl.ds(r0, kt_rows), :]
            rows = r0 + lax.broadcasted_iota(jnp.int32, (kt_rows, qb), 0)
            dropped = jnp.where(tile == tv, rows, -1) >= split
            ibuf[pl.ds(r0, kt_rows), :] = jnp.where(dropped, -jnp.inf, tile)
            return carry

        lax.fori_loop(0, n_tiles, drop_body, 0)

    qpos = i * qb + lax.broadcasted_iota(jnp.int32, (1, qb), 1)
    n_adm = (qpos // CHUNK + 1) * CHUNK
    select_all = n_adm <= TOPK_MAX
    kf = float(TOPK_MAX)

    thr_ref[...] = jnp.full((1, qb), F32_LOWEST, f32)
    pending_ref[0] = jnp.int32(0)

    @pl.when(n_tiles * kt_rows > TOPK_MAX)
    def _():
        n_f = n_adm.astype(f32)
        mean = col_sum / n_f
        sigma = jnp.sqrt(jnp.maximum(col_ssq / n_f - mean * mean, 0.0))
        spacing = tab_ref[1:2, :] * sigma

        def inside(c, lo, hi):
            return (c > lo) & (c < hi)

        c0 = mean + tab_ref[0:1, :] * sigma
        c0 = jnp.where(inside(c0, col_min, col_max), c0, 0.5 * col_min + 0.5 * col_max)
        zero = jnp.zeros((1, qb), f32)
        one = jnp.ones((1, qb), f32)
        state0 = dict(it=jnp.int32(0), active=jnp.int32(1), c=c0, lo=col_min, hi=col_max,
                      glo=n_f, ghi=zero, tlo=zero, thi=zero, boost=one, side=zero,
                      thr=jnp.full((1, qb), F32_LOWEST, f32),
                      open=jnp.where(select_all, 0.0, 1.0), tie=zero, kind=zero, emin=zero)

        def search_cond(st):
            return (st["it"] < SEARCH_MAX_ITERS) & (st["active"] > 0)

        def search_body(st):
            c, lo, hi = st["c"], st["lo"], st["hi"]
            g = count(lambda tile, rows: tile >= c)
            hit = (g == kf) & (st["open"] > 0.0)
            thr = jnp.where(hit, c, st["thr"])
            still = jnp.where(hit, 0.0, st["open"])
            above = g > kf
            lo = jnp.where(above, c, lo)
            hi = jnp.where(above, hi, c)
            glo = jnp.where(above, g, st["glo"])
            ghi = jnp.where(above, st["ghi"], g)
            tlo = jnp.where(above, 1.0, st["tlo"])
            thi = jnp.where(above, st["thi"], 1.0)
            side = jnp.where(above, 1.0, -1.0)
            repeat = side == st["side"]
            boost = jnp.where(repeat, 2.0 * st["boost"], 1.0)
            mid = 0.5 * lo + 0.5 * hi
            stale = 1.0 / jnp.minimum(boost, 256.0)
            w_lo = (glo - kf - 0.5) * jnp.where(above, 1.0, stale)
            w_hi = (kf + 0.5 - ghi) * jnp.where(above, stale, 1.0)
            c_two = lo + w_lo / (w_lo + w_hi) * (hi - lo)
            step = 2.0 * boost * spacing
            c_one = jnp.where(tlo > 0.0, lo + jnp.maximum(glo - kf, 1.0) * step,
                              hi - jnp.maximum(kf - ghi, 1.0) * step)
            c_one = jnp.where(inside(c_one, lo, hi), c_one, mid)
            both = (tlo > 0.0) & (thi > 0.0)
            c_new = jnp.where(both, c_two, c_one)
            c_new = jnp.where(both & (lo < 0.0) & (hi > 0.0), 0.0, c_new)
            c_new = jnp.where((lo == 0.0) & (hi > F32_TINY), F32_TINY, c_new)
            movable = inside(c_new, lo, hi) & ~((lo == 0.0) & (hi <= F32_TINY))
            missed = (still > 0.0) & (((st["kind"] == 1.0) & above) |
                                      ((st["kind"] == 2.0) & (g < kf)))
            lo = jnp.where(missed & (st["kind"] == 2.0), st["emin"], lo)
            closed = (~movable & both & (still > 0.0)) | missed
            tie = jnp.where(closed, 1.0, st["tie"])
            still = jnp.where(closed, 0.0, still)
            searching = jnp.where(movable, still, 0.0)

            from_hi = both & (kf - ghi == 1.0)
            from_lo = both & (glo - kf == 1.0) & ~from_hi
            ready = jnp.where(searching > 0.0, jnp.where(from_hi | from_lo, 1.0, 2.0), 0.0)
            code = jnp.max(ready)
            ext_ref[...] = jnp.zeros_like(ext_ref)
            finish_ref[0] = jnp.int32(0)

            @pl.when((code == 1.0) & (st["it"] + 1 >= SEARCH_FINISH_FROM))
            def _():
                def ext_body(r, carry):
                    r0 = pl.multiple_of(r * kt_rows, kt_rows)
                    tile = ibuf[pl.ds(r0, kt_rows), :]
                    below = fold(jnp.where(tile < hi, tile, -jnp.inf), jnp.max)
                    above_lo = fold(jnp.where(tile >= lo, tile, jnp.inf), jnp.min)
                    return jnp.maximum(carry[0], below), jnp.minimum(carry[1], above_lo)
                mx8, mn8 = lax.fori_loop(0, n_tiles, ext_body,
                                         (jnp.full((SUBLANES, qb), -jnp.inf, f32),
                                          jnp.full((SUBLANES, qb), jnp.inf, f32)))
                ext_ref[0:1, :] = jnp.max(mx8, axis=0, keepdims=True)
                ext_ref[1:2, :] = jnp.min(mn8, axis=0, keepdims=True)
                finish_ref[0] = jnp.int32(1)

            finishing = finish_ref[0] > 0
            emax = ext_ref[0:1, :]
            emin = ext_ref[1:2, :]
            ebits = lax.bitcast_convert_type(emin, jnp.int32)
            next_up = lax.bitcast_convert_type(ebits + jnp.where(emin > 0.0, 1, -1), f32)
            next_up = jnp.where(emin == 0.0, F32_TINY, next_up)
            kind = jnp.where(finishing & (searching > 0.0),
                             jnp.where(from_hi, 1.0, jnp.where(from_lo, 2.0, 0.0)), 0.0)
            c_new = jnp.where(kind == 1.0, emax, jnp.where(kind == 2.0, next_up, c_new))
            return dict(it=st["it"] + 1, active=(code > 0.0).astype(jnp.int32),
                        c=c_new, lo=lo, hi=hi, glo=glo, ghi=ghi, tlo=tlo, thi=thi,
                        boost=boost, side=side, thr=thr, open=still, tie=tie,
                        kind=kind, emin=emin)

        final = lax.while_loop(search_cond, search_body, state0)
        tied = final["tie"] > 0.0
        thr_ref[...] = jnp.where(tied, final["lo"], final["thr"])
        pending_ref[0] = (jnp.max(final["open"]) > 0.0).astype(jnp.int32)

        @pl.when(jnp.max(final["tie"]) > 0.0)
        def _():
            resolve_ties(jnp.where(tied, final["lo"], jnp.nan), kf - final["ghi"],
                         final["glo"] - final["ghi"])

    @pl.when(pending_ref[0] > 0)
    def _():
        def key_to_f32(key):
            bits = jnp.where(key < 0, key ^ jnp.int32(0x7FFFFFFF), key)
            return lax.bitcast_convert_type(bits, f32)

        def bit_body(t, prefix):
            step = lax.shift_left(jnp.int32(1), jnp.int32(31) - t)
            cand = prefix + step
            cand_f = key_to_f32(cand)
            cnt = count(lambda tile, rows: tile >= cand_f)
            return jnp.where(cnt >= kf, cand, prefix)

        prefix = lax.fori_loop(0, 32, bit_body, jnp.full((1, qb), INT32_MIN, jnp.int32))
        thr = jnp.where(select_all, F32_LOWEST, key_to_f32(prefix))
        thr_ref[...] = thr

        cnt_ge = count(lambda tile, rows: tile >= thr)
        excess = jnp.where(select_all, 0.0, cnt_ge - kf)

        @pl.when(jnp.max(excess) > 0.0)
        def _():
            cnt_gt = count(lambda tile, rows: tile > thr)
            resolve_ties(jnp.where(excess > 0.0, thr, jnp.nan), kf - cnt_gt, cnt_ge - cnt_gt)

    thr = thr_ref[...]
    ones_rows = jnp.ones((2 * SUBLANES, kt_rows), bf16)

    m_ref[1] = jnp.full(m_ref.shape[1:], NEG_BIG, f32)

    def logits_stage(kt, slot):
        r0 = pl.multiple_of(kt * kt_rows, kt_rows)
        bias_ref[...] = jnp.where(ibuf[pl.ds(r0, kt_rows), :] >= thr, 0.0, NEG_BIG)
        for h in range(N_ATTN_HEADS):
            k_pair = k_ref[pl.ds(r0, kt_rows), (h // 2) * LANES:(h // 2 + 1) * LANES]
            s_ref[slot, h] = jnp.dot(k_pair, qTm[h], preferred_element_type=f32) + bias_ref[...]
            m_old = m_ref[1 - slot, h]
            m_new = jnp.maximum(m_old, jnp.max(s_ref[slot, h], axis=0, keepdims=True))
            alpha_ref[slot, h] = jnp.exp2(m_old - m_new)
            m_ref[slot, h] = m_new

    def values_stage(kt, slot):
        r0 = pl.multiple_of(kt * kt_rows, kt_rows)
        for h in range(N_ATTN_HEADS):
            p = jnp.exp2(s_ref[slot, h] - m_ref[slot, h]).astype(bf16)
            hs = slice(h * ATTN_HEAD_DIM, (h + 1) * ATTN_HEAD_DIM)
            lhs = jnp.concatenate([vT_ref[hs, pl.ds(r0, kt_rows)], ones_rows], axis=0)
            pv = jnp.dot(lhs, p, preferred_element_type=f32)
            alpha = alpha_ref[slot, h]
            acc_ref[hs, :] = alpha * acc_ref[hs, :] + pv[:ATTN_HEAD_DIM]
            l_ref[h] = alpha * l_ref[h] + pv[ATTN_HEAD_DIM:ATTN_HEAD_DIM + 1]

    def att_body(j, carry):
        kt = 2 * j
        logits_stage(kt + 1, 1)
        values_stage(kt, 0)
        logits_stage(kt + 2, 0)
        values_stage(kt + 1, 1)
        return carry

    logits_stage(0, 0)
    lax.fori_loop(0, i // 2, att_body, 0)

    @pl.when(i % 2 == 1)
    def _():
        logits_stage(i, 1)
        values_stage(i - 1, 0)
        values_stage(i, 1)

    @pl.when(i % 2 == 0)
    def _():
        values_stage(i, 0)

    for h in range(N_ATTN_HEADS):
        hs = slice(h * ATTN_HEAD_DIM, (h + 1) * ATTN_HEAD_DIM)
        acc_ref[hs, :] = acc_ref[hs, :] / l_ref[h]
    o_ref[...] = acc_ref[...].T.astype(o_ref.dtype)


def _search_hints(seq_len):
    n = (np.arange(seq_len) // CHUNK + 1) * CHUNK
    frac = np.minimum(TOPK_MAX / n, 0.5)
    nd = statistics.NormalDist()
    z_of = {f: nd.inv_cdf(1.0 - f) for f in np.unique(frac)}
    z = np.array([z_of[f] for f in frac])
    dens = n * np.exp(-0.5 * z * z) / np.sqrt(2.0 * np.pi)
    return jnp.asarray(np.stack([z, 1.0 / dens]), f32)


def _dsa(qT, iqT, iwT, z3, vT, ik3):
    b, _, s = qT.shape
    qb = DSA_BLOCK
    return pl.pallas_call(
        functools.partial(_dsa_kernel, seq_len=s),
        grid=(b, s // qb),
        in_specs=[
            pl.BlockSpec((None, ATTN_W, qb), lambda bi, i: (bi, 0, i)),
            pl.BlockSpec((None, IDX_Q_W, qb), lambda bi, i: (bi, 0, i)),
            pl.BlockSpec((None, N_IDX_HEADS, qb), lambda bi, i: (bi, 0, i)),
            pl.BlockSpec((2, qb), lambda bi, i: (0, i)),
            pl.BlockSpec((None, s, COL_TILE), lambda bi, i: (bi, 0, Z_AK)),
            pl.BlockSpec((None, ATTN_W, s), lambda bi, i: (bi, 0, 0)),
            pl.BlockSpec((None, s, IDX_HEAD_DIM), lambda bi, i: (bi, 0, 0)),
        ],
        out_specs=pl.BlockSpec((None, qb, ATTN_W), lambda bi, i: (bi, i, 0)),
        out_shape=jax.ShapeDtypeStruct((b, s, ATTN_W), bf16),
        scratch_shapes=[
            pltpu.VMEM((s, qb), f32),
            pltpu.VMEM((N_ATTN_HEADS, LANES, qb), bf16),
            pltpu.VMEM((2, N_ATTN_HEADS, qb, qb), f32),
            pltpu.VMEM((qb, qb), f32),
            pltpu.VMEM((2, N_ATTN_HEADS, 1, qb), f32),
            pltpu.VMEM((N_ATTN_HEADS, 1, qb), f32),
            pltpu.VMEM((2, N_ATTN_HEADS, 1, qb), f32),
            pltpu.VMEM((ATTN_W, qb), f32),
            pltpu.VMEM((1, qb), f32),
            pltpu.VMEM((2, qb), f32),
            pltpu.SMEM((1,), jnp.int32),
            pltpu.SMEM((1,), jnp.int32),
        ],
        compiler_params=_cparams(("parallel", "arbitrary")),
        name="dsa",
    )(qT, iqT, iwT, _search_hints(s), z3, vT, ik3)


def _gla_kernel(gin_ref, dec_ref, v_ref, r_ref, gn_ref, o_ref, state_ref):
    c = CHUNK
    n_batch = gin_ref.shape[0]

    @pl.when(pl.program_id(0) == 0)
    def _():
        state_ref[...] = jnp.zeros_like(state_ref)

    row = lax.broadcasted_iota(jnp.int32, (c, c), 0)
    col = lax.broadcasted_iota(jnp.int32, (c, c), 1)
    causal = row >= col
    lane = lax.broadcasted_iota(jnp.int32, (c, LANES), 1)
    half = (lane < GLA_HEAD_K, lane >= GLA_HEAD_K)
    contract_last = (((1,), (1,)), ((), ()))
    contract_first = (((0,), (0,)), ((), ()))
    zero = jnp.zeros((c, LANES), bf16)

    for ci, bi in [(ci, bi) for ci in range(GLA_TILE // c) for bi in range(n_batch)]:
        rows = slice(ci * c, (ci + 1) * c)
        decay = dec_ref[bi, ci:ci + 1, :]
        for pr in range(N_GLA_HEADS // 2):
            ps = slice(pr * LANES, (pr + 1) * LANES)
            part = lambda g: gin_ref[bi, rows, g * GLA_K_W + pr * LANES:
                                     g * GLA_K_W + (pr + 1) * LANES]
            q_in, q_mid, k_mid, k_out = part(0), part(1), part(2), part(3)
            st = state_ref[bi, pr]
            st_b = st.astype(bf16)
            upd = jnp.zeros_like(st)
            for hh in range(2):
                h = pr * 2 + hh
                vs = slice(h * GLA_HEAD_V, (h + 1) * GLA_HEAD_V)
                v_h = v_ref[bi, rows, vs]
                attn = lax.dot_general(jnp.where(half[hh], q_mid, zero), k_mid, contract_last,
                                       preferred_element_type=f32)
                attn = jnp.where(causal, attn, 0.0).astype(bf16)
                o = lax.dot_general(jnp.where(half[hh], q_in, zero), st_b, contract_last,
                                    preferred_element_type=f32)
                o = o + jnp.dot(attn, v_h, preferred_element_type=f32)
                upd = upd + lax.dot_general(v_h, jnp.where(half[hh], k_out, zero),
                                            contract_first, preferred_element_type=f32)
                y = _rms(o, gn_ref[...])
                r = r_ref[bi, rows, vs].astype(f32)
                o_ref[bi, rows, vs] = (y * r * _sigmoid(r)).astype(o_ref.dtype)
            state_ref[bi, pr] = decay[:, ps] * st + upd


def _gla(gin3, dec3, z3, gn):
    b, s, _ = z3.shape
    tb = GLA_TILE
    return pl.pallas_call(
        _gla_kernel,
        grid=(s // tb,),
        in_specs=[
            pl.BlockSpec((b, tb, GLA_IN_W), lambda j: (0, j, 0)),
            pl.BlockSpec((b, tb // CHUNK, GLA_K_W), lambda j: (0, j, 0)),
            pl.BlockSpec((b, tb, COL_TILE), lambda j: (0, j, Z_GV)),
            pl.BlockSpec((b, tb, COL_TILE), lambda j: (0, j, Z_GR)),
            pl.BlockSpec((1, GLA_HEAD_V), lambda j: (0, 0)),
        ],
        out_specs=pl.BlockSpec((b, tb, GLA_V_W), lambda j: (0, j, 0)),
        out_shape=jax.ShapeDtypeStruct((b, s, GLA_V_W), bf16),
        scratch_shapes=[pltpu.VMEM((b, N_GLA_HEADS // 2, GLA_HEAD_V, LANES), f32)],
        compiler_params=_cparams(("arbitrary",)),
        name="gla",
    )(gin3, dec3, z3, z3, gn)


def _tail_kernel(x_ref, oa_ref, ob_ref, ga_ref, gb_ref, wa_ref, wb_ref, wo_ref,
                 g_ref, wg_ref, wu_ref, wd_ref, p_ref, gp_ref, wpg_ref, wpp_ref, gf_ref,
                 o_ref, *, final):
    ya = jnp.dot(oa_ref[...], wa_ref[...], preferred_element_type=f32)
    yb = jnp.dot(ob_ref[...], wb_ref[...], preferred_element_type=f32)
    mg = _sigmoid(ga_ref[...].astype(f32)) * ya + _sigmoid(gb_ref[...].astype(f32)) * yb
    x = x_ref[...] + jnp.dot(mg.astype(bf16), wo_ref[...], preferred_element_type=f32)
    x = _swiglu_half_step(x, g_ref, wg_ref, wu_ref, wd_ref)
    h = _rms(x, gp_ref[...]).astype(bf16)
    gate = _sigmoid(jnp.dot(h, wpg_ref[...], preferred_element_type=f32))
    e = jnp.dot(p_ref[...].astype(bf16), wpp_ref[...], preferred_element_type=f32)
    y = x + gate * e
    if final:
        y = _rms(y, gf_ref[...])
    o_ref[...] = y


def _tail(x, oa, ob, z, wa, wb, wo, g, wg, wu, wd, p, gp, wpg, wpp, gf, final):
    t = x.shape[0]
    tm = TOKEN_TILE
    rows = lambda width, col=0: pl.BlockSpec((tm, width), lambda i: (i, col))
    return pl.pallas_call(
        functools.partial(_tail_kernel, final=final),
        grid=(t // tm,),
        in_specs=[
            rows(D_MODEL), rows(ATTN_W), rows(GLA_V_W), rows(D_MODEL, Z_MGA), rows(D_MODEL, Z_MGB),
            _resident(wa), _resident(wb), _resident(wo),
            _resident(g), _resident(wg), _resident(wu), _resident(wd),
            rows(PLE_DIM), _resident(gp), _resident(wpg), _resident(wpp), _resident(gf),
        ],
        out_specs=rows(D_MODEL),
        out_shape=jax.ShapeDtypeStruct((t, D_MODEL), f32),
        compiler_params=_cparams(("parallel",)),
        name="tail",
    )(x, oa, ob, z, z, wa, wb, wo, g, wg, wu, wd, p, gp, wpg, wpp, gf)


def _split_w_in(w_in):
    cuts = np.cumsum(np.array(SPLIT_SIZES))[:-1].tolist()
    aq, ak, av, iq, ik, iw, gq, gk, gv, gr, ga, mga, mgb = jnp.split(w_in, cuts, axis=-1)
    att_scale = ATTN_HEAD_DIM ** -0.5 * float(np.log2(np.e))
    idx_scale = IDX_HEAD_DIM ** -0.5 * N_IDX_HEADS ** -0.5
    main = jnp.concatenate([mga, mgb, ak, gv, gr], axis=-1)
    feature_major = jnp.concatenate([aq * att_scale, av, iq], axis=-1)
    gla_qk = jnp.concatenate([gq * (GLA_HEAD_K ** -0.5), gk], axis=-1)
    zeros = lambda n: jnp.zeros(w_in.shape[:-1] + (n,), w_in.dtype)
    small = jnp.concatenate(
        [ik, zeros(LANES - IDX_HEAD_DIM), iw * idx_scale, ga,
         zeros(LANES - N_IDX_HEADS - GLA_GATE_RANK)], axis=-1)
    return main.astype(bf16), feature_major.astype(bf16), gla_qk.astype(bf16), small.astype(bf16)


def kernel(x, p, w_in, gla_gate_w2, gla_gate_b, gla_norm, w_branch_a, w_branch_b, w_out,
           norm_ff1, norm_mix, norm_ff2, norm_ple, ff1_w_gate, ff1_w_up, ff1_w_down,
           ff2_w_gate, ff2_w_up, ff2_w_down, ple_w_proj, ple_w_gate, norm_final):
    b, s, d = x.shape
    t = b * s
    depth = w_in.shape[0]
    w_main, w_fm, w_gqk, w_small = _split_w_in(w_in)
    cast = lambda w: w.astype(bf16)
    ff1 = (cast(ff1_w_gate), cast(ff1_w_up), cast(ff1_w_down))
    ff2 = (cast(ff2_w_gate), cast(ff2_w_up), cast(ff2_w_down))
    wa, wb, wo = cast(w_branch_a), cast(w_branch_b), cast(w_out)
    wpg, wpp = cast(ple_w_gate), cast(ple_w_proj)
    row = lambda v: v.reshape(1, -1)

    xf = x.reshape(t, d)
    for l in range(depth):
        xf = _ffn(xf, row(norm_ff1[l]), ff1[0][l], ff1[1][l], ff1[2][l])

        z, qT, vT, iqT, ik, iwT, gin, dec = _proj(
            xf, row(norm_mix[l]), w_main[l], w_fm[l], w_gqk[l], w_small[l],
            gla_gate_w2[l], row(gla_gate_b[l]), b)
        z3 = z.reshape(b, s, Z_WIDTH)
        oa = _dsa(qT, iqT, iwT, z3, vT, ik.reshape(b, s, IDX_HEAD_DIM))
        ob = _gla(gin.reshape(b, s, GLA_IN_W), dec.reshape(b, s // CHUNK, GLA_K_W), z3,
                  row(gla_norm[l]))
        xf = _tail(xf, oa.reshape(t, ATTN_W), ob.reshape(t, GLA_V_W), z, wa[l], wb[l], wo[l],
                   row(norm_ff2[l]), ff2[0][l], ff2[1][l], ff2[2][l],
                   p[l].reshape(t, PLE_DIM), row(norm_ple[l]), wpg[l], wpp[l],
                   row(norm_final), final=(l == depth - 1))
    return xf.reshape(b, s, d)
```

```python
import functools
import statistics

import jax
import jax.numpy as jnp
import numpy as np
from jax import lax
from jax.experimental import pallas as pl
from jax.experimental.pallas import tpu as pltpu

D_MODEL = 1024
D_FF = 2816
PLE_DIM = 256
DEPTH = 2
EPS = 1e-6

CHUNK = 64
N_ATTN_HEADS = 8
ATTN_HEAD_DIM = 64
N_IDX_HEADS = 8
IDX_HEAD_DIM = 64
TOPK_MAX = 256
N_GLA_HEADS = 4
GLA_HEAD_K = 64
GLA_HEAD_V = 128
GLA_GATE_RANK = 16
GLA_GATE_TAU = 16.0

ATTN_W = N_ATTN_HEADS * ATTN_HEAD_DIM
IDX_Q_W = N_IDX_HEADS * IDX_HEAD_DIM
GLA_K_W = N_GLA_HEADS * GLA_HEAD_K
GLA_V_W = N_GLA_HEADS * GLA_HEAD_V
SPLIT_SIZES = (ATTN_W, ATTN_W, ATTN_W, IDX_Q_W, IDX_HEAD_DIM, N_IDX_HEADS,
               GLA_K_W, GLA_K_W, GLA_V_W, GLA_V_W, GLA_GATE_RANK, D_MODEL, D_MODEL)

LANES = 128
SUBLANES = 8
VMEM_LIMIT_BYTES = 56 * 1024 * 1024

COL_TILE = 512
Z_MGA, Z_MGB = 0, 1
Z_AK, Z_GV, Z_GR = 4, 5, 6
Z_TILES = 7
Z_WIDTH = Z_TILES * COL_TILE
ZT_TILES = 3
SM_IW0 = 0
SM_GA0 = N_IDX_HEADS

TOKEN_TILE = 512
FF_TILE = 1408
DSA_BLOCK = 256
GLA_TILE = 512
GLA_IN_W = 4 * GLA_K_W
SEARCH_MAX_ITERS = 24
SEARCH_FINISH_FROM = 4

NEG_BIG = -1e30
F32_LOWEST = float(np.finfo(np.float32).min)
F32_TINY = float(np.finfo(np.float32).tiny)
INT32_MIN = int(np.iinfo(np.int32).min)

f32 = jnp.float32
bf16 = jnp.bfloat16


def _rms(x, g):
    return x * lax.rsqrt(jnp.mean(x * x, axis=-1, keepdims=True) + EPS) * g


def _sigmoid(x):
    return 1.0 / (1.0 + jnp.exp(-x))


def _cparams(sem):
    return pltpu.CompilerParams(dimension_semantics=sem, vmem_limit_bytes=VMEM_LIMIT_BYTES)


def _resident(a):
    return pl.BlockSpec(a.shape, lambda i: (0, 0), pipeline_mode=pl.Buffered(1))


def _swiglu_half_step(x, g_ref, wg_ref, wu_ref, wd_ref):
    h = _rms(x, g_ref[...]).astype(bf16)
    y = x
    for f in range(D_FF // FF_TILE):
        cols = slice(f * FF_TILE, (f + 1) * FF_TILE)
        gate = jnp.dot(h, wg_ref[:, cols], preferred_element_type=f32)
        up = jnp.dot(h, wu_ref[:, cols], preferred_element_type=f32)
        a = (gate * _sigmoid(gate) * up).astype(bf16)
        y = y + 0.5 * jnp.dot(a, wd_ref[cols, :], preferred_element_type=f32)
    return y


def _ffn_kernel(x_ref, g_ref, wg_ref, wu_ref, wd_ref, o_ref):
    o_ref[...] = _swiglu_half_step(x_ref[...], g_ref, wg_ref, wu_ref, wd_ref)


def _token_rows(shape):
    if len(shape) == 2:
        return pl.BlockSpec((TOKEN_TILE, shape[1]), lambda i: (i, 0))
    per_b = shape[1] // TOKEN_TILE
    return pl.BlockSpec((None, TOKEN_TILE, shape[2]), lambda i: (i // per_b, i % per_b, 0))


def _ffn(x, g, wg, wu, wd):
    t = x.size // D_MODEL
    return pl.pallas_call(
        _ffn_kernel,
        grid=(t // TOKEN_TILE,),
        in_specs=[
            _token_rows(x.shape),
            _resident(g), _resident(wg), _resident(wu), _resident(wd),
        ],
        out_specs=pl.BlockSpec((TOKEN_TILE, D_MODEL), lambda i: (i, 0)),
        out_shape=jax.ShapeDtypeStruct((t, D_MODEL), f32),
        compiler_params=_cparams(("parallel",)),
        name="ffn",
    )(x, g, wg, wu, wd)


def _gla_operands(q, k, ga, w2, gb):
    n = q.shape[0]
    c = CHUNK
    glin = jnp.dot(ga, w2, preferred_element_type=f32, precision=lax.Precision.HIGHEST) + gb
    logg = (jnp.minimum(glin, 0.0) - jnp.log(1.0 + jnp.exp(-jnp.abs(glin)))) / GLA_GATE_TAU
    row = lax.broadcasted_iota(jnp.int32, (n, n), 0)
    col = lax.broadcasted_iota(jnp.int32, (n, n), 1)
    tril = jnp.where((row // c == col // c) & (col <= row), 1.0, 0.0).astype(bf16)
    hi = logg.astype(bf16)
    lo = (logg - hi.astype(f32)).astype(bf16)
    bcum = (jnp.dot(tril, hi, preferred_element_type=f32)
            + jnp.dot(tril, lo, preferred_element_type=f32))
    by_chunk = bcum.reshape(n // c, c, GLA_K_W)
    spread = lambda r: jnp.broadcast_to(r, (n // c, c, GLA_K_W)).reshape(n, GLA_K_W)
    b_last = by_chunk[:, c - 1:c, :]
    b_mid = spread(by_chunk[:, c // 2 - 1:c // 2, :])
    operands = jnp.concatenate(
        [q * jnp.exp(bcum), q * jnp.exp(bcum - b_mid), k * jnp.exp(b_mid - bcum),
         k * jnp.exp(spread(b_last) - bcum)], axis=1)
    return operands, jnp.exp(b_last.reshape(n // c, GLA_K_W))


def _proj_kernel(x_ref, g_ref, w_ref, wt_ref, wg_ref, ws_ref, w2_ref, gb_ref,
                 z_ref, qT_ref, vT_ref, iqT_ref, ik_ref, iwT_ref, gin_ref, dec_ref):
    h = _rms(x_ref[...], g_ref[...]).astype(bf16)
    for j in range(Z_TILES):
        cols = slice(j * COL_TILE, (j + 1) * COL_TILE)
        z_ref[:, cols] = jnp.dot(h, w_ref[:, cols], preferred_element_type=f32).astype(bf16)
    for j, out_ref in enumerate((qT_ref, vT_ref, iqT_ref)):
        cols = slice(j * COL_TILE, (j + 1) * COL_TILE)
        zt = jnp.dot(h, wt_ref[:, cols], preferred_element_type=f32)
        out_ref[...] = zt.T.astype(bf16)
    s = jnp.dot(h, ws_ref[...], preferred_element_type=f32)
    ik_ref[...] = s[:, :IDX_HEAD_DIM].astype(bf16)
    sm = s[:, LANES:]
    iwT_ref[...] = sm.T[SM_IW0:SM_IW0 + N_IDX_HEADS, :]
    gqk = jnp.dot(h, wg_ref[...], preferred_element_type=f32)
    operands, decay = _gla_operands(gqk[:, :GLA_K_W], gqk[:, GLA_K_W:],
                                    sm[:, SM_GA0:SM_GA0 + GLA_GATE_RANK],
                                    w2_ref[...], gb_ref[...])
    gin_ref[...] = operands.astype(bf16)
    dec_ref[...] = decay


def _proj(x, g, w_main, w_t, w_g, w_small, w2, gb, batch):
    t = x.shape[0]
    tm = TOKEN_TILE
    s = t // batch
    per_b = s // tm
    fm = lambda rows: pl.BlockSpec((None, rows, tm), lambda i: (i // per_b, 0, i % per_b))
    whole = lambda a: pl.BlockSpec(a.shape, lambda i: (0, 0))
    return pl.pallas_call(
        _proj_kernel,
        grid=(t // tm,),
        in_specs=[
            pl.BlockSpec((tm, D_MODEL), lambda i: (i, 0)),
            whole(g), whole(w_main), whole(w_t), whole(w_g), whole(w_small), whole(w2), whole(gb),
        ],
        out_specs=[
            pl.BlockSpec((tm, Z_WIDTH), lambda i: (i, 0)),
            fm(ATTN_W), fm(ATTN_W), fm(IDX_Q_W),
            pl.BlockSpec((tm, IDX_HEAD_DIM), lambda i: (i, 0)),
            fm(N_IDX_HEADS),
            pl.BlockSpec((tm, GLA_IN_W), lambda i: (i, 0)),
            pl.BlockSpec((tm // CHUNK, GLA_K_W), lambda i: (i, 0)),
        ],
        out_shape=[
            jax.ShapeDtypeStruct((t, Z_WIDTH), bf16),
            jax.ShapeDtypeStruct((batch, ATTN_W, s), bf16),
            jax.ShapeDtypeStruct((batch, ATTN_W, s), bf16),
            jax.ShapeDtypeStruct((batch, IDX_Q_W, s), bf16),
            jax.ShapeDtypeStruct((t, IDX_HEAD_DIM), bf16),
            jax.ShapeDtypeStruct((batch, N_IDX_HEADS, s), f32),
            jax.ShapeDtypeStruct((t, GLA_IN_W), bf16),
            jax.ShapeDtypeStruct((t // CHUNK, GLA_K_W), f32),
        ],
        compiler_params=_cparams(("parallel",)),
        name="proj",
    )(x, g, w_main, w_t, w_g, w_small, w2, gb)


def _dsa_kernel(qT_ref, iqT_ref, iwT_ref, tab_ref, k_ref, vT_ref, ik_ref, o_ref,
                ibuf, qTm, s_ref, bias_ref, m_ref, l_ref, alpha_ref, acc_ref, thr_ref,
                ext_ref, pending_ref, finish_ref, *, seq_len):
    qb = DSA_BLOCK
    kt_rows = DSA_BLOCK
    i = pl.program_id(1)
    n_tiles = i + 1

    rowid = lax.broadcasted_iota(jnp.int32, (LANES, qb), 0)
    for h in range(N_ATTN_HEADS):
        pair = qT_ref[(h // 2) * LANES:(h // 2 + 1) * LANES, :]
        keep = (rowid < ATTN_HEAD_DIM) if h % 2 == 0 else (rowid >= ATTN_HEAD_DIM)
        qTm[h] = jnp.where(keep, pair, jnp.zeros_like(pair))
    l_ref[...] = jnp.zeros_like(l_ref)
    acc_ref[...] = jnp.zeros_like(acc_ref)

    def fold(x, op):
        return op(x.reshape(kt_rows // SUBLANES, SUBLANES, qb), axis=0)

    def idx_tile(kt, diagonal, stats):
        r0 = pl.multiple_of(kt * kt_rows, kt_rows)
        ki_t = ik_ref[pl.ds(r0, kt_rows), :]
        sc = jnp.zeros((kt_rows, qb), f32)
        for h in range(N_IDX_HEADS):
            s = jnp.dot(ki_t, iqT_ref[h * IDX_HEAD_DIM:(h + 1) * IDX_HEAD_DIM, :],
                        preferred_element_type=f32)
            sc = sc + iwT_ref[h:h + 1, :] * jnp.maximum(s, 0.0)
        sc_hi = sc_lo = sc_0 = sc
        if diagonal:
            kc = lax.broadcasted_iota(jnp.int32, (kt_rows, qb), 0) // CHUNK
            qc = lax.broadcasted_iota(jnp.int32, (kt_rows, qb), 1) // CHUNK
            adm = kc <= qc
            sc_hi = jnp.where(adm, sc, -jnp.inf)
            sc_lo = jnp.where(adm, sc, jnp.inf)
            sc_0 = jnp.where(adm, sc, 0.0)
        ibuf[pl.ds(r0, kt_rows), :] = sc_hi
        mx8, mn8, s8, ss8 = stats
        return (jnp.maximum(mx8, fold(sc_hi, jnp.max)), jnp.minimum(mn8, fold(sc_lo, jnp.min)),
                s8 + fold(sc_0, jnp.sum), ss8 + fold(sc_0 * sc_0, jnp.sum))

    stats0 = (jnp.full((SUBLANES, qb), -jnp.inf, f32), jnp.full((SUBLANES, qb), jnp.inf, f32),
              jnp.zeros((SUBLANES, qb), f32), jnp.zeros((SUBLANES, qb), f32))
    odd = i % 2
    stats = lax.cond(odd == 1, lambda st: idx_tile(0, False, st), lambda st: st, stats0)
    stats = lax.fori_loop(
        0, i // 2,
        lambda j, st: idx_tile(odd + 2 * j + 1, False, idx_tile(odd + 2 * j, False, st)),
        stats)
    mx8, mn8, s8, ss8 = idx_tile(i, True, stats)
    col_max = jnp.max(mx8, axis=0, keepdims=True)
    col_min = jnp.min(mn8, axis=0, keepdims=True)
    col_sum = jnp.sum(s8, axis=0, keepdims=True)
    col_ssq = jnp.sum(ss8, axis=0, keepdims=True)

    def count(pred):
        def body(r, acc):
            r0 = pl.multiple_of(r * kt_rows, kt_rows)
            tile = ibuf[pl.ds(r0, kt_rows), :]
            rows = r0 + lax.broadcasted_iota(jnp.int32, (kt_rows, qb), 0)
            ind = jnp.where(pred(tile, rows), 1.0, 0.0)
            part = ind.reshape(kt_rows // SUBLANES, SUBLANES, qb)
            while part.shape[0] > 1:
                half = part.shape[0] // 2
                part = part[:half] + part[half:]
            return acc + part[0]
        acc = jnp.zeros((SUBLANES, qb), f32)
        acc = lax.cond(i % 2 == 0, lambda a: body(0, a), lambda a: a, acc)
        first = 1 - i % 2
        acc = lax.fori_loop(
            0, (i + 1) // 2,
            lambda j, a: body(first + 2 * j + 1, body(first + 2 * j, a)), acc)
        return jnp.sum(acc, axis=0, keepdims=True)

    def resolve_ties(tv, need, mult):
        is_tie = tv == tv
        n_keys = n_tiles * kt_rows

        def tie_cond(st):
            return st["active"] > 0

        def tie_body(st):
            plo, phi, flo, fhi = st["plo"], st["phi"], st["flo"], st["fhi"]
            span = phi - plo
            est = ((need - flo) / jnp.maximum(fhi - flo, 1.0) * span.astype(f32)).astype(jnp.int32)
            pick = jnp.where(st["bisect"] == 2, span // 2, est)
            cand = plo + jnp.clip(pick, 1, jnp.maximum(span - 1, 1))
            f = count(lambda tile, rows: jnp.where(tile == tv, rows, seq_len) < cand)
            lower = f < need
            exact = f == need
            plo = jnp.where(lower, cand, jnp.where(exact, cand - 1, plo))
            phi = jnp.where(lower, phi, cand)
            flo = jnp.where(lower, f, flo)
            fhi = jnp.where(lower, fhi, f)
            unsplit = jnp.where(is_tie & (phi - plo > 1), 1, 0)
            return dict(active=jnp.max(unsplit), bisect=(st["bisect"] + 1) % 3,
                        plo=plo, phi=phi, flo=flo, fhi=fhi)

        split = lax.while_loop(tie_cond, tie_body, dict(
            active=jnp.int32(1), bisect=jnp.int32(0),
            plo=jnp.zeros((1, qb), jnp.int32), phi=jnp.zeros((1, qb), jnp.int32) + n_keys,
            flo=jnp.zeros((1, qb), f32), fhi=mult))["phi"]

        def drop_body(r, carry):
            r0 = pl.multiple_of(r * kt_rows, kt_rows)
            tile = ibuf[pl.ds(r0, kt_rows), :]
            rows = r0 + lax.broadcasted_iota(jnp.int32, (kt_rows, qb), 0)
            dropped = jnp.where(tile == tv, rows, -1) >= split
            ibuf[pl.ds(r0, kt_rows), :] = jnp.where(dropped, -jnp.inf, tile)
            return carry

        lax.fori_loop(0, n_tiles, drop_body, 0)

    qpos = i * qb + lax.broadcasted_iota(jnp.int32, (1, qb), 1)
    n_adm = (qpos // CHUNK + 1) * CHUNK
    select_all = n_adm <= TOPK_MAX
    kf = float(TOPK_MAX)

    thr_ref[...] = jnp.full((1, qb), F32_LOWEST, f32)
    pending_ref[0] = jnp.int32(0)

    @pl.when(n_tiles * kt_rows > TOPK_MAX)
    def _():
        n_f = n_adm.astype(f32)
        mean = col_sum / n_f
        sigma = jnp.sqrt(jnp.maximum(col_ssq / n_f - mean * mean, 0.0))
        spacing = tab_ref[1:2, :] * sigma

        def inside(c, lo, hi):
            return (c > lo) & (c < hi)

        c0 = mean + tab_ref[0:1, :] * sigma
        c0 = jnp.where(inside(c0, col_min, col_max), c0, 0.5 * col_min + 0.5 * col_max)
        zero = jnp.zeros((1, qb), f32)
        one = jnp.ones((1, qb), f32)
        state0 = dict(it=jnp.int32(0), active=jnp.int32(1), c=c0, lo=col_min, hi=col_max,
                      glo=n_f, ghi=zero, tlo=zero, thi=zero, boost=one, side=zero,
                      thr=jnp.full((1, qb), F32_LOWEST, f32),
                      open=jnp.where(select_all, 0.0, 1.0), tie=zero, kind=zero, emin=zero)

        def search_cond(st):
            return (st["it"] < SEARCH_MAX_ITERS) & (st["active"] > 0)

        def search_body(st):
            c, lo, hi = st["c"], st["lo"], st["hi"]
            g = count(lambda tile, rows: tile >= c)
            hit = (g == kf) & (st["open"] > 0.0)
            thr = jnp.where(hit, c, st["thr"])
            still = jnp.where(hit, 0.0, st["open"])
            above = g > kf
            lo = jnp.where(above, c, lo)
            hi = jnp.where(above, hi, c)
            glo = jnp.where(above, g, st["glo"])
            ghi = jnp.where(above, st["ghi"], g)
            tlo = jnp.where(above, 1.0, st["tlo"])
            thi = jnp.where(above, st["thi"], 1.0)
            side = jnp.where(above, 1.0, -1.0)
            repeat = side == st["side"]
            boost = jnp.where(repeat, 2.0 * st["boost"], 1.0)
            mid = 0.5 * lo + 0.5 * hi
            stale = 1.0 / jnp.minimum(boost, 256.0)
            w_lo = (glo - kf - 0.5) * jnp.where(above, 1.0, stale)
            w_hi = (kf + 0.5 - ghi) * jnp.where(above, stale, 1.0)
            c_two = lo + w_lo / (w_lo + w_hi) * (hi - lo)
            step = 2.0 * boost * spacing
            c_one = jnp.where(tlo > 0.0, lo + jnp.maximum(glo - kf, 1.0) * step,
                              hi - jnp.maximum(kf - ghi, 1.0) * step)
            c_one = jnp.where(inside(c_one, lo, hi), c_one, mid)
            both = (tlo > 0.0) & (thi > 0.0)
            c_new = jnp.where(both, c_two, c_one)
            c_new = jnp.where(both & (lo < 0.0) & (hi > 0.0), 0.0, c_new)
            c_new = jnp.where((lo == 0.0) & (hi > F32_TINY), F32_TINY, c_new)
            movable = inside(c_new, lo, hi) & ~((lo == 0.0) & (hi <= F32_TINY))
            missed = (still > 0.0) & (((st["kind"] == 1.0) & above) |
                                      ((st["kind"] == 2.0) & (g < kf)))
            lo = jnp.where(missed & (st["kind"] == 2.0), st["emin"], lo)
            closed = (~movable & both & (still > 0.0)) | missed
            tie = jnp.where(closed, 1.0, st["tie"])
            still = jnp.where(closed, 0.0, still)
            searching = jnp.where(movable, still, 0.0)

            from_hi = both & (kf - ghi == 1.0)
            from_lo = both & (glo - kf == 1.0) & ~from_hi
            ready = jnp.where(searching > 0.0, jnp.where(from_hi | from_lo, 1.0, 2.0), 0.0)
            code = jnp.max(ready)
            ext_ref[...] = jnp.zeros_like(ext_ref)
            finish_ref[0] = jnp.int32(0)

            @pl.when((code == 1.0) & (st["it"] + 1 >= SEARCH_FINISH_FROM))
            def _():
                def ext_body(r, carry):
                    r0 = pl.multiple_of(r * kt_rows, kt_rows)
                    tile = ibuf[pl.ds(r0, kt_rows), :]
                    below = fold(jnp.where(tile < hi, tile, -jnp.inf), jnp.max)
                    above_lo = fold(jnp.where(tile >= lo, tile, jnp.inf), jnp.min)
                    return jnp.maximum(carry[0], below), jnp.minimum(carry[1], above_lo)
                mx8, mn8 = lax.fori_loop(0, n_tiles, ext_body,
                                         (jnp.full((SUBLANES, qb), -jnp.inf, f32),
                                          jnp.full((SUBLANES, qb), jnp.inf, f32)))
                ext_ref[0:1, :] = jnp.max(mx8, axis=0, keepdims=True)
                ext_ref[1:2, :] = jnp.min(mn8, axis=0, keepdims=True)
                finish_ref[0] = jnp.int32(1)

            finishing = finish_ref[0] > 0
            emax = ext_ref[0:1, :]
            emin = ext_ref[1:2, :]
            ebits = lax.bitcast_convert_type(emin, jnp.int32)
            next_up = lax.bitcast_convert_type(ebits + jnp.where(emin > 0.0, 1, -1), f32)
            next_up = jnp.where(emin == 0.0, F32_TINY, next_up)
            kind = jnp.where(finishing & (searching > 0.0),
                             jnp.where(from_hi, 1.0, jnp.where(from_lo, 2.0, 0.0)), 0.0)
            c_new = jnp.where(kind == 1.0, emax, jnp.where(kind == 2.0, next_up, c_new))
            return dict(it=st["it"] + 1, active=(code > 0.0).astype(jnp.int32),
                        c=c_new, lo=lo, hi=hi, glo=glo, ghi=ghi, tlo=tlo, thi=thi,
                        boost=boost, side=side, thr=thr, open=still, tie=tie,
                        kind=kind, emin=emin)

        final = lax.while_loop(search_cond, search_body, state0)
        tied = final["tie"] > 0.0
        thr_ref[...] = jnp.where(tied, final["lo"], final["thr"])
        pending_ref[0] = (jnp.max(final["open"]) > 0.0).astype(jnp.int32)

        @pl.when(jnp.max(final["tie"]) > 0.0)
        def _():
            resolve_ties(jnp.where(tied, final["lo"], jnp.nan), kf - final["ghi"],
                         final["glo"] - final["ghi"])

    @pl.when(pending_ref[0] > 0)
    def _():
        def key_to_f32(key):
            bits = jnp.where(key < 0, key ^ jnp.int32(0x7FFFFFFF), key)
            return lax.bitcast_convert_type(bits, f32)

        def bit_body(t, prefix):
            step = lax.shift_left(jnp.int32(1), jnp.int32(31) - t)
            cand = prefix + step
            cand_f = key_to_f32(cand)
            cnt = count(lambda tile, rows: tile >= cand_f)
            return jnp.where(cnt >= kf, cand, prefix)

        prefix = lax.fori_loop(0, 32, bit_body, jnp.full((1, qb), INT32_MIN, jnp.int32))
        thr = jnp.where(select_all, F32_LOWEST, key_to_f32(prefix))
        thr_ref[...] = thr

        cnt_ge = count(lambda tile, rows: tile >= thr)
        excess = jnp.where(select_all, 0.0, cnt_ge - kf)

        @pl.when(jnp.max(excess) > 0.0)
        def _():
            cnt_gt = count(lambda tile, rows: tile > thr)
            resolve_ties(jnp.where(excess > 0.0, thr, jnp.nan), kf - cnt_gt, cnt_ge - cnt_gt)

    thr = thr_ref[...]
    ones_rows = jnp.ones((2 * SUBLANES, kt_rows), bf16)

    m_ref[1] = jnp.full(m_ref.shape[1:], NEG_BIG, f32)

    def logits_stage(kt, slot):
        r0 = pl.multiple_of(kt * kt_rows, kt_rows)
        bias_ref[...] = jnp.where(ibuf[pl.ds(r0, kt_rows), :] >= thr, 0.0, NEG_BIG)
        for h in range(N_ATTN_HEADS):
            k_pair = k_ref[pl.ds(r0, kt_rows), (h // 2) * LANES:(h // 2 + 1) * LANES]
            s_ref[slot, h] = jnp.dot(k_pair, qTm[h], preferred_element_type=f32) + bias_ref[...]
            m_old = m_ref[1 - slot, h]
            m_new = jnp.maximum(m_old, jnp.max(s_ref[slot, h], axis=0, keepdims=True))
            alpha_ref[slot, h] = jnp.exp2(m_old - m_new)
            m_ref[slot, h] = m_new

    def values_stage(kt, slot):
        r0 = pl.multiple_of(kt * kt_rows, kt_rows)
        for h in range(N_ATTN_HEADS):
            p = jnp.exp2(s_ref[slot, h] - m_ref[slot, h]).astype(bf16)
            hs = slice(h * ATTN_HEAD_DIM, (h + 1) * ATTN_HEAD_DIM)
            lhs = jnp.concatenate([vT_ref[hs, pl.ds(r0, kt_rows)], ones_rows], axis=0)
            pv = jnp.dot(lhs, p, preferred_element_type=f32)
            alpha = alpha_ref[slot, h]
            acc_ref[hs, :] = alpha * acc_ref[hs, :] + pv[:ATTN_HEAD_DIM]
            l_ref[h] = alpha * l_ref[h] + pv[ATTN_HEAD_DIM:ATTN_HEAD_DIM + 1]

    def att_body(j, carry):
        kt = 2 * j
        logits_stage(kt + 1, 1)
        values_stage(kt, 0)
        logits_stage(kt + 2, 0)
        values_stage(kt + 1, 1)
        return carry

    logits_stage(0, 0)
    lax.fori_loop(0, i // 2, att_body, 0)

    @pl.when(i % 2 == 1)
    def _():
        logits_stage(i, 1)
        values_stage(i - 1, 0)
        values_stage(i, 1)

    @pl.when(i % 2 == 0)
    def _():
        values_stage(i, 0)

    for h in range(N_ATTN_HEADS):
        hs = slice(h * ATTN_HEAD_DIM, (h + 1) * ATTN_HEAD_DIM)
        acc_ref[hs, :] = acc_ref[hs, :] / l_ref[h]
    o_ref[...] = acc_ref[...].T.astype(o_ref.dtype)


def _search_hints(seq_len):
    n = (np.arange(seq_len) // CHUNK + 1) * CHUNK
    frac = np.minimum(TOPK_MAX / n, 0.5)
    nd = statistics.NormalDist()
    z_of = {f: nd.inv_cdf(1.0 - f) for f in np.unique(frac)}
    z = np.array([z_of[f] for f in frac])
    dens = n * np.exp(-0.5 * z * z) / np.sqrt(2.0 * np.pi)
    return jnp.asarray(np.stack([z, 1.0 / dens]), f32)


def _dsa(qT, iqT, iwT, z3, vT, ik3):
    b, _, s = qT.shape
    qb = DSA_BLOCK
    return pl.pallas_call(
        functools.partial(_dsa_kernel, seq_len=s),
        grid=(b, s // qb),
        in_specs=[
            pl.BlockSpec((None, ATTN_W, qb), lambda bi, i: (bi, 0, i)),
            pl.BlockSpec((None, IDX_Q_W, qb), lambda bi, i: (bi, 0, i)),
            pl.BlockSpec((None, N_IDX_HEADS, qb), lambda bi, i: (bi, 0, i)),
            pl.BlockSpec((2, qb), lambda bi, i: (0, i)),
            pl.BlockSpec((None, s, COL_TILE), lambda bi, i: (bi, 0, Z_AK)),
            pl.BlockSpec((None, ATTN_W, s), lambda bi, i: (bi, 0, 0)),
            pl.BlockSpec((None, s, IDX_HEAD_DIM), lambda bi, i: (bi, 0, 0)),
        ],
        out_specs=pl.BlockSpec((None, qb, ATTN_W), lambda bi, i: (bi, i, 0)),
        out_shape=jax.ShapeDtypeStruct((b, s, ATTN_W), bf16),
        scratch_shapes=[
            pltpu.VMEM((s, qb), f32),
            pltpu.VMEM((N_ATTN_HEADS, LANES, qb), bf16),
            pltpu.VMEM((2, N_ATTN_HEADS, qb, qb), f32),
            pltpu.VMEM((qb, qb), f32),
            pltpu.VMEM((2, N_ATTN_HEADS, 1, qb), f32),
            pltpu.VMEM((N_ATTN_HEADS, 1, qb), f32),
            pltpu.VMEM((2, N_ATTN_HEADS, 1, qb), f32),
            pltpu.VMEM((ATTN_W, qb), f32),
            pltpu.VMEM((1, qb), f32),
            pltpu.VMEM((2, qb), f32),
            pltpu.SMEM((1,), jnp.int32),
            pltpu.SMEM((1,), jnp.int32),
        ],
        compiler_params=_cparams(("parallel", "arbitrary")),
        name="dsa",
    )(qT, iqT, iwT, _search_hints(s), z3, vT, ik3)


def _gla_kernel(gin_ref, dec_ref, v_ref, r_ref, gn_ref, o_ref, state_ref):
    c = CHUNK
    n_batch = gin_ref.shape[0]

    @pl.when(pl.program_id(0) == 0)
    def _():
        state_ref[...] = jnp.zeros_like(state_ref)

    row = lax.broadcasted_iota(jnp.int32, (c, c), 0)
    col = lax.broadcasted_iota(jnp.int32, (c, c), 1)
    causal = row >= col
    lane = lax.broadcasted_iota(jnp.int32, (c, LANES), 1)
    half = (lane < GLA_HEAD_K, lane >= GLA_HEAD_K)
    contract_last = (((1,), (1,)), ((), ()))
    contract_first = (((0,), (0,)), ((), ()))
    zero = jnp.zeros((c, LANES), bf16)

    for ci, bi in [(ci, bi) for ci in range(GLA_TILE // c) for bi in range(n_batch)]:
        rows = slice(ci * c, (ci + 1) * c)
        decay = dec_ref[bi, ci:ci + 1, :]
        for pr in range(N_GLA_HEADS // 2):
            ps = slice(pr * LANES, (pr + 1) * LANES)
            part = lambda g: gin_ref[bi, rows, g * GLA_K_W + pr * LANES:
                                     g * GLA_K_W + (pr + 1) * LANES]
            q_in, q_mid, k_mid, k_out = part(0), part(1), part(2), part(3)
            st = state_ref[bi, pr]
            st_b = st.astype(bf16)
            upd = jnp.zeros_like(st)
            for hh in range(2):
                h = pr * 2 + hh
                vs = slice(h * GLA_HEAD_V, (h + 1) * GLA_HEAD_V)
                v_h = v_ref[bi, rows, vs]
                attn = lax.dot_general(jnp.where(half[hh], q_mid, zero), k_mid, contract_last,
                                       preferred_element_type=f32)
                attn = jnp.where(causal, attn, 0.0).astype(bf16)
                o = lax.dot_general(jnp.where(half[hh], q_in, zero), st_b, contract_last,
                                    preferred_element_type=f32)
                o = o + jnp.dot(attn, v_h, preferred_element_type=f32)
                upd = upd + lax.dot_general(v_h, jnp.where(half[hh], k_out, zero),
                                            contract_first, preferred_element_type=f32)
                y = _rms(o, gn_ref[...])
                r = r_ref[bi, rows, vs].astype(f32)
                o_ref[bi, rows, vs] = (y * r * _sigmoid(r)).astype(o_ref.dtype)
            state_ref[bi, pr] = decay[:, ps] * st + upd


def _gla(gin3, dec3, z3, gn):
    b, s, _ = z3.shape
    tb = GLA_TILE
    return pl.pallas_call(
        _gla_kernel,
        grid=(s // tb,),
        in_specs=[
            pl.BlockSpec((b, tb, GLA_IN_W), lambda j: (0, j, 0)),
            pl.BlockSpec((b, tb // CHUNK, GLA_K_W), lambda j: (0, j, 0)),
            pl.BlockSpec((b, tb, COL_TILE), lambda j: (0, j, Z_GV)),
            pl.BlockSpec((b, tb, COL_TILE), lambda j: (0, j, Z_GR)),
            pl.BlockSpec((1, GLA_HEAD_V), lambda j: (0, 0)),
        ],
        out_specs=pl.BlockSpec((b, tb, GLA_V_W), lambda j: (0, j, 0)),
        out_shape=jax.ShapeDtypeStruct((b, s, GLA_V_W), bf16),
        scratch_shapes=[pltpu.VMEM((b, N_GLA_HEADS // 2, GLA_HEAD_V, LANES), f32)],
        compiler_params=_cparams(("arbitrary",)),
        name="gla",
    )(gin3, dec3, z3, z3, gn)


def _tail_kernel(x_ref, oa_ref, ob_ref, ga_ref, gb_ref, wa_ref, wb_ref, wo_ref,
                 g_ref, wg_ref, wu_ref, wd_ref, p_ref, gp_ref, wpg_ref, wpp_ref, gf_ref,
                 o_ref, *, final):
    ya = jnp.dot(oa_ref[...], wa_ref[...], preferred_element_type=f32)
    yb = jnp.dot(ob_ref[...], wb_ref[...], preferred_element_type=f32)
    mg = _sigmoid(ga_ref[...].astype(f32)) * ya + _sigmoid(gb_ref[...].astype(f32)) * yb
    x = x_ref[...] + jnp.dot(mg.astype(bf16), wo_ref[...], preferred_element_type=f32)
    x = _swiglu_half_step(x, g_ref, wg_ref, wu_ref, wd_ref)
    h = _rms(x, gp_ref[...]).astype(bf16)
    gate = _sigmoid(jnp.dot(h, wpg_ref[...], preferred_element_type=f32))
    e = jnp.dot(p_ref[...].astype(bf16), wpp_ref[...], preferred_element_type=f32)
    y = x + gate * e
    if final:
        y = _rms(y, gf_ref[...])
    o_ref[...] = y


def _tail(x, oa, ob, z, wa, wb, wo, g, wg, wu, wd, p, gp, wpg, wpp, gf, final, out_shape):
    t = x.shape[0]
    tm = TOKEN_TILE
    rows = lambda width, col=0: pl.BlockSpec((tm, width), lambda i: (i, col))
    return pl.pallas_call(
        functools.partial(_tail_kernel, final=final),
        grid=(t // tm,),
        in_specs=[
            rows(D_MODEL), rows(ATTN_W), rows(GLA_V_W), rows(D_MODEL, Z_MGA), rows(D_MODEL, Z_MGB),
            _resident(wa), _resident(wb), _resident(wo),
            _resident(g), _resident(wg), _resident(wu), _resident(wd),
            rows(PLE_DIM), _resident(gp), _resident(wpg), _resident(wpp), _resident(gf),
        ],
        out_specs=_token_rows(out_shape),
        out_shape=jax.ShapeDtypeStruct(out_shape, f32),
        compiler_params=_cparams(("parallel",)),
        name="tail",
    )(x, oa, ob, z, z, wa, wb, wo, g, wg, wu, wd, p, gp, wpg, wpp, gf)


def _split_w_in(w_in):
    cuts = np.cumsum(np.array(SPLIT_SIZES))[:-1].tolist()
    aq, ak, av, iq, ik, iw, gq, gk, gv, gr, ga, mga, mgb = jnp.split(w_in, cuts, axis=-1)
    att_scale = ATTN_HEAD_DIM ** -0.5 * float(np.log2(np.e))
    idx_scale = IDX_HEAD_DIM ** -0.5 * N_IDX_HEADS ** -0.5
    main = jnp.concatenate([mga, mgb, ak, gv, gr], axis=-1)
    feature_major = jnp.concatenate([aq * att_scale, av, iq], axis=-1)
    gla_qk = jnp.concatenate([gq * (GLA_HEAD_K ** -0.5), gk], axis=-1)
    zeros = lambda n: jnp.zeros(w_in.shape[:-1] + (n,), w_in.dtype)
    small = jnp.concatenate(
        [ik, zeros(LANES - IDX_HEAD_DIM), iw * idx_scale, ga,
         zeros(LANES - N_IDX_HEADS - GLA_GATE_RANK)], axis=-1)
    return main.astype(bf16), feature_major.astype(bf16), gla_qk.astype(bf16), small.astype(bf16)


def kernel(x, p, w_in, gla_gate_w2, gla_gate_b, gla_norm, w_branch_a, w_branch_b, w_out,
           norm_ff1, norm_mix, norm_ff2, norm_ple, ff1_w_gate, ff1_w_up, ff1_w_down,
           ff2_w_gate, ff2_w_up, ff2_w_down, ple_w_proj, ple_w_gate, norm_final):
    b, s, d = x.shape
    t = b * s
    depth = w_in.shape[0]
    w_main, w_fm, w_gqk, w_small = _split_w_in(w_in)
    cast = lambda w: w.astype(bf16)
    ff1 = (cast(ff1_w_gate), cast(ff1_w_up), cast(ff1_w_down))
    ff2 = (cast(ff2_w_gate), cast(ff2_w_up), cast(ff2_w_down))
    wa, wb, wo = cast(w_branch_a), cast(w_branch_b), cast(w_out)
    wpg, wpp = cast(ple_w_gate), cast(ple_w_proj)
    row = lambda v: v.reshape(1, -1)

    xf = x
    for l in range(depth):
        last = l == depth - 1
        xf = _ffn(xf, row(norm_ff1[l]), ff1[0][l], ff1[1][l], ff1[2][l])

        z, qT, vT, iqT, ik, iwT, gin, dec = _proj(
            xf, row(norm_mix[l]), w_main[l], w_fm[l], w_gqk[l], w_small[l],
            gla_gate_w2[l], row(gla_gate_b[l]), b)
        z3 = z.reshape(b, s, Z_WIDTH)
        oa = _dsa(qT, iqT, iwT, z3, vT, ik.reshape(b, s, IDX_HEAD_DIM))
        ob = _gla(gin.reshape(b, s, GLA_IN_W), dec.reshape(b, s // CHUNK, GLA_K_W), z3,
                  row(gla_norm[l]))
        xf = _tail(xf, oa.reshape(t, ATTN_W), ob.reshape(t, GLA_V_W), z, wa[l], wb[l], wo[l],
                   row(norm_ff2[l]), ff2[0][l], ff2[1][l], ff2[2][l],
                   p[l].reshape(t, PLE_DIM), row(norm_ple[l]), wpg[l], wpp[l],
                   row(norm_final), final=last, out_shape=(b, s, d) if last else (t, d))
    return xf
```

```python
import functools
import statistics

import jax
import jax.numpy as jnp
import numpy as np
from jax import lax
from jax.experimental import pallas as pl
from jax.experimental.pallas import tpu as pltpu

D_MODEL = 1024
D_FF = 2816
PLE_DIM = 256
DEPTH = 2
EPS = 1e-6

CHUNK = 64
N_ATTN_HEADS = 8
ATTN_HEAD_DIM = 64
N_IDX_HEADS = 8
IDX_HEAD_DIM = 64
TOPK_MAX = 256
N_GLA_HEADS = 4
GLA_HEAD_K = 64
GLA_HEAD_V = 128
GLA_GATE_RANK = 16
GLA_GATE_TAU = 16.0

ATTN_W = N_ATTN_HEADS * ATTN_HEAD_DIM
IDX_Q_W = N_IDX_HEADS * IDX_HEAD_DIM
GLA_K_W = N_GLA_HEADS * GLA_HEAD_K
GLA_V_W = N_GLA_HEADS * GLA_HEAD_V
SPLIT_SIZES = (ATTN_W, ATTN_W, ATTN_W, IDX_Q_W, IDX_HEAD_DIM, N_IDX_HEADS,
               GLA_K_W, GLA_K_W, GLA_V_W, GLA_V_W, GLA_GATE_RANK, D_MODEL, D_MODEL)

LANES = 128
SUBLANES = 8
VMEM_LIMIT_BYTES = 56 * 1024 * 1024

COL_TILE = 512
Z_MGA, Z_MGB = 0, 1
Z_AK, Z_GV, Z_GR = 4, 5, 6
Z_TILES = 7
Z_WIDTH = Z_TILES * COL_TILE
ZT_TILES = 3
SM_IW0 = 0
SM_GA0 = N_IDX_HEADS

TOKEN_TILE = 512
FF_TILE = 1408
DSA_BLOCK = 256
GLA_TILE = 512
GLA_IN_W = 4 * GLA_K_W
SEARCH_MAX_ITERS = 24
SEARCH_FINISH_FROM = 4

NEG_BIG = -1e30
F32_LOWEST = float(np.finfo(np.float32).min)
F32_TINY = float(np.finfo(np.float32).tiny)
INT32_MIN = int(np.iinfo(np.int32).min)

f32 = jnp.float32
bf16 = jnp.bfloat16


def _rms(x, g):
    return x * lax.rsqrt(jnp.mean(x * x, axis=-1, keepdims=True) + EPS) * g


def _sigmoid(x):
    return 1.0 / (1.0 + jnp.exp(-x))


def _cparams(sem):
    return pltpu.CompilerParams(dimension_semantics=sem, vmem_limit_bytes=VMEM_LIMIT_BYTES)


def _resident(a):
    return pl.BlockSpec(a.shape, lambda i: (0, 0), pipeline_mode=pl.Buffered(1))


def _swiglu_half_step(x, g_ref, wg_ref, wu_ref, wd_ref):
    h = _rms(x, g_ref[...]).astype(bf16)
    y = x
    for f in range(D_FF // FF_TILE):
        cols = slice(f * FF_TILE, (f + 1) * FF_TILE)
        gate = jnp.dot(h, wg_ref[:, cols], preferred_element_type=f32)
        up = jnp.dot(h, wu_ref[:, cols], preferred_element_type=f32)
        a = (gate * _sigmoid(gate) * up).astype(bf16)
        y = y + 0.5 * jnp.dot(a, wd_ref[cols, :], preferred_element_type=f32)
    return y


def _ffn_kernel(x_ref, g_ref, wg_ref, wu_ref, wd_ref, o_ref):
    o_ref[...] = _swiglu_half_step(x_ref[...], g_ref, wg_ref, wu_ref, wd_ref)


def _token_rows(shape):
    if len(shape) == 2:
        return pl.BlockSpec((TOKEN_TILE, shape[1]), lambda i: (i, 0))
    per_b = shape[1] // TOKEN_TILE
    return pl.BlockSpec((None, TOKEN_TILE, shape[2]), lambda i: (i // per_b, i % per_b, 0))


def _ffn(x, g, wg, wu, wd):
    t = x.size // D_MODEL
    return pl.pallas_call(
        _ffn_kernel,
        grid=(t // TOKEN_TILE,),
        in_specs=[
            _token_rows(x.shape),
            _resident(g), _resident(wg), _resident(wu), _resident(wd),
        ],
        out_specs=pl.BlockSpec((TOKEN_TILE, D_MODEL), lambda i: (i, 0)),
        out_shape=jax.ShapeDtypeStruct((t, D_MODEL), f32),
        compiler_params=_cparams(("parallel",)),
        name="ffn",
    )(x, g, wg, wu, wd)


def _gla_operands(q, k, ga, w2, gb):
    n = q.shape[0]
    c = CHUNK
    def split(a):
        hi = a.astype(bf16)
        return hi, (a - hi.astype(f32)).astype(bf16)

    dot = lambda a, b: jnp.dot(a, b, preferred_element_type=f32)
    ga_hi, ga_lo = split(ga)
    w2_hi, w2_lo = split(w2)
    glin = dot(ga_hi, w2_hi) + dot(ga_hi, w2_lo) + dot(ga_lo, w2_hi) + gb
    logg = (jnp.minimum(glin, 0.0) - jnp.log(1.0 + jnp.exp(-jnp.abs(glin)))) / GLA_GATE_TAU
    row = lax.broadcasted_iota(jnp.int32, (n, n), 0)
    col = lax.broadcasted_iota(jnp.int32, (n, n), 1)
    tril = jnp.where((row // c == col // c) & (col <= row), 1.0, 0.0).astype(bf16)
    hi, lo = split(logg)
    bcum = dot(tril, hi) + dot(tril, lo)
    by_chunk = bcum.reshape(n // c, c, GLA_K_W)
    spread = lambda r: jnp.broadcast_to(r, (n // c, c, GLA_K_W)).reshape(n, GLA_K_W)
    b_last = by_chunk[:, c - 1:c, :]
    b_mid = spread(by_chunk[:, c // 2 - 1:c // 2, :])
    operands = jnp.concatenate(
        [q * jnp.exp(bcum), q * jnp.exp(bcum - b_mid), k * jnp.exp(b_mid - bcum),
         k * jnp.exp(spread(b_last) - bcum)], axis=1)
    return operands, jnp.exp(b_last.reshape(n // c, GLA_K_W))


def _proj_kernel(x_ref, g_ref, w_ref, wt_ref, wg_ref, ws_ref, w2_ref, gb_ref,
                 z_ref, qT_ref, vT_ref, iqT_ref, ik_ref, iwT_ref, gin_ref, dec_ref):
    h = _rms(x_ref[...], g_ref[...]).astype(bf16)
    for j in range(Z_TILES):
        cols = slice(j * COL_TILE, (j + 1) * COL_TILE)
        z_ref[:, cols] = jnp.dot(h, w_ref[:, cols], preferred_element_type=f32).astype(bf16)
    for j, out_ref in enumerate((qT_ref, vT_ref, iqT_ref)):
        cols = slice(j * COL_TILE, (j + 1) * COL_TILE)
        zt = jnp.dot(h, wt_ref[:, cols], preferred_element_type=f32)
        out_ref[...] = zt.T.astype(bf16)
    s = jnp.dot(h, ws_ref[...], preferred_element_type=f32)
    ik_ref[...] = s[:, :IDX_HEAD_DIM].astype(bf16)
    sm = s[:, LANES:]
    iwT_ref[...] = sm.T[SM_IW0:SM_IW0 + N_IDX_HEADS, :]
    gqk = jnp.dot(h, wg_ref[...], preferred_element_type=f32)
    operands, decay = _gla_operands(gqk[:, :GLA_K_W], gqk[:, GLA_K_W:],
                                    sm[:, SM_GA0:SM_GA0 + GLA_GATE_RANK],
                                    w2_ref[...], gb_ref[...])
    gin_ref[...] = operands.astype(bf16)
    dec_ref[...] = decay


def _proj(x, g, w_main, w_t, w_g, w_small, w2, gb, batch):
    t = x.shape[0]
    tm = TOKEN_TILE
    s = t // batch
    per_b = s // tm
    fm = lambda rows: pl.BlockSpec((None, rows, tm), lambda i: (i // per_b, 0, i % per_b))
    whole = lambda a: pl.BlockSpec(a.shape, lambda i: (0, 0))
    return pl.pallas_call(
        _proj_kernel,
        grid=(t // tm,),
        in_specs=[
            pl.BlockSpec((tm, D_MODEL), lambda i: (i, 0)),
            whole(g), whole(w_main), whole(w_t), whole(w_g), whole(w_small), whole(w2), whole(gb),
        ],
        out_specs=[
            pl.BlockSpec((tm, Z_WIDTH), lambda i: (i, 0)),
            fm(ATTN_W), fm(ATTN_W), fm(IDX_Q_W),
            pl.BlockSpec((tm, IDX_HEAD_DIM), lambda i: (i, 0)),
            fm(N_IDX_HEADS),
            pl.BlockSpec((tm, GLA_IN_W), lambda i: (i, 0)),
            pl.BlockSpec((tm // CHUNK, GLA_K_W), lambda i: (i, 0)),
        ],
        out_shape=[
            jax.ShapeDtypeStruct((t, Z_WIDTH), bf16),
            jax.ShapeDtypeStruct((batch, ATTN_W, s), bf16),
            jax.ShapeDtypeStruct((batch, ATTN_W, s), bf16),
            jax.ShapeDtypeStruct((batch, IDX_Q_W, s), bf16),
            jax.ShapeDtypeStruct((t, IDX_HEAD_DIM), bf16),
            jax.ShapeDtypeStruct((batch, N_IDX_HEADS, s), f32),
            jax.ShapeDtypeStruct((t, GLA_IN_W), bf16),
            jax.ShapeDtypeStruct((t // CHUNK, GLA_K_W), f32),
        ],
        compiler_params=_cparams(("parallel",)),
        name="proj",
    )(x, g, w_main, w_t, w_g, w_small, w2, gb)


def _dsa_kernel(qT_ref, iqT_ref, iwT_ref, tab_ref, k_ref, vT_ref, ik_ref, o_ref,
                ibuf, qTm, s_ref, bias_ref, m_ref, l_ref, alpha_ref, acc_ref, thr_ref,
                ext_ref, pending_ref, finish_ref, *, seq_len):
    qb = DSA_BLOCK
    kt_rows = DSA_BLOCK
    i = pl.program_id(1)
    n_tiles = i + 1

    rowid = lax.broadcasted_iota(jnp.int32, (LANES, qb), 0)
    for h in range(N_ATTN_HEADS):
        pair = qT_ref[(h // 2) * LANES:(h // 2 + 1) * LANES, :]
        keep = (rowid < ATTN_HEAD_DIM) if h % 2 == 0 else (rowid >= ATTN_HEAD_DIM)
        qTm[h] = jnp.where(keep, pair, jnp.zeros_like(pair))
    l_ref[...] = jnp.zeros_like(l_ref)
    acc_ref[...] = jnp.zeros_like(acc_ref)

    def fold(x, op):
        return op(x.reshape(kt_rows // SUBLANES, SUBLANES, qb), axis=0)

    def idx_tile(kt, diagonal, stats):
        r0 = pl.multiple_of(kt * kt_rows, kt_rows)
        ki_t = ik_ref[pl.ds(r0, kt_rows), :]
        sc = jnp.zeros((kt_rows, qb), f32)
        for h in range(N_IDX_HEADS):
            s = jnp.dot(ki_t, iqT_ref[h * IDX_HEAD_DIM:(h + 1) * IDX_HEAD_DIM, :],
                        preferred_element_type=f32)
            sc = sc + iwT_ref[h:h + 1, :] * jnp.maximum(s, 0.0)
        sc_hi = sc_lo = sc_0 = sc
        if diagonal:
            kc = lax.broadcasted_iota(jnp.int32, (kt_rows, qb), 0) // CHUNK
            qc = lax.broadcasted_iota(jnp.int32, (kt_rows, qb), 1) // CHUNK
            adm = kc <= qc
            sc_hi = jnp.where(adm, sc, -jnp.inf)
            sc_lo = jnp.where(adm, sc, jnp.inf)
            sc_0 = jnp.where(adm, sc, 0.0)
        ibuf[pl.ds(r0, kt_rows), :] = sc_hi
        mx8, mn8, s8, ss8 = stats
        return (jnp.maximum(mx8, fold(sc_hi, jnp.max)), jnp.minimum(mn8, fold(sc_lo, jnp.min)),
                s8 + fold(sc_0, jnp.sum), ss8 + fold(sc_0 * sc_0, jnp.sum))

    stats0 = (jnp.full((SUBLANES, qb), -jnp.inf, f32), jnp.full((SUBLANES, qb), jnp.inf, f32),
              jnp.zeros((SUBLANES, qb), f32), jnp.zeros((SUBLANES, qb), f32))
    odd = i % 2
    stats = lax.cond(odd == 1, lambda st: idx_tile(0, False, st), lambda st: st, stats0)
    stats = lax.fori_loop(
        0, i // 2,
        lambda j, st: idx_tile(odd + 2 * j + 1, False, idx_tile(odd + 2 * j, False, st)),
        stats)
    mx8, mn8, s8, ss8 = idx_tile(i, True, stats)
    col_max = jnp.max(mx8, axis=0, keepdims=True)
    col_min = jnp.min(mn8, axis=0, keepdims=True)
    col_sum = jnp.sum(s8, axis=0, keepdims=True)
    col_ssq = jnp.sum(ss8, axis=0, keepdims=True)

    def count(pred):
        def body(r, acc):
            r0 = pl.multiple_of(r * kt_rows, kt_rows)
            tile = ibuf[pl.ds(r0, kt_rows), :]
            rows = r0 + lax.broadcasted_iota(jnp.int32, (kt_rows, qb), 0)
            ind = jnp.where(pred(tile, rows), 1.0, 0.0)
            part = ind.reshape(kt_rows // SUBLANES, SUBLANES, qb)
            while part.shape[0] > 1:
                half = part.shape[0] // 2
                part = part[:half] + part[half:]
            return acc + part[0]
        acc = jnp.zeros((SUBLANES, qb), f32)
        acc = lax.cond(i % 2 == 0, lambda a: body(0, a), lambda a: a, acc)
        first = 1 - i % 2
        acc = lax.fori_loop(
            0, (i + 1) // 2,
            lambda j, a: body(first + 2 * j + 1, body(first + 2 * j, a)), acc)
        return jnp.sum(acc, axis=0, keepdims=True)

    def resolve_ties(tv, need, mult):
        is_tie = tv == tv
        n_keys = n_tiles * kt_rows

        def tie_cond(st):
            return st["active"] > 0

        def tie_body(st):
            plo, phi, flo, fhi = st["plo"], st["phi"], st["flo"], st["fhi"]
            span = phi - plo
            est = ((need - flo) / jnp.maximum(fhi - flo, 1.0) * span.astype(f32)).astype(jnp.int32)
            pick = jnp.where(st["bisect"] == 2, span // 2, est)
            cand = plo + jnp.clip(pick, 1, jnp.maximum(span - 1, 1))
            f = count(lambda tile, rows: jnp.where(tile == tv, rows, seq_len) < cand)
            lower = f < need
            exact = f == need
            plo = jnp.where(lower, cand, jnp.where(exact, cand - 1, plo))
            phi = jnp.where(lower, phi, cand)
            flo = jnp.where(lower, f, flo)
            fhi = jnp.where(lower, fhi, f)
            unsplit = jnp.where(is_tie & (phi - plo > 1), 1, 0)
            return dict(active=jnp.max(unsplit), bisect=(st["bisect"] + 1) % 3,
                        plo=plo, phi=phi, flo=flo, fhi=fhi)

        split = lax.while_loop(tie_cond, tie_body, dict(
            active=jnp.int32(1), bisect=jnp.int32(0),
            plo=jnp.zeros((1, qb), jnp.int32), phi=jnp.zeros((1, qb), jnp.int32) + n_keys,
            flo=jnp.zeros((1, qb), f32), fhi=mult))["phi"]

        def drop_body(r, carry):
            r0 = pl.multiple_of(r * kt_rows, kt_rows)
            tile = ibuf[pl.ds(r0, kt_rows), :]
            rows = r0 + lax.broadcasted_iota(jnp.int32, (kt_rows, qb), 0)
            dropped = jnp.where(tile == tv, rows, -1) >= split
            ibuf[pl.ds(r0, kt_rows), :] = jnp.where(dropped, -jnp.inf, tile)
            return carry

        lax.fori_loop(0, n_tiles, drop_body, 0)

    qpos = i * qb + lax.broadcasted_iota(jnp.int32, (1, qb), 1)
    n_adm = (qpos // CHUNK + 1) * CHUNK
    select_all = n_adm <= TOPK_MAX
    kf = float(TOPK_MAX)

    thr_ref[...] = jnp.full((1, qb), F32_LOWEST, f32)
    pending_ref[0] = jnp.int32(0)

    @pl.when(n_tiles * kt_rows > TOPK_MAX)
    def _():
        n_f = n_adm.astype(f32)
        mean = col_sum / n_f
        sigma = jnp.sqrt(jnp.maximum(col_ssq / n_f - mean * mean, 0.0))
        spacing = tab_ref[1:2, :] * sigma

        def inside(c, lo, hi):
            return (c > lo) & (c < hi)

        c0 = mean + tab_ref[0:1, :] * sigma
        c0 = jnp.where(inside(c0, col_min, col_max), c0, 0.5 * col_min + 0.5 * col_max)
        zero = jnp.zeros((1, qb), f32)
        one = jnp.ones((1, qb), f32)
        state0 = dict(it=jnp.int32(0), active=jnp.int32(1), c=c0, lo=col_min, hi=col_max,
                      glo=n_f, ghi=zero, tlo=zero, thi=zero, boost=one, side=zero,
                      thr=jnp.full((1, qb), F32_LOWEST, f32),
                      open=jnp.where(select_all, 0.0, 1.0), tie=zero, kind=zero, emin=zero)

        def search_cond(st):
            return (st["it"] < SEARCH_MAX_ITERS) & (st["active"] > 0)

        def search_body(st):
            c, lo, hi = st["c"], st["lo"], st["hi"]
            g = count(lambda tile, rows: tile >= c)
            hit = (g == kf) & (st["open"] > 0.0)
            thr = jnp.where(hit, c, st["thr"])
            still = jnp.where(hit, 0.0, st["open"])
            above = g > kf
            lo = jnp.where(above, c, lo)
            hi = jnp.where(above, hi, c)
            glo = jnp.where(above, g, st["glo"])
            ghi = jnp.where(above, st["ghi"], g)
            tlo = jnp.where(above, 1.0, st["tlo"])
            thi = jnp.where(above, st["thi"], 1.0)
            side = jnp.where(above, 1.0, -1.0)
            repeat = side == st["side"]
            boost = jnp.where(repeat, 2.0 * st["boost"], 1.0)
            mid = 0.5 * lo + 0.5 * hi
            stale = 1.0 / jnp.minimum(boost, 256.0)
            w_lo = (glo - kf - 0.5) * jnp.where(above, 1.0, stale)
            w_hi = (kf + 0.5 - ghi) * jnp.where(above, stale, 1.0)
            c_two = lo + w_lo / (w_lo + w_hi) * (hi - lo)
            step = 2.0 * boost * spacing
            c_one = jnp.where(tlo > 0.0, lo + jnp.maximum(glo - kf, 1.0) * step,
                              hi - jnp.maximum(kf - ghi, 1.0) * step)
            c_one = jnp.where(inside(c_one, lo, hi), c_one, mid)
            both = (tlo > 0.0) & (thi > 0.0)
            c_new = jnp.where(both, c_two, c_one)
            c_new = jnp.where(both & (lo < 0.0) & (hi > 0.0), 0.0, c_new)
            c_new = jnp.where((lo == 0.0) & (hi > F32_TINY), F32_TINY, c_new)
            movable = inside(c_new, lo, hi) & ~((lo == 0.0) & (hi <= F32_TINY))
            missed = (still > 0.0) & (((st["kind"] == 1.0) & above) |
                                      ((st["kind"] == 2.0) & (g < kf)))
            lo = jnp.where(missed & (st["kind"] == 2.0), st["emin"], lo)
            closed = (~movable & both & (still > 0.0)) | missed
            tie = jnp.where(closed, 1.0, st["tie"])
            still = jnp.where(closed, 0.0, still)
            searching = jnp.where(movable, still, 0.0)

            from_hi = both & (kf - ghi == 1.0)
            from_lo = both & (glo - kf == 1.0) & ~from_hi
            ready = jnp.where(searching > 0.0, jnp.where(from_hi | from_lo, 1.0, 2.0), 0.0)
            code = jnp.max(ready)
            ext_ref[...] = jnp.zeros_like(ext_ref)
            finish_ref[0] = jnp.int32(0)

            @pl.when((code == 1.0) & (st["it"] + 1 >= SEARCH_FINISH_FROM))
            def _():
                def ext_body(r, carry):
                    r0 = pl.multiple_of(r * kt_rows, kt_rows)
                    tile = ibuf[pl.ds(r0, kt_rows), :]
                    below = fold(jnp.where(tile < hi, tile, -jnp.inf), jnp.max)
                    above_lo = fold(jnp.where(tile >= lo, tile, jnp.inf), jnp.min)
                    return jnp.maximum(carry[0], below), jnp.minimum(carry[1], above_lo)
                mx8, mn8 = lax.fori_loop(0, n_tiles, ext_body,
                                         (jnp.full((SUBLANES, qb), -jnp.inf, f32),
                                          jnp.full((SUBLANES, qb), jnp.inf, f32)))
                ext_ref[0:1, :] = jnp.max(mx8, axis=0, keepdims=True)
                ext_ref[1:2, :] = jnp.min(mn8, axis=0, keepdims=True)
                finish_ref[0] = jnp.int32(1)

            finishing = finish_ref[0] > 0
            emax = ext_ref[0:1, :]
            emin = ext_ref[1:2, :]
            ebits = lax.bitcast_convert_type(emin, jnp.int32)
            next_up = lax.bitcast_convert_type(ebits + jnp.where(emin > 0.0, 1, -1), f32)
            next_up = jnp.where(emin == 0.0, F32_TINY, next_up)
            kind = jnp.where(finishing & (searching > 0.0),
                             jnp.where(from_hi, 1.0, jnp.where(from_lo, 2.0, 0.0)), 0.0)
            c_new = jnp.where(kind == 1.0, emax, jnp.where(kind == 2.0, next_up, c_new))
            return dict(it=st["it"] + 1, active=(code > 0.0).astype(jnp.int32),
                        c=c_new, lo=lo, hi=hi, glo=glo, ghi=ghi, tlo=tlo, thi=thi,
                        boost=boost, side=side, thr=thr, open=still, tie=tie,
                        kind=kind, emin=emin)

        final = lax.while_loop(search_cond, search_body, state0)
        tied = final["tie"] > 0.0
        thr_ref[...] = jnp.where(tied, final["lo"], final["thr"])
        pending_ref[0] = (jnp.max(final["open"]) > 0.0).astype(jnp.int32)

        @pl.when(jnp.max(final["tie"]) > 0.0)
        def _():
            resolve_ties(jnp.where(tied, final["lo"], jnp.nan), kf - final["ghi"],
                         final["glo"] - final["ghi"])

    @pl.when(pending_ref[0] > 0)
    def _():
        def key_to_f32(key):
            bits = jnp.where(key < 0, key ^ jnp.int32(0x7FFFFFFF), key)
            return lax.bitcast_convert_type(bits, f32)

        def bit_body(t, prefix):
            step = lax.shift_left(jnp.int32(1), jnp.int32(31) - t)
            cand = prefix + step
            cand_f = key_to_f32(cand)
            cnt = count(lambda tile, rows: tile >= cand_f)
            return jnp.where(cnt >= kf, cand, prefix)

        prefix = lax.fori_loop(0, 32, bit_body, jnp.full((1, qb), INT32_MIN, jnp.int32))
        thr = jnp.where(select_all, F32_LOWEST, key_to_f32(prefix))
        thr_ref[...] = thr

        cnt_ge = count(lambda tile, rows: tile >= thr)
        excess = jnp.where(select_all, 0.0, cnt_ge - kf)

        @pl.when(jnp.max(excess) > 0.0)
        def _():
            cnt_gt = count(lambda tile, rows: tile > thr)
            resolve_ties(jnp.where(excess > 0.0, thr, jnp.nan), kf - cnt_gt, cnt_ge - cnt_gt)

    thr = thr_ref[...]
    ones_rows = jnp.ones((2 * SUBLANES, kt_rows), bf16)

    m_ref[1] = jnp.full(m_ref.shape[1:], NEG_BIG, f32)

    def logits_stage(kt, slot):
        r0 = pl.multiple_of(kt * kt_rows, kt_rows)
        bias_ref[...] = jnp.where(ibuf[pl.ds(r0, kt_rows), :] >= thr, 0.0, NEG_BIG)
        for h in range(N_ATTN_HEADS):
            k_pair = k_ref[pl.ds(r0, kt_rows), (h // 2) * LANES:(h // 2 + 1) * LANES]
            s_ref[slot, h] = jnp.dot(k_pair, qTm[h], preferred_element_type=f32) + bias_ref[...]
            m_old = m_ref[1 - slot, h]
            m_new = jnp.maximum(m_old, jnp.max(s_ref[slot, h], axis=0, keepdims=True))
            alpha_ref[slot, h] = jnp.exp2(m_old - m_new)
            m_ref[slot, h] = m_new

    def values_stage(kt, slot):
        r0 = pl.multiple_of(kt * kt_rows, kt_rows)
        for h in range(N_ATTN_HEADS):
            p = jnp.exp2(s_ref[slot, h] - m_ref[slot, h]).astype(bf16)
            hs = slice(h * ATTN_HEAD_DIM, (h + 1) * ATTN_HEAD_DIM)
            lhs = jnp.concatenate([vT_ref[hs, pl.ds(r0, kt_rows)], ones_rows], axis=0)
            pv = jnp.dot(lhs, p, preferred_element_type=f32)
            alpha = alpha_ref[slot, h]
            acc_ref[hs, :] = alpha * acc_ref[hs, :] + pv[:ATTN_HEAD_DIM]
            l_ref[h] = alpha * l_ref[h] + pv[ATTN_HEAD_DIM:ATTN_HEAD_DIM + 1]

    def att_body(j, carry):
        kt = 2 * j
        logits_stage(kt + 1, 1)
        values_stage(kt, 0)
        logits_stage(kt + 2, 0)
        values_stage(kt + 1, 1)
        return carry

    logits_stage(0, 0)
    lax.fori_loop(0, i // 2, att_body, 0)

    @pl.when(i % 2 == 1)
    def _():
        logits_stage(i, 1)
        values_stage(i - 1, 0)
        values_stage(i, 1)

    @pl.when(i % 2 == 0)
    def _():
        values_stage(i, 0)

    for h in range(N_ATTN_HEADS):
        hs = slice(h * ATTN_HEAD_DIM, (h + 1) * ATTN_HEAD_DIM)
        acc_ref[hs, :] = acc_ref[hs, :] / l_ref[h]
    o_ref[...] = acc_ref[...].T.astype(o_ref.dtype)


def _search_hints(seq_len):
    n = (np.arange(seq_len) // CHUNK + 1) * CHUNK
    frac = np.minimum(TOPK_MAX / n, 0.5)
    nd = statistics.NormalDist()
    z_of = {f: nd.inv_cdf(1.0 - f) for f in np.unique(frac)}
    z = np.array([z_of[f] for f in frac])
    dens = n * np.exp(-0.5 * z * z) / np.sqrt(2.0 * np.pi)
    return jnp.asarray(np.stack([z, 1.0 / dens]), f32)


def _dsa(qT, iqT, iwT, z3, vT, ik3):
    b, _, s = qT.shape
    qb = DSA_BLOCK
    return pl.pallas_call(
        functools.partial(_dsa_kernel, seq_len=s),
        grid=(b, s // qb),
        in_specs=[
            pl.BlockSpec((None, ATTN_W, qb), lambda bi, i: (bi, 0, i)),
            pl.BlockSpec((None, IDX_Q_W, qb), lambda bi, i: (bi, 0, i)),
            pl.BlockSpec((None, N_IDX_HEADS, qb), lambda bi, i: (bi, 0, i)),
            pl.BlockSpec((2, qb), lambda bi, i: (0, i)),
            pl.BlockSpec((None, s, COL_TILE), lambda bi, i: (bi, 0, Z_AK)),
            pl.BlockSpec((None, ATTN_W, s), lambda bi, i: (bi, 0, 0)),
            pl.BlockSpec((None, s, IDX_HEAD_DIM), lambda bi, i: (bi, 0, 0)),
        ],
        out_specs=pl.BlockSpec((None, qb, ATTN_W), lambda bi, i: (bi, i, 0)),
        out_shape=jax.ShapeDtypeStruct((b, s, ATTN_W), bf16),
        scratch_shapes=[
            pltpu.VMEM((s, qb), f32),
            pltpu.VMEM((N_ATTN_HEADS, LANES, qb), bf16),
            pltpu.VMEM((2, N_ATTN_HEADS, qb, qb), f32),
            pltpu.VMEM((qb, qb), f32),
            pltpu.VMEM((2, N_ATTN_HEADS, 1, qb), f32),
            pltpu.VMEM((N_ATTN_HEADS, 1, qb), f32),
            pltpu.VMEM((2, N_ATTN_HEADS, 1, qb), f32),
            pltpu.VMEM((ATTN_W, qb), f32),
            pltpu.VMEM((1, qb), f32),
            pltpu.VMEM((2, qb), f32),
            pltpu.SMEM((1,), jnp.int32),
            pltpu.SMEM((1,), jnp.int32),
        ],
        compiler_params=_cparams(("parallel", "arbitrary")),
        name="dsa",
    )(qT, iqT, iwT, _search_hints(s), z3, vT, ik3)


def _gla_kernel(gin_ref, dec_ref, v_ref, r_ref, gn_ref, o_ref, state_ref):
    c = CHUNK
    n_batch = gin_ref.shape[0]

    @pl.when(pl.program_id(0) == 0)
    def _():
        state_ref[...] = jnp.zeros_like(state_ref)

    row = lax.broadcasted_iota(jnp.int32, (c, c), 0)
    col = lax.broadcasted_iota(jnp.int32, (c, c), 1)
    causal = row >= col
    lane = lax.broadcasted_iota(jnp.int32, (c, LANES), 1)
    half = (lane < GLA_HEAD_K, lane >= GLA_HEAD_K)
    contract_last = (((1,), (1,)), ((), ()))
    contract_first = (((0,), (0,)), ((), ()))
    zero = jnp.zeros((c, LANES), bf16)

    for ci, bi in [(ci, bi) for ci in range(GLA_TILE // c) for bi in range(n_batch)]:
        rows = slice(ci * c, (ci + 1) * c)
        decay = dec_ref[bi, ci:ci + 1, :]
        for pr in range(N_GLA_HEADS // 2):
            ps = slice(pr * LANES, (pr + 1) * LANES)
            part = lambda g: gin_ref[bi, rows, g * GLA_K_W + pr * LANES:
                                     g * GLA_K_W + (pr + 1) * LANES]
            q_in, q_mid, k_mid, k_out = part(0), part(1), part(2), part(3)
            st = state_ref[bi, pr]
            st_b = st.astype(bf16)
            upd = jnp.zeros_like(st)
            for hh in range(2):
                h = pr * 2 + hh
                vs = slice(h * GLA_HEAD_V, (h + 1) * GLA_HEAD_V)
                v_h = v_ref[bi, rows, vs]
                attn = lax.dot_general(jnp.where(half[hh], q_mid, zero), k_mid, contract_last,
                                       preferred_element_type=f32)
                attn = jnp.where(causal, attn, 0.0).astype(bf16)
                o = lax.dot_general(jnp.where(half[hh], q_in, zero), st_b, contract_last,
                                    preferred_element_type=f32)
                o = o + jnp.dot(attn, v_h, preferred_element_type=f32)
                upd = upd + lax.dot_general(v_h, jnp.where(half[hh], k_out, zero),
                                            contract_first, preferred_element_type=f32)
                y = _rms(o, gn_ref[...])
                r = r_ref[bi, rows, vs].astype(f32)
                o_ref[bi, rows, vs] = (y * r * _sigmoid(r)).astype(o_ref.dtype)
            state_ref[bi, pr] = decay[:, ps] * st + upd


def _gla(gin3, dec3, z3, gn):
    b, s, _ = z3.shape
    tb = GLA_TILE
    return pl.pallas_call(
        _gla_kernel,
        grid=(s // tb,),
        in_specs=[
            pl.BlockSpec((b, tb, GLA_IN_W), lambda j: (0, j, 0)),
            pl.BlockSpec((b, tb // CHUNK, GLA_K_W), lambda j: (0, j, 0)),
            pl.BlockSpec((b, tb, COL_TILE), lambda j: (0, j, Z_GV)),
            pl.BlockSpec((b, tb, COL_TILE), lambda j: (0, j, Z_GR)),
            pl.BlockSpec((1, GLA_HEAD_V), lambda j: (0, 0)),
        ],
        out_specs=pl.BlockSpec((b, tb, GLA_V_W), lambda j: (0, j, 0)),
        out_shape=jax.ShapeDtypeStruct((b, s, GLA_V_W), bf16),
        scratch_shapes=[pltpu.VMEM((b, N_GLA_HEADS // 2, GLA_HEAD_V, LANES), f32)],
        compiler_params=_cparams(("arbitrary",)),
        name="gla",
    )(gin3, dec3, z3, z3, gn)


def _tail_kernel(x_ref, oa_ref, ob_ref, ga_ref, gb_ref, wa_ref, wb_ref, wo_ref,
                 g_ref, wg_ref, wu_ref, wd_ref, p_ref, gp_ref, wpg_ref, wpp_ref, gf_ref,
                 o_ref, *, final):
    ya = jnp.dot(oa_ref[...], wa_ref[...], preferred_element_type=f32)
    yb = jnp.dot(ob_ref[...], wb_ref[...], preferred_element_type=f32)
    mg = _sigmoid(ga_ref[...].astype(f32)) * ya + _sigmoid(gb_ref[...].astype(f32)) * yb
    x = x_ref[...] + jnp.dot(mg.astype(bf16), wo_ref[...], preferred_element_type=f32)
    x = _swiglu_half_step(x, g_ref, wg_ref, wu_ref, wd_ref)
    h = _rms(x, gp_ref[...]).astype(bf16)
    gate = _sigmoid(jnp.dot(h, wpg_ref[...], preferred_element_type=f32))
    e = jnp.dot(p_ref[...].astype(bf16), wpp_ref[...], preferred_element_type=f32)
    y = x + gate * e
    if final:
        y = _rms(y, gf_ref[...])
    o_ref[...] = y


def _tail(x, oa, ob, z, wa, wb, wo, g, wg, wu, wd, p, gp, wpg, wpp, gf, final, out_shape):
    t = x.shape[0]
    tm = TOKEN_TILE
    rows = lambda width, col=0: pl.BlockSpec((tm, width), lambda i: (i, col))
    return pl.pallas_call(
        functools.partial(_tail_kernel, final=final),
        grid=(t // tm,),
        in_specs=[
            rows(D_MODEL), rows(ATTN_W), rows(GLA_V_W), rows(D_MODEL, Z_MGA), rows(D_MODEL, Z_MGB),
            _resident(wa), _resident(wb), _resident(wo),
            _resident(g), _resident(wg), _resident(wu), _resident(wd),
            rows(PLE_DIM), _resident(gp), _resident(wpg), _resident(wpp), _resident(gf),
        ],
        out_specs=_token_rows(out_shape),
        out_shape=jax.ShapeDtypeStruct(out_shape, f32),
        compiler_params=_cparams(("parallel",)),
        name="tail",
    )(x, oa, ob, z, z, wa, wb, wo, g, wg, wu, wd, p, gp, wpg, wpp, gf)


def _split_w_in(w_in):
    cuts = np.cumsum(np.array(SPLIT_SIZES))[:-1].tolist()
    aq, ak, av, iq, ik, iw, gq, gk, gv, gr, ga, mga, mgb = jnp.split(w_in, cuts, axis=-1)
    att_scale = ATTN_HEAD_DIM ** -0.5 * float(np.log2(np.e))
    idx_scale = IDX_HEAD_DIM ** -0.5 * N_IDX_HEADS ** -0.5
    c = lambda w: w.astype(bf16)
    main = jnp.concatenate([c(mga), c(mgb), c(ak), c(gv), c(gr)], axis=-1)
    feature_major = jnp.concatenate([c(aq * att_scale), c(av), c(iq)], axis=-1)
    gla_qk = jnp.concatenate([c(gq * (GLA_HEAD_K ** -0.5)), c(gk)], axis=-1)
    zeros = lambda n: jnp.zeros(w_in.shape[:-1] + (n,), bf16)
    small = jnp.concatenate(
        [c(ik), zeros(LANES - IDX_HEAD_DIM), c(iw * idx_scale), c(ga),
         zeros(LANES - N_IDX_HEADS - GLA_GATE_RANK)], axis=-1)
    return main, feature_major, gla_qk, small


def kernel(x, p, w_in, gla_gate_w2, gla_gate_b, gla_norm, w_branch_a, w_branch_b, w_out,
           norm_ff1, norm_mix, norm_ff2, norm_ple, ff1_w_gate, ff1_w_up, ff1_w_down,
           ff2_w_gate, ff2_w_up, ff2_w_down, ple_w_proj, ple_w_gate, norm_final):
    b, s, d = x.shape
    t = b * s
    depth = w_in.shape[0]
    w_main, w_fm, w_gqk, w_small = _split_w_in(w_in)
    cast = lambda w: w.astype(bf16)
    ff1 = (cast(ff1_w_gate), cast(ff1_w_up), cast(ff1_w_down))
    ff2 = (cast(ff2_w_gate), cast(ff2_w_up), cast(ff2_w_down))
    wa, wb, wo = cast(w_branch_a), cast(w_branch_b), cast(w_out)
    wpg, wpp = cast(ple_w_gate), cast(ple_w_proj)
    row = lambda v: v.reshape(1, -1)

    xf = x
    for l in range(depth):
        last = l == depth - 1
        xf = _ffn(xf, row(norm_ff1[l]), ff1[0][l], ff1[1][l], ff1[2][l])

        z, qT, vT, iqT, ik, iwT, gin, dec = _proj(
            xf, row(norm_mix[l]), w_main[l], w_fm[l], w_gqk[l], w_small[l],
            gla_gate_w2[l], row(gla_gate_b[l]), b)
        z3 = z.reshape(b, s, Z_WIDTH)
        oa = _dsa(qT, iqT, iwT, z3, vT, ik.reshape(b, s, IDX_HEAD_DIM))
        ob = _gla(gin.reshape(b, s, GLA_IN_W), dec.reshape(b, s // CHUNK, GLA_K_W), z3,
                  row(gla_norm[l]))
        xf = _tail(xf, oa.reshape(t, ATTN_W), ob.reshape(t, GLA_V_W), z, wa[l], wb[l], wo[l],
                   row(norm_ff2[l]), ff2[0][l], ff2[1][l], ff2[2][l],
                   p[l].reshape(t, PLE_DIM), row(norm_ple[l]), wpg[l], wpp[l],
                   row(norm_final), final=last, out_shape=(b, s, d) if last else (t, d))
    return xf
```

```python
import functools
import statistics

import jax
import jax.numpy as jnp
import numpy as np
from jax import lax
from jax.experimental import pallas as pl
from jax.experimental.pallas import tpu as pltpu

D_MODEL = 1024
D_FF = 2816
PLE_DIM = 256
EPS = 1e-6

CHUNK = 64
N_ATTN_HEADS = 8
ATTN_HEAD_DIM = 64
N_IDX_HEADS = 8
IDX_HEAD_DIM = 64
TOPK_MAX = 256
N_GLA_HEADS = 4
GLA_HEAD_K = 64
GLA_HEAD_V = 128
GLA_GATE_RANK = 16
GLA_GATE_TAU = 16.0

ATTN_W = N_ATTN_HEADS * ATTN_HEAD_DIM
IDX_Q_W = N_IDX_HEADS * IDX_HEAD_DIM
GLA_K_W = N_GLA_HEADS * GLA_HEAD_K
GLA_V_W = N_GLA_HEADS * GLA_HEAD_V
SPLIT_SIZES = (ATTN_W, ATTN_W, ATTN_W, IDX_Q_W, IDX_HEAD_DIM, N_IDX_HEADS,
               GLA_K_W, GLA_K_W, GLA_V_W, GLA_V_W, GLA_GATE_RANK, D_MODEL, D_MODEL)

LANES = 128
SUBLANES = 8
V7X_VMEM_BYTES = 64 * 1024 * 1024
VMEM_LIMIT_BYTES = V7X_VMEM_BYTES // 8 * 7

COL_TILE = 512
Z_MGA, Z_MGB = 0, 1
Z_AK, Z_GV, Z_GR = 4, 5, 6
Z_TILES = 7
Z_WIDTH = Z_TILES * COL_TILE
ZT_TILES = 3
SM_IW0 = 0
SM_GA0 = N_IDX_HEADS

TOKEN_TILE = 512
FF_TILE = 1408
DSA_BLOCK = 256
GLA_TILE = 512
GLA_IN_W = 4 * GLA_K_W
SEARCH_MAX_ITERS = 24
SEARCH_FINISH_FROM = 4

NEG_BIG = -1e30
F32_LOWEST = float(np.finfo(np.float32).min)
F32_TINY = float(np.finfo(np.float32).tiny)
INT32_MIN = int(np.iinfo(np.int32).min)

f32 = jnp.float32
bf16 = jnp.bfloat16


def _rms(x, g):
    return x * lax.rsqrt(jnp.mean(x * x, axis=-1, keepdims=True) + EPS) * g


def _sigmoid(x):
    return 1.0 / (1.0 + jnp.exp(-x))


def _cparams(sem):
    return pltpu.CompilerParams(dimension_semantics=sem, vmem_limit_bytes=VMEM_LIMIT_BYTES)


def _resident(a):
    return pl.BlockSpec(a.shape, lambda i: (0, 0), pipeline_mode=pl.Buffered(1))


def _swiglu_half_step(x, g_ref, wg_ref, wu_ref, wd_ref):
    h = _rms(x, g_ref[...]).astype(bf16)
    y = x
    for f in range(D_FF // FF_TILE):
        cols = slice(f * FF_TILE, (f + 1) * FF_TILE)
        gate = jnp.dot(h, wg_ref[:, cols], preferred_element_type=f32)
        up = jnp.dot(h, wu_ref[:, cols], preferred_element_type=f32)
        a = (gate * _sigmoid(gate) * up).astype(bf16)
        y = y + 0.5 * jnp.dot(a, wd_ref[cols, :], preferred_element_type=f32)
    return y


def _ffn_kernel(x_ref, g_ref, wg_ref, wu_ref, wd_ref, o_ref):
    o_ref[...] = _swiglu_half_step(x_ref[...], g_ref, wg_ref, wu_ref, wd_ref)


def _token_rows(shape):
    if len(shape) == 2:
        return pl.BlockSpec((TOKEN_TILE, shape[1]), lambda i: (i, 0))
    per_b = shape[1] // TOKEN_TILE
    return pl.BlockSpec((None, TOKEN_TILE, shape[2]), lambda i: (i // per_b, i % per_b, 0))


def _ffn(x, g, wg, wu, wd):
    t = x.size // D_MODEL
    return pl.pallas_call(
        _ffn_kernel,
        grid=(t // TOKEN_TILE,),
        in_specs=[
            _token_rows(x.shape),
            _resident(g), _resident(wg), _resident(wu), _resident(wd),
        ],
        out_specs=pl.BlockSpec((TOKEN_TILE, D_MODEL), lambda i: (i, 0)),
        out_shape=jax.ShapeDtypeStruct((t, D_MODEL), f32),
        compiler_params=_cparams(("parallel",)),
        name="ffn",
    )(x, g, wg, wu, wd)


def _gla_operands(q, k, ga, w2, gb):
    n = q.shape[0]
    c = CHUNK
    def split(a):
        hi = a.astype(bf16)
        return hi, (a - hi.astype(f32)).astype(bf16)

    dot = lambda a, b: jnp.dot(a, b, preferred_element_type=f32)
    ga_hi, ga_lo = split(ga)
    w2_hi, w2_lo = split(w2)
    glin = dot(ga_hi, w2_hi) + dot(ga_hi, w2_lo) + dot(ga_lo, w2_hi) + gb
    logg = (jnp.minimum(glin, 0.0) - jnp.log(1.0 + jnp.exp(-jnp.abs(glin)))) / GLA_GATE_TAU
    row = lax.broadcasted_iota(jnp.int32, (n, n), 0)
    col = lax.broadcasted_iota(jnp.int32, (n, n), 1)
    tril = jnp.where((row // c == col // c) & (col <= row), 1.0, 0.0).astype(bf16)
    hi, lo = split(logg)
    bcum = dot(tril, hi) + dot(tril, lo)
    by_chunk = bcum.reshape(n // c, c, GLA_K_W)
    spread = lambda r: jnp.broadcast_to(r, (n // c, c, GLA_K_W)).reshape(n, GLA_K_W)
    b_last = by_chunk[:, c - 1:c, :]
    b_mid = spread(by_chunk[:, c // 2 - 1:c // 2, :])
    operands = jnp.concatenate(
        [q * jnp.exp(bcum), q * jnp.exp(bcum - b_mid), k * jnp.exp(b_mid - bcum),
         k * jnp.exp(spread(b_last) - bcum)], axis=1)
    return operands, jnp.exp(b_last.reshape(n // c, GLA_K_W))


def _proj_kernel(x_ref, g_ref, w_ref, wt_ref, wg_ref, ws_ref, w2_ref, gb_ref,
                 z_ref, qT_ref, vT_ref, iqT_ref, ik_ref, iwT_ref, gin_ref, dec_ref):
    h = _rms(x_ref[...], g_ref[...]).astype(bf16)
    for j in range(Z_TILES):
        cols = slice(j * COL_TILE, (j + 1) * COL_TILE)
        z_ref[:, cols] = jnp.dot(h, w_ref[:, cols], preferred_element_type=f32).astype(bf16)
    for j, out_ref in enumerate((qT_ref, vT_ref, iqT_ref)):
        cols = slice(j * COL_TILE, (j + 1) * COL_TILE)
        zt = jnp.dot(h, wt_ref[:, cols], preferred_element_type=f32)
        out_ref[...] = zt.T.astype(bf16)
    s = jnp.dot(h, ws_ref[...], preferred_element_type=f32)
    ik_ref[...] = s[:, :IDX_HEAD_DIM].astype(bf16)
    sm = s[:, LANES:]
    iwT_ref[...] = sm.T[SM_IW0:SM_IW0 + N_IDX_HEADS, :]
    gqk = jnp.dot(h, wg_ref[...], preferred_element_type=f32)
    operands, decay = _gla_operands(gqk[:, :GLA_K_W], gqk[:, GLA_K_W:],
                                    sm[:, SM_GA0:SM_GA0 + GLA_GATE_RANK],
                                    w2_ref[...], gb_ref[...])
    gin_ref[...] = operands.astype(bf16)
    dec_ref[...] = decay


def _proj(x, g, w_main, w_t, w_g, w_small, w2, gb, batch):
    t = x.shape[0]
    tm = TOKEN_TILE
    s = t // batch
    per_b = s // tm
    fm = lambda rows: pl.BlockSpec((None, rows, tm), lambda i: (i // per_b, 0, i % per_b))
    return pl.pallas_call(
        _proj_kernel,
        grid=(t // tm,),
        in_specs=[
            pl.BlockSpec((tm, D_MODEL), lambda i: (i, 0)),
            _resident(g), _resident(w_main), _resident(w_t), _resident(w_g), _resident(w_small),
            _resident(w2), _resident(gb),
        ],
        out_specs=[
            pl.BlockSpec((tm, Z_WIDTH), lambda i: (i, 0)),
            fm(ATTN_W), fm(ATTN_W), fm(IDX_Q_W),
            pl.BlockSpec((tm, IDX_HEAD_DIM), lambda i: (i, 0)),
            fm(N_IDX_HEADS),
            pl.BlockSpec((tm, GLA_IN_W), lambda i: (i, 0)),
            pl.BlockSpec((tm // CHUNK, GLA_K_W), lambda i: (i, 0)),
        ],
        out_shape=[
            jax.ShapeDtypeStruct((t, Z_WIDTH), bf16),
            jax.ShapeDtypeStruct((batch, ATTN_W, s), bf16),
            jax.ShapeDtypeStruct((batch, ATTN_W, s), bf16),
            jax.ShapeDtypeStruct((batch, IDX_Q_W, s), bf16),
            jax.ShapeDtypeStruct((t, IDX_HEAD_DIM), bf16),
            jax.ShapeDtypeStruct((batch, N_IDX_HEADS, s), f32),
            jax.ShapeDtypeStruct((t, GLA_IN_W), bf16),
            jax.ShapeDtypeStruct((t // CHUNK, GLA_K_W), f32),
        ],
        compiler_params=_cparams(("parallel",)),
        name="proj",
    )(x, g, w_main, w_t, w_g, w_small, w2, gb)


def _dsa_kernel(qT_ref, iqT_ref, iwT_ref, tab_ref, k_ref, vT_ref, ik_ref, o_ref,
                ibuf, qTm, s_ref, bias_ref, m_ref, l_ref, alpha_ref, acc_ref, thr_ref,
                ext_ref, pending_ref, finish_ref, *, seq_len):
    qb = DSA_BLOCK
    kt_rows = DSA_BLOCK
    i = pl.program_id(1)
    n_tiles = i + 1

    rowid = lax.broadcasted_iota(jnp.int32, (LANES, qb), 0)
    for h in range(N_ATTN_HEADS):
        pair = qT_ref[(h // 2) * LANES:(h // 2 + 1) * LANES, :]
        keep = (rowid < ATTN_HEAD_DIM) if h % 2 == 0 else (rowid >= ATTN_HEAD_DIM)
        qTm[h] = jnp.where(keep, pair, jnp.zeros_like(pair))
    l_ref[...] = jnp.zeros_like(l_ref)
    acc_ref[...] = jnp.zeros_like(acc_ref)

    def fold(x, op):
        return op(x.reshape(kt_rows // SUBLANES, SUBLANES, qb), axis=0)

    def idx_tile(kt, diagonal, stats):
        r0 = pl.multiple_of(kt * kt_rows, kt_rows)
        ki_t = ik_ref[pl.ds(r0, kt_rows), :]
        sc = jnp.zeros((kt_rows, qb), f32)
        for h in range(N_IDX_HEADS):
            s = jnp.dot(ki_t, iqT_ref[h * IDX_HEAD_DIM:(h + 1) * IDX_HEAD_DIM, :],
                        preferred_element_type=f32)
            sc = sc + iwT_ref[h:h + 1, :] * jnp.maximum(s, 0.0)
        sc_hi = sc_lo = sc_0 = sc
        if diagonal:
            kc = lax.broadcasted_iota(jnp.int32, (kt_rows, qb), 0) // CHUNK
            qc = lax.broadcasted_iota(jnp.int32, (kt_rows, qb), 1) // CHUNK
            adm = kc <= qc
            sc_hi = jnp.where(adm, sc, -jnp.inf)
            sc_lo = jnp.where(adm, sc, jnp.inf)
            sc_0 = jnp.where(adm, sc, 0.0)
        ibuf[pl.ds(r0, kt_rows), :] = sc_hi
        mx8, mn8, s8, ss8 = stats
        return (jnp.maximum(mx8, fold(sc_hi, jnp.max)), jnp.minimum(mn8, fold(sc_lo, jnp.min)),
                s8 + fold(sc_0, jnp.sum), ss8 + fold(sc_0 * sc_0, jnp.sum))

    stats0 = (jnp.full((SUBLANES, qb), -jnp.inf, f32), jnp.full((SUBLANES, qb), jnp.inf, f32),
              jnp.zeros((SUBLANES, qb), f32), jnp.zeros((SUBLANES, qb), f32))
    odd = i % 2
    stats = lax.cond(odd == 1, lambda st: idx_tile(0, False, st), lambda st: st, stats0)
    stats = lax.fori_loop(
        0, i // 2,
        lambda j, st: idx_tile(odd + 2 * j + 1, False, idx_tile(odd + 2 * j, False, st)),
        stats)
    mx8, mn8, s8, ss8 = idx_tile(i, True, stats)
    col_max = jnp.max(mx8, axis=0, keepdims=True)
    col_min = jnp.min(mn8, axis=0, keepdims=True)
    col_sum = jnp.sum(s8, axis=0, keepdims=True)
    col_ssq = jnp.sum(ss8, axis=0, keepdims=True)

    def count(pred):
        def body(r, acc):
            r0 = pl.multiple_of(r * kt_rows, kt_rows)
            tile = ibuf[pl.ds(r0, kt_rows), :]
            rows = r0 + lax.broadcasted_iota(jnp.int32, (kt_rows, qb), 0)
            ind = jnp.where(pred(tile, rows), 1.0, 0.0)
            part = ind.reshape(kt_rows // SUBLANES, SUBLANES, qb)
            while part.shape[0] > 1:
                half = part.shape[0] // 2
                part = part[:half] + part[half:]
            return acc + part[0]
        acc = jnp.zeros((SUBLANES, qb), f32)
        acc = lax.cond(i % 2 == 0, lambda a: body(0, a), lambda a: a, acc)
        first = 1 - i % 2
        acc = lax.fori_loop(
            0, (i + 1) // 2,
            lambda j, a: body(first + 2 * j + 1, body(first + 2 * j, a)), acc)
        return jnp.sum(acc, axis=0, keepdims=True)

    def resolve_ties(tv, need, mult):
        is_tie = tv == tv
        n_keys = n_tiles * kt_rows

        def tie_cond(st):
            return st["active"] > 0

        def tie_body(st):
            plo, phi, flo, fhi = st["plo"], st["phi"], st["flo"], st["fhi"]
            span = phi - plo
            est = ((need - flo) / jnp.maximum(fhi - flo, 1.0) * span.astype(f32)).astype(jnp.int32)
            pick = jnp.where(st["bisect"] == 2, span // 2, est)
            cand = plo + jnp.clip(pick, 1, jnp.maximum(span - 1, 1))
            f = count(lambda tile, rows: jnp.where(tile == tv, rows, seq_len) < cand)
            lower = f < need
            exact = f == need
            plo = jnp.where(lower, cand, jnp.where(exact, cand - 1, plo))
            phi = jnp.where(lower, phi, cand)
            flo = jnp.where(lower, f, flo)
            fhi = jnp.where(lower, fhi, f)
            unsplit = jnp.where(is_tie & (phi - plo > 1), 1, 0)
            return dict(active=jnp.max(unsplit), bisect=(st["bisect"] + 1) % 3,
                        plo=plo, phi=phi, flo=flo, fhi=fhi)

        split = lax.while_loop(tie_cond, tie_body, dict(
            active=jnp.int32(1), bisect=jnp.int32(0),
            plo=jnp.zeros((1, qb), jnp.int32), phi=jnp.zeros((1, qb), jnp.int32) + n_keys,
            flo=jnp.zeros((1, qb), f32), fhi=mult))["phi"]

        def drop_body(r, carry):
            r0 = pl.multiple_of(r * kt_rows, kt_rows)
            tile = ibuf[pl.ds(r0, kt_rows), :]
            rows = r0 + lax.broadcasted_iota(jnp.int32, (kt_rows, qb), 0)
            dropped = jnp.where(tile == tv, rows, -1) >= split
            ibuf[pl.ds(r0, kt_rows), :] = jnp.where(dropped, -jnp.inf, tile)
            return carry

        lax.fori_loop(0, n_tiles, drop_body, 0)

    qpos = i * qb + lax.broadcasted_iota(jnp.int32, (1, qb), 1)
    n_adm = (qpos // CHUNK + 1) * CHUNK
    select_all = n_adm <= TOPK_MAX
    kf = float(TOPK_MAX)

    thr_ref[...] = jnp.full((1, qb), F32_LOWEST, f32)
    pending_ref[0] = jnp.int32(0)

    @pl.when(n_tiles * kt_rows > TOPK_MAX)
    def _():
        n_f = n_adm.astype(f32)
        mean = col_sum / n_f
        sigma = jnp.sqrt(jnp.maximum(col_ssq / n_f - mean * mean, 0.0))
        spacing = tab_ref[1:2, :] * sigma

        def inside(c, lo, hi):
            return (c > lo) & (c < hi)

        c0 = mean + tab_ref[0:1, :] * sigma
        c0 = jnp.where(inside(c0, col_min, col_max), c0, 0.5 * col_min + 0.5 * col_max)
        zero = jnp.zeros((1, qb), f32)
        one = jnp.ones((1, qb), f32)
        state0 = dict(it=jnp.int32(0), active=jnp.int32(1), c=c0, lo=col_min, hi=col_max,
                      glo=n_f, ghi=zero, tlo=zero, thi=zero, boost=one, side=zero,
                      thr=jnp.full((1, qb), F32_LOWEST, f32),
                      open=jnp.where(select_all, 0.0, 1.0), tie=zero, kind=zero, emin=zero)

        def search_cond(st):
            return (st["it"] < SEARCH_MAX_ITERS) & (st["active"] > 0)

        def search_body(st):
            c, lo, hi = st["c"], st["lo"], st["hi"]
            g = count(lambda tile, rows: tile >= c)
            hit = (g == kf) & (st["open"] > 0.0)
            thr = jnp.where(hit, c, st["thr"])
            still = jnp.where(hit, 0.0, st["open"])
            above = g > kf
            lo = jnp.where(above, c, lo)
            hi = jnp.where(above, hi, c)
            glo = jnp.where(above, g, st["glo"])
            ghi = jnp.where(above, st["ghi"], g)
            tlo = jnp.where(above, 1.0, st["tlo"])
            thi = jnp.where(above, st["thi"], 1.0)
            side = jnp.where(above, 1.0, -1.0)
            repeat = side == st["side"]
            boost = jnp.where(repeat, 2.0 * st["boost"], 1.0)
            mid = 0.5 * lo + 0.5 * hi
            stale = 1.0 / jnp.minimum(boost, 256.0)
            w_lo = (glo - kf - 0.5) * jnp.where(above, 1.0, stale)
            w_hi = (kf + 0.5 - ghi) * jnp.where(above, stale, 1.0)
            c_two = lo + w_lo / (w_lo + w_hi) * (hi - lo)
            step = 2.0 * boost * spacing
            c_one = jnp.where(tlo > 0.0, lo + jnp.maximum(glo - kf, 1.0) * step,
                              hi - jnp.maximum(kf - ghi, 1.0) * step)
            c_one = jnp.where(inside(c_one, lo, hi), c_one, mid)
            both = (tlo > 0.0) & (thi > 0.0)
            c_new = jnp.where(both, c_two, c_one)
            c_new = jnp.where(both & (lo < 0.0) & (hi > 0.0), 0.0, c_new)
            c_new = jnp.where((lo == 0.0) & (hi > F32_TINY), F32_TINY, c_new)
            movable = inside(c_new, lo, hi) & ~((lo == 0.0) & (hi <= F32_TINY))
            missed = (still > 0.0) & (((st["kind"] == 1.0) & above) |
                                      ((st["kind"] == 2.0) & (g < kf)))
            lo = jnp.where(missed & (st["kind"] == 2.0), st["emin"], lo)
            closed = (~movable & both & (still > 0.0)) | missed
            tie = jnp.where(closed, 1.0, st["tie"])
            still = jnp.where(closed, 0.0, still)
            searching = jnp.where(movable, still, 0.0)

            from_hi = both & (kf - ghi == 1.0)
            from_lo = both & (glo - kf == 1.0) & ~from_hi
            ready = jnp.where(searching > 0.0, jnp.where(from_hi | from_lo, 1.0, 2.0), 0.0)
            code = jnp.max(ready)
            ext_ref[...] = jnp.zeros_like(ext_ref)
            finish_ref[0] = jnp.int32(0)

            @pl.when((code == 1.0) & (st["it"] + 1 >= SEARCH_FINISH_FROM))
            def _():
                def ext_body(r, carry):
                    r0 = pl.multiple_of(r * kt_rows, kt_rows)
                    tile = ibuf[pl.ds(r0, kt_rows), :]
                    below = fold(jnp.where(tile < hi, tile, -jnp.inf), jnp.max)
                    above_lo = fold(jnp.where(tile >= lo, tile, jnp.inf), jnp.min)
                    return jnp.maximum(carry[0], below), jnp.minimum(carry[1], above_lo)
                mx8, mn8 = lax.fori_loop(0, n_tiles, ext_body,
                                         (jnp.full((SUBLANES, qb), -jnp.inf, f32),
                                          jnp.full((SUBLANES, qb), jnp.inf, f32)))
                ext_ref[0:1, :] = jnp.max(mx8, axis=0, keepdims=True)
                ext_ref[1:2, :] = jnp.min(mn8, axis=0, keepdims=True)
                finish_ref[0] = jnp.int32(1)

            finishing = finish_ref[0] > 0
            emax = ext_ref[0:1, :]
            emin = ext_ref[1:2, :]
            ebits = lax.bitcast_convert_type(emin, jnp.int32)
            next_up = lax.bitcast_convert_type(ebits + jnp.where(emin > 0.0, 1, -1), f32)
            next_up = jnp.where(emin == 0.0, F32_TINY, next_up)
            kind = jnp.where(finishing & (searching > 0.0),
                             jnp.where(from_hi, 1.0, jnp.where(from_lo, 2.0, 0.0)), 0.0)
            c_new = jnp.where(kind == 1.0, emax, jnp.where(kind == 2.0, next_up, c_new))
            return dict(it=st["it"] + 1, active=(code > 0.0).astype(jnp.int32),
                        c=c_new, lo=lo, hi=hi, glo=glo, ghi=ghi, tlo=tlo, thi=thi,
                        boost=boost, side=side, thr=thr, open=still, tie=tie,
                        kind=kind, emin=emin)

        final = lax.while_loop(search_cond, search_body, state0)
        tied = final["tie"] > 0.0
        thr_ref[...] = jnp.where(tied, final["lo"], final["thr"])
        pending_ref[0] = (jnp.max(final["open"]) > 0.0).astype(jnp.int32)

        @pl.when(jnp.max(final["tie"]) > 0.0)
        def _():
            resolve_ties(jnp.where(tied, final["lo"], jnp.nan), kf - final["ghi"],
                         final["glo"] - final["ghi"])

    @pl.when(pending_ref[0] > 0)
    def _():
        def key_to_f32(key):
            bits = jnp.where(key < 0, key ^ jnp.int32(0x7FFFFFFF), key)
            return lax.bitcast_convert_type(bits, f32)

        def bit_body(t, prefix):
            step = lax.shift_left(jnp.int32(1), jnp.int32(31) - t)
            cand = prefix + step
            cand_f = key_to_f32(cand)
            cnt = count(lambda tile, rows: tile >= cand_f)
            return jnp.where(cnt >= kf, cand, prefix)

        prefix = lax.fori_loop(0, 32, bit_body, jnp.full((1, qb), INT32_MIN, jnp.int32))
        thr = jnp.where(select_all, F32_LOWEST, key_to_f32(prefix))
        thr_ref[...] = thr

        cnt_ge = count(lambda tile, rows: tile >= thr)
        excess = jnp.where(select_all, 0.0, cnt_ge - kf)

        @pl.when(jnp.max(excess) > 0.0)
        def _():
            cnt_gt = count(lambda tile, rows: tile > thr)
            resolve_ties(jnp.where(excess > 0.0, thr, jnp.nan), kf - cnt_gt, cnt_ge - cnt_gt)

    thr = thr_ref[...]
    ones_rows = jnp.ones((2 * SUBLANES, kt_rows), bf16)

    m_ref[1] = jnp.full(m_ref.shape[1:], NEG_BIG, f32)

    def logits_stage(kt, slot):
        r0 = pl.multiple_of(kt * kt_rows, kt_rows)
        bias_ref[...] = jnp.where(ibuf[pl.ds(r0, kt_rows), :] >= thr, 0.0, NEG_BIG)
        for h in range(N_ATTN_HEADS):
            k_pair = k_ref[pl.ds(r0, kt_rows), (h // 2) * LANES:(h // 2 + 1) * LANES]
            s_ref[slot, h] = jnp.dot(k_pair, qTm[h], preferred_element_type=f32) + bias_ref[...]
            m_old = m_ref[1 - slot, h]
            m_new = jnp.maximum(m_old, jnp.max(s_ref[slot, h], axis=0, keepdims=True))
            alpha_ref[slot, h] = jnp.exp2(m_old - m_new)
            m_ref[slot, h] = m_new

    def values_stage(kt, slot):
        r0 = pl.multiple_of(kt * kt_rows, kt_rows)
        for h in range(N_ATTN_HEADS):
            p = jnp.exp2(s_ref[slot, h] - m_ref[slot, h]).astype(bf16)
            hs = slice(h * ATTN_HEAD_DIM, (h + 1) * ATTN_HEAD_DIM)
            lhs = jnp.concatenate([vT_ref[hs, pl.ds(r0, kt_rows)], ones_rows], axis=0)
            pv = jnp.dot(lhs, p, preferred_element_type=f32)
            alpha = alpha_ref[slot, h]
            acc_ref[hs, :] = alpha * acc_ref[hs, :] + pv[:ATTN_HEAD_DIM]
            l_ref[h] = alpha * l_ref[h] + pv[ATTN_HEAD_DIM:ATTN_HEAD_DIM + 1]

    def att_body(j, carry):
        kt = 2 * j
        logits_stage(kt + 1, 1)
        values_stage(kt, 0)
        logits_stage(kt + 2, 0)
        values_stage(kt + 1, 1)
        return carry

    logits_stage(0, 0)
    lax.fori_loop(0, i // 2, att_body, 0)

    @pl.when(i % 2 == 1)
    def _():
        logits_stage(i, 1)
        values_stage(i - 1, 0)
        values_stage(i, 1)

    @pl.when(i % 2 == 0)
    def _():
        values_stage(i, 0)

    for h in range(N_ATTN_HEADS):
        hs = slice(h * ATTN_HEAD_DIM, (h + 1) * ATTN_HEAD_DIM)
        acc_ref[hs, :] = acc_ref[hs, :] / l_ref[h]
    o_ref[...] = acc_ref[...].T.astype(o_ref.dtype)


def _search_hints(seq_len):
    n = (np.arange(seq_len) // CHUNK + 1) * CHUNK
    frac = np.minimum(TOPK_MAX / n, 0.5)
    nd = statistics.NormalDist()
    z_of = {f: nd.inv_cdf(1.0 - f) for f in np.unique(frac)}
    z = np.array([z_of[f] for f in frac])
    dens = n * np.exp(-0.5 * z * z) / np.sqrt(2.0 * np.pi)
    return jnp.asarray(np.stack([z, 1.0 / dens]), f32)


def _dsa(qT, iqT, iwT, z3, vT, ik3):
    b, _, s = qT.shape
    qb = DSA_BLOCK
    return pl.pallas_call(
        functools.partial(_dsa_kernel, seq_len=s),
        grid=(b, s // qb),
        in_specs=[
            pl.BlockSpec((None, ATTN_W, qb), lambda bi, i: (bi, 0, i)),
            pl.BlockSpec((None, IDX_Q_W, qb), lambda bi, i: (bi, 0, i)),
            pl.BlockSpec((None, N_IDX_HEADS, qb), lambda bi, i: (bi, 0, i)),
            pl.BlockSpec((2, qb), lambda bi, i: (0, i)),
            pl.BlockSpec((None, s, COL_TILE), lambda bi, i: (bi, 0, Z_AK)),
            pl.BlockSpec((None, ATTN_W, s), lambda bi, i: (bi, 0, 0)),
            pl.BlockSpec((None, s, IDX_HEAD_DIM), lambda bi, i: (bi, 0, 0)),
        ],
        out_specs=pl.BlockSpec((None, qb, ATTN_W), lambda bi, i: (bi, i, 0)),
        out_shape=jax.ShapeDtypeStruct((b, s, ATTN_W), bf16),
        scratch_shapes=[
            pltpu.VMEM((s, qb), f32),
            pltpu.VMEM((N_ATTN_HEADS, LANES, qb), bf16),
            pltpu.VMEM((2, N_ATTN_HEADS, qb, qb), f32),
            pltpu.VMEM((qb, qb), f32),
            pltpu.VMEM((2, N_ATTN_HEADS, 1, qb), f32),
            pltpu.VMEM((N_ATTN_HEADS, 1, qb), f32),
            pltpu.VMEM((2, N_ATTN_HEADS, 1, qb), f32),
            pltpu.VMEM((ATTN_W, qb), f32),
            pltpu.VMEM((1, qb), f32),
            pltpu.VMEM((2, qb), f32),
            pltpu.SMEM((1,), jnp.int32),
            pltpu.SMEM((1,), jnp.int32),
        ],
        compiler_params=_cparams(("parallel", "arbitrary")),
        name="dsa",
    )(qT, iqT, iwT, _search_hints(s), z3, vT, ik3)


def _gla_kernel(gin_ref, dec_ref, v_ref, r_ref, gn_ref, o_ref, state_ref):
    c = CHUNK
    n_batch = gin_ref.shape[0]

    @pl.when(pl.program_id(0) == 0)
    def _():
        state_ref[...] = jnp.zeros_like(state_ref)

    row = lax.broadcasted_iota(jnp.int32, (c, c), 0)
    col = lax.broadcasted_iota(jnp.int32, (c, c), 1)
    causal = row >= col
    lane = lax.broadcasted_iota(jnp.int32, (c, LANES), 1)
    half = (lane < GLA_HEAD_K, lane >= GLA_HEAD_K)
    contract_last = (((1,), (1,)), ((), ()))
    contract_first = (((0,), (0,)), ((), ()))
    zero = jnp.zeros((c, LANES), bf16)

    for ci, bi in [(ci, bi) for ci in range(GLA_TILE // c) for bi in range(n_batch)]:
        rows = slice(ci * c, (ci + 1) * c)
        decay = dec_ref[bi, ci:ci + 1, :]
        for pr in range(N_GLA_HEADS // 2):
            ps = slice(pr * LANES, (pr + 1) * LANES)
            part = lambda g: gin_ref[bi, rows, g * GLA_K_W + pr * LANES:
                                     g * GLA_K_W + (pr + 1) * LANES]
            q_in, q_mid, k_mid, k_out = part(0), part(1), part(2), part(3)
            st = state_ref[bi, pr]
            st_b = st.astype(bf16)
            upd = jnp.zeros_like(st)
            for hh in range(2):
                h = pr * 2 + hh
                vs = slice(h * GLA_HEAD_V, (h + 1) * GLA_HEAD_V)
                v_h = v_ref[bi, rows, vs]
                attn = lax.dot_general(jnp.where(half[hh], q_mid, zero), k_mid, contract_last,
                                       preferred_element_type=f32)
                attn = jnp.where(causal, attn, 0.0).astype(bf16)
                o = lax.dot_general(jnp.where(half[hh], q_in, zero), st_b, contract_last,
                                    preferred_element_type=f32)
                o = o + jnp.dot(attn, v_h, preferred_element_type=f32)
                upd = upd + lax.dot_general(v_h, jnp.where(half[hh], k_out, zero),
                                            contract_first, preferred_element_type=f32)
                y = _rms(o, gn_ref[...])
                r = r_ref[bi, rows, vs].astype(f32)
                o_ref[bi, rows, vs] = (y * r * _sigmoid(r)).astype(o_ref.dtype)
            state_ref[bi, pr] = decay[:, ps] * st + upd


def _gla(gin3, dec3, z3, gn):
    b, s, _ = z3.shape
    tb = GLA_TILE
    return pl.pallas_call(
        _gla_kernel,
        grid=(s // tb,),
        in_specs=[
            pl.BlockSpec((b, tb, GLA_IN_W), lambda j: (0, j, 0)),
            pl.BlockSpec((b, tb // CHUNK, GLA_K_W), lambda j: (0, j, 0)),
            pl.BlockSpec((b, tb, COL_TILE), lambda j: (0, j, Z_GV)),
            pl.BlockSpec((b, tb, COL_TILE), lambda j: (0, j, Z_GR)),
            pl.BlockSpec((1, GLA_HEAD_V), lambda j: (0, 0)),
        ],
        out_specs=pl.BlockSpec((b, tb, GLA_V_W), lambda j: (0, j, 0)),
        out_shape=jax.ShapeDtypeStruct((b, s, GLA_V_W), bf16),
        scratch_shapes=[pltpu.VMEM((b, N_GLA_HEADS // 2, GLA_HEAD_V, LANES), f32)],
        compiler_params=_cparams(("arbitrary",)),
        name="gla",
    )(gin3, dec3, z3, z3, gn)


def _tail_kernel(x_ref, oa_ref, ob_ref, ga_ref, gb_ref, wa_ref, wb_ref, wo_ref,
                 g_ref, wg_ref, wu_ref, wd_ref, p_ref, gp_ref, wpg_ref, wpp_ref, gf_ref,
                 o_ref, *, final):
    ya = jnp.dot(oa_ref[...], wa_ref[...], preferred_element_type=f32)
    yb = jnp.dot(ob_ref[...], wb_ref[...], preferred_element_type=f32)
    mg = _sigmoid(ga_ref[...].astype(f32)) * ya + _sigmoid(gb_ref[...].astype(f32)) * yb
    x = x_ref[...] + jnp.dot(mg.astype(bf16), wo_ref[...], preferred_element_type=f32)
    x = _swiglu_half_step(x, g_ref, wg_ref, wu_ref, wd_ref)
    h = _rms(x, gp_ref[...]).astype(bf16)
    gate = _sigmoid(jnp.dot(h, wpg_ref[...], preferred_element_type=f32))
    e = jnp.dot(p_ref[...].astype(bf16), wpp_ref[...], preferred_element_type=f32)
    y = x + gate * e
    if final:
        y = _rms(y, gf_ref[...])
    o_ref[...] = y


def _tail(x, oa, ob, z, wa, wb, wo, g, wg, wu, wd, p, gp, wpg, wpp, gf, final, out_shape):
    t = x.shape[0]
    tm = TOKEN_TILE
    rows = lambda width, col=0: pl.BlockSpec((tm, width), lambda i: (i, col))
    return pl.pallas_call(
        functools.partial(_tail_kernel, final=final),
        grid=(t // tm,),
        in_specs=[
            rows(D_MODEL), rows(ATTN_W), rows(GLA_V_W), rows(D_MODEL, Z_MGA), rows(D_MODEL, Z_MGB),
            _resident(wa), _resident(wb), _resident(wo),
            _resident(g), _resident(wg), _resident(wu), _resident(wd),
            rows(PLE_DIM), _resident(gp), _resident(wpg), _resident(wpp), _resident(gf),
        ],
        out_specs=_token_rows(out_shape),
        out_shape=jax.ShapeDtypeStruct(out_shape, f32),
        compiler_params=_cparams(("parallel",)),
        name="tail",
    )(x, oa, ob, z, z, wa, wb, wo, g, wg, wu, wd, p, gp, wpg, wpp, gf)


def _split_w_in(w_in):
    cuts = np.cumsum(np.array(SPLIT_SIZES))[:-1].tolist()
    aq, ak, av, iq, ik, iw, gq, gk, gv, gr, ga, mga, mgb = jnp.split(w_in, cuts, axis=-1)
    att_scale = ATTN_HEAD_DIM ** -0.5 * float(np.log2(np.e))
    idx_scale = IDX_HEAD_DIM ** -0.5 * N_IDX_HEADS ** -0.5
    c = lambda w: w.astype(bf16)
    main = jnp.concatenate([c(mga), c(mgb), c(ak), c(gv), c(gr)], axis=-1)
    feature_major = jnp.concatenate([c(aq * att_scale), c(av), c(iq)], axis=-1)
    gla_qk = jnp.concatenate([c(gq * (GLA_HEAD_K ** -0.5)), c(gk)], axis=-1)
    zeros = lambda n: jnp.zeros(w_in.shape[:-1] + (n,), bf16)
    small = jnp.concatenate(
        [c(ik), zeros(LANES - IDX_HEAD_DIM), c(iw * idx_scale), c(ga),
         zeros(LANES - N_IDX_HEADS - GLA_GATE_RANK)], axis=-1)
    return main, feature_major, gla_qk, small


def kernel(x, p, w_in, gla_gate_w2, gla_gate_b, gla_norm, w_branch_a, w_branch_b, w_out,
           norm_ff1, norm_mix, norm_ff2, norm_ple, ff1_w_gate, ff1_w_up, ff1_w_down,
           ff2_w_gate, ff2_w_up, ff2_w_down, ple_w_proj, ple_w_gate, norm_final):
    b, s, d = x.shape
    t = b * s
    depth = w_in.shape[0]
    assert d == D_MODEL and w_in.shape[1:] == (D_MODEL, sum(SPLIT_SIZES))
    assert s % TOKEN_TILE == 0 and s % GLA_TILE == 0 and s % DSA_BLOCK == 0
    assert ff1_w_gate.shape[1:] == (D_MODEL, D_FF) and D_FF % FF_TILE == 0
    w_main, w_fm, w_gqk, w_small = _split_w_in(w_in)
    cast = lambda w: w.astype(bf16)
    ff1 = (cast(ff1_w_gate), cast(ff1_w_up), cast(ff1_w_down))
    ff2 = (cast(ff2_w_gate), cast(ff2_w_up), cast(ff2_w_down))
    wa, wb, wo = cast(w_branch_a), cast(w_branch_b), cast(w_out)
    wpg, wpp = cast(ple_w_gate), cast(ple_w_proj)
    row = lambda v: v.reshape(1, -1)

    xf = x
    for l in range(depth):
        last = l == depth - 1
        xf = _ffn(xf, row(norm_ff1[l]), ff1[0][l], ff1[1][l], ff1[2][l])

        z, qT, vT, iqT, ik, iwT, gin, dec = _proj(
            xf, row(norm_mix[l]), w_main[l], w_fm[l], w_gqk[l], w_small[l],
            gla_gate_w2[l], row(gla_gate_b[l]), b)
        z3 = z.reshape(b, s, Z_WIDTH)
        oa = _dsa(qT, iqT, iwT, z3, vT, ik.reshape(b, s, IDX_HEAD_DIM))
        ob = _gla(gin.reshape(b, s, GLA_IN_W), dec.reshape(b, s // CHUNK, GLA_K_W), z3,
                  row(gla_norm[l]))
        xf = _tail(xf, oa.reshape(t, ATTN_W), ob.reshape(t, GLA_V_W), z, wa[l], wb[l], wo[l],
                   row(norm_ff2[l]), ff2[0][l], ff2[1][l], ff2[2][l],
                   p[l].reshape(t, PLE_DIM), row(norm_ple[l]), wpg[l], wpp[l],
                   row(norm_final), final=last, out_shape=(b, s, d) if last else (t, d))
    return xf
```

```python
import functools
import statistics

import jax
import jax.numpy as jnp
import numpy as np
from jax import lax
from jax.experimental import pallas as pl
from jax.experimental.pallas import tpu as pltpu

D_MODEL = 1024
D_FF = 2816
PLE_DIM = 256
EPS = 1e-6

CHUNK = 64
N_ATTN_HEADS = 8
ATTN_HEAD_DIM = 64
N_IDX_HEADS = 8
IDX_HEAD_DIM = 64
TOPK_MAX = 256
N_GLA_HEADS = 4
GLA_HEAD_K = 64
GLA_HEAD_V = 128
GLA_GATE_RANK = 16
GLA_GATE_TAU = 16.0

ATTN_W = N_ATTN_HEADS * ATTN_HEAD_DIM
IDX_Q_W = N_IDX_HEADS * IDX_HEAD_DIM
GLA_K_W = N_GLA_HEADS * GLA_HEAD_K
GLA_V_W = N_GLA_HEADS * GLA_HEAD_V
SPLIT_SIZES = (ATTN_W, ATTN_W, ATTN_W, IDX_Q_W, IDX_HEAD_DIM, N_IDX_HEADS,
               GLA_K_W, GLA_K_W, GLA_V_W, GLA_V_W, GLA_GATE_RANK, D_MODEL, D_MODEL)

LANES = 128
SUBLANES = 8
V7X_VMEM_BYTES = 64 * 1024 * 1024
VMEM_LIMIT_BYTES = V7X_VMEM_BYTES // 8 * 7

COL_TILE = 512
Z_MGA, Z_MGB = 0, 1
Z_AK, Z_GV, Z_GR = 4, 5, 6
Z_TILES = 7
Z_WIDTH = Z_TILES * COL_TILE
ZT_TILES = 3
SM_IW0 = 0
SM_GA0 = N_IDX_HEADS

TOKEN_TILE = 512
FF_TILE = 256
DSA_BLOCK = 256
GLA_TILE = 512
GLA_IN_W = 4 * GLA_K_W
SEARCH_MAX_ITERS = 24
SEARCH_FINISH_FROM = 4

NEG_BIG = -1e30
F32_LOWEST = float(np.finfo(np.float32).min)
F32_TINY = float(np.finfo(np.float32).tiny)
INT32_MIN = int(np.iinfo(np.int32).min)

f32 = jnp.float32
bf16 = jnp.bfloat16


def _rms(x, g):
    return x * lax.rsqrt(jnp.mean(x * x, axis=-1, keepdims=True) + EPS) * g


def _sigmoid(x):
    return 1.0 / (1.0 + jnp.exp(-x))


def _cparams(sem):
    return pltpu.CompilerParams(dimension_semantics=sem, vmem_limit_bytes=VMEM_LIMIT_BYTES)


def _resident(a):
    return pl.BlockSpec(a.shape, lambda i: (0, 0), pipeline_mode=pl.Buffered(1))


def _swiglu_half_step(x, g_ref, wg_ref, wu_ref, wd_ref):
    h = _rms(x, g_ref[...]).astype(bf16)
    y = x
    for f in range(D_FF // FF_TILE):
        cols = slice(f * FF_TILE, (f + 1) * FF_TILE)
        gate = jnp.dot(h, wg_ref[:, cols], preferred_element_type=f32)
        up = jnp.dot(h, wu_ref[:, cols], preferred_element_type=f32)
        a = (gate * _sigmoid(gate) * up).astype(bf16)
        y = y + 0.5 * jnp.dot(a, wd_ref[cols, :], preferred_element_type=f32)
    return y


def _ffn_kernel(x_ref, g_ref, wg_ref, wu_ref, wd_ref, o_ref):
    o_ref[...] = _swiglu_half_step(x_ref[...], g_ref, wg_ref, wu_ref, wd_ref)


def _token_rows(shape):
    if len(shape) == 2:
        return pl.BlockSpec((TOKEN_TILE, shape[1]), lambda i: (i, 0))
    per_b = shape[1] // TOKEN_TILE
    return pl.BlockSpec((None, TOKEN_TILE, shape[2]), lambda i: (i // per_b, i % per_b, 0))


def _ffn(x, g, wg, wu, wd):
    t = x.size // D_MODEL
    return pl.pallas_call(
        _ffn_kernel,
        grid=(t // TOKEN_TILE,),
        in_specs=[
            _token_rows(x.shape),
            _resident(g), _resident(wg), _resident(wu), _resident(wd),
        ],
        out_specs=pl.BlockSpec((TOKEN_TILE, D_MODEL), lambda i: (i, 0)),
        out_shape=jax.ShapeDtypeStruct((t, D_MODEL), f32),
        compiler_params=_cparams(("parallel",)),
        name="ffn",
    )(x, g, wg, wu, wd)


def _gla_operands(q, k, ga, w2, gb):
    n = q.shape[0]
    c = CHUNK
    def split(a):
        hi = a.astype(bf16)
        return hi, (a - hi.astype(f32)).astype(bf16)

    dot = lambda a, b: jnp.dot(a, b, preferred_element_type=f32)
    ga_hi, ga_lo = split(ga)
    w2_hi, w2_lo = split(w2)
    glin = dot(ga_hi, w2_hi) + dot(ga_hi, w2_lo) + dot(ga_lo, w2_hi) + gb
    logg = (jnp.minimum(glin, 0.0) - jnp.log(1.0 + jnp.exp(-jnp.abs(glin)))) / GLA_GATE_TAU
    row = lax.broadcasted_iota(jnp.int32, (n, n), 0)
    col = lax.broadcasted_iota(jnp.int32, (n, n), 1)
    tril = jnp.where((row // c == col // c) & (col <= row), 1.0, 0.0).astype(bf16)
    hi, lo = split(logg)
    bcum = dot(tril, hi) + dot(tril, lo)
    by_chunk = bcum.reshape(n // c, c, GLA_K_W)
    spread = lambda r: jnp.broadcast_to(r, (n // c, c, GLA_K_W)).reshape(n, GLA_K_W)
    b_last = by_chunk[:, c - 1:c, :]
    b_mid = spread(by_chunk[:, c // 2 - 1:c // 2, :])
    operands = jnp.concatenate(
        [q * jnp.exp(bcum), q * jnp.exp(bcum - b_mid), k * jnp.exp(b_mid - bcum),
         k * jnp.exp(spread(b_last) - bcum)], axis=1)
    return operands, jnp.exp(b_last.reshape(n // c, GLA_K_W))


def _proj_kernel(x_ref, g_ref, w_ref, wt_ref, wg_ref, ws_ref, w2_ref, gb_ref,
                 z_ref, qT_ref, vT_ref, iqT_ref, ik_ref, iwT_ref, gin_ref, dec_ref):
    h = _rms(x_ref[...], g_ref[...]).astype(bf16)
    for j in range(Z_TILES):
        cols = slice(j * COL_TILE, (j + 1) * COL_TILE)
        z_ref[:, cols] = jnp.dot(h, w_ref[:, cols], preferred_element_type=f32).astype(bf16)
    for j, out_ref in enumerate((qT_ref, vT_ref, iqT_ref)):
        cols = slice(j * COL_TILE, (j + 1) * COL_TILE)
        zt = jnp.dot(h, wt_ref[:, cols], preferred_element_type=f32)
        out_ref[...] = zt.T.astype(bf16)
    s = jnp.dot(h, ws_ref[...], preferred_element_type=f32)
    ik_ref[...] = s[:, :IDX_HEAD_DIM].astype(bf16)
    sm = s[:, LANES:]
    iwT_ref[...] = sm.T[SM_IW0:SM_IW0 + N_IDX_HEADS, :]
    gqk = jnp.dot(h, wg_ref[...], preferred_element_type=f32)
    operands, decay = _gla_operands(gqk[:, :GLA_K_W], gqk[:, GLA_K_W:],
                                    sm[:, SM_GA0:SM_GA0 + GLA_GATE_RANK],
                                    w2_ref[...], gb_ref[...])
    gin_ref[...] = operands.astype(bf16)
    dec_ref[...] = decay


def _proj(x, g, w_main, w_t, w_g, w_small, w2, gb, batch):
    t = x.shape[0]
    tm = TOKEN_TILE
    s = t // batch
    per_b = s // tm
    fm = lambda rows: pl.BlockSpec((None, rows, tm), lambda i: (i // per_b, 0, i % per_b))
    return pl.pallas_call(
        _proj_kernel,
        grid=(t // tm,),
        in_specs=[
            pl.BlockSpec((tm, D_MODEL), lambda i: (i, 0)),
            _resident(g), _resident(w_main), _resident(w_t), _resident(w_g), _resident(w_small),
            _resident(w2), _resident(gb),
        ],
        out_specs=[
            pl.BlockSpec((tm, Z_WIDTH), lambda i: (i, 0)),
            fm(ATTN_W), fm(ATTN_W), fm(IDX_Q_W),
            pl.BlockSpec((tm, IDX_HEAD_DIM), lambda i: (i, 0)),
            fm(N_IDX_HEADS),
            pl.BlockSpec((tm, GLA_IN_W), lambda i: (i, 0)),
            pl.BlockSpec((tm // CHUNK, GLA_K_W), lambda i: (i, 0)),
        ],
        out_shape=[
            jax.ShapeDtypeStruct((t, Z_WIDTH), bf16),
            jax.ShapeDtypeStruct((batch, ATTN_W, s), bf16),
            jax.ShapeDtypeStruct((batch, ATTN_W, s), bf16),
            jax.ShapeDtypeStruct((batch, IDX_Q_W, s), bf16),
            jax.ShapeDtypeStruct((t, IDX_HEAD_DIM), bf16),
            jax.ShapeDtypeStruct((batch, N_IDX_HEADS, s), f32),
            jax.ShapeDtypeStruct((t, GLA_IN_W), bf16),
            jax.ShapeDtypeStruct((t // CHUNK, GLA_K_W), f32),
        ],
        compiler_params=_cparams(("parallel",)),
        name="proj",
    )(x, g, w_main, w_t, w_g, w_small, w2, gb)


def _dsa_kernel(qT_ref, iqT_ref, iwT_ref, tab_ref, k_ref, vT_ref, ik_ref, o_ref,
                ibuf, qTm, s_ref, bias_ref, m_ref, l_ref, alpha_ref, acc_ref, thr_ref,
                ext_ref, pending_ref, finish_ref, *, seq_len):
    qb = DSA_BLOCK
    kt_rows = DSA_BLOCK
    i = pl.program_id(1)
    n_tiles = i + 1

    rowid = lax.broadcasted_iota(jnp.int32, (LANES, qb), 0)
    for h in range(N_ATTN_HEADS):
        pair = qT_ref[(h // 2) * LANES:(h // 2 + 1) * LANES, :]
        keep = (rowid < ATTN_HEAD_DIM) if h % 2 == 0 else (rowid >= ATTN_HEAD_DIM)
        qTm[h] = jnp.where(keep, pair, jnp.zeros_like(pair))
    l_ref[...] = jnp.zeros_like(l_ref)
    acc_ref[...] = jnp.zeros_like(acc_ref)

    def fold(x, op):
        return op(x.reshape(kt_rows // SUBLANES, SUBLANES, qb), axis=0)

    def idx_tile(kt, diagonal, stats):
        r0 = pl.multiple_of(kt * kt_rows, kt_rows)
        ki_t = ik_ref[pl.ds(r0, kt_rows), :]
        sc = jnp.zeros((kt_rows, qb), f32)
        for h in range(N_IDX_HEADS):
            s = jnp.dot(ki_t, iqT_ref[h * IDX_HEAD_DIM:(h + 1) * IDX_HEAD_DIM, :],
                        preferred_element_type=f32)
            sc = sc + iwT_ref[h:h + 1, :] * jnp.maximum(s, 0.0)
        sc_hi = sc_lo = sc_0 = sc
        if diagonal:
            kc = lax.broadcasted_iota(jnp.int32, (kt_rows, qb), 0) // CHUNK
            qc = lax.broadcasted_iota(jnp.int32, (kt_rows, qb), 1) // CHUNK
            adm = kc <= qc
            sc_hi = jnp.where(adm, sc, -jnp.inf)
            sc_lo = jnp.where(adm, sc, jnp.inf)
            sc_0 = jnp.where(adm, sc, 0.0)
        ibuf[pl.ds(r0, kt_rows), :] = sc_hi
        mx8, mn8, s8, ss8 = stats
        return (jnp.maximum(mx8, fold(sc_hi, jnp.max)), jnp.minimum(mn8, fold(sc_lo, jnp.min)),
                s8 + fold(sc_0, jnp.sum), ss8 + fold(sc_0 * sc_0, jnp.sum))

    stats0 = (jnp.full((SUBLANES, qb), -jnp.inf, f32), jnp.full((SUBLANES, qb), jnp.inf, f32),
              jnp.zeros((SUBLANES, qb), f32), jnp.zeros((SUBLANES, qb), f32))
    odd = i % 2
    stats = lax.cond(odd == 1, lambda st: idx_tile(0, False, st), lambda st: st, stats0)
    stats = lax.fori_loop(
        0, i // 2,
        lambda j, st: idx_tile(odd + 2 * j + 1, False, idx_tile(odd + 2 * j, False, st)),
        stats)
    mx8, mn8, s8, ss8 = idx_tile(i, True, stats)
    col_max = jnp.max(mx8, axis=0, keepdims=True)
    col_min = jnp.min(mn8, axis=0, keepdims=True)
    col_sum = jnp.sum(s8, axis=0, keepdims=True)
    col_ssq = jnp.sum(ss8, axis=0, keepdims=True)

    def count(pred):
        def body(r, acc):
            r0 = pl.multiple_of(r * kt_rows, kt_rows)
            tile = ibuf[pl.ds(r0, kt_rows), :]
            rows = r0 + lax.broadcasted_iota(jnp.int32, (kt_rows, qb), 0)
            ind = jnp.where(pred(tile, rows), 1.0, 0.0)
            part = ind.reshape(kt_rows // SUBLANES, SUBLANES, qb)
            while part.shape[0] > 1:
                half = part.shape[0] // 2
                part = part[:half] + part[half:]
            return acc + part[0]
        acc = jnp.zeros((SUBLANES, qb), f32)
        acc = lax.cond(i % 2 == 0, lambda a: body(0, a), lambda a: a, acc)
        first = 1 - i % 2
        acc = lax.fori_loop(
            0, (i + 1) // 2,
            lambda j, a: body(first + 2 * j + 1, body(first + 2 * j, a)), acc)
        return jnp.sum(acc, axis=0, keepdims=True)

    def resolve_ties(tv, need, mult):
        is_tie = tv < jnp.inf
        n_keys = n_tiles * kt_rows

        def tie_cond(st):
            return st["active"] > 0

        def tie_body(st):
            plo, phi, flo, fhi = st["plo"], st["phi"], st["flo"], st["fhi"]
            span = phi - plo
            est = ((need - flo) / jnp.maximum(fhi - flo, 1.0) * span.astype(f32)).astype(jnp.int32)
            pick = jnp.where(st["bisect"] == 2, span // 2, est)
            cand = plo + jnp.clip(pick, 1, jnp.maximum(span - 1, 1))
            f = count(lambda tile, rows: jnp.where(tile == tv, rows, seq_len) < cand)
            lower = f < need
            exact = f == need
            plo = jnp.where(lower, cand, jnp.where(exact, cand - 1, plo))
            phi = jnp.where(lower, phi, cand)
            flo = jnp.where(lower, f, flo)
            fhi = jnp.where(lower, fhi, f)
            unsplit = jnp.where(is_tie & (phi - plo > 1), 1, 0)
            return dict(active=jnp.max(unsplit), bisect=(st["bisect"] + 1) % 3,
                        plo=plo, phi=phi, flo=flo, fhi=fhi)

        split = lax.while_loop(tie_cond, tie_body, dict(
            active=jnp.int32(1), bisect=jnp.int32(0),
            plo=jnp.zeros((1, qb), jnp.int32), phi=jnp.zeros((1, qb), jnp.int32) + n_keys,
            flo=jnp.zeros((1, qb), f32), fhi=mult))["phi"]

        def drop_body(r, carry):
            r0 = pl.multiple_of(r * kt_rows, kt_rows)
            tile = ibuf[pl.ds(r0, kt_rows), :]
            rows = r0 + lax.broadcasted_iota(jnp.int32, (kt_rows, qb), 0)
            dropped = jnp.where(tile == tv, rows, -1) >= split
            ibuf[pl.ds(r0, kt_rows), :] = jnp.where(dropped, -jnp.inf, tile)
            return carry

        lax.fori_loop(0, n_tiles, drop_body, 0)

    qpos = i * qb + lax.broadcasted_iota(jnp.int32, (1, qb), 1)
    n_adm = (qpos // CHUNK + 1) * CHUNK
    select_all = n_adm <= TOPK_MAX
    kf = float(TOPK_MAX)

    thr_ref[...] = jnp.full((1, qb), F32_LOWEST, f32)
    pending_ref[0] = jnp.int32(0)

    @pl.when(n_tiles * kt_rows > TOPK_MAX)
    def _():
        n_f = n_adm.astype(f32)
        mean = col_sum / n_f
        sigma = jnp.sqrt(jnp.maximum(col_ssq / n_f - mean * mean, 0.0))
        spacing = tab_ref[1:2, :] * sigma

        def inside(c, lo, hi):
            return (c > lo) & (c < hi)

        c0 = mean + tab_ref[0:1, :] * sigma
        c0 = jnp.where(inside(c0, col_min, col_max), c0, 0.5 * col_min + 0.5 * col_max)
        zero = jnp.zeros((1, qb), f32)
        one = jnp.ones((1, qb), f32)
        state0 = dict(it=jnp.int32(0), active=jnp.int32(1), c=c0, lo=col_min, hi=col_max,
                      glo=n_f, ghi=zero, tlo=zero, thi=zero, boost=one, side=zero,
                      thr=jnp.full((1, qb), F32_LOWEST, f32),
                      open=jnp.where(select_all, 0.0, 1.0), tie=zero, kind=zero, emin=zero)

        def search_cond(st):
            return (st["it"] < SEARCH_MAX_ITERS) & (st["active"] > 0)

        def search_body(st):
            c, lo, hi = st["c"], st["lo"], st["hi"]
            g = count(lambda tile, rows: tile >= c)
            hit = (g == kf) & (st["open"] > 0.0)
            thr = jnp.where(hit, c, st["thr"])
            still = jnp.where(hit, 0.0, st["open"])
            above = g > kf
            lo = jnp.where(above, c, lo)
            hi = jnp.where(above, hi, c)
            glo = jnp.where(above, g, st["glo"])
            ghi = jnp.where(above, st["ghi"], g)
            tlo = jnp.where(above, 1.0, st["tlo"])
            thi = jnp.where(above, st["thi"], 1.0)
            side = jnp.where(above, 1.0, -1.0)
            repeat = side == st["side"]
            boost = jnp.where(repeat, 2.0 * st["boost"], 1.0)
            mid = 0.5 * lo + 0.5 * hi
            stale = 1.0 / jnp.minimum(boost, 256.0)
            w_lo = (glo - kf - 0.5) * jnp.where(above, 1.0, stale)
            w_hi = (kf + 0.5 - ghi) * jnp.where(above, stale, 1.0)
            c_two = lo + w_lo / (w_lo + w_hi) * (hi - lo)
            step = 2.0 * boost * spacing
            c_one = jnp.where(tlo > 0.0, lo + jnp.maximum(glo - kf, 1.0) * step,
                              hi - jnp.maximum(kf - ghi, 1.0) * step)
            c_one = jnp.where(inside(c_one, lo, hi), c_one, mid)
            both = (tlo > 0.0) & (thi > 0.0)
            c_new = jnp.where(both, c_two, c_one)
            c_new = jnp.where(both & (lo < 0.0) & (hi > 0.0), 0.0, c_new)
            c_new = jnp.where((lo == 0.0) & (hi > F32_TINY), F32_TINY, c_new)
            movable = inside(c_new, lo, hi) & ~((lo == 0.0) & (hi <= F32_TINY))
            missed = (still > 0.0) & (((st["kind"] == 1.0) & above) |
                                      ((st["kind"] == 2.0) & (g < kf)))
            lo = jnp.where(missed & (st["kind"] == 2.0), st["emin"], lo)
            closed = (~movable & both & (still > 0.0)) | missed
            tie = jnp.where(closed, 1.0, st["tie"])
            still = jnp.where(closed, 0.0, still)
            searching = jnp.where(movable, still, 0.0)

            from_hi = both & (kf - ghi == 1.0)
            from_lo = both & (glo - kf == 1.0) & ~from_hi
            ready = jnp.where(searching > 0.0, jnp.where(from_hi | from_lo, 1.0, 2.0), 0.0)
            code = jnp.max(ready)
            ext_ref[...] = jnp.zeros_like(ext_ref)
            finish_ref[0] = jnp.int32(0)

            @pl.when((code == 1.0) & (st["it"] + 1 >= SEARCH_FINISH_FROM))
            def _():
                def ext_body(r, carry):
                    r0 = pl.multiple_of(r * kt_rows, kt_rows)
                    tile = ibuf[pl.ds(r0, kt_rows), :]
                    below = fold(jnp.where(tile < hi, tile, -jnp.inf), jnp.max)
                    above_lo = fold(jnp.where(tile >= lo, tile, jnp.inf), jnp.min)
                    return jnp.maximum(carry[0], below), jnp.minimum(carry[1], above_lo)
                mx8, mn8 = lax.fori_loop(0, n_tiles, ext_body,
                                         (jnp.full((SUBLANES, qb), -jnp.inf, f32),
                                          jnp.full((SUBLANES, qb), jnp.inf, f32)))
                ext_ref[0:1, :] = jnp.max(mx8, axis=0, keepdims=True)
                ext_ref[1:2, :] = jnp.min(mn8, axis=0, keepdims=True)
                finish_ref[0] = jnp.int32(1)

            finishing = finish_ref[0] > 0
            emax = ext_ref[0:1, :]
            emin = ext_ref[1:2, :]
            ebits = lax.bitcast_convert_type(emin, jnp.int32)
            next_up = lax.bitcast_convert_type(ebits + jnp.where(emin > 0.0, 1, -1), f32)
            next_up = jnp.where(emin == 0.0, F32_TINY, next_up)
            kind = jnp.where(finishing & (searching > 0.0),
                             jnp.where(from_hi, 1.0, jnp.where(from_lo, 2.0, 0.0)), 0.0)
            c_new = jnp.where(kind == 1.0, emax, jnp.where(kind == 2.0, next_up, c_new))
            return dict(it=st["it"] + 1, active=(code > 0.0).astype(jnp.int32),
                        c=c_new, lo=lo, hi=hi, glo=glo, ghi=ghi, tlo=tlo, thi=thi,
                        boost=boost, side=side, thr=thr, open=still, tie=tie,
                        kind=kind, emin=emin)

        final = lax.while_loop(search_cond, search_body, state0)
        tied = final["tie"] > 0.0
        thr_ref[...] = jnp.where(tied, final["lo"], final["thr"])
        pending_ref[0] = (jnp.max(final["open"]) > 0.0).astype(jnp.int32)

        @pl.when(jnp.max(final["tie"]) > 0.0)
        def _():
            resolve_ties(jnp.where(tied, final["lo"], jnp.inf), kf - final["ghi"],
                         final["glo"] - final["ghi"])

    @pl.when(pending_ref[0] > 0)
    def _():
        def key_to_f32(key):
            bits = jnp.where(key < 0, key ^ jnp.int32(0x7FFFFFFF), key)
            return lax.bitcast_convert_type(bits, f32)

        def bit_body(t, prefix):
            step = lax.shift_left(jnp.int32(1), jnp.int32(31) - t)
            cand = prefix + step
            cand_f = key_to_f32(cand)
            cnt = count(lambda tile, rows: tile >= cand_f)
            return jnp.where(cnt >= kf, cand, prefix)

        prefix = lax.fori_loop(0, 32, bit_body, jnp.full((1, qb), INT32_MIN, jnp.int32))
        thr = jnp.where(select_all, F32_LOWEST, key_to_f32(prefix))
        thr_ref[...] = thr

        cnt_ge = count(lambda tile, rows: tile >= thr)
        excess = jnp.where(select_all, 0.0, cnt_ge - kf)

        @pl.when(jnp.max(excess) > 0.0)
        def _():
            cnt_gt = count(lambda tile, rows: tile > thr)
            resolve_ties(jnp.where(excess > 0.0, thr, jnp.inf), kf - cnt_gt, cnt_ge - cnt_gt)

    thr = thr_ref[...]
    ones_rows = jnp.ones((2 * SUBLANES, kt_rows), bf16)

    m_ref[1] = jnp.full(m_ref.shape[1:], NEG_BIG, f32)

    def logits_stage(kt, slot):
        r0 = pl.multiple_of(kt * kt_rows, kt_rows)
        bias_ref[...] = jnp.where(ibuf[pl.ds(r0, kt_rows), :] >= thr, 0.0, NEG_BIG)
        for h in range(N_ATTN_HEADS):
            k_pair = k_ref[pl.ds(r0, kt_rows), (h // 2) * LANES:(h // 2 + 1) * LANES]
            s_ref[slot, h] = jnp.dot(k_pair, qTm[h], preferred_element_type=f32) + bias_ref[...]
            m_old = m_ref[1 - slot, h]
            m_new = jnp.maximum(m_old, jnp.max(s_ref[slot, h], axis=0, keepdims=True))
            alpha_ref[slot, h] = jnp.exp2(m_old - m_new)
            m_ref[slot, h] = m_new

    def values_stage(kt, slot):
        r0 = pl.multiple_of(kt * kt_rows, kt_rows)
        for h in range(N_ATTN_HEADS):
            p = jnp.exp2(s_ref[slot, h] - m_ref[slot, h]).astype(bf16)
            hs = slice(h * ATTN_HEAD_DIM, (h + 1) * ATTN_HEAD_DIM)
            lhs = jnp.concatenate([vT_ref[hs, pl.ds(r0, kt_rows)], ones_rows], axis=0)
            pv = jnp.dot(lhs, p, preferred_element_type=f32)
            alpha = alpha_ref[slot, h]
            acc_ref[hs, :] = alpha * acc_ref[hs, :] + pv[:ATTN_HEAD_DIM]
            l_ref[h] = alpha * l_ref[h] + pv[ATTN_HEAD_DIM:ATTN_HEAD_DIM + 1]

    def att_body(j, carry):
        kt = 2 * j
        logits_stage(kt + 1, 1)
        values_stage(kt, 0)
        logits_stage(kt + 2, 0)
        values_stage(kt + 1, 1)
        return carry

    logits_stage(0, 0)
    lax.fori_loop(0, i // 2, att_body, 0)

    @pl.when(i % 2 == 1)
    def _():
        logits_stage(i, 1)
        values_stage(i - 1, 0)
        values_stage(i, 1)

    @pl.when(i % 2 == 0)
    def _():
        values_stage(i, 0)

    for h in range(N_ATTN_HEADS):
        hs = slice(h * ATTN_HEAD_DIM, (h + 1) * ATTN_HEAD_DIM)
        acc_ref[hs, :] = acc_ref[hs, :] / l_ref[h]
    o_ref[...] = acc_ref[...].T.astype(o_ref.dtype)


def _search_hints(seq_len):
    n = (np.arange(seq_len) // CHUNK + 1) * CHUNK
    frac = np.minimum(TOPK_MAX / n, 0.5)
    nd = statistics.NormalDist()
    z_of = {f: nd.inv_cdf(1.0 - f) for f in np.unique(frac)}
    z = np.array([z_of[f] for f in frac])
    dens = n * np.exp(-0.5 * z * z) / np.sqrt(2.0 * np.pi)
    return jnp.asarray(np.stack([z, 1.0 / dens]), f32)


def _dsa(qT, iqT, iwT, z3, vT, ik3):
    b, _, s = qT.shape
    qb = DSA_BLOCK
    return pl.pallas_call(
        functools.partial(_dsa_kernel, seq_len=s),
        grid=(b, s // qb),
        in_specs=[
            pl.BlockSpec((None, ATTN_W, qb), lambda bi, i: (bi, 0, i)),
            pl.BlockSpec((None, IDX_Q_W, qb), lambda bi, i: (bi, 0, i)),
            pl.BlockSpec((None, N_IDX_HEADS, qb), lambda bi, i: (bi, 0, i)),
            pl.BlockSpec((2, qb), lambda bi, i: (0, i)),
            pl.BlockSpec((None, s, COL_TILE), lambda bi, i: (bi, 0, Z_AK)),
            pl.BlockSpec((None, ATTN_W, s), lambda bi, i: (bi, 0, 0)),
            pl.BlockSpec((None, s, IDX_HEAD_DIM), lambda bi, i: (bi, 0, 0)),
        ],
        out_specs=pl.BlockSpec((None, qb, ATTN_W), lambda bi, i: (bi, i, 0)),
        out_shape=jax.ShapeDtypeStruct((b, s, ATTN_W), bf16),
        scratch_shapes=[
            pltpu.VMEM((s, qb), f32),
            pltpu.VMEM((N_ATTN_HEADS, LANES, qb), bf16),
            pltpu.VMEM((2, N_ATTN_HEADS, qb, qb), f32),
            pltpu.VMEM((qb, qb), f32),
            pltpu.VMEM((2, N_ATTN_HEADS, 1, qb), f32),
            pltpu.VMEM((N_ATTN_HEADS, 1, qb), f32),
            pltpu.VMEM((2, N_ATTN_HEADS, 1, qb), f32),
            pltpu.VMEM((ATTN_W, qb), f32),
            pltpu.VMEM((1, qb), f32),
            pltpu.VMEM((2, qb), f32),
            pltpu.SMEM((1,), jnp.int32),
            pltpu.SMEM((1,), jnp.int32),
        ],
        compiler_params=_cparams(("parallel", "arbitrary")),
        name="dsa",
    )(qT, iqT, iwT, _search_hints(s), z3, vT, ik3)


def _gla_kernel(gin_ref, dec_ref, v_ref, r_ref, gn_ref, o_ref, state_ref):
    c = CHUNK
    n_batch = gin_ref.shape[0]

    @pl.when(pl.program_id(0) == 0)
    def _():
        state_ref[...] = jnp.zeros_like(state_ref)

    row = lax.broadcasted_iota(jnp.int32, (c, c), 0)
    col = lax.broadcasted_iota(jnp.int32, (c, c), 1)
    causal = row >= col
    lane = lax.broadcasted_iota(jnp.int32, (c, LANES), 1)
    half = (lane < GLA_HEAD_K, lane >= GLA_HEAD_K)
    contract_last = (((1,), (1,)), ((), ()))
    contract_first = (((0,), (0,)), ((), ()))
    zero = jnp.zeros((c, LANES), bf16)

    for ci, bi in [(ci, bi) for ci in range(GLA_TILE // c) for bi in range(n_batch)]:
        rows = slice(ci * c, (ci + 1) * c)
        decay = dec_ref[bi, ci:ci + 1, :]
        for pr in range(N_GLA_HEADS // 2):
            ps = slice(pr * LANES, (pr + 1) * LANES)
            part = lambda g: gin_ref[bi, rows, g * GLA_K_W + pr * LANES:
                                     g * GLA_K_W + (pr + 1) * LANES]
            q_in, q_mid, k_mid, k_out = part(0), part(1), part(2), part(3)
            st = state_ref[bi, pr]
            st_b = st.astype(bf16)
            upd = jnp.zeros_like(st)
            for hh in range(2):
                h = pr * 2 + hh
                vs = slice(h * GLA_HEAD_V, (h + 1) * GLA_HEAD_V)
                v_h = v_ref[bi, rows, vs]
                attn = lax.dot_general(jnp.where(half[hh], q_mid, zero), k_mid, contract_last,
                                       preferred_element_type=f32)
                attn = jnp.where(causal, attn, 0.0).astype(bf16)
                o = lax.dot_general(jnp.where(half[hh], q_in, zero), st_b, contract_last,
                                    preferred_element_type=f32)
                o = o + jnp.dot(attn, v_h, preferred_element_type=f32)
                upd = upd + lax.dot_general(v_h, jnp.where(half[hh], k_out, zero),
                                            contract_first, preferred_element_type=f32)
                y = _rms(o, gn_ref[...])
                r = r_ref[bi, rows, vs].astype(f32)
                o_ref[bi, rows, vs] = (y * r * _sigmoid(r)).astype(o_ref.dtype)
            state_ref[bi, pr] = decay[:, ps] * st + upd


def _gla(gin3, dec3, z3, gn):
    b, s, _ = z3.shape
    tb = GLA_TILE
    return pl.pallas_call(
        _gla_kernel,
        grid=(s // tb,),
        in_specs=[
            pl.BlockSpec((b, tb, GLA_IN_W), lambda j: (0, j, 0)),
            pl.BlockSpec((b, tb // CHUNK, GLA_K_W), lambda j: (0, j, 0)),
            pl.BlockSpec((b, tb, COL_TILE), lambda j: (0, j, Z_GV)),
            pl.BlockSpec((b, tb, COL_TILE), lambda j: (0, j, Z_GR)),
            pl.BlockSpec((1, GLA_HEAD_V), lambda j: (0, 0)),
        ],
        out_specs=pl.BlockSpec((b, tb, GLA_V_W), lambda j: (0, j, 0)),
        out_shape=jax.ShapeDtypeStruct((b, s, GLA_V_W), bf16),
        scratch_shapes=[pltpu.VMEM((b, N_GLA_HEADS // 2, GLA_HEAD_V, LANES), f32)],
        compiler_params=_cparams(("arbitrary",)),
        name="gla",
    )(gin3, dec3, z3, z3, gn)


def _tail_kernel(x_ref, oa_ref, ob_ref, ga_ref, gb_ref, wa_ref, wb_ref, wo_ref,
                 g_ref, wg_ref, wu_ref, wd_ref, p_ref, gp_ref, wpg_ref, wpp_ref, gf_ref,
                 o_ref, *, final):
    ya = jnp.dot(oa_ref[...], wa_ref[...], preferred_element_type=f32)
    yb = jnp.dot(ob_ref[...], wb_ref[...], preferred_element_type=f32)
    mg = _sigmoid(ga_ref[...].astype(f32)) * ya + _sigmoid(gb_ref[...].astype(f32)) * yb
    x = x_ref[...] + jnp.dot(mg.astype(bf16), wo_ref[...], preferred_element_type=f32)
    x = _swiglu_half_step(x, g_ref, wg_ref, wu_ref, wd_ref)
    h = _rms(x, gp_ref[...]).astype(bf16)
    gate = _sigmoid(jnp.dot(h, wpg_ref[...], preferred_element_type=f32))
    e = jnp.dot(p_ref[...].astype(bf16), wpp_ref[...], preferred_element_type=f32)
    y = x + gate * e
    if final:
        y = _rms(y, gf_ref[...])
    o_ref[...] = y


def _tail(x, oa, ob, z, wa, wb, wo, g, wg, wu, wd, p, gp, wpg, wpp, gf, final, out_shape):
    t = x.shape[0]
    tm = TOKEN_TILE
    rows = lambda width, col=0: pl.BlockSpec((tm, width), lambda i: (i, col))
    return pl.pallas_call(
        functools.partial(_tail_kernel, final=final),
        grid=(t // tm,),
        in_specs=[
            rows(D_MODEL), rows(ATTN_W), rows(GLA_V_W), rows(D_MODEL, Z_MGA), rows(D_MODEL, Z_MGB),
            _resident(wa), _resident(wb), _resident(wo),
            _resident(g), _resident(wg), _resident(wu), _resident(wd),
            rows(PLE_DIM), _resident(gp), _resident(wpg), _resident(wpp), _resident(gf),
        ],
        out_specs=_token_rows(out_shape),
        out_shape=jax.ShapeDtypeStruct(out_shape, f32),
        compiler_params=_cparams(("parallel",)),
        name="tail",
    )(x, oa, ob, z, z, wa, wb, wo, g, wg, wu, wd, p, gp, wpg, wpp, gf)


def _split_w_in(w_in):
    cuts = np.cumsum(np.array(SPLIT_SIZES))[:-1].tolist()
    aq, ak, av, iq, ik, iw, gq, gk, gv, gr, ga, mga, mgb = jnp.split(w_in, cuts, axis=-1)
    att_scale = ATTN_HEAD_DIM ** -0.5 * float(np.log2(np.e))
    idx_scale = IDX_HEAD_DIM ** -0.5 * N_IDX_HEADS ** -0.5
    c = lambda w: w.astype(bf16)
    main = jnp.concatenate([c(mga), c(mgb), c(ak), c(gv), c(gr)], axis=-1)
    feature_major = jnp.concatenate([c(aq * att_scale), c(av), c(iq)], axis=-1)
    gla_qk = jnp.concatenate([c(gq * (GLA_HEAD_K ** -0.5)), c(gk)], axis=-1)
    zeros = lambda n: jnp.zeros(w_in.shape[:-1] + (n,), bf16)
    small = jnp.concatenate(
        [c(ik), zeros(LANES - IDX_HEAD_DIM), c(iw * idx_scale), c(ga),
         zeros(LANES - N_IDX_HEADS - GLA_GATE_RANK)], axis=-1)
    return main, feature_major, gla_qk, small


def kernel(x, p, w_in, gla_gate_w2, gla_gate_b, gla_norm, w_branch_a, w_branch_b, w_out,
           norm_ff1, norm_mix, norm_ff2, norm_ple, ff1_w_gate, ff1_w_up, ff1_w_down,
           ff2_w_gate, ff2_w_up, ff2_w_down, ple_w_proj, ple_w_gate, norm_final):
    b, s, d = x.shape
    t = b * s
    depth = w_in.shape[0]
    assert d == D_MODEL and w_in.shape[1:] == (D_MODEL, sum(SPLIT_SIZES))
    assert s % TOKEN_TILE == 0 and s % GLA_TILE == 0 and s % DSA_BLOCK == 0
    assert ff1_w_gate.shape[1:] == (D_MODEL, D_FF) and D_FF % FF_TILE == 0
    w_main, w_fm, w_gqk, w_small = _split_w_in(w_in)
    cast = lambda w: w.astype(bf16)
    ff1 = (cast(ff1_w_gate), cast(ff1_w_up), cast(ff1_w_down))
    ff2 = (cast(ff2_w_gate), cast(ff2_w_up), cast(ff2_w_down))
    wa, wb, wo = cast(w_branch_a), cast(w_branch_b), cast(w_out)
    wpg, wpp = cast(ple_w_gate), cast(ple_w_proj)
    row = lambda v: v.reshape(1, -1)

    xf = x
    for l in range(depth):
        last = l == depth - 1
        xf = _ffn(xf, row(norm_ff1[l]), ff1[0][l], ff1[1][l], ff1[2][l])

        z, qT, vT, iqT, ik, iwT, gin, dec = _proj(
            xf, row(norm_mix[l]), w_main[l], w_fm[l], w_gqk[l], w_small[l],
            gla_gate_w2[l], row(gla_gate_b[l]), b)
        z3 = z.reshape(b, s, Z_WIDTH)
        oa = _dsa(qT, iqT, iwT, z3, vT, ik.reshape(b, s, IDX_HEAD_DIM))
        ob = _gla(gin.reshape(b, s, GLA_IN_W), dec.reshape(b, s // CHUNK, GLA_K_W), z3,
                  row(gla_norm[l]))
        xf = _tail(xf, oa.reshape(t, ATTN_W), ob.reshape(t, GLA_V_W), z, wa[l], wb[l], wo[l],
                   row(norm_ff2[l]), ff2[0][l], ff2[1][l], ff2[2][l],
                   p[l].reshape(t, PLE_DIM), row(norm_ple[l]), wpg[l], wpp[l],
                   row(norm_final), final=last, out_shape=(b, s, d) if last else (t, d))
    return xf
```

```python
import functools
import statistics

import jax
import jax.numpy as jnp
import numpy as np
from jax import lax
from jax.experimental import pallas as pl
from jax.experimental.pallas import tpu as pltpu

D_MODEL = 1024
D_FF = 2816
PLE_DIM = 256
EPS = 1e-6

CHUNK = 64
N_ATTN_HEADS = 8
ATTN_HEAD_DIM = 64
N_IDX_HEADS = 8
IDX_HEAD_DIM = 64
TOPK_MAX = 256
N_GLA_HEADS = 4
GLA_HEAD_K = 64
GLA_HEAD_V = 128
GLA_GATE_RANK = 16
GLA_GATE_TAU = 16.0

ATTN_W = N_ATTN_HEADS * ATTN_HEAD_DIM
IDX_Q_W = N_IDX_HEADS * IDX_HEAD_DIM
GLA_K_W = N_GLA_HEADS * GLA_HEAD_K
GLA_V_W = N_GLA_HEADS * GLA_HEAD_V
SPLIT_SIZES = (ATTN_W, ATTN_W, ATTN_W, IDX_Q_W, IDX_HEAD_DIM, N_IDX_HEADS,
               GLA_K_W, GLA_K_W, GLA_V_W, GLA_V_W, GLA_GATE_RANK, D_MODEL, D_MODEL)

LANES = 128
SUBLANES = 8
V7X_VMEM_BYTES = 64 * 1024 * 1024
VMEM_LIMIT_BYTES = V7X_VMEM_BYTES // 8 * 7

COL_TILE = 512
Z_MGA, Z_MGB = 0, 1
Z_AK, Z_GV, Z_GR = 4, 5, 6
Z_TILES = 7
Z_WIDTH = Z_TILES * COL_TILE
ZT_TILES = 3
SM_IW0 = 0
SM_GA0 = N_IDX_HEADS

TOKEN_TILE = 512
FF_TILE = 256
DSA_BLOCK = 256
GLA_TILE = 512
GLA_IN_W = 4 * GLA_K_W
SEARCH_MAX_ITERS = 24
SEARCH_FINISH_FROM = 4

NEG_BIG = -1e30
F32_LOWEST = float(np.finfo(np.float32).min)
F32_TINY = float(np.finfo(np.float32).tiny)
INT32_MIN = int(np.iinfo(np.int32).min)

f32 = jnp.float32
bf16 = jnp.bfloat16


def _rms(x, g):
    return x * lax.rsqrt(jnp.mean(x * x, axis=-1, keepdims=True) + EPS) * g


def _sigmoid(x):
    return 1.0 / (1.0 + jnp.exp(-x))


def _cparams(sem):
    return pltpu.CompilerParams(dimension_semantics=sem, vmem_limit_bytes=VMEM_LIMIT_BYTES)


def _resident(a):
    return pl.BlockSpec(a.shape, lambda i: (0, 0), pipeline_mode=pl.Buffered(1))


def _swiglu_half_step(x, g_ref, wg_ref, wu_ref, wd_ref):
    h = _rms(x, g_ref[...]).astype(bf16)
    y = x
    for f in range(D_FF // FF_TILE):
        cols = slice(f * FF_TILE, (f + 1) * FF_TILE)
        gate = jnp.dot(h, wg_ref[:, cols], preferred_element_type=f32)
        up = jnp.dot(h, wu_ref[:, cols], preferred_element_type=f32)
        a = (gate * _sigmoid(gate) * up).astype(bf16)
        y = y + 0.5 * jnp.dot(a, wd_ref[cols, :], preferred_element_type=f32)
    return y


def _ffn_kernel(x_ref, g_ref, wg_ref, wu_ref, wd_ref, o_ref):
    o_ref[...] = _swiglu_half_step(x_ref[...], g_ref, wg_ref, wu_ref, wd_ref)


def _token_rows(shape):
    if len(shape) == 2:
        return pl.BlockSpec((TOKEN_TILE, shape[1]), lambda i: (i, 0))
    per_b = shape[1] // TOKEN_TILE
    return pl.BlockSpec((None, TOKEN_TILE, shape[2]), lambda i: (i // per_b, i % per_b, 0))


def _ffn(x, g, wg, wu, wd):
    t = x.size // D_MODEL
    return pl.pallas_call(
        _ffn_kernel,
        grid=(t // TOKEN_TILE,),
        in_specs=[
            _token_rows(x.shape),
            _resident(g), _resident(wg), _resident(wu), _resident(wd),
        ],
        out_specs=pl.BlockSpec((TOKEN_TILE, D_MODEL), lambda i: (i, 0)),
        out_shape=jax.ShapeDtypeStruct((t, D_MODEL), f32),
        compiler_params=_cparams(("parallel",)),
        name="ffn",
    )(x, g, wg, wu, wd)


def _gla_operands(q, k, ga, w2, gb):
    n = q.shape[0]
    c = CHUNK
    def split(a):
        hi = a.astype(bf16)
        return hi, (a - hi.astype(f32)).astype(bf16)

    dot = lambda a, b: jnp.dot(a, b, preferred_element_type=f32)
    ga_hi, ga_lo = split(ga)
    w2_hi, w2_lo = split(w2)
    glin = dot(ga_hi, w2_hi) + dot(ga_hi, w2_lo) + dot(ga_lo, w2_hi) + gb
    logg = (jnp.minimum(glin, 0.0) - jnp.log(1.0 + jnp.exp(-jnp.abs(glin)))) / GLA_GATE_TAU
    row = lax.broadcasted_iota(jnp.int32, (n, n), 0)
    col = lax.broadcasted_iota(jnp.int32, (n, n), 1)
    tril = jnp.where((row // c == col // c) & (col <= row), 1.0, 0.0).astype(bf16)
    hi, lo = split(logg)
    bcum = dot(tril, hi) + dot(tril, lo)
    by_chunk = bcum.reshape(n // c, c, GLA_K_W)
    spread = lambda r: jnp.broadcast_to(r, (n // c, c, GLA_K_W)).reshape(n, GLA_K_W)
    b_last = by_chunk[:, c - 1:c, :]
    b_mid = spread(by_chunk[:, c // 2 - 1:c // 2, :])
    operands = jnp.concatenate(
        [q * jnp.exp(bcum), q * jnp.exp(bcum - b_mid), k * jnp.exp(b_mid - bcum),
         k * jnp.exp(spread(b_last) - bcum)], axis=1)
    return operands, jnp.exp(b_last.reshape(n // c, GLA_K_W))


def _proj_kernel(x_ref, g_ref, w_ref, wt_ref, wg_ref, ws_ref, w2_ref, gb_ref,
                 z_ref, qT_ref, vT_ref, iqT_ref, ik_ref, iwT_ref, gin_ref, dec_ref):
    h = _rms(x_ref[...], g_ref[...]).astype(bf16)
    for j in range(Z_TILES):
        cols = slice(j * COL_TILE, (j + 1) * COL_TILE)
        z_ref[:, cols] = jnp.dot(h, w_ref[:, cols], preferred_element_type=f32).astype(bf16)
    for j, out_ref in enumerate((qT_ref, vT_ref, iqT_ref)):
        cols = slice(j * COL_TILE, (j + 1) * COL_TILE)
        zt = jnp.dot(h, wt_ref[:, cols], preferred_element_type=f32)
        out_ref[...] = zt.T.astype(bf16)
    s = jnp.dot(h, ws_ref[...], preferred_element_type=f32)
    ik_ref[...] = s[:, :IDX_HEAD_DIM].astype(bf16)
    sm = s[:, LANES:]
    iwT_ref[...] = sm.T[SM_IW0:SM_IW0 + N_IDX_HEADS, :]
    gqk = jnp.dot(h, wg_ref[...], preferred_element_type=f32)
    operands, decay = _gla_operands(gqk[:, :GLA_K_W], gqk[:, GLA_K_W:],
                                    sm[:, SM_GA0:SM_GA0 + GLA_GATE_RANK],
                                    w2_ref[...], gb_ref[...])
    gin_ref[...] = operands.astype(bf16)
    dec_ref[...] = decay


def _proj(x, g, w_main, w_t, w_g, w_small, w2, gb, batch):
    t = x.shape[0]
    tm = TOKEN_TILE
    s = t // batch
    per_b = s // tm
    fm = lambda rows: pl.BlockSpec((None, rows, tm), lambda i: (i // per_b, 0, i % per_b))
    return pl.pallas_call(
        _proj_kernel,
        grid=(t // tm,),
        in_specs=[
            pl.BlockSpec((tm, D_MODEL), lambda i: (i, 0)),
            _resident(g), _resident(w_main), _resident(w_t), _resident(w_g), _resident(w_small),
            _resident(w2), _resident(gb),
        ],
        out_specs=[
            pl.BlockSpec((tm, Z_WIDTH), lambda i: (i, 0)),
            fm(ATTN_W), fm(ATTN_W), fm(IDX_Q_W),
            pl.BlockSpec((tm, IDX_HEAD_DIM), lambda i: (i, 0)),
            fm(N_IDX_HEADS),
            pl.BlockSpec((tm, GLA_IN_W), lambda i: (i, 0)),
            pl.BlockSpec((tm // CHUNK, GLA_K_W), lambda i: (i, 0)),
        ],
        out_shape=[
            jax.ShapeDtypeStruct((t, Z_WIDTH), bf16),
            jax.ShapeDtypeStruct((batch, ATTN_W, s), bf16),
            jax.ShapeDtypeStruct((batch, ATTN_W, s), bf16),
            jax.ShapeDtypeStruct((batch, IDX_Q_W, s), bf16),
            jax.ShapeDtypeStruct((t, IDX_HEAD_DIM), bf16),
            jax.ShapeDtypeStruct((batch, N_IDX_HEADS, s), f32),
            jax.ShapeDtypeStruct((t, GLA_IN_W), bf16),
            jax.ShapeDtypeStruct((t // CHUNK, GLA_K_W), f32),
        ],
        compiler_params=_cparams(("parallel",)),
        name="proj",
    )(x, g, w_main, w_t, w_g, w_small, w2, gb)


def _dsa_kernel(qT_ref, iqT_ref, iwT_ref, tab_ref, k_ref, vT_ref, ik_ref, o_ref,
                ibuf, qTm, s_ref, bias_ref, m_ref, l_ref, alpha_ref, acc_ref, thr_ref,
                ext_ref, pending_ref, finish_ref, *, seq_len):
    qb = DSA_BLOCK
    kt_rows = DSA_BLOCK
    i = pl.program_id(1)
    n_tiles = i + 1

    rowid = lax.broadcasted_iota(jnp.int32, (LANES, qb), 0)
    for h in range(N_ATTN_HEADS):
        pair = qT_ref[(h // 2) * LANES:(h // 2 + 1) * LANES, :]
        keep = (rowid < ATTN_HEAD_DIM) if h % 2 == 0 else (rowid >= ATTN_HEAD_DIM)
        qTm[h] = jnp.where(keep, pair, jnp.zeros_like(pair))
    l_ref[...] = jnp.zeros_like(l_ref)
    acc_ref[...] = jnp.zeros_like(acc_ref)

    def fold(x, op):
        return op(x.reshape(kt_rows // SUBLANES, SUBLANES, qb), axis=0)

    def idx_tile(kt, diagonal, stats):
        r0 = pl.multiple_of(kt * kt_rows, kt_rows)
        ki_t = ik_ref[pl.ds(r0, kt_rows), :]
        sc = jnp.zeros((kt_rows, qb), f32)
        for h in range(N_IDX_HEADS):
            s = jnp.dot(ki_t, iqT_ref[h * IDX_HEAD_DIM:(h + 1) * IDX_HEAD_DIM, :],
                        preferred_element_type=f32)
            sc = sc + iwT_ref[h:h + 1, :] * jnp.maximum(s, 0.0)
        sc_hi = sc_lo = sc_0 = sc
        if diagonal:
            kc = lax.broadcasted_iota(jnp.int32, (kt_rows, qb), 0) // CHUNK
            qc = lax.broadcasted_iota(jnp.int32, (kt_rows, qb), 1) // CHUNK
            adm = kc <= qc
            sc_hi = jnp.where(adm, sc, -jnp.inf)
            sc_lo = jnp.where(adm, sc, jnp.inf)
            sc_0 = jnp.where(adm, sc, 0.0)
        ibuf[pl.ds(r0, kt_rows), :] = sc_hi
        mx8, mn8, s8, ss8 = stats
        return (jnp.maximum(mx8, fold(sc_hi, jnp.max)), jnp.minimum(mn8, fold(sc_lo, jnp.min)),
                s8 + fold(sc_0, jnp.sum), ss8 + fold(sc_0 * sc_0, jnp.sum))

    stats0 = (jnp.full((SUBLANES, qb), -jnp.inf, f32), jnp.full((SUBLANES, qb), jnp.inf, f32),
              jnp.zeros((SUBLANES, qb), f32), jnp.zeros((SUBLANES, qb), f32))
    odd = i % 2
    stats = lax.cond(odd == 1, lambda st: idx_tile(0, False, st), lambda st: st, stats0)
    stats = lax.fori_loop(
        0, i // 2,
        lambda j, st: idx_tile(odd + 2 * j + 1, False, idx_tile(odd + 2 * j, False, st)),
        stats)
    mx8, mn8, s8, ss8 = idx_tile(i, True, stats)
    col_max = jnp.max(mx8, axis=0, keepdims=True)
    col_min = jnp.min(mn8, axis=0, keepdims=True)
    col_sum = jnp.sum(s8, axis=0, keepdims=True)
    col_ssq = jnp.sum(ss8, axis=0, keepdims=True)

    def count(pred):
        def body(r, acc):
            r0 = pl.multiple_of(r * kt_rows, kt_rows)
            tile = ibuf[pl.ds(r0, kt_rows), :]
            rows = r0 + lax.broadcasted_iota(jnp.int32, (kt_rows, qb), 0)
            ind = jnp.where(pred(tile, rows), 1.0, 0.0)
            part = ind.reshape(kt_rows // SUBLANES, SUBLANES, qb)
            while part.shape[0] > 1:
                half = part.shape[0] // 2
                part = part[:half] + part[half:]
            return acc + part[0]
        acc = jnp.zeros((SUBLANES, qb), f32)
        acc = lax.cond(i % 2 == 0, lambda a: body(0, a), lambda a: a, acc)
        first = 1 - i % 2
        acc = lax.fori_loop(
            0, (i + 1) // 2,
            lambda j, a: body(first + 2 * j + 1, body(first + 2 * j, a)), acc)
        return jnp.sum(acc, axis=0, keepdims=True)

    def resolve_ties(tv, need, mult):
        is_tie = tv < jnp.inf
        n_keys = n_tiles * kt_rows

        def tie_cond(st):
            return st["active"] > 0

        def tie_body(st):
            plo, phi, flo, fhi = st["plo"], st["phi"], st["flo"], st["fhi"]
            span = phi - plo
            est = ((need - flo) / jnp.maximum(fhi - flo, 1.0) * span.astype(f32)).astype(jnp.int32)
            pick = jnp.where(st["bisect"] == 2, span // 2, est)
            cand = plo + jnp.clip(pick, 1, jnp.maximum(span - 1, 1))
            f = count(lambda tile, rows: jnp.where(tile == tv, rows, seq_len) < cand)
            lower = f < need
            exact = f == need
            plo = jnp.where(lower, cand, jnp.where(exact, cand - 1, plo))
            phi = jnp.where(lower, phi, cand)
            flo = jnp.where(lower, f, flo)
            fhi = jnp.where(lower, fhi, f)
            unsplit = jnp.where(is_tie & (phi - plo > 1), 1, 0)
            return dict(active=jnp.max(unsplit), bisect=(st["bisect"] + 1) % 3,
                        plo=plo, phi=phi, flo=flo, fhi=fhi)

        split = lax.while_loop(tie_cond, tie_body, dict(
            active=jnp.int32(1), bisect=jnp.int32(0),
            plo=jnp.zeros((1, qb), jnp.int32), phi=jnp.zeros((1, qb), jnp.int32) + n_keys,
            flo=jnp.zeros((1, qb), f32), fhi=mult))["phi"]

        def drop_body(r, carry):
            r0 = pl.multiple_of(r * kt_rows, kt_rows)
            tile = ibuf[pl.ds(r0, kt_rows), :]
            rows = r0 + lax.broadcasted_iota(jnp.int32, (kt_rows, qb), 0)
            dropped = jnp.where(tile == tv, rows, -1) >= split
            ibuf[pl.ds(r0, kt_rows), :] = jnp.where(dropped, -jnp.inf, tile)
            return carry

        lax.fori_loop(0, n_tiles, drop_body, 0)

    qpos = i * qb + lax.broadcasted_iota(jnp.int32, (1, qb), 1)
    n_adm = (qpos // CHUNK + 1) * CHUNK
    select_all = n_adm <= TOPK_MAX
    kf = float(TOPK_MAX)

    thr_ref[...] = jnp.full((1, qb), F32_LOWEST, f32)
    pending_ref[0] = jnp.int32(0)

    @pl.when(n_tiles * kt_rows > TOPK_MAX)
    def _():
        n_f = n_adm.astype(f32)
        mean = col_sum / n_f
        sigma = jnp.sqrt(jnp.maximum(col_ssq / n_f - mean * mean, 0.0))
        spacing = tab_ref[1:2, :] * sigma

        def inside(c, lo, hi):
            return (c > lo) & (c < hi)

        c0 = mean + tab_ref[0:1, :] * sigma
        c0 = jnp.where(inside(c0, col_min, col_max), c0, 0.5 * col_min + 0.5 * col_max)
        zero = jnp.zeros((1, qb), f32)
        one = jnp.ones((1, qb), f32)
        state0 = dict(it=jnp.int32(0), active=jnp.int32(1), c=c0, lo=col_min, hi=col_max,
                      glo=n_f, ghi=zero, tlo=zero, thi=zero, boost=one, side=zero,
                      thr=jnp.full((1, qb), F32_LOWEST, f32),
                      open=jnp.where(select_all, 0.0, 1.0), tie=zero, kind=zero, emin=zero)

        def search_cond(st):
            return (st["it"] < SEARCH_MAX_ITERS) & (st["active"] > 0)

        def search_body(st):
            c, lo, hi = st["c"], st["lo"], st["hi"]
            g = count(lambda tile, rows: tile >= c)
            hit = (g == kf) & (st["open"] > 0.0)
            thr = jnp.where(hit, c, st["thr"])
            still = jnp.where(hit, 0.0, st["open"])
            above = g > kf
            lo = jnp.where(above, c, lo)
            hi = jnp.where(above, hi, c)
            glo = jnp.where(above, g, st["glo"])
            ghi = jnp.where(above, st["ghi"], g)
            tlo = jnp.where(above, 1.0, st["tlo"])
            thi = jnp.where(above, st["thi"], 1.0)
            side = jnp.where(above, 1.0, -1.0)
            repeat = side == st["side"]
            boost = jnp.where(repeat, 2.0 * st["boost"], 1.0)
            mid = 0.5 * lo + 0.5 * hi
            stale = 1.0 / jnp.minimum(boost, 256.0)
            w_lo = (glo - kf - 0.5) * jnp.where(above, 1.0, stale)
            w_hi = (kf + 0.5 - ghi) * jnp.where(above, stale, 1.0)
            c_two = lo + w_lo / (w_lo + w_hi) * (hi - lo)
            step = 2.0 * boost * spacing
            c_one = jnp.where(tlo > 0.0, lo + jnp.maximum(glo - kf, 1.0) * step,
                              hi - jnp.maximum(kf - ghi, 1.0) * step)
            c_one = jnp.where(inside(c_one, lo, hi), c_one, mid)
            both = (tlo > 0.0) & (thi > 0.0)
            c_new = jnp.where(both, c_two, c_one)
            c_new = jnp.where(both & (lo < 0.0) & (hi > 0.0), 0.0, c_new)
            c_new = jnp.where((lo == 0.0) & (hi > F32_TINY), F32_TINY, c_new)
            movable = inside(c_new, lo, hi) & ~((lo == 0.0) & (hi <= F32_TINY))
            missed = (still > 0.0) & (((st["kind"] == 1.0) & above) |
                                      ((st["kind"] == 2.0) & (g < kf)))
            lo = jnp.where(missed & (st["kind"] == 2.0), st["emin"], lo)
            closed = (~movable & both & (still > 0.0)) | missed
            tie = jnp.where(closed, 1.0, st["tie"])
            still = jnp.where(closed, 0.0, still)
            searching = jnp.where(movable, still, 0.0)

            from_hi = both & (kf - ghi == 1.0)
            from_lo = both & (glo - kf == 1.0) & ~from_hi
            ready = jnp.where(searching > 0.0, jnp.where(from_hi | from_lo, 1.0, 2.0), 0.0)
            code = jnp.max(ready)
            ext_ref[...] = jnp.zeros_like(ext_ref)
            finish_ref[0] = jnp.int32(0)

            @pl.when((code == 1.0) & (st["it"] + 1 >= SEARCH_FINISH_FROM))
            def _():
                def ext_body(r, carry):
                    r0 = pl.multiple_of(r * kt_rows, kt_rows)
                    tile = ibuf[pl.ds(r0, kt_rows), :]
                    below = fold(jnp.where(tile < hi, tile, -jnp.inf), jnp.max)
                    above_lo = fold(jnp.where(tile >= lo, tile, jnp.inf), jnp.min)
                    return jnp.maximum(carry[0], below), jnp.minimum(carry[1], above_lo)
                mx8, mn8 = lax.fori_loop(0, n_tiles, ext_body,
                                         (jnp.full((SUBLANES, qb), -jnp.inf, f32),
                                          jnp.full((SUBLANES, qb), jnp.inf, f32)))
                ext_ref[0:1, :] = jnp.max(mx8, axis=0, keepdims=True)
                smallest = jnp.min(mn8, axis=0, keepdims=True)
                ext_ref[1:2, :] = smallest

                def next_body(r, carry):
                    r0 = pl.multiple_of(r * kt_rows, kt_rows)
                    tile = ibuf[pl.ds(r0, kt_rows), :]
                    return jnp.minimum(carry, fold(jnp.where(tile > smallest, tile, jnp.inf),
                                                   jnp.min))
                nx8 = lax.fori_loop(0, n_tiles, next_body,
                                    jnp.full((SUBLANES, qb), jnp.inf, f32))
                ext_ref[2:3, :] = jnp.min(nx8, axis=0, keepdims=True)
                finish_ref[0] = jnp.int32(1)

            finishing = finish_ref[0] > 0
            emax = ext_ref[0:1, :]
            emin = ext_ref[1:2, :]
            next_up = ext_ref[2:3, :]
            kind = jnp.where(finishing & (searching > 0.0),
                             jnp.where(from_hi, 1.0, jnp.where(from_lo, 2.0, 0.0)), 0.0)
            c_new = jnp.where(kind == 1.0, emax, jnp.where(kind == 2.0, next_up, c_new))
            return dict(it=st["it"] + 1, active=(code > 0.0).astype(jnp.int32),
                        c=c_new, lo=lo, hi=hi, glo=glo, ghi=ghi, tlo=tlo, thi=thi,
                        boost=boost, side=side, thr=thr, open=still, tie=tie,
                        kind=kind, emin=emin)

        final = lax.while_loop(search_cond, search_body, state0)
        tied = final["tie"] > 0.0
        thr_ref[...] = jnp.where(tied, final["lo"], final["thr"])
        pending_ref[0] = (jnp.max(final["open"]) > 0.0).astype(jnp.int32)

        @pl.when(jnp.max(final["tie"]) > 0.0)
        def _():
            resolve_ties(jnp.where(tied, final["lo"], jnp.inf), kf - final["ghi"],
                         final["glo"] - final["ghi"])

    @pl.when(pending_ref[0] > 0)
    def _():
        def key_to_f32(key):
            bits = jnp.where(key < 0, key ^ jnp.int32(0x7FFFFFFF), key)
            return lax.bitcast_convert_type(bits, f32)

        def bit_body(t, prefix):
            step = lax.shift_left(jnp.int32(1), jnp.int32(31) - t)
            cand = prefix + step
            cand_f = key_to_f32(cand)
            cnt = count(lambda tile, rows: tile >= cand_f)
            return jnp.where(cnt >= kf, cand, prefix)

        prefix = lax.fori_loop(0, 32, bit_body, jnp.full((1, qb), INT32_MIN, jnp.int32))
        thr = jnp.where(select_all, F32_LOWEST, key_to_f32(prefix))
        thr_ref[...] = thr

        cnt_ge = count(lambda tile, rows: tile >= thr)
        excess = jnp.where(select_all, 0.0, cnt_ge - kf)

        @pl.when(jnp.max(excess) > 0.0)
        def _():
            cnt_gt = count(lambda tile, rows: tile > thr)
            resolve_ties(jnp.where(excess > 0.0, thr, jnp.inf), kf - cnt_gt, cnt_ge - cnt_gt)

    thr = thr_ref[...]
    ones_rows = jnp.ones((2 * SUBLANES, kt_rows), bf16)

    m_ref[1] = jnp.full(m_ref.shape[1:], NEG_BIG, f32)

    def logits_stage(kt, slot):
        r0 = pl.multiple_of(kt * kt_rows, kt_rows)
        bias_ref[...] = jnp.where(ibuf[pl.ds(r0, kt_rows), :] >= thr, 0.0, NEG_BIG)
        for h in range(N_ATTN_HEADS):
            k_pair = k_ref[pl.ds(r0, kt_rows), (h // 2) * LANES:(h // 2 + 1) * LANES]
            s_ref[slot, h] = jnp.dot(k_pair, qTm[h], preferred_element_type=f32) + bias_ref[...]
            m_old = m_ref[1 - slot, h]
            m_new = jnp.maximum(m_old, jnp.max(s_ref[slot, h], axis=0, keepdims=True))
            alpha_ref[slot, h] = jnp.exp2(m_old - m_new)
            m_ref[slot, h] = m_new

    def values_stage(kt, slot):
        r0 = pl.multiple_of(kt * kt_rows, kt_rows)
        for h in range(N_ATTN_HEADS):
            p = jnp.exp2(s_ref[slot, h] - m_ref[slot, h]).astype(bf16)
            hs = slice(h * ATTN_HEAD_DIM, (h + 1) * ATTN_HEAD_DIM)
            lhs = jnp.concatenate([vT_ref[hs, pl.ds(r0, kt_rows)], ones_rows], axis=0)
            pv = jnp.dot(lhs, p, preferred_element_type=f32)
            alpha = alpha_ref[slot, h]
            acc_ref[hs, :] = alpha * acc_ref[hs, :] + pv[:ATTN_HEAD_DIM]
            l_ref[h] = alpha * l_ref[h] + pv[ATTN_HEAD_DIM:ATTN_HEAD_DIM + 1]

    def att_body(j, carry):
        kt = 2 * j
        logits_stage(kt + 1, 1)
        values_stage(kt, 0)
        logits_stage(kt + 2, 0)
        values_stage(kt + 1, 1)
        return carry

    logits_stage(0, 0)
    lax.fori_loop(0, i // 2, att_body, 0)

    @pl.when(i % 2 == 1)
    def _():
        logits_stage(i, 1)
        values_stage(i - 1, 0)
        values_stage(i, 1)

    @pl.when(i % 2 == 0)
    def _():
        values_stage(i, 0)

    for h in range(N_ATTN_HEADS):
        hs = slice(h * ATTN_HEAD_DIM, (h + 1) * ATTN_HEAD_DIM)
        acc_ref[hs, :] = acc_ref[hs, :] / l_ref[h]
    o_ref[...] = acc_ref[...].T.astype(o_ref.dtype)


def _search_hints(seq_len):
    n = (np.arange(seq_len) // CHUNK + 1) * CHUNK
    frac = np.minimum(TOPK_MAX / n, 0.5)
    nd = statistics.NormalDist()
    z_of = {f: nd.inv_cdf(1.0 - f) for f in np.unique(frac)}
    z = np.array([z_of[f] for f in frac])
    dens = n * np.exp(-0.5 * z * z) / np.sqrt(2.0 * np.pi)
    return jnp.asarray(np.stack([z, 1.0 / dens]), f32)


def _dsa(qT, iqT, iwT, z3, vT, ik3):
    b, _, s = qT.shape
    qb = DSA_BLOCK
    return pl.pallas_call(
        functools.partial(_dsa_kernel, seq_len=s),
        grid=(b, s // qb),
        in_specs=[
            pl.BlockSpec((None, ATTN_W, qb), lambda bi, i: (bi, 0, i)),
            pl.BlockSpec((None, IDX_Q_W, qb), lambda bi, i: (bi, 0, i)),
            pl.BlockSpec((None, N_IDX_HEADS, qb), lambda bi, i: (bi, 0, i)),
            pl.BlockSpec((2, qb), lambda bi, i: (0, i)),
            pl.BlockSpec((None, s, COL_TILE), lambda bi, i: (bi, 0, Z_AK)),
            pl.BlockSpec((None, ATTN_W, s), lambda bi, i: (bi, 0, 0)),
            pl.BlockSpec((None, s, IDX_HEAD_DIM), lambda bi, i: (bi, 0, 0)),
        ],
        out_specs=pl.BlockSpec((None, qb, ATTN_W), lambda bi, i: (bi, i, 0)),
        out_shape=jax.ShapeDtypeStruct((b, s, ATTN_W), bf16),
        scratch_shapes=[
            pltpu.VMEM((s, qb), f32),
            pltpu.VMEM((N_ATTN_HEADS, LANES, qb), bf16),
            pltpu.VMEM((2, N_ATTN_HEADS, qb, qb), f32),
            pltpu.VMEM((qb, qb), f32),
            pltpu.VMEM((2, N_ATTN_HEADS, 1, qb), f32),
            pltpu.VMEM((N_ATTN_HEADS, 1, qb), f32),
            pltpu.VMEM((2, N_ATTN_HEADS, 1, qb), f32),
            pltpu.VMEM((ATTN_W, qb), f32),
            pltpu.VMEM((1, qb), f32),
            pltpu.VMEM((3, qb), f32),
            pltpu.SMEM((1,), jnp.int32),
            pltpu.SMEM((1,), jnp.int32),
        ],
        compiler_params=_cparams(("parallel", "arbitrary")),
        name="dsa",
    )(qT, iqT, iwT, _search_hints(s), z3, vT, ik3)


def _gla_kernel(gin_ref, dec_ref, v_ref, r_ref, gn_ref, o_ref, state_ref):
    c = CHUNK
    n_batch = gin_ref.shape[0]

    @pl.when(pl.program_id(0) == 0)
    def _():
        state_ref[...] = jnp.zeros_like(state_ref)

    row = lax.broadcasted_iota(jnp.int32, (c, c), 0)
    col = lax.broadcasted_iota(jnp.int32, (c, c), 1)
    causal = row >= col
    lane = lax.broadcasted_iota(jnp.int32, (c, LANES), 1)
    half = (lane < GLA_HEAD_K, lane >= GLA_HEAD_K)
    contract_last = (((1,), (1,)), ((), ()))
    contract_first = (((0,), (0,)), ((), ()))
    zero = jnp.zeros((c, LANES), bf16)

    for ci, bi in [(ci, bi) for ci in range(GLA_TILE // c) for bi in range(n_batch)]:
        rows = slice(ci * c, (ci + 1) * c)
        decay = dec_ref[bi, ci:ci + 1, :]
        for pr in range(N_GLA_HEADS // 2):
            ps = slice(pr * LANES, (pr + 1) * LANES)
            part = lambda g: gin_ref[bi, rows, g * GLA_K_W + pr * LANES:
                                     g * GLA_K_W + (pr + 1) * LANES]
            q_in, q_mid, k_mid, k_out = part(0), part(1), part(2), part(3)
            st = state_ref[bi, pr]
            st_b = st.astype(bf16)
            upd = jnp.zeros_like(st)
            for hh in range(2):
                h = pr * 2 + hh
                vs = slice(h * GLA_HEAD_V, (h + 1) * GLA_HEAD_V)
                v_h = v_ref[bi, rows, vs]
                attn = lax.dot_general(jnp.where(half[hh], q_mid, zero), k_mid, contract_last,
                                       preferred_element_type=f32)
                attn = jnp.where(causal, attn, 0.0).astype(bf16)
                o = lax.dot_general(jnp.where(half[hh], q_in, zero), st_b, contract_last,
                                    preferred_element_type=f32)
                o = o + jnp.dot(attn, v_h, preferred_element_type=f32)
                upd = upd + lax.dot_general(v_h, jnp.where(half[hh], k_out, zero),
                                            contract_first, preferred_element_type=f32)
                y = _rms(o, gn_ref[...])
                r = r_ref[bi, rows, vs].astype(f32)
                o_ref[bi, rows, vs] = (y * r * _sigmoid(r)).astype(o_ref.dtype)
            state_ref[bi, pr] = decay[:, ps] * st + upd


def _gla(gin3, dec3, z3, gn):
    b, s, _ = z3.shape
    tb = GLA_TILE
    return pl.pallas_call(
        _gla_kernel,
        grid=(s // tb,),
        in_specs=[
            pl.BlockSpec((b, tb, GLA_IN_W), lambda j: (0, j, 0)),
            pl.BlockSpec((b, tb // CHUNK, GLA_K_W), lambda j: (0, j, 0)),
            pl.BlockSpec((b, tb, COL_TILE), lambda j: (0, j, Z_GV)),
            pl.BlockSpec((b, tb, COL_TILE), lambda j: (0, j, Z_GR)),
            pl.BlockSpec((1, GLA_HEAD_V), lambda j: (0, 0)),
        ],
        out_specs=pl.BlockSpec((b, tb, GLA_V_W), lambda j: (0, j, 0)),
        out_shape=jax.ShapeDtypeStruct((b, s, GLA_V_W), bf16),
        scratch_shapes=[pltpu.VMEM((b, N_GLA_HEADS // 2, GLA_HEAD_V, LANES), f32)],
        compiler_params=_cparams(("arbitrary",)),
        name="gla",
    )(gin3, dec3, z3, z3, gn)


def _tail_kernel(x_ref, oa_ref, ob_ref, ga_ref, gb_ref, wa_ref, wb_ref, wo_ref,
                 g_ref, wg_ref, wu_ref, wd_ref, p_ref, gp_ref, wpg_ref, wpp_ref, gf_ref,
                 o_ref, *, final):
    ya = jnp.dot(oa_ref[...], wa_ref[...], preferred_element_type=f32)
    yb = jnp.dot(ob_ref[...], wb_ref[...], preferred_element_type=f32)
    mg = _sigmoid(ga_ref[...].astype(f32)) * ya + _sigmoid(gb_ref[...].astype(f32)) * yb
    x = x_ref[...] + jnp.dot(mg.astype(bf16), wo_ref[...], preferred_element_type=f32)
    x = _swiglu_half_step(x, g_ref, wg_ref, wu_ref, wd_ref)
    h = _rms(x, gp_ref[...]).astype(bf16)
    gate = _sigmoid(jnp.dot(h, wpg_ref[...], preferred_element_type=f32))
    e = jnp.dot(p_ref[...].astype(bf16), wpp_ref[...], preferred_element_type=f32)
    y = x + gate * e
    if final:
        y = _rms(y, gf_ref[...])
    o_ref[...] = y


def _tail(x, oa, ob, z, wa, wb, wo, g, wg, wu, wd, p, gp, wpg, wpp, gf, final, out_shape):
    t = x.shape[0]
    tm = TOKEN_TILE
    rows = lambda width, col=0: pl.BlockSpec((tm, width), lambda i: (i, col))
    return pl.pallas_call(
        functools.partial(_tail_kernel, final=final),
        grid=(t // tm,),
        in_specs=[
            rows(D_MODEL), rows(ATTN_W), rows(GLA_V_W), rows(D_MODEL, Z_MGA), rows(D_MODEL, Z_MGB),
            _resident(wa), _resident(wb), _resident(wo),
            _resident(g), _resident(wg), _resident(wu), _resident(wd),
            rows(PLE_DIM), _resident(gp), _resident(wpg), _resident(wpp), _resident(gf),
        ],
        out_specs=_token_rows(out_shape),
        out_shape=jax.ShapeDtypeStruct(out_shape, f32),
        compiler_params=_cparams(("parallel",)),
        name="tail",
    )(x, oa, ob, z, z, wa, wb, wo, g, wg, wu, wd, p, gp, wpg, wpp, gf)


def _split_w_in(w_in):
    cuts = np.cumsum(np.array(SPLIT_SIZES))[:-1].tolist()
    aq, ak, av, iq, ik, iw, gq, gk, gv, gr, ga, mga, mgb = jnp.split(w_in, cuts, axis=-1)
    att_scale = ATTN_HEAD_DIM ** -0.5 * float(np.log2(np.e))
    idx_scale = IDX_HEAD_DIM ** -0.5 * N_IDX_HEADS ** -0.5
    c = lambda w: w.astype(bf16)
    main = jnp.concatenate([c(mga), c(mgb), c(ak), c(gv), c(gr)], axis=-1)
    feature_major = jnp.concatenate([c(aq * att_scale), c(av), c(iq)], axis=-1)
    gla_qk = jnp.concatenate([c(gq * (GLA_HEAD_K ** -0.5)), c(gk)], axis=-1)
    zeros = lambda n: jnp.zeros(w_in.shape[:-1] + (n,), bf16)
    small = jnp.concatenate(
        [c(ik), zeros(LANES - IDX_HEAD_DIM), c(iw * idx_scale), c(ga),
         zeros(LANES - N_IDX_HEADS - GLA_GATE_RANK)], axis=-1)
    return main, feature_major, gla_qk, small


def kernel(x, p, w_in, gla_gate_w2, gla_gate_b, gla_norm, w_branch_a, w_branch_b, w_out,
           norm_ff1, norm_mix, norm_ff2, norm_ple, ff1_w_gate, ff1_w_up, ff1_w_down,
           ff2_w_gate, ff2_w_up, ff2_w_down, ple_w_proj, ple_w_gate, norm_final):
    b, s, d = x.shape
    t = b * s
    depth = w_in.shape[0]
    assert d == D_MODEL and w_in.shape[1:] == (D_MODEL, sum(SPLIT_SIZES))
    assert s % TOKEN_TILE == 0 and s % GLA_TILE == 0 and s % DSA_BLOCK == 0
    assert ff1_w_gate.shape[1:] == (D_MODEL, D_FF) and D_FF % FF_TILE == 0
    w_main, w_fm, w_gqk, w_small = _split_w_in(w_in)
    cast = lambda w: w.astype(bf16)
    ff1 = (cast(ff1_w_gate), cast(ff1_w_up), cast(ff1_w_down))
    ff2 = (cast(ff2_w_gate), cast(ff2_w_up), cast(ff2_w_down))
    wa, wb, wo = cast(w_branch_a), cast(w_branch_b), cast(w_out)
    wpg, wpp = cast(ple_w_gate), cast(ple_w_proj)
    row = lambda v: v.reshape(1, -1)

    xf = x
    for l in range(depth):
        last = l == depth - 1
        xf = _ffn(xf, row(norm_ff1[l]), ff1[0][l], ff1[1][l], ff1[2][l])

        z, qT, vT, iqT, ik, iwT, gin, dec = _proj(
            xf, row(norm_mix[l]), w_main[l], w_fm[l], w_gqk[l], w_small[l],
            gla_gate_w2[l], row(gla_gate_b[l]), b)
        z3 = z.reshape(b, s, Z_WIDTH)
        oa = _dsa(qT, iqT, iwT, z3, vT, ik.reshape(b, s, IDX_HEAD_DIM))
        ob = _gla(gin.reshape(b, s, GLA_IN_W), dec.reshape(b, s // CHUNK, GLA_K_W), z3,
                  row(gla_norm[l]))
        xf = _tail(xf, oa.reshape(t, ATTN_W), ob.reshape(t, GLA_V_W), z, wa[l], wb[l], wo[l],
                   row(norm_ff2[l]), ff2[0][l], ff2[1][l], ff2[2][l],
                   p[l].reshape(t, PLE_DIM), row(norm_ple[l]), wpg[l], wpp[l],
                   row(norm_final), final=last, out_shape=(b, s, d) if last else (t, d))
    return xf
```

```python
import functools
import statistics

import jax
import jax.numpy as jnp
import numpy as np
from jax import lax
from jax.experimental import pallas as pl
from jax.experimental.pallas import tpu as pltpu

D_MODEL = 1024
D_FF = 2816
PLE_DIM = 256
EPS = 1e-6

CHUNK = 64
N_ATTN_HEADS = 8
ATTN_HEAD_DIM = 64
N_IDX_HEADS = 8
IDX_HEAD_DIM = 64
TOPK_MAX = 256
N_GLA_HEADS = 4
GLA_HEAD_K = 64
GLA_HEAD_V = 128
GLA_GATE_RANK = 16
GLA_GATE_TAU = 16.0

ATTN_W = N_ATTN_HEADS * ATTN_HEAD_DIM
IDX_Q_W = N_IDX_HEADS * IDX_HEAD_DIM
GLA_K_W = N_GLA_HEADS * GLA_HEAD_K
GLA_V_W = N_GLA_HEADS * GLA_HEAD_V
SPLIT_SIZES = (ATTN_W, ATTN_W, ATTN_W, IDX_Q_W, IDX_HEAD_DIM, N_IDX_HEADS,
               GLA_K_W, GLA_K_W, GLA_V_W, GLA_V_W, GLA_GATE_RANK, D_MODEL, D_MODEL)

LANES = 128
SUBLANES = 8
V7X_VMEM_BYTES = 64 * 1024 * 1024
VMEM_LIMIT_BYTES = V7X_VMEM_BYTES // 8 * 7

COL_TILE = 512
Z_MGA, Z_MGB = 0, 1
Z_AK, Z_GV, Z_GR = 4, 5, 6
Z_TILES = 7
Z_WIDTH = Z_TILES * COL_TILE
ZT_TILES = 3
SM_IW0 = 0
SM_GA0 = N_IDX_HEADS

TOKEN_TILE = 512
FF_TILE = 256
DSA_BLOCK = 256
GLA_TILE = 512
GLA_IN_W = 4 * GLA_K_W
SEARCH_MAX_ITERS = 24
SEARCH_FINISH_FROM = 4

NEG_BIG = -1e30
F32_LOWEST = float(np.finfo(np.float32).min)
F32_TINY = float(np.finfo(np.float32).tiny)
INT32_MIN = int(np.iinfo(np.int32).min)

f32 = jnp.float32
bf16 = jnp.bfloat16


def _rms(x, g):
    return x * lax.rsqrt(jnp.mean(x * x, axis=-1, keepdims=True) + EPS) * g


def _sigmoid(x):
    return 1.0 / (1.0 + jnp.exp(-x))


def _cparams(sem):
    return pltpu.CompilerParams(dimension_semantics=sem, vmem_limit_bytes=VMEM_LIMIT_BYTES)


def _resident(a):
    return pl.BlockSpec(a.shape, lambda i: (0, 0), pipeline_mode=pl.Buffered(1))


def _swiglu_half_step(x, g_ref, wg_ref, wu_ref, wd_ref):
    h = _rms(x, g_ref[...]).astype(bf16)
    y = x
    for f in range(D_FF // FF_TILE):
        cols = slice(f * FF_TILE, (f + 1) * FF_TILE)
        gate = jnp.dot(h, wg_ref[:, cols], preferred_element_type=f32)
        up = jnp.dot(h, wu_ref[:, cols], preferred_element_type=f32)
        a = (gate * _sigmoid(gate) * up).astype(bf16)
        y = y + 0.5 * jnp.dot(a, wd_ref[cols, :], preferred_element_type=f32)
    return y


def _ffn_kernel(x_ref, g_ref, wg_ref, wu_ref, wd_ref, o_ref):
    o_ref[...] = _swiglu_half_step(x_ref[...], g_ref, wg_ref, wu_ref, wd_ref)


def _token_rows(shape):
    if len(shape) == 2:
        return pl.BlockSpec((TOKEN_TILE, shape[1]), lambda i: (i, 0))
    per_b = shape[1] // TOKEN_TILE
    return pl.BlockSpec((None, TOKEN_TILE, shape[2]), lambda i: (i // per_b, i % per_b, 0))


def _ffn(x, g, wg, wu, wd):
    t = x.size // D_MODEL
    return pl.pallas_call(
        _ffn_kernel,
        grid=(t // TOKEN_TILE,),
        in_specs=[
            _token_rows(x.shape),
            _resident(g), _resident(wg), _resident(wu), _resident(wd),
        ],
        out_specs=pl.BlockSpec((TOKEN_TILE, D_MODEL), lambda i: (i, 0)),
        out_shape=jax.ShapeDtypeStruct((t, D_MODEL), f32),
        compiler_params=_cparams(("parallel",)),
        name="ffn",
    )(x, g, wg, wu, wd)


def _gla_operands(q, k, ga, w2, gb):
    n = q.shape[0]
    c = CHUNK
    def split(a):
        hi = a.astype(bf16)
        return hi, (a - hi.astype(f32)).astype(bf16)

    dot = lambda a, b: jnp.dot(a, b, preferred_element_type=f32)
    ga_hi, ga_lo = split(ga)
    w2_hi, w2_lo = split(w2)
    glin = dot(ga_hi, w2_hi) + dot(ga_hi, w2_lo) + dot(ga_lo, w2_hi) + gb
    logg = (jnp.minimum(glin, 0.0) - jnp.log(1.0 + jnp.exp(-jnp.abs(glin)))) / GLA_GATE_TAU
    row = lax.broadcasted_iota(jnp.int32, (n, n), 0)
    col = lax.broadcasted_iota(jnp.int32, (n, n), 1)
    tril = jnp.where((row // c == col // c) & (col <= row), 1.0, 0.0).astype(bf16)
    hi, lo = split(logg)
    bcum = dot(tril, hi) + dot(tril, lo)
    by_chunk = bcum.reshape(n // c, c, GLA_K_W)
    spread = lambda r: jnp.broadcast_to(r, (n // c, c, GLA_K_W)).reshape(n, GLA_K_W)
    b_last = by_chunk[:, c - 1:c, :]
    b_mid = spread(by_chunk[:, c // 2 - 1:c // 2, :])
    operands = jnp.concatenate(
        [q * jnp.exp(bcum), q * jnp.exp(bcum - b_mid), k * jnp.exp(b_mid - bcum),
         k * jnp.exp(spread(b_last) - bcum)], axis=1)
    return operands, jnp.exp(b_last.reshape(n // c, GLA_K_W))


def _proj_kernel(x_ref, g_ref, w_ref, wt_ref, wg_ref, ws_ref, w2_ref, gb_ref,
                 z_ref, qT_ref, vT_ref, iqT_ref, ik_ref, iwT_ref, gin_ref, dec_ref):
    h = _rms(x_ref[...], g_ref[...]).astype(bf16)
    for j in range(Z_TILES):
        cols = slice(j * COL_TILE, (j + 1) * COL_TILE)
        z_ref[:, cols] = jnp.dot(h, w_ref[:, cols], preferred_element_type=f32).astype(bf16)
    for j, out_ref in enumerate((qT_ref, vT_ref, iqT_ref)):
        cols = slice(j * COL_TILE, (j + 1) * COL_TILE)
        zt = jnp.dot(h, wt_ref[:, cols], preferred_element_type=f32)
        out_ref[...] = zt.T.astype(bf16)
    s = jnp.dot(h, ws_ref[...], preferred_element_type=f32)
    ik_ref[...] = s[:, :IDX_HEAD_DIM].astype(bf16)
    sm = s[:, LANES:]
    iwT_ref[...] = sm.T[SM_IW0:SM_IW0 + N_IDX_HEADS, :]
    gqk = jnp.dot(h, wg_ref[...], preferred_element_type=f32)
    operands, decay = _gla_operands(gqk[:, :GLA_K_W], gqk[:, GLA_K_W:],
                                    sm[:, SM_GA0:SM_GA0 + GLA_GATE_RANK],
                                    w2_ref[...], gb_ref[...])
    gin_ref[...] = operands.astype(bf16)
    dec_ref[...] = decay


def _proj(x, g, w_main, w_t, w_g, w_small, w2, gb, batch):
    t = x.shape[0]
    tm = TOKEN_TILE
    s = t // batch
    per_b = s // tm
    fm = lambda rows: pl.BlockSpec((None, rows, tm), lambda i: (i // per_b, 0, i % per_b))
    return pl.pallas_call(
        _proj_kernel,
        grid=(t // tm,),
        in_specs=[
            pl.BlockSpec((tm, D_MODEL), lambda i: (i, 0)),
            _resident(g), _resident(w_main), _resident(w_t), _resident(w_g), _resident(w_small),
            _resident(w2), _resident(gb),
        ],
        out_specs=[
            pl.BlockSpec((tm, Z_WIDTH), lambda i: (i, 0)),
            fm(ATTN_W), fm(ATTN_W), fm(IDX_Q_W),
            pl.BlockSpec((tm, IDX_HEAD_DIM), lambda i: (i, 0)),
            fm(N_IDX_HEADS),
            pl.BlockSpec((tm, GLA_IN_W), lambda i: (i, 0)),
            pl.BlockSpec((tm // CHUNK, GLA_K_W), lambda i: (i, 0)),
        ],
        out_shape=[
            jax.ShapeDtypeStruct((t, Z_WIDTH), bf16),
            jax.ShapeDtypeStruct((batch, ATTN_W, s), bf16),
            jax.ShapeDtypeStruct((batch, ATTN_W, s), bf16),
            jax.ShapeDtypeStruct((batch, IDX_Q_W, s), bf16),
            jax.ShapeDtypeStruct((t, IDX_HEAD_DIM), bf16),
            jax.ShapeDtypeStruct((batch, N_IDX_HEADS, s), f32),
            jax.ShapeDtypeStruct((t, GLA_IN_W), bf16),
            jax.ShapeDtypeStruct((t // CHUNK, GLA_K_W), f32),
        ],
        compiler_params=_cparams(("parallel",)),
        name="proj",
    )(x, g, w_main, w_t, w_g, w_small, w2, gb)


def _dsa_kernel(qT_ref, iqT_ref, iwT_ref, tab_ref, k_ref, vT_ref, ik_ref, o_ref,
                ibuf, qTm, s_ref, bias_ref, m_ref, l_ref, alpha_ref, acc_ref, thr_ref,
                ext_ref, pending_ref, finish_ref, *, seq_len):
    qb = DSA_BLOCK
    kt_rows = DSA_BLOCK
    i = pl.program_id(1)
    n_tiles = i + 1

    rowid = lax.broadcasted_iota(jnp.int32, (LANES, qb), 0)
    for h in range(N_ATTN_HEADS):
        pair = qT_ref[(h // 2) * LANES:(h // 2 + 1) * LANES, :]
        keep = (rowid < ATTN_HEAD_DIM) if h % 2 == 0 else (rowid >= ATTN_HEAD_DIM)
        qTm[h] = jnp.where(keep, pair, jnp.zeros_like(pair))
    l_ref[...] = jnp.zeros_like(l_ref)
    acc_ref[...] = jnp.zeros_like(acc_ref)

    def fold(x, op):
        return op(x.reshape(kt_rows // SUBLANES, SUBLANES, qb), axis=0)

    def idx_tile(kt, diagonal, stats):
        r0 = pl.multiple_of(kt * kt_rows, kt_rows)
        ki_t = ik_ref[pl.ds(r0, kt_rows), :]
        sc = jnp.zeros((kt_rows, qb), f32)
        for h in range(N_IDX_HEADS):
            s = jnp.dot(ki_t, iqT_ref[h * IDX_HEAD_DIM:(h + 1) * IDX_HEAD_DIM, :],
                        preferred_element_type=f32)
            sc = sc + iwT_ref[h:h + 1, :] * jnp.maximum(s, 0.0)
        sc_hi = sc_lo = sc_0 = sc
        if diagonal:
            kc = lax.broadcasted_iota(jnp.int32, (kt_rows, qb), 0) // CHUNK
            qc = lax.broadcasted_iota(jnp.int32, (kt_rows, qb), 1) // CHUNK
            adm = kc <= qc
            sc_hi = jnp.where(adm, sc, -jnp.inf)
            sc_lo = jnp.where(adm, sc, jnp.inf)
            sc_0 = jnp.where(adm, sc, 0.0)
        ibuf[pl.ds(r0, kt_rows), :] = sc_hi
        mx8, mn8, s8, ss8 = stats
        return (jnp.maximum(mx8, fold(sc_hi, jnp.max)), jnp.minimum(mn8, fold(sc_lo, jnp.min)),
                s8 + fold(sc_0, jnp.sum), ss8 + fold(sc_0 * sc_0, jnp.sum))

    stats0 = (jnp.full((SUBLANES, qb), -jnp.inf, f32), jnp.full((SUBLANES, qb), jnp.inf, f32),
              jnp.zeros((SUBLANES, qb), f32), jnp.zeros((SUBLANES, qb), f32))
    odd = i % 2
    stats = lax.cond(odd == 1, lambda st: idx_tile(0, False, st), lambda st: st, stats0)
    stats = lax.fori_loop(
        0, i // 2,
        lambda j, st: idx_tile(odd + 2 * j + 1, False, idx_tile(odd + 2 * j, False, st)),
        stats)
    mx8, mn8, s8, ss8 = idx_tile(i, True, stats)
    col_max = jnp.max(mx8, axis=0, keepdims=True)
    col_min = jnp.min(mn8, axis=0, keepdims=True)
    col_sum = jnp.sum(s8, axis=0, keepdims=True)
    col_ssq = jnp.sum(ss8, axis=0, keepdims=True)

    def count(pred):
        def body(r, acc):
            r0 = pl.multiple_of(r * kt_rows, kt_rows)
            tile = ibuf[pl.ds(r0, kt_rows), :]
            rows = r0 + lax.broadcasted_iota(jnp.int32, (kt_rows, qb), 0)
            ind = jnp.where(pred(tile, rows), 1.0, 0.0)
            part = ind.reshape(kt_rows // SUBLANES, SUBLANES, qb)
            while part.shape[0] > 1:
                half = part.shape[0] // 2
                part = part[:half] + part[half:]
            return acc + part[0]
        acc = jnp.zeros((SUBLANES, qb), f32)
        acc = lax.cond(i % 2 == 0, lambda a: body(0, a), lambda a: a, acc)
        first = 1 - i % 2
        acc = lax.fori_loop(
            0, (i + 1) // 2,
            lambda j, a: body(first + 2 * j + 1, body(first + 2 * j, a)), acc)
        return jnp.sum(acc, axis=0, keepdims=True)

    def resolve_ties(tv, need, mult):
        is_tie = tv < jnp.inf
        n_keys = n_tiles * kt_rows

        def tie_cond(st):
            return st["active"] > 0

        def tie_body(st):
            plo, phi, flo, fhi = st["plo"], st["phi"], st["flo"], st["fhi"]
            span = phi - plo
            est = ((need - flo) / jnp.maximum(fhi - flo, 1.0) * span.astype(f32)).astype(jnp.int32)
            pick = jnp.where(st["bisect"] == 2, span // 2, est)
            cand = plo + jnp.clip(pick, 1, jnp.maximum(span - 1, 1))
            f = count(lambda tile, rows: jnp.where(tile == tv, rows, seq_len) < cand)
            lower = f < need
            exact = f == need
            plo = jnp.where(lower, cand, jnp.where(exact, cand - 1, plo))
            phi = jnp.where(lower, phi, cand)
            flo = jnp.where(lower, f, flo)
            fhi = jnp.where(lower, fhi, f)
            unsplit = jnp.where(is_tie & (phi - plo > 1), 1, 0)
            return dict(active=jnp.max(unsplit), bisect=(st["bisect"] + 1) % 3,
                        plo=plo, phi=phi, flo=flo, fhi=fhi)

        split = lax.while_loop(tie_cond, tie_body, dict(
            active=jnp.int32(1), bisect=jnp.int32(0),
            plo=jnp.zeros((1, qb), jnp.int32), phi=jnp.zeros((1, qb), jnp.int32) + n_keys,
            flo=jnp.zeros((1, qb), f32), fhi=mult))["phi"]

        def drop_body(r, carry):
            r0 = pl.multiple_of(r * kt_rows, kt_rows)
            tile = ibuf[pl.ds(r0, kt_rows), :]
            rows = r0 + lax.broadcasted_iota(jnp.int32, (kt_rows, qb), 0)
            dropped = jnp.where(tile == tv, rows, -1) >= split
            ibuf[pl.ds(r0, kt_rows), :] = jnp.where(dropped, -jnp.inf, tile)
            return carry

        lax.fori_loop(0, n_tiles, drop_body, 0)

    qpos = i * qb + lax.broadcasted_iota(jnp.int32, (1, qb), 1)
    n_adm = (qpos // CHUNK + 1) * CHUNK
    select_all = n_adm <= TOPK_MAX
    kf = float(TOPK_MAX)

    thr_ref[...] = jnp.full((1, qb), F32_LOWEST, f32)
    pending_ref[0] = jnp.int32(0)

    @pl.when(n_tiles * kt_rows > TOPK_MAX)
    def _():
        n_f = n_adm.astype(f32)
        mean = col_sum / n_f
        sigma = jnp.sqrt(jnp.maximum(col_ssq / n_f - mean * mean, 0.0))
        spacing = tab_ref[1:2, :] * sigma

        def inside(c, lo, hi):
            return (c > lo) & (c < hi)

        c0 = mean + tab_ref[0:1, :] * sigma
        c0 = jnp.where(inside(c0, col_min, col_max), c0, 0.5 * col_min + 0.5 * col_max)
        zero = jnp.zeros((1, qb), f32)
        one = jnp.ones((1, qb), f32)
        state0 = dict(it=jnp.int32(0), active=jnp.int32(1), c=c0, lo=col_min, hi=col_max,
                      glo=n_f, ghi=zero, tlo=zero, thi=zero, boost=one, side=zero,
                      thr=jnp.full((1, qb), F32_LOWEST, f32),
                      open=jnp.where(select_all, 0.0, 1.0), tie=zero, kind=zero, emin=zero)

        def search_cond(st):
            return (st["it"] < SEARCH_MAX_ITERS) & (st["active"] > 0)

        def search_body(st):
            c, lo, hi = st["c"], st["lo"], st["hi"]
            g = count(lambda tile, rows: tile >= c)
            hit = (g == kf) & (st["open"] > 0.0)
            thr = jnp.where(hit, c, st["thr"])
            still = jnp.where(hit, 0.0, st["open"])
            above = g > kf
            lo = jnp.where(above, c, lo)
            hi = jnp.where(above, hi, c)
            glo = jnp.where(above, g, st["glo"])
            ghi = jnp.where(above, st["ghi"], g)
            tlo = jnp.where(above, 1.0, st["tlo"])
            thi = jnp.where(above, st["thi"], 1.0)
            side = jnp.where(above, 1.0, -1.0)
            repeat = side == st["side"]
            boost = jnp.where(repeat, 2.0 * st["boost"], 1.0)
            mid = 0.5 * lo + 0.5 * hi
            stale = 1.0 / jnp.minimum(boost, 256.0)
            w_lo = (glo - kf - 0.5) * jnp.where(above, 1.0, stale)
            w_hi = (kf + 0.5 - ghi) * jnp.where(above, stale, 1.0)
            c_two = lo + w_lo / (w_lo + w_hi) * (hi - lo)
            step = 2.0 * boost * spacing
            c_one = jnp.where(tlo > 0.0, lo + jnp.maximum(glo - kf, 1.0) * step,
                              hi - jnp.maximum(kf - ghi, 1.0) * step)
            c_one = jnp.where(inside(c_one, lo, hi), c_one, mid)
            both = (tlo > 0.0) & (thi > 0.0)
            c_new = jnp.where(both, c_two, c_one)
            c_new = jnp.where(both & (lo < 0.0) & (hi > 0.0), 0.0, c_new)
            c_new = jnp.where((lo == 0.0) & (hi > F32_TINY), F32_TINY, c_new)
            movable = inside(c_new, lo, hi) & ~((lo == 0.0) & (hi <= F32_TINY))
            missed = (still > 0.0) & (((st["kind"] == 1.0) & above) |
                                      ((st["kind"] == 2.0) & (g < kf)))
            lo = jnp.where(missed & (st["kind"] == 2.0), st["emin"], lo)
            closed = (~movable & both & (still > 0.0)) | missed
            tie = jnp.where(closed, 1.0, st["tie"])
            still = jnp.where(closed, 0.0, still)
            searching = jnp.where(movable, still, 0.0)

            from_hi = both & (kf - ghi == 1.0)
            from_lo = both & (glo - kf == 1.0) & ~from_hi
            ready = jnp.where(searching > 0.0, jnp.where(from_hi | from_lo, 1.0, 2.0), 0.0)
            code = jnp.max(ready)
            ext_ref[...] = jnp.zeros_like(ext_ref)
            finish_ref[0] = jnp.int32(0)

            @pl.when((code == 1.0) & (st["it"] + 1 >= SEARCH_FINISH_FROM))
            def _():
                def ext_body(r, carry):
                    r0 = pl.multiple_of(r * kt_rows, kt_rows)
                    tile = ibuf[pl.ds(r0, kt_rows), :]
                    below = fold(jnp.where(tile < hi, tile, -jnp.inf), jnp.max)
                    above_lo = fold(jnp.where(tile >= lo, tile, jnp.inf), jnp.min)
                    return jnp.maximum(carry[0], below), jnp.minimum(carry[1], above_lo)
                mx8, mn8 = lax.fori_loop(0, n_tiles, ext_body,
                                         (jnp.full((SUBLANES, qb), -jnp.inf, f32),
                                          jnp.full((SUBLANES, qb), jnp.inf, f32)))
                ext_ref[0:1, :] = jnp.max(mx8, axis=0, keepdims=True)
                smallest = jnp.min(mn8, axis=0, keepdims=True)
                ext_ref[1:2, :] = smallest

                finish_ref[0] = jnp.int32(1)

                @pl.when(jnp.max(jnp.where(from_lo & (searching > 0.0), 1.0, 0.0)) > 0.0)
                def _():
                    def next_body(r, carry):
                        r0 = pl.multiple_of(r * kt_rows, kt_rows)
                        tile = ibuf[pl.ds(r0, kt_rows), :]
                        above = fold(jnp.where(tile > smallest, tile, jnp.inf), jnp.min)
                        return jnp.minimum(carry, above)
                    nx8 = lax.fori_loop(0, n_tiles, next_body,
                                        jnp.full((SUBLANES, qb), jnp.inf, f32))
                    ext_ref[2:3, :] = jnp.min(nx8, axis=0, keepdims=True)

            finishing = finish_ref[0] > 0
            emax = ext_ref[0:1, :]
            emin = ext_ref[1:2, :]
            next_up = ext_ref[2:3, :]
            kind = jnp.where(finishing & (searching > 0.0),
                             jnp.where(from_hi, 1.0, jnp.where(from_lo, 2.0, 0.0)), 0.0)
            c_new = jnp.where(kind == 1.0, emax, jnp.where(kind == 2.0, next_up, c_new))
            return dict(it=st["it"] + 1, active=(code > 0.0).astype(jnp.int32),
                        c=c_new, lo=lo, hi=hi, glo=glo, ghi=ghi, tlo=tlo, thi=thi,
                        boost=boost, side=side, thr=thr, open=still, tie=tie,
                        kind=kind, emin=emin)

        final = lax.while_loop(search_cond, search_body, state0)
        tied = final["tie"] > 0.0
        thr_ref[...] = jnp.where(tied, final["lo"], final["thr"])
        pending_ref[0] = (jnp.max(final["open"]) > 0.0).astype(jnp.int32)

        @pl.when(jnp.max(final["tie"]) > 0.0)
        def _():
            resolve_ties(jnp.where(tied, final["lo"], jnp.inf), kf - final["ghi"],
                         final["glo"] - final["ghi"])

    @pl.when(pending_ref[0] > 0)
    def _():
        def key_to_f32(key):
            bits = jnp.where(key < 0, key ^ jnp.int32(0x7FFFFFFF), key)
            return lax.bitcast_convert_type(bits, f32)

        def bit_body(t, prefix):
            step = lax.shift_left(jnp.int32(1), jnp.int32(31) - t)
            cand = prefix + step
            cand_f = key_to_f32(cand)
            cnt = count(lambda tile, rows: tile >= cand_f)
            return jnp.where(cnt >= kf, cand, prefix)

        prefix = lax.fori_loop(0, 32, bit_body, jnp.full((1, qb), INT32_MIN, jnp.int32))
        thr = jnp.where(select_all, F32_LOWEST, key_to_f32(prefix))
        thr_ref[...] = thr

        cnt_ge = count(lambda tile, rows: tile >= thr)
        excess = jnp.where(select_all, 0.0, cnt_ge - kf)

        @pl.when(jnp.max(excess) > 0.0)
        def _():
            cnt_gt = count(lambda tile, rows: tile > thr)
            resolve_ties(jnp.where(excess > 0.0, thr, jnp.inf), kf - cnt_gt, cnt_ge - cnt_gt)

    thr = thr_ref[...]
    ones_rows = jnp.ones((2 * SUBLANES, kt_rows), bf16)

    m_ref[1] = jnp.full(m_ref.shape[1:], NEG_BIG, f32)

    def logits_stage(kt, slot):
        r0 = pl.multiple_of(kt * kt_rows, kt_rows)
        bias_ref[...] = jnp.where(ibuf[pl.ds(r0, kt_rows), :] >= thr, 0.0, NEG_BIG)
        for h in range(N_ATTN_HEADS):
            k_pair = k_ref[pl.ds(r0, kt_rows), (h // 2) * LANES:(h // 2 + 1) * LANES]
            s_ref[slot, h] = jnp.dot(k_pair, qTm[h], preferred_element_type=f32) + bias_ref[...]
            m_old = m_ref[1 - slot, h]
            m_new = jnp.maximum(m_old, jnp.max(s_ref[slot, h], axis=0, keepdims=True))
            alpha_ref[slot, h] = jnp.exp2(m_old - m_new)
            m_ref[slot, h] = m_new

    def values_stage(kt, slot):
        r0 = pl.multiple_of(kt * kt_rows, kt_rows)
        for h in range(N_ATTN_HEADS):
            p = jnp.exp2(s_ref[slot, h] - m_ref[slot, h]).astype(bf16)
            hs = slice(h * ATTN_HEAD_DIM, (h + 1) * ATTN_HEAD_DIM)
            lhs = jnp.concatenate([vT_ref[hs, pl.ds(r0, kt_rows)], ones_rows], axis=0)
            pv = jnp.dot(lhs, p, preferred_element_type=f32)
            alpha = alpha_ref[slot, h]
            acc_ref[hs, :] = alpha * acc_ref[hs, :] + pv[:ATTN_HEAD_DIM]
            l_ref[h] = alpha * l_ref[h] + pv[ATTN_HEAD_DIM:ATTN_HEAD_DIM + 1]

    def att_body(j, carry):
        kt = 2 * j
        logits_stage(kt + 1, 1)
        values_stage(kt, 0)
        logits_stage(kt + 2, 0)
        values_stage(kt + 1, 1)
        return carry

    logits_stage(0, 0)
    lax.fori_loop(0, i // 2, att_body, 0)

    @pl.when(i % 2 == 1)
    def _():
        logits_stage(i, 1)
        values_stage(i - 1, 0)
        values_stage(i, 1)

    @pl.when(i % 2 == 0)
    def _():
        values_stage(i, 0)

    for h in range(N_ATTN_HEADS):
        hs = slice(h * ATTN_HEAD_DIM, (h + 1) * ATTN_HEAD_DIM)
        acc_ref[hs, :] = acc_ref[hs, :] / l_ref[h]
    o_ref[...] = acc_ref[...].T.astype(o_ref.dtype)


def _search_hints(seq_len):
    n = (np.arange(seq_len) // CHUNK + 1) * CHUNK
    frac = np.minimum(TOPK_MAX / n, 0.5)
    nd = statistics.NormalDist()
    z_of = {f: nd.inv_cdf(1.0 - f) for f in np.unique(frac)}
    z = np.array([z_of[f] for f in frac])
    dens = n * np.exp(-0.5 * z * z) / np.sqrt(2.0 * np.pi)
    return jnp.asarray(np.stack([z, 1.0 / dens]), f32)


def _dsa(qT, iqT, iwT, z3, vT, ik3):
    b, _, s = qT.shape
    qb = DSA_BLOCK
    return pl.pallas_call(
        functools.partial(_dsa_kernel, seq_len=s),
        grid=(b, s // qb),
        in_specs=[
            pl.BlockSpec((None, ATTN_W, qb), lambda bi, i: (bi, 0, i)),
            pl.BlockSpec((None, IDX_Q_W, qb), lambda bi, i: (bi, 0, i)),
            pl.BlockSpec((None, N_IDX_HEADS, qb), lambda bi, i: (bi, 0, i)),
            pl.BlockSpec((2, qb), lambda bi, i: (0, i)),
            pl.BlockSpec((None, s, COL_TILE), lambda bi, i: (bi, 0, Z_AK)),
            pl.BlockSpec((None, ATTN_W, s), lambda bi, i: (bi, 0, 0)),
            pl.BlockSpec((None, s, IDX_HEAD_DIM), lambda bi, i: (bi, 0, 0)),
        ],
        out_specs=pl.BlockSpec((None, qb, ATTN_W), lambda bi, i: (bi, i, 0)),
        out_shape=jax.ShapeDtypeStruct((b, s, ATTN_W), bf16),
        scratch_shapes=[
            pltpu.VMEM((s, qb), f32),
            pltpu.VMEM((N_ATTN_HEADS, LANES, qb), bf16),
            pltpu.VMEM((2, N_ATTN_HEADS, qb, qb), f32),
            pltpu.VMEM((qb, qb), f32),
            pltpu.VMEM((2, N_ATTN_HEADS, 1, qb), f32),
            pltpu.VMEM((N_ATTN_HEADS, 1, qb), f32),
            pltpu.VMEM((2, N_ATTN_HEADS, 1, qb), f32),
            pltpu.VMEM((ATTN_W, qb), f32),
            pltpu.VMEM((1, qb), f32),
            pltpu.VMEM((3, qb), f32),
            pltpu.SMEM((1,), jnp.int32),
            pltpu.SMEM((1,), jnp.int32),
        ],
        compiler_params=_cparams(("parallel", "arbitrary")),
        name="dsa",
    )(qT, iqT, iwT, _search_hints(s), z3, vT, ik3)


def _gla_kernel(gin_ref, dec_ref, v_ref, r_ref, gn_ref, o_ref, state_ref):
    c = CHUNK
    n_batch = gin_ref.shape[0]

    @pl.when(pl.program_id(0) == 0)
    def _():
        state_ref[...] = jnp.zeros_like(state_ref)

    row = lax.broadcasted_iota(jnp.int32, (c, c), 0)
    col = lax.broadcasted_iota(jnp.int32, (c, c), 1)
    causal = row >= col
    lane = lax.broadcasted_iota(jnp.int32, (c, LANES), 1)
    half = (lane < GLA_HEAD_K, lane >= GLA_HEAD_K)
    contract_last = (((1,), (1,)), ((), ()))
    contract_first = (((0,), (0,)), ((), ()))
    zero = jnp.zeros((c, LANES), bf16)

    for ci, bi in [(ci, bi) for ci in range(GLA_TILE // c) for bi in range(n_batch)]:
        rows = slice(ci * c, (ci + 1) * c)
        decay = dec_ref[bi, ci:ci + 1, :]
        for pr in range(N_GLA_HEADS // 2):
            ps = slice(pr * LANES, (pr + 1) * LANES)
            part = lambda g: gin_ref[bi, rows, g * GLA_K_W + pr * LANES:
                                     g * GLA_K_W + (pr + 1) * LANES]
            q_in, q_mid, k_mid, k_out = part(0), part(1), part(2), part(3)
            st = state_ref[bi, pr]
            st_b = st.astype(bf16)
            upd = jnp.zeros_like(st)
            for hh in range(2):
                h = pr * 2 + hh
                vs = slice(h * GLA_HEAD_V, (h + 1) * GLA_HEAD_V)
                v_h = v_ref[bi, rows, vs]
                attn = lax.dot_general(jnp.where(half[hh], q_mid, zero), k_mid, contract_last,
                                       preferred_element_type=f32)
                attn = jnp.where(causal, attn, 0.0).astype(bf16)
                o = lax.dot_general(jnp.where(half[hh], q_in, zero), st_b, contract_last,
                                    preferred_element_type=f32)
                o = o + jnp.dot(attn, v_h, preferred_element_type=f32)
                upd = upd + lax.dot_general(v_h, jnp.where(half[hh], k_out, zero),
                                            contract_first, preferred_element_type=f32)
                y = _rms(o, gn_ref[...])
                r = r_ref[bi, rows, vs].astype(f32)
                o_ref[bi, rows, vs] = (y * r * _sigmoid(r)).astype(o_ref.dtype)
            state_ref[bi, pr] = decay[:, ps] * st + upd


def _gla(gin3, dec3, z3, gn):
    b, s, _ = z3.shape
    tb = GLA_TILE
    return pl.pallas_call(
        _gla_kernel,
        grid=(s // tb,),
        in_specs=[
            pl.BlockSpec((b, tb, GLA_IN_W), lambda j: (0, j, 0)),
            pl.BlockSpec((b, tb // CHUNK, GLA_K_W), lambda j: (0, j, 0)),
            pl.BlockSpec((b, tb, COL_TILE), lambda j: (0, j, Z_GV)),
            pl.BlockSpec((b, tb, COL_TILE), lambda j: (0, j, Z_GR)),
            pl.BlockSpec((1, GLA_HEAD_V), lambda j: (0, 0)),
        ],
        out_specs=pl.BlockSpec((b, tb, GLA_V_W), lambda j: (0, j, 0)),
        out_shape=jax.ShapeDtypeStruct((b, s, GLA_V_W), bf16),
        scratch_shapes=[pltpu.VMEM((b, N_GLA_HEADS // 2, GLA_HEAD_V, LANES), f32)],
        compiler_params=_cparams(("arbitrary",)),
        name="gla",
    )(gin3, dec3, z3, z3, gn)


def _tail_kernel(x_ref, oa_ref, ob_ref, ga_ref, gb_ref, wa_ref, wb_ref, wo_ref,
                 g_ref, wg_ref, wu_ref, wd_ref, p_ref, gp_ref, wpg_ref, wpp_ref, gf_ref,
                 o_ref, *, final):
    ya = jnp.dot(oa_ref[...], wa_ref[...], preferred_element_type=f32)
    yb = jnp.dot(ob_ref[...], wb_ref[...], preferred_element_type=f32)
    mg = _sigmoid(ga_ref[...].astype(f32)) * ya + _sigmoid(gb_ref[...].astype(f32)) * yb
    x = x_ref[...] + jnp.dot(mg.astype(bf16), wo_ref[...], preferred_element_type=f32)
    x = _swiglu_half_step(x, g_ref, wg_ref, wu_ref, wd_ref)
    h = _rms(x, gp_ref[...]).astype(bf16)
    gate = _sigmoid(jnp.dot(h, wpg_ref[...], preferred_element_type=f32))
    e = jnp.dot(p_ref[...].astype(bf16), wpp_ref[...], preferred_element_type=f32)
    y = x + gate * e
    if final:
        y = _rms(y, gf_ref[...])
    o_ref[...] = y


def _tail(x, oa, ob, z, wa, wb, wo, g, wg, wu, wd, p, gp, wpg, wpp, gf, final, out_shape):
    t = x.shape[0]
    tm = TOKEN_TILE
    rows = lambda width, col=0: pl.BlockSpec((tm, width), lambda i: (i, col))
    return pl.pallas_call(
        functools.partial(_tail_kernel, final=final),
        grid=(t // tm,),
        in_specs=[
            rows(D_MODEL), rows(ATTN_W), rows(GLA_V_W), rows(D_MODEL, Z_MGA), rows(D_MODEL, Z_MGB),
            _resident(wa), _resident(wb), _resident(wo),
            _resident(g), _resident(wg), _resident(wu), _resident(wd),
            rows(PLE_DIM), _resident(gp), _resident(wpg), _resident(wpp), _resident(gf),
        ],
        out_specs=_token_rows(out_shape),
        out_shape=jax.ShapeDtypeStruct(out_shape, f32),
        compiler_params=_cparams(("parallel",)),
        name="tail",
    )(x, oa, ob, z, z, wa, wb, wo, g, wg, wu, wd, p, gp, wpg, wpp, gf)


def _split_w_in(w_in):
    cuts = np.cumsum(np.array(SPLIT_SIZES))[:-1].tolist()
    aq, ak, av, iq, ik, iw, gq, gk, gv, gr, ga, mga, mgb = jnp.split(w_in, cuts, axis=-1)
    att_scale = ATTN_HEAD_DIM ** -0.5 * float(np.log2(np.e))
    idx_scale = IDX_HEAD_DIM ** -0.5 * N_IDX_HEADS ** -0.5
    c = lambda w: w.astype(bf16)
    main = jnp.concatenate([c(mga), c(mgb), c(ak), c(gv), c(gr)], axis=-1)
    feature_major = jnp.concatenate([c(aq * att_scale), c(av), c(iq)], axis=-1)
    gla_qk = jnp.concatenate([c(gq * (GLA_HEAD_K ** -0.5)), c(gk)], axis=-1)
    zeros = lambda n: jnp.zeros(w_in.shape[:-1] + (n,), bf16)
    small = jnp.concatenate(
        [c(ik), zeros(LANES - IDX_HEAD_DIM), c(iw * idx_scale), c(ga),
         zeros(LANES - N_IDX_HEADS - GLA_GATE_RANK)], axis=-1)
    return main, feature_major, gla_qk, small


def kernel(x, p, w_in, gla_gate_w2, gla_gate_b, gla_norm, w_branch_a, w_branch_b, w_out,
           norm_ff1, norm_mix, norm_ff2, norm_ple, ff1_w_gate, ff1_w_up, ff1_w_down,
           ff2_w_gate, ff2_w_up, ff2_w_down, ple_w_proj, ple_w_gate, norm_final):
    b, s, d = x.shape
    t = b * s
    depth = w_in.shape[0]
    assert d == D_MODEL and w_in.shape[1:] == (D_MODEL, sum(SPLIT_SIZES))
    assert s % TOKEN_TILE == 0 and s % GLA_TILE == 0 and s % DSA_BLOCK == 0
    assert ff1_w_gate.shape[1:] == (D_MODEL, D_FF) and D_FF % FF_TILE == 0
    w_main, w_fm, w_gqk, w_small = _split_w_in(w_in)
    cast = lambda w: w.astype(bf16)
    ff1 = (cast(ff1_w_gate), cast(ff1_w_up), cast(ff1_w_down))
    ff2 = (cast(ff2_w_gate), cast(ff2_w_up), cast(ff2_w_down))
    wa, wb, wo = cast(w_branch_a), cast(w_branch_b), cast(w_out)
    wpg, wpp = cast(ple_w_gate), cast(ple_w_proj)
    row = lambda v: v.reshape(1, -1)

    xf = x
    for l in range(depth):
        last = l == depth - 1
        xf = _ffn(xf, row(norm_ff1[l]), ff1[0][l], ff1[1][l], ff1[2][l])

        z, qT, vT, iqT, ik, iwT, gin, dec = _proj(
            xf, row(norm_mix[l]), w_main[l], w_fm[l], w_gqk[l], w_small[l],
            gla_gate_w2[l], row(gla_gate_b[l]), b)
        z3 = z.reshape(b, s, Z_WIDTH)
        oa = _dsa(qT, iqT, iwT, z3, vT, ik.reshape(b, s, IDX_HEAD_DIM))
        ob = _gla(gin.reshape(b, s, GLA_IN_W), dec.reshape(b, s // CHUNK, GLA_K_W), z3,
                  row(gla_norm[l]))
        xf = _tail(xf, oa.reshape(t, ATTN_W), ob.reshape(t, GLA_V_W), z, wa[l], wb[l], wo[l],
                   row(norm_ff2[l]), ff2[0][l], ff2[1][l], ff2[2][l],
                   p[l].reshape(t, PLE_DIM), row(norm_ple[l]), wpg[l], wpp[l],
                   row(norm_final), final=last, out_shape=(b, s, d) if last else (t, d))
    return xf
```

```python
import functools
import statistics

import jax
import jax.numpy as jnp
import numpy as np
from jax import lax
from jax.experimental import pallas as pl
from jax.experimental.pallas import tpu as pltpu

D_MODEL = 1024
D_FF = 2816
PLE_DIM = 256
EPS = 1e-6

CHUNK = 64
N_ATTN_HEADS = 8
ATTN_HEAD_DIM = 64
N_IDX_HEADS = 8
IDX_HEAD_DIM = 64
TOPK_MAX = 256
N_GLA_HEADS = 4
GLA_HEAD_K = 64
GLA_HEAD_V = 128
GLA_GATE_RANK = 16
GLA_GATE_TAU = 16.0

ATTN_W = N_ATTN_HEADS * ATTN_HEAD_DIM
IDX_Q_W = N_IDX_HEADS * IDX_HEAD_DIM
GLA_K_W = N_GLA_HEADS * GLA_HEAD_K
GLA_V_W = N_GLA_HEADS * GLA_HEAD_V
SPLIT_SIZES = (ATTN_W, ATTN_W, ATTN_W, IDX_Q_W, IDX_HEAD_DIM, N_IDX_HEADS,
               GLA_K_W, GLA_K_W, GLA_V_W, GLA_V_W, GLA_GATE_RANK, D_MODEL, D_MODEL)

LANES = 128
SUBLANES = 8
V7X_VMEM_BYTES = 64 * 1024 * 1024
VMEM_LIMIT_BYTES = V7X_VMEM_BYTES // 8 * 7

COL_TILE = 512
Z_MGA, Z_MGB = 0, 1
Z_AK, Z_GV, Z_GR = 4, 5, 6
Z_TILES = 7
Z_WIDTH = Z_TILES * COL_TILE
ZT_TILES = 3
SM_IW0 = 0
SM_GA0 = N_IDX_HEADS

TOKEN_TILE = 512
FF_TILE = 256
DSA_BLOCK = 256
GLA_TILE = 512
GLA_IN_W = 4 * GLA_K_W
SEARCH_MAX_ITERS = 24
SEARCH_FINISH_FROM = 4

NEG_BIG = -1e30
F32_LOWEST = float(np.finfo(np.float32).min)
F32_TINY = float(np.finfo(np.float32).tiny)
INT32_MIN = int(np.iinfo(np.int32).min)

f32 = jnp.float32
bf16 = jnp.bfloat16


def _rms(x, g):
    return x * lax.rsqrt(jnp.mean(x * x, axis=-1, keepdims=True) + EPS) * g


def _sigmoid(x):
    return 1.0 / (1.0 + jnp.exp(-x))


def _cparams(sem):
    return pltpu.CompilerParams(dimension_semantics=sem, vmem_limit_bytes=VMEM_LIMIT_BYTES)


def _resident(a):
    return pl.BlockSpec(a.shape, lambda i: (0, 0), pipeline_mode=pl.Buffered(1))


def _swiglu_half_step(x, g_ref, wg_ref, wu_ref, wd_ref):
    h = _rms(x, g_ref[...]).astype(bf16)
    y = x
    for f in range(D_FF // FF_TILE):
        cols = slice(f * FF_TILE, (f + 1) * FF_TILE)
        gate = jnp.dot(h, wg_ref[:, cols], preferred_element_type=f32)
        up = jnp.dot(h, wu_ref[:, cols], preferred_element_type=f32)
        a = (gate * _sigmoid(gate) * up).astype(bf16)
        y = y + 0.5 * jnp.dot(a, wd_ref[cols, :], preferred_element_type=f32)
    return y


def _ffn_kernel(x_ref, g_ref, wg_ref, wu_ref, wd_ref, o_ref):
    o_ref[...] = _swiglu_half_step(x_ref[...], g_ref, wg_ref, wu_ref, wd_ref)


def _token_rows(shape):
    if len(shape) == 2:
        return pl.BlockSpec((TOKEN_TILE, shape[1]), lambda i: (i, 0))
    per_b = shape[1] // TOKEN_TILE
    return pl.BlockSpec((None, TOKEN_TILE, shape[2]), lambda i: (i // per_b, i % per_b, 0))


def _ffn(x, g, wg, wu, wd):
    t = x.size // D_MODEL
    return pl.pallas_call(
        _ffn_kernel,
        grid=(t // TOKEN_TILE,),
        in_specs=[
            _token_rows(x.shape),
            _resident(g), _resident(wg), _resident(wu), _resident(wd),
        ],
        out_specs=pl.BlockSpec((TOKEN_TILE, D_MODEL), lambda i: (i, 0)),
        out_shape=jax.ShapeDtypeStruct((t, D_MODEL), f32),
        compiler_params=_cparams(("parallel",)),
        name="ffn",
    )(x, g, wg, wu, wd)


def _gla_operands(q, k, ga, w2, gb):
    n = q.shape[0]
    c = CHUNK
    def split(a):
        hi = a.astype(bf16)
        return hi, (a - hi.astype(f32)).astype(bf16)

    dot = lambda a, b: jnp.dot(a, b, preferred_element_type=f32)
    ga_hi, ga_lo = split(ga)
    w2_hi, w2_lo = split(w2)
    glin = dot(ga_hi, w2_hi) + dot(ga_hi, w2_lo) + dot(ga_lo, w2_hi) + gb
    logg = (jnp.minimum(glin, 0.0) - jnp.log(1.0 + jnp.exp(-jnp.abs(glin)))) / GLA_GATE_TAU
    row = lax.broadcasted_iota(jnp.int32, (n, n), 0)
    col = lax.broadcasted_iota(jnp.int32, (n, n), 1)
    tril = jnp.where((row // c == col // c) & (col <= row), 1.0, 0.0).astype(bf16)
    hi, lo = split(logg)
    bcum = dot(tril, hi) + dot(tril, lo)
    by_chunk = bcum.reshape(n // c, c, GLA_K_W)
    spread = lambda r: jnp.broadcast_to(r, (n // c, c, GLA_K_W)).reshape(n, GLA_K_W)
    b_last = by_chunk[:, c - 1:c, :]
    b_mid = spread(by_chunk[:, c // 2 - 1:c // 2, :])
    operands = jnp.concatenate(
        [q * jnp.exp(bcum), q * jnp.exp(bcum - b_mid), k * jnp.exp(b_mid - bcum),
         k * jnp.exp(spread(b_last) - bcum)], axis=1)
    return operands, jnp.exp(b_last.reshape(n // c, GLA_K_W))


def _proj_kernel(x_ref, g_ref, w_ref, wt_ref, wg_ref, ws_ref, w2_ref, gb_ref,
                 z_ref, qT_ref, vT_ref, iqT_ref, ik_ref, iwT_ref, gin_ref, dec_ref):
    h = _rms(x_ref[...], g_ref[...]).astype(bf16)
    for j in range(Z_TILES):
        cols = slice(j * COL_TILE, (j + 1) * COL_TILE)
        z_ref[:, cols] = jnp.dot(h, w_ref[:, cols], preferred_element_type=f32).astype(bf16)
    for j, out_ref in enumerate((qT_ref, vT_ref, iqT_ref)):
        cols = slice(j * COL_TILE, (j + 1) * COL_TILE)
        zt = jnp.dot(h, wt_ref[:, cols], preferred_element_type=f32)
        out_ref[...] = zt.T.astype(bf16)
    s = jnp.dot(h, ws_ref[...], preferred_element_type=f32)
    ik_ref[...] = s[:, :IDX_HEAD_DIM].astype(bf16)
    sm = s[:, LANES:]
    iwT_ref[...] = sm.T[SM_IW0:SM_IW0 + N_IDX_HEADS, :]
    gqk = jnp.dot(h, wg_ref[...], preferred_element_type=f32)
    operands, decay = _gla_operands(gqk[:, :GLA_K_W], gqk[:, GLA_K_W:],
                                    sm[:, SM_GA0:SM_GA0 + GLA_GATE_RANK],
                                    w2_ref[...], gb_ref[...])
    gin_ref[...] = operands.astype(bf16)
    dec_ref[...] = decay


def _proj(x, g, w_main, w_t, w_g, w_small, w2, gb, batch):
    t = x.shape[0]
    tm = TOKEN_TILE
    s = t // batch
    per_b = s // tm
    fm = lambda rows: pl.BlockSpec((None, rows, tm), lambda i: (i // per_b, 0, i % per_b))
    return pl.pallas_call(
        _proj_kernel,
        grid=(t // tm,),
        in_specs=[
            pl.BlockSpec((tm, D_MODEL), lambda i: (i, 0)),
            _resident(g), _resident(w_main), _resident(w_t), _resident(w_g), _resident(w_small),
            _resident(w2), _resident(gb),
        ],
        out_specs=[
            pl.BlockSpec((tm, Z_WIDTH), lambda i: (i, 0)),
            fm(ATTN_W), fm(ATTN_W), fm(IDX_Q_W),
            pl.BlockSpec((tm, IDX_HEAD_DIM), lambda i: (i, 0)),
            fm(N_IDX_HEADS),
            pl.BlockSpec((tm, GLA_IN_W), lambda i: (i, 0)),
            pl.BlockSpec((tm // CHUNK, GLA_K_W), lambda i: (i, 0)),
        ],
        out_shape=[
            jax.ShapeDtypeStruct((t, Z_WIDTH), bf16),
            jax.ShapeDtypeStruct((batch, ATTN_W, s), bf16),
            jax.ShapeDtypeStruct((batch, ATTN_W, s), bf16),
            jax.ShapeDtypeStruct((batch, IDX_Q_W, s), bf16),
            jax.ShapeDtypeStruct((t, IDX_HEAD_DIM), bf16),
            jax.ShapeDtypeStruct((batch, N_IDX_HEADS, s), f32),
            jax.ShapeDtypeStruct((t, GLA_IN_W), bf16),
            jax.ShapeDtypeStruct((t // CHUNK, GLA_K_W), f32),
        ],
        compiler_params=_cparams(("parallel",)),
        name="proj",
    )(x, g, w_main, w_t, w_g, w_small, w2, gb)


def _dsa_kernel(qT_ref, iqT_ref, iwT_ref, tab_ref, k_ref, vT_ref, ik_ref, o_ref,
                ibuf, qTm, s_ref, bias_ref, m_ref, l_ref, alpha_ref, acc_ref, thr_ref,
                ext_ref, pending_ref, finish_ref, *, seq_len):
    qb = DSA_BLOCK
    kt_rows = DSA_BLOCK
    i = pl.program_id(1)
    n_tiles = i + 1

    rowid = lax.broadcasted_iota(jnp.int32, (LANES, qb), 0)
    for h in range(N_ATTN_HEADS):
        pair = qT_ref[(h // 2) * LANES:(h // 2 + 1) * LANES, :]
        keep = (rowid < ATTN_HEAD_DIM) if h % 2 == 0 else (rowid >= ATTN_HEAD_DIM)
        qTm[h] = jnp.where(keep, pair, jnp.zeros_like(pair))
    l_ref[...] = jnp.zeros_like(l_ref)
    acc_ref[...] = jnp.zeros_like(acc_ref)

    def fold(x, op):
        return op(x.reshape(kt_rows // SUBLANES, SUBLANES, qb), axis=0)

    def idx_tile(kt, diagonal, stats):
        r0 = pl.multiple_of(kt * kt_rows, kt_rows)
        ki_t = ik_ref[pl.ds(r0, kt_rows), :]
        sc = jnp.zeros((kt_rows, qb), f32)
        for h in range(N_IDX_HEADS):
            s = jnp.dot(ki_t, iqT_ref[h * IDX_HEAD_DIM:(h + 1) * IDX_HEAD_DIM, :],
                        preferred_element_type=f32)
            sc = sc + iwT_ref[h:h + 1, :] * jnp.maximum(s, 0.0)
        sc_hi = sc_lo = sc_0 = sc
        if diagonal:
            kc = lax.broadcasted_iota(jnp.int32, (kt_rows, qb), 0) // CHUNK
            qc = lax.broadcasted_iota(jnp.int32, (kt_rows, qb), 1) // CHUNK
            adm = kc <= qc
            sc_hi = jnp.where(adm, sc, -jnp.inf)
            sc_lo = jnp.where(adm, sc, jnp.inf)
            sc_0 = jnp.where(adm, sc, 0.0)
        ibuf[pl.ds(r0, kt_rows), :] = sc_hi
        mx8, mn8, s8, ss8 = stats
        return (jnp.maximum(mx8, fold(sc_hi, jnp.max)), jnp.minimum(mn8, fold(sc_lo, jnp.min)),
                s8 + fold(sc_0, jnp.sum), ss8 + fold(sc_0 * sc_0, jnp.sum))

    stats0 = (jnp.full((SUBLANES, qb), -jnp.inf, f32), jnp.full((SUBLANES, qb), jnp.inf, f32),
              jnp.zeros((SUBLANES, qb), f32), jnp.zeros((SUBLANES, qb), f32))
    odd = i % 2
    stats = lax.cond(odd == 1, lambda st: idx_tile(0, False, st), lambda st: st, stats0)
    stats = lax.fori_loop(
        0, i // 2,
        lambda j, st: idx_tile(odd + 2 * j + 1, False, idx_tile(odd + 2 * j, False, st)),
        stats)
    mx8, mn8, s8, ss8 = idx_tile(i, True, stats)
    col_max = jnp.max(mx8, axis=0, keepdims=True)
    col_min = jnp.min(mn8, axis=0, keepdims=True)
    col_sum = jnp.sum(s8, axis=0, keepdims=True)
    col_ssq = jnp.sum(ss8, axis=0, keepdims=True)

    def count(pred):
        def body(r, acc):
            r0 = pl.multiple_of(r * kt_rows, kt_rows)
            tile = ibuf[pl.ds(r0, kt_rows), :]
            rows = r0 + lax.broadcasted_iota(jnp.int32, (kt_rows, qb), 0)
            ind = jnp.where(pred(tile, rows), 1.0, 0.0)
            part = ind.reshape(kt_rows // SUBLANES, SUBLANES, qb)
            while part.shape[0] > 1:
                half = part.shape[0] // 2
                part = part[:half] + part[half:]
            return acc + part[0]
        acc = jnp.zeros((SUBLANES, qb), f32)
        acc = lax.cond(i % 2 == 0, lambda a: body(0, a), lambda a: a, acc)
        first = 1 - i % 2
        acc = lax.fori_loop(
            0, (i + 1) // 2,
            lambda j, a: body(first + 2 * j + 1, body(first + 2 * j, a)), acc)
        return jnp.sum(acc, axis=0, keepdims=True)

    def resolve_ties(tv, need, mult):
        is_tie = tv < jnp.inf
        n_keys = n_tiles * kt_rows

        def tie_cond(st):
            return st["active"] > 0

        def tie_body(st):
            plo, phi, flo, fhi = st["plo"], st["phi"], st["flo"], st["fhi"]
            span = phi - plo
            est = ((need - flo) / jnp.maximum(fhi - flo, 1.0) * span.astype(f32)).astype(jnp.int32)
            pick = jnp.where(st["bisect"] == 2, span // 2, est)
            cand = plo + jnp.clip(pick, 1, jnp.maximum(span - 1, 1))
            f = count(lambda tile, rows: jnp.where(tile == tv, rows, seq_len) < cand)
            lower = f < need
            exact = f == need
            plo = jnp.where(lower, cand, jnp.where(exact, cand - 1, plo))
            phi = jnp.where(lower, phi, cand)
            flo = jnp.where(lower, f, flo)
            fhi = jnp.where(lower, fhi, f)
            unsplit = jnp.where(is_tie & (phi - plo > 1), 1, 0)
            return dict(active=jnp.max(unsplit), bisect=(st["bisect"] + 1) % 3,
                        plo=plo, phi=phi, flo=flo, fhi=fhi)

        split = lax.while_loop(tie_cond, tie_body, dict(
            active=jnp.int32(1), bisect=jnp.int32(0),
            plo=jnp.zeros((1, qb), jnp.int32), phi=jnp.zeros((1, qb), jnp.int32) + n_keys,
            flo=jnp.zeros((1, qb), f32), fhi=mult))["phi"]

        def drop_body(r, carry):
            r0 = pl.multiple_of(r * kt_rows, kt_rows)
            tile = ibuf[pl.ds(r0, kt_rows), :]
            rows = r0 + lax.broadcasted_iota(jnp.int32, (kt_rows, qb), 0)
            dropped = jnp.where(tile == tv, rows, -1) >= split
            ibuf[pl.ds(r0, kt_rows), :] = jnp.where(dropped, -jnp.inf, tile)
            return carry

        lax.fori_loop(0, n_tiles, drop_body, 0)

    qpos = i * qb + lax.broadcasted_iota(jnp.int32, (1, qb), 1)
    n_adm = (qpos // CHUNK + 1) * CHUNK
    select_all = n_adm <= TOPK_MAX
    kf = float(TOPK_MAX)

    thr_ref[...] = jnp.full((1, qb), F32_LOWEST, f32)
    pending_ref[0] = jnp.int32(0)

    @pl.when(n_tiles * kt_rows > TOPK_MAX)
    def _():
        n_f = n_adm.astype(f32)
        mean = col_sum / n_f
        sigma = jnp.sqrt(jnp.maximum(col_ssq / n_f - mean * mean, 0.0))
        log_slope = tab_ref[1:2, :] * sigma

        def inside(c, lo, hi):
            return (c > lo) & (c < hi)

        c0 = mean + tab_ref[0:1, :] * sigma
        c0 = jnp.where(inside(c0, col_min, col_max), c0, 0.5 * col_min + 0.5 * col_max)
        zero = jnp.zeros((1, qb), f32)
        one = jnp.ones((1, qb), f32)
        state0 = dict(it=jnp.int32(0), active=jnp.int32(1), c=c0, lo=col_min, hi=col_max,
                      glo=n_f, ghi=zero, tlo=zero, thi=zero, boost=one, side=zero,
                      thr=jnp.full((1, qb), F32_LOWEST, f32),
                      open=jnp.where(select_all, 0.0, 1.0), tie=zero, kind=zero, emin=zero)

        def search_cond(st):
            return (st["it"] < SEARCH_MAX_ITERS) & (st["active"] > 0)

        def search_body(st):
            c, lo, hi = st["c"], st["lo"], st["hi"]
            g = count(lambda tile, rows: tile >= c)
            hit = (g == kf) & (st["open"] > 0.0)
            thr = jnp.where(hit, c, st["thr"])
            still = jnp.where(hit, 0.0, st["open"])
            above = g > kf
            lo = jnp.where(above, c, lo)
            hi = jnp.where(above, hi, c)
            glo = jnp.where(above, g, st["glo"])
            ghi = jnp.where(above, st["ghi"], g)
            tlo = jnp.where(above, 1.0, st["tlo"])
            thi = jnp.where(above, st["thi"], 1.0)
            side = jnp.where(above, 1.0, -1.0)
            repeat = side == st["side"]
            boost = jnp.where(repeat, 2.0 * st["boost"], 1.0)
            mid = 0.5 * lo + 0.5 * hi
            stale = 1.0 / jnp.minimum(boost, 256.0)
            log_lo = jnp.log(jnp.maximum(glo, 0.5) / kf)
            log_hi = jnp.log(jnp.maximum(ghi, 0.5) / kf)
            w_lo = log_lo * jnp.where(above, 1.0, stale)
            w_hi = -log_hi * jnp.where(above, stale, 1.0)
            c_two = lo + w_lo / (w_lo + w_hi) * (hi - lo)
            reach = boost * log_slope
            c_one = jnp.where(tlo > 0.0, lo + reach * log_lo, hi + reach * log_hi)
            c_one = jnp.where(inside(c_one, lo, hi), c_one, mid)
            both = (tlo > 0.0) & (thi > 0.0)
            c_new = jnp.where(both, c_two, c_one)
            c_new = jnp.where(both & (lo < 0.0) & (hi > 0.0), 0.0, c_new)
            c_new = jnp.where((lo == 0.0) & (hi > F32_TINY), F32_TINY, c_new)
            movable = inside(c_new, lo, hi) & ~((lo == 0.0) & (hi <= F32_TINY))
            missed = (still > 0.0) & (((st["kind"] == 1.0) & above) |
                                      ((st["kind"] == 2.0) & (g < kf)))
            lo = jnp.where(missed & (st["kind"] == 2.0), st["emin"], lo)
            closed = (~movable & both & (still > 0.0)) | missed
            tie = jnp.where(closed, 1.0, st["tie"])
            still = jnp.where(closed, 0.0, still)
            searching = jnp.where(movable, still, 0.0)

            from_hi = both & (kf - ghi == 1.0)
            from_lo = both & (glo - kf == 1.0) & ~from_hi
            ready = jnp.where(searching > 0.0, jnp.where(from_hi | from_lo, 1.0, 2.0), 0.0)
            code = jnp.max(ready)
            ext_ref[...] = jnp.zeros_like(ext_ref)
            finish_ref[0] = jnp.int32(0)

            @pl.when((code == 1.0) & (st["it"] + 1 >= SEARCH_FINISH_FROM))
            def _():
                def ext_body(r, carry):
                    r0 = pl.multiple_of(r * kt_rows, kt_rows)
                    tile = ibuf[pl.ds(r0, kt_rows), :]
                    below = fold(jnp.where(tile < hi, tile, -jnp.inf), jnp.max)
                    above_lo = fold(jnp.where(tile >= lo, tile, jnp.inf), jnp.min)
                    return jnp.maximum(carry[0], below), jnp.minimum(carry[1], above_lo)
                mx8, mn8 = lax.fori_loop(0, n_tiles, ext_body,
                                         (jnp.full((SUBLANES, qb), -jnp.inf, f32),
                                          jnp.full((SUBLANES, qb), jnp.inf, f32)))
                ext_ref[0:1, :] = jnp.max(mx8, axis=0, keepdims=True)
                smallest = jnp.min(mn8, axis=0, keepdims=True)
                ext_ref[1:2, :] = smallest

                def next_body(r, carry):
                    r0 = pl.multiple_of(r * kt_rows, kt_rows)
                    tile = ibuf[pl.ds(r0, kt_rows), :]
                    return jnp.minimum(carry, fold(jnp.where(tile > smallest, tile, jnp.inf),
                                                   jnp.min))
                nx8 = lax.fori_loop(0, n_tiles, next_body,
                                    jnp.full((SUBLANES, qb), jnp.inf, f32))
                ext_ref[2:3, :] = jnp.min(nx8, axis=0, keepdims=True)
                finish_ref[0] = jnp.int32(1)

            finishing = finish_ref[0] > 0
            emax = ext_ref[0:1, :]
            emin = ext_ref[1:2, :]
            next_up = ext_ref[2:3, :]
            kind = jnp.where(finishing & (searching > 0.0),
                             jnp.where(from_hi, 1.0, jnp.where(from_lo, 2.0, 0.0)), 0.0)
            c_new = jnp.where(kind == 1.0, emax, jnp.where(kind == 2.0, next_up, c_new))
            return dict(it=st["it"] + 1, active=(code > 0.0).astype(jnp.int32),
                        c=c_new, lo=lo, hi=hi, glo=glo, ghi=ghi, tlo=tlo, thi=thi,
                        boost=boost, side=side, thr=thr, open=still, tie=tie,
                        kind=kind, emin=emin)

        final = lax.while_loop(search_cond, search_body, state0)
        tied = final["tie"] > 0.0
        thr_ref[...] = jnp.where(tied, final["lo"], final["thr"])
        pending_ref[0] = (jnp.max(final["open"]) > 0.0).astype(jnp.int32)

        @pl.when(jnp.max(final["tie"]) > 0.0)
        def _():
            resolve_ties(jnp.where(tied, final["lo"], jnp.inf), kf - final["ghi"],
                         final["glo"] - final["ghi"])

    @pl.when(pending_ref[0] > 0)
    def _():
        def key_to_f32(key):
            bits = jnp.where(key < 0, key ^ jnp.int32(0x7FFFFFFF), key)
            return lax.bitcast_convert_type(bits, f32)

        def bit_body(t, prefix):
            step = lax.shift_left(jnp.int32(1), jnp.int32(31) - t)
            cand = prefix + step
            cand_f = key_to_f32(cand)
            cnt = count(lambda tile, rows: tile >= cand_f)
            return jnp.where(cnt >= kf, cand, prefix)

        prefix = lax.fori_loop(0, 32, bit_body, jnp.full((1, qb), INT32_MIN, jnp.int32))
        thr = jnp.where(select_all, F32_LOWEST, key_to_f32(prefix))
        thr_ref[...] = thr

        cnt_ge = count(lambda tile, rows: tile >= thr)
        excess = jnp.where(select_all, 0.0, cnt_ge - kf)

        @pl.when(jnp.max(excess) > 0.0)
        def _():
            cnt_gt = count(lambda tile, rows: tile > thr)
            resolve_ties(jnp.where(excess > 0.0, thr, jnp.inf), kf - cnt_gt, cnt_ge - cnt_gt)

    thr = thr_ref[...]
    ones_rows = jnp.ones((2 * SUBLANES, kt_rows), bf16)

    m_ref[1] = jnp.full(m_ref.shape[1:], NEG_BIG, f32)

    def logits_stage(kt, slot):
        r0 = pl.multiple_of(kt * kt_rows, kt_rows)
        bias_ref[...] = jnp.where(ibuf[pl.ds(r0, kt_rows), :] >= thr, 0.0, NEG_BIG)
        for h in range(N_ATTN_HEADS):
            k_pair = k_ref[pl.ds(r0, kt_rows), (h // 2) * LANES:(h // 2 + 1) * LANES]
            s_ref[slot, h] = jnp.dot(k_pair, qTm[h], preferred_element_type=f32) + bias_ref[...]
            m_old = m_ref[1 - slot, h]
            m_new = jnp.maximum(m_old, jnp.max(s_ref[slot, h], axis=0, keepdims=True))
            alpha_ref[slot, h] = jnp.exp2(m_old - m_new)
            m_ref[slot, h] = m_new

    def values_stage(kt, slot):
        r0 = pl.multiple_of(kt * kt_rows, kt_rows)
        for h in range(N_ATTN_HEADS):
            p = jnp.exp2(s_ref[slot, h] - m_ref[slot, h]).astype(bf16)
            hs = slice(h * ATTN_HEAD_DIM, (h + 1) * ATTN_HEAD_DIM)
            lhs = jnp.concatenate([vT_ref[hs, pl.ds(r0, kt_rows)], ones_rows], axis=0)
            pv = jnp.dot(lhs, p, preferred_element_type=f32)
            alpha = alpha_ref[slot, h]
            acc_ref[hs, :] = alpha * acc_ref[hs, :] + pv[:ATTN_HEAD_DIM]
            l_ref[h] = alpha * l_ref[h] + pv[ATTN_HEAD_DIM:ATTN_HEAD_DIM + 1]

    def att_body(j, carry):
        kt = 2 * j
        logits_stage(kt + 1, 1)
        values_stage(kt, 0)
        logits_stage(kt + 2, 0)
        values_stage(kt + 1, 1)
        return carry

    logits_stage(0, 0)
    lax.fori_loop(0, i // 2, att_body, 0)

    @pl.when(i % 2 == 1)
    def _():
        logits_stage(i, 1)
        values_stage(i - 1, 0)
        values_stage(i, 1)

    @pl.when(i % 2 == 0)
    def _():
        values_stage(i, 0)

    for h in range(N_ATTN_HEADS):
        hs = slice(h * ATTN_HEAD_DIM, (h + 1) * ATTN_HEAD_DIM)
        acc_ref[hs, :] = acc_ref[hs, :] / l_ref[h]
    o_ref[...] = acc_ref[...].T.astype(o_ref.dtype)


def _search_hints(seq_len):
    n = (np.arange(seq_len) // CHUNK + 1) * CHUNK
    frac = np.minimum(TOPK_MAX / n, 0.5)
    nd = statistics.NormalDist()
    z_of = {f: nd.inv_cdf(1.0 - f) for f in np.unique(frac)}
    z = np.array([z_of[f] for f in frac])
    pdf = np.exp(-0.5 * z * z) / np.sqrt(2.0 * np.pi)
    return jnp.asarray(np.stack([z, frac / pdf]), f32)


def _dsa(qT, iqT, iwT, z3, vT, ik3):
    b, _, s = qT.shape
    qb = DSA_BLOCK
    return pl.pallas_call(
        functools.partial(_dsa_kernel, seq_len=s),
        grid=(b, s // qb),
        in_specs=[
            pl.BlockSpec((None, ATTN_W, qb), lambda bi, i: (bi, 0, i)),
            pl.BlockSpec((None, IDX_Q_W, qb), lambda bi, i: (bi, 0, i)),
            pl.BlockSpec((None, N_IDX_HEADS, qb), lambda bi, i: (bi, 0, i)),
            pl.BlockSpec((2, qb), lambda bi, i: (0, i)),
            pl.BlockSpec((None, s, COL_TILE), lambda bi, i: (bi, 0, Z_AK)),
            pl.BlockSpec((None, ATTN_W, s), lambda bi, i: (bi, 0, 0)),
            pl.BlockSpec((None, s, IDX_HEAD_DIM), lambda bi, i: (bi, 0, 0)),
        ],
        out_specs=pl.BlockSpec((None, qb, ATTN_W), lambda bi, i: (bi, i, 0)),
        out_shape=jax.ShapeDtypeStruct((b, s, ATTN_W), bf16),
        scratch_shapes=[
            pltpu.VMEM((s, qb), f32),
            pltpu.VMEM((N_ATTN_HEADS, LANES, qb), bf16),
            pltpu.VMEM((2, N_ATTN_HEADS, qb, qb), f32),
            pltpu.VMEM((qb, qb), f32),
            pltpu.VMEM((2, N_ATTN_HEADS, 1, qb), f32),
            pltpu.VMEM((N_ATTN_HEADS, 1, qb), f32),
            pltpu.VMEM((2, N_ATTN_HEADS, 1, qb), f32),
            pltpu.VMEM((ATTN_W, qb), f32),
            pltpu.VMEM((1, qb), f32),
            pltpu.VMEM((3, qb), f32),
            pltpu.SMEM((1,), jnp.int32),
            pltpu.SMEM((1,), jnp.int32),
        ],
        compiler_params=_cparams(("parallel", "arbitrary")),
        name="dsa",
    )(qT, iqT, iwT, _search_hints(s), z3, vT, ik3)


def _gla_kernel(gin_ref, dec_ref, v_ref, r_ref, gn_ref, o_ref, state_ref):
    c = CHUNK
    n_batch = gin_ref.shape[0]

    @pl.when(pl.program_id(0) == 0)
    def _():
        state_ref[...] = jnp.zeros_like(state_ref)

    row = lax.broadcasted_iota(jnp.int32, (c, c), 0)
    col = lax.broadcasted_iota(jnp.int32, (c, c), 1)
    causal = row >= col
    lane = lax.broadcasted_iota(jnp.int32, (c, LANES), 1)
    half = (lane < GLA_HEAD_K, lane >= GLA_HEAD_K)
    contract_last = (((1,), (1,)), ((), ()))
    contract_first = (((0,), (0,)), ((), ()))
    zero = jnp.zeros((c, LANES), bf16)

    for ci, bi in [(ci, bi) for ci in range(GLA_TILE // c) for bi in range(n_batch)]:
        rows = slice(ci * c, (ci + 1) * c)
        decay = dec_ref[bi, ci:ci + 1, :]
        for pr in range(N_GLA_HEADS // 2):
            ps = slice(pr * LANES, (pr + 1) * LANES)
            part = lambda g: gin_ref[bi, rows, g * GLA_K_W + pr * LANES:
                                     g * GLA_K_W + (pr + 1) * LANES]
            q_in, q_mid, k_mid, k_out = part(0), part(1), part(2), part(3)
            st = state_ref[bi, pr]
            st_b = st.astype(bf16)
            upd = jnp.zeros_like(st)
            for hh in range(2):
                h = pr * 2 + hh
                vs = slice(h * GLA_HEAD_V, (h + 1) * GLA_HEAD_V)
                v_h = v_ref[bi, rows, vs]
                attn = lax.dot_general(jnp.where(half[hh], q_mid, zero), k_mid, contract_last,
                                       preferred_element_type=f32)
                attn = jnp.where(causal, attn, 0.0).astype(bf16)
                o = lax.dot_general(jnp.where(half[hh], q_in, zero), st_b, contract_last,
                                    preferred_element_type=f32)
                o = o + jnp.dot(attn, v_h, preferred_element_type=f32)
                upd = upd + lax.dot_general(v_h, jnp.where(half[hh], k_out, zero),
                                            contract_first, preferred_element_type=f32)
                y = _rms(o, gn_ref[...])
                r = r_ref[bi, rows, vs].astype(f32)
                o_ref[bi, rows, vs] = (y * r * _sigmoid(r)).astype(o_ref.dtype)
            state_ref[bi, pr] = decay[:, ps] * st + upd


def _gla(gin3, dec3, z3, gn):
    b, s, _ = z3.shape
    tb = GLA_TILE
    return pl.pallas_call(
        _gla_kernel,
        grid=(s // tb,),
        in_specs=[
            pl.BlockSpec((b, tb, GLA_IN_W), lambda j: (0, j, 0)),
            pl.BlockSpec((b, tb // CHUNK, GLA_K_W), lambda j: (0, j, 0)),
            pl.BlockSpec((b, tb, COL_TILE), lambda j: (0, j, Z_GV)),
            pl.BlockSpec((b, tb, COL_TILE), lambda j: (0, j, Z_GR)),
            pl.BlockSpec((1, GLA_HEAD_V), lambda j: (0, 0)),
        ],
        out_specs=pl.BlockSpec((b, tb, GLA_V_W), lambda j: (0, j, 0)),
        out_shape=jax.ShapeDtypeStruct((b, s, GLA_V_W), bf16),
        scratch_shapes=[pltpu.VMEM((b, N_GLA_HEADS // 2, GLA_HEAD_V, LANES), f32)],
        compiler_params=_cparams(("arbitrary",)),
        name="gla",
    )(gin3, dec3, z3, z3, gn)


def _tail_kernel(x_ref, oa_ref, ob_ref, ga_ref, gb_ref, wa_ref, wb_ref, wo_ref,
                 g_ref, wg_ref, wu_ref, wd_ref, p_ref, gp_ref, wpg_ref, wpp_ref, gf_ref,
                 o_ref, *, final):
    ya = jnp.dot(oa_ref[...], wa_ref[...], preferred_element_type=f32)
    yb = jnp.dot(ob_ref[...], wb_ref[...], preferred_element_type=f32)
    mg = _sigmoid(ga_ref[...].astype(f32)) * ya + _sigmoid(gb_ref[...].astype(f32)) * yb
    x = x_ref[...] + jnp.dot(mg.astype(bf16), wo_ref[...], preferred_element_type=f32)
    x = _swiglu_half_step(x, g_ref, wg_ref, wu_ref, wd_ref)
    h = _rms(x, gp_ref[...]).astype(bf16)
    gate = _sigmoid(jnp.dot(h, wpg_ref[...], preferred_element_type=f32))
    e = jnp.dot(p_ref[...].astype(bf16), wpp_ref[...], preferred_element_type=f32)
    y = x + gate * e
    if final:
        y = _rms(y, gf_ref[...])
    o_ref[...] = y


def _tail(x, oa, ob, z, wa, wb, wo, g, wg, wu, wd, p, gp, wpg, wpp, gf, final, out_shape):
    t = x.shape[0]
    tm = TOKEN_TILE
    rows = lambda width, col=0: pl.BlockSpec((tm, width), lambda i: (i, col))
    return pl.pallas_call(
        functools.partial(_tail_kernel, final=final),
        grid=(t // tm,),
        in_specs=[
            rows(D_MODEL), rows(ATTN_W), rows(GLA_V_W), rows(D_MODEL, Z_MGA), rows(D_MODEL, Z_MGB),
            _resident(wa), _resident(wb), _resident(wo),
            _resident(g), _resident(wg), _resident(wu), _resident(wd),
            rows(PLE_DIM), _resident(gp), _resident(wpg), _resident(wpp), _resident(gf),
        ],
        out_specs=_token_rows(out_shape),
        out_shape=jax.ShapeDtypeStruct(out_shape, f32),
        compiler_params=_cparams(("parallel",)),
        name="tail",
    )(x, oa, ob, z, z, wa, wb, wo, g, wg, wu, wd, p, gp, wpg, wpp, gf)


def _split_w_in(w_in):
    cuts = np.cumsum(np.array(SPLIT_SIZES))[:-1].tolist()
    aq, ak, av, iq, ik, iw, gq, gk, gv, gr, ga, mga, mgb = jnp.split(w_in, cuts, axis=-1)
    att_scale = ATTN_HEAD_DIM ** -0.5 * float(np.log2(np.e))
    idx_scale = IDX_HEAD_DIM ** -0.5 * N_IDX_HEADS ** -0.5
    c = lambda w: w.astype(bf16)
    main = jnp.concatenate([c(mga), c(mgb), c(ak), c(gv), c(gr)], axis=-1)
    feature_major = jnp.concatenate([c(aq * att_scale), c(av), c(iq)], axis=-1)
    gla_qk = jnp.concatenate([c(gq * (GLA_HEAD_K ** -0.5)), c(gk)], axis=-1)
    zeros = lambda n: jnp.zeros(w_in.shape[:-1] + (n,), bf16)
    small = jnp.concatenate(
        [c(ik), zeros(LANES - IDX_HEAD_DIM), c(iw * idx_scale), c(ga),
         zeros(LANES - N_IDX_HEADS - GLA_GATE_RANK)], axis=-1)
    return main, feature_major, gla_qk, small


def kernel(x, p, w_in, gla_gate_w2, gla_gate_b, gla_norm, w_branch_a, w_branch_b, w_out,
           norm_ff1, norm_mix, norm_ff2, norm_ple, ff1_w_gate, ff1_w_up, ff1_w_down,
           ff2_w_gate, ff2_w_up, ff2_w_down, ple_w_proj, ple_w_gate, norm_final):
    b, s, d = x.shape
    t = b * s
    depth = w_in.shape[0]
    assert d == D_MODEL and w_in.shape[1:] == (D_MODEL, sum(SPLIT_SIZES))
    assert s % TOKEN_TILE == 0 and s % GLA_TILE == 0 and s % DSA_BLOCK == 0
    assert ff1_w_gate.shape[1:] == (D_MODEL, D_FF) and D_FF % FF_TILE == 0
    w_main, w_fm, w_gqk, w_small = _split_w_in(w_in)
    cast = lambda w: w.astype(bf16)
    ff1 = (cast(ff1_w_gate), cast(ff1_w_up), cast(ff1_w_down))
    ff2 = (cast(ff2_w_gate), cast(ff2_w_up), cast(ff2_w_down))
    wa, wb, wo = cast(w_branch_a), cast(w_branch_b), cast(w_out)
    wpg, wpp = cast(ple_w_gate), cast(ple_w_proj)
    row = lambda v: v.reshape(1, -1)

    xf = x
    for l in range(depth):
        last = l == depth - 1
        xf = _ffn(xf, row(norm_ff1[l]), ff1[0][l], ff1[1][l], ff1[2][l])

        z, qT, vT, iqT, ik, iwT, gin, dec = _proj(
            xf, row(norm_mix[l]), w_main[l], w_fm[l], w_gqk[l], w_small[l],
            gla_gate_w2[l], row(gla_gate_b[l]), b)
        z3 = z.reshape(b, s, Z_WIDTH)
        oa = _dsa(qT, iqT, iwT, z3, vT, ik.reshape(b, s, IDX_HEAD_DIM))
        ob = _gla(gin.reshape(b, s, GLA_IN_W), dec.reshape(b, s // CHUNK, GLA_K_W), z3,
                  row(gla_norm[l]))
        xf = _tail(xf, oa.reshape(t, ATTN_W), ob.reshape(t, GLA_V_W), z, wa[l], wb[l], wo[l],
                   row(norm_ff2[l]), ff2[0][l], ff2[1][l], ff2[2][l],
                   p[l].reshape(t, PLE_DIM), row(norm_ple[l]), wpg[l], wpp[l],
                   row(norm_final), final=last, out_shape=(b, s, d) if last else (t, d))
    return xf
```

```python
import functools
import statistics

import jax
import jax.numpy as jnp
import numpy as np
from jax import lax
from jax.experimental import pallas as pl
from jax.experimental.pallas import tpu as pltpu

D_MODEL = 1024
D_FF = 2816
PLE_DIM = 256
EPS = 1e-6

CHUNK = 64
N_ATTN_HEADS = 8
ATTN_HEAD_DIM = 64
N_IDX_HEADS = 8
IDX_HEAD_DIM = 64
TOPK_MAX = 256
N_GLA_HEADS = 4
GLA_HEAD_K = 64
GLA_HEAD_V = 128
GLA_GATE_RANK = 16
GLA_GATE_TAU = 16.0

ATTN_W = N_ATTN_HEADS * ATTN_HEAD_DIM
IDX_Q_W = N_IDX_HEADS * IDX_HEAD_DIM
GLA_K_W = N_GLA_HEADS * GLA_HEAD_K
GLA_V_W = N_GLA_HEADS * GLA_HEAD_V
SPLIT_SIZES = (ATTN_W, ATTN_W, ATTN_W, IDX_Q_W, IDX_HEAD_DIM, N_IDX_HEADS,
               GLA_K_W, GLA_K_W, GLA_V_W, GLA_V_W, GLA_GATE_RANK, D_MODEL, D_MODEL)

LANES = 128
SUBLANES = 8
V7X_VMEM_BYTES = 64 * 1024 * 1024
VMEM_LIMIT_BYTES = V7X_VMEM_BYTES // 8 * 7

COL_TILE = 512
Z_MGA, Z_MGB = 0, 1
Z_AK, Z_GV, Z_GR = 4, 5, 6
Z_TILES = 7
Z_WIDTH = Z_TILES * COL_TILE
ZT_TILES = 3
SM_IW0 = 0
SM_GA0 = N_IDX_HEADS

TOKEN_TILE = 512
FF_TILE = 256
DSA_BLOCK = 256
GLA_TILE = 512
GLA_IN_W = 4 * GLA_K_W
SEARCH_MAX_ITERS = 24
SEARCH_FINISH_FROM = 4

NEG_BIG = -1e30
F32_LOWEST = float(np.finfo(np.float32).min)
F32_TINY = float(np.finfo(np.float32).tiny)
INT32_MIN = int(np.iinfo(np.int32).min)

f32 = jnp.float32
bf16 = jnp.bfloat16


def _rms(x, g):
    return x * lax.rsqrt(jnp.mean(x * x, axis=-1, keepdims=True) + EPS) * g


def _sigmoid(x):
    return 1.0 / (1.0 + jnp.exp(-x))


def _cparams(sem):
    return pltpu.CompilerParams(dimension_semantics=sem, vmem_limit_bytes=VMEM_LIMIT_BYTES)


def _resident(a):
    return pl.BlockSpec(a.shape, lambda i: (0, 0), pipeline_mode=pl.Buffered(1))


def _swiglu_half_step(x, g_ref, wg_ref, wu_ref, wd_ref):
    h = _rms(x, g_ref[...]).astype(bf16)
    y = x
    for f in range(D_FF // FF_TILE):
        cols = slice(f * FF_TILE, (f + 1) * FF_TILE)
        gate = jnp.dot(h, wg_ref[:, cols], preferred_element_type=f32)
        up = jnp.dot(h, wu_ref[:, cols], preferred_element_type=f32)
        a = (gate * _sigmoid(gate) * up).astype(bf16)
        y = y + 0.5 * jnp.dot(a, wd_ref[cols, :], preferred_element_type=f32)
    return y


def _ffn_kernel(x_ref, g_ref, wg_ref, wu_ref, wd_ref, o_ref):
    o_ref[...] = _swiglu_half_step(x_ref[...], g_ref, wg_ref, wu_ref, wd_ref)


def _token_rows(shape):
    if len(shape) == 2:
        return pl.BlockSpec((TOKEN_TILE, shape[1]), lambda i: (i, 0))
    per_b = shape[1] // TOKEN_TILE
    return pl.BlockSpec((None, TOKEN_TILE, shape[2]), lambda i: (i // per_b, i % per_b, 0))


def _ffn(x, g, wg, wu, wd):
    t = x.size // D_MODEL
    return pl.pallas_call(
        _ffn_kernel,
        grid=(t // TOKEN_TILE,),
        in_specs=[
            _token_rows(x.shape),
            _resident(g), _resident(wg), _resident(wu), _resident(wd),
        ],
        out_specs=pl.BlockSpec((TOKEN_TILE, D_MODEL), lambda i: (i, 0)),
        out_shape=jax.ShapeDtypeStruct((t, D_MODEL), f32),
        compiler_params=_cparams(("parallel",)),
        name="ffn",
    )(x, g, wg, wu, wd)


def _gla_operands(q, k, ga, w2, gb):
    n = q.shape[0]
    c = CHUNK
    def split(a):
        hi = a.astype(bf16)
        return hi, (a - hi.astype(f32)).astype(bf16)

    dot = lambda a, b: jnp.dot(a, b, preferred_element_type=f32)
    ga_hi, ga_lo = split(ga)
    w2_hi, w2_lo = split(w2)
    glin = dot(ga_hi, w2_hi) + dot(ga_hi, w2_lo) + dot(ga_lo, w2_hi) + gb
    logg = (jnp.minimum(glin, 0.0) - jnp.log(1.0 + jnp.exp(-jnp.abs(glin)))) / GLA_GATE_TAU
    row = lax.broadcasted_iota(jnp.int32, (n, n), 0)
    col = lax.broadcasted_iota(jnp.int32, (n, n), 1)
    tril = jnp.where((row // c == col // c) & (col <= row), 1.0, 0.0).astype(bf16)
    hi, lo = split(logg)
    bcum = dot(tril, hi) + dot(tril, lo)
    by_chunk = bcum.reshape(n // c, c, GLA_K_W)
    spread = lambda r: jnp.broadcast_to(r, (n // c, c, GLA_K_W)).reshape(n, GLA_K_W)
    b_last = by_chunk[:, c - 1:c, :]
    b_mid = spread(by_chunk[:, c // 2 - 1:c // 2, :])
    operands = jnp.concatenate(
        [q * jnp.exp(bcum), q * jnp.exp(bcum - b_mid), k * jnp.exp(b_mid - bcum),
         k * jnp.exp(spread(b_last) - bcum)], axis=1)
    return operands, jnp.exp(b_last.reshape(n // c, GLA_K_W))


def _proj_kernel(x_ref, g_ref, w_ref, wt_ref, wg_ref, ws_ref, w2_ref, gb_ref,
                 z_ref, qT_ref, vT_ref, iqT_ref, ik_ref, iwT_ref, gin_ref, dec_ref):
    h = _rms(x_ref[...], g_ref[...]).astype(bf16)
    for j in range(Z_TILES):
        cols = slice(j * COL_TILE, (j + 1) * COL_TILE)
        z_ref[:, cols] = jnp.dot(h, w_ref[:, cols], preferred_element_type=f32).astype(bf16)
    for j, out_ref in enumerate((qT_ref, vT_ref, iqT_ref)):
        cols = slice(j * COL_TILE, (j + 1) * COL_TILE)
        zt = jnp.dot(h, wt_ref[:, cols], preferred_element_type=f32)
        out_ref[...] = zt.T.astype(bf16)
    s = jnp.dot(h, ws_ref[...], preferred_element_type=f32)
    ik_ref[...] = s[:, :IDX_HEAD_DIM].astype(bf16)
    sm = s[:, LANES:]
    iwT_ref[...] = sm.T[SM_IW0:SM_IW0 + N_IDX_HEADS, :]
    gqk = jnp.dot(h, wg_ref[...], preferred_element_type=f32)
    operands, decay = _gla_operands(gqk[:, :GLA_K_W], gqk[:, GLA_K_W:],
                                    sm[:, SM_GA0:SM_GA0 + GLA_GATE_RANK],
                                    w2_ref[...], gb_ref[...])
    gin_ref[...] = operands.astype(bf16)
    dec_ref[...] = decay


def _proj(x, g, w_main, w_t, w_g, w_small, w2, gb, batch):
    t = x.shape[0]
    tm = TOKEN_TILE
    s = t // batch
    per_b = s // tm
    fm = lambda rows: pl.BlockSpec((None, rows, tm), lambda i: (i // per_b, 0, i % per_b))
    return pl.pallas_call(
        _proj_kernel,
        grid=(t // tm,),
        in_specs=[
            pl.BlockSpec((tm, D_MODEL), lambda i: (i, 0)),
            _resident(g), _resident(w_main), _resident(w_t), _resident(w_g), _resident(w_small),
            _resident(w2), _resident(gb),
        ],
        out_specs=[
            pl.BlockSpec((tm, Z_WIDTH), lambda i: (i, 0)),
            fm(ATTN_W), fm(ATTN_W), fm(IDX_Q_W),
            pl.BlockSpec((tm, IDX_HEAD_DIM), lambda i: (i, 0)),
            fm(N_IDX_HEADS),
            pl.BlockSpec((tm, GLA_IN_W), lambda i: (i, 0)),
            pl.BlockSpec((tm // CHUNK, GLA_K_W), lambda i: (i, 0)),
        ],
        out_shape=[
            jax.ShapeDtypeStruct((t, Z_WIDTH), bf16),
            jax.ShapeDtypeStruct((batch, ATTN_W, s), bf16),
            jax.ShapeDtypeStruct((batch, ATTN_W, s), bf16),
            jax.ShapeDtypeStruct((batch, IDX_Q_W, s), bf16),
            jax.ShapeDtypeStruct((t, IDX_HEAD_DIM), bf16),
            jax.ShapeDtypeStruct((batch, N_IDX_HEADS, s), f32),
            jax.ShapeDtypeStruct((t, GLA_IN_W), bf16),
            jax.ShapeDtypeStruct((t // CHUNK, GLA_K_W), f32),
        ],
        compiler_params=_cparams(("parallel",)),
        name="proj",
    )(x, g, w_main, w_t, w_g, w_small, w2, gb)


def _dsa_kernel(qT_ref, iqT_ref, iwT_ref, iqTn_ref, iwTn_ref, tab_ref, k_ref, vT_ref, ik_ref,
                o_ref, ibuf_all, stats_ref, qTm, s_ref, bias_ref, m_ref, l_ref, alpha_ref,
                acc_ref, thr_ref, ext_ref, pending_ref, finish_ref, *, seq_len):
    qb = DSA_BLOCK
    kt_rows = DSA_BLOCK
    i = pl.program_id(1)
    n_tiles = i + 1

    rowid = lax.broadcasted_iota(jnp.int32, (LANES, qb), 0)
    for h in range(N_ATTN_HEADS):
        pair = qT_ref[(h // 2) * LANES:(h // 2 + 1) * LANES, :]
        keep = (rowid < ATTN_HEAD_DIM) if h % 2 == 0 else (rowid >= ATTN_HEAD_DIM)
        qTm[h] = jnp.where(keep, pair, jnp.zeros_like(pair))
    l_ref[...] = jnp.zeros_like(l_ref)
    acc_ref[...] = jnp.zeros_like(acc_ref)

    def fold(x, op):
        return op(x.reshape(kt_rows // SUBLANES, SUBLANES, qb), axis=0)

    ibuf = ibuf_all.at[i % 2]
    ibuf_next = ibuf_all.at[1 - i % 2]

    def idx_tile(kt, diagonal, dst, iq_ref, iw_ref):
        r0 = pl.multiple_of(kt * kt_rows, kt_rows)
        ki_t = ik_ref[pl.ds(r0, kt_rows), :]
        sc = jnp.zeros((kt_rows, qb), f32)
        for h in range(N_IDX_HEADS):
            s = jnp.dot(ki_t, iq_ref[h * IDX_HEAD_DIM:(h + 1) * IDX_HEAD_DIM, :],
                        preferred_element_type=f32)
            sc = sc + iw_ref[h:h + 1, :] * jnp.maximum(s, 0.0)
        sc_hi = sc_lo = sc_0 = sc
        if diagonal:
            kc = lax.broadcasted_iota(jnp.int32, (kt_rows, qb), 0) // CHUNK
            qc = lax.broadcasted_iota(jnp.int32, (kt_rows, qb), 1) // CHUNK
            adm = kc <= qc
            sc_hi = jnp.where(adm, sc, -jnp.inf)
            sc_lo = jnp.where(adm, sc, jnp.inf)
            sc_0 = jnp.where(adm, sc, 0.0)
        dst[pl.ds(r0, kt_rows), :] = sc_hi
        stats_ref[0] = jnp.maximum(stats_ref[0], fold(sc_hi, jnp.max))
        stats_ref[1] = jnp.minimum(stats_ref[1], fold(sc_lo, jnp.min))
        stats_ref[2] = stats_ref[2] + fold(sc_0, jnp.sum)
        stats_ref[3] = stats_ref[3] + fold(sc_0 * sc_0, jnp.sum)

    def reset_stats():
        stats_ref[0] = jnp.full((SUBLANES, qb), -jnp.inf, f32)
        stats_ref[1] = jnp.full((SUBLANES, qb), jnp.inf, f32)
        stats_ref[2] = jnp.zeros((SUBLANES, qb), f32)
        stats_ref[3] = jnp.zeros((SUBLANES, qb), f32)

    @pl.when(i == 0)
    def _():
        reset_stats()
        idx_tile(0, True, ibuf, iqT_ref, iwT_ref)

    col_max = jnp.max(stats_ref[0], axis=0, keepdims=True)
    col_min = jnp.min(stats_ref[1], axis=0, keepdims=True)
    col_sum = jnp.sum(stats_ref[2], axis=0, keepdims=True)
    col_ssq = jnp.sum(stats_ref[3], axis=0, keepdims=True)
    reset_stats()

    def next_idx_tile(kt):
        idx_tile(kt, False, ibuf_next, iqTn_ref, iwTn_ref)

    def count(pred):
        def body(r, acc):
            r0 = pl.multiple_of(r * kt_rows, kt_rows)
            tile = ibuf[pl.ds(r0, kt_rows), :]
            rows = r0 + lax.broadcasted_iota(jnp.int32, (kt_rows, qb), 0)
            ind = jnp.where(pred(tile, rows), 1.0, 0.0)
            part = ind.reshape(kt_rows // SUBLANES, SUBLANES, qb)
            while part.shape[0] > 1:
                half = part.shape[0] // 2
                part = part[:half] + part[half:]
            return acc + part[0]
        acc = jnp.zeros((SUBLANES, qb), f32)
        acc = lax.cond(i % 2 == 0, lambda a: body(0, a), lambda a: a, acc)
        first = 1 - i % 2
        acc = lax.fori_loop(
            0, (i + 1) // 2,
            lambda j, a: body(first + 2 * j + 1, body(first + 2 * j, a)), acc)
        return jnp.sum(acc, axis=0, keepdims=True)

    def resolve_ties(tv, need, mult):
        is_tie = tv < jnp.inf
        n_keys = n_tiles * kt_rows

        def tie_cond(st):
            return st["active"] > 0

        def tie_body(st):
            plo, phi, flo, fhi = st["plo"], st["phi"], st["flo"], st["fhi"]
            span = phi - plo
            est = ((need - flo) / jnp.maximum(fhi - flo, 1.0) * span.astype(f32)).astype(jnp.int32)
            pick = jnp.where(st["bisect"] == 2, span // 2, est)
            cand = plo + jnp.clip(pick, 1, jnp.maximum(span - 1, 1))
            f = count(lambda tile, rows: jnp.where(tile == tv, rows, seq_len) < cand)
            lower = f < need
            exact = f == need
            plo = jnp.where(lower, cand, jnp.where(exact, cand - 1, plo))
            phi = jnp.where(lower, phi, cand)
            flo = jnp.where(lower, f, flo)
            fhi = jnp.where(lower, fhi, f)
            unsplit = jnp.where(is_tie & (phi - plo > 1), 1, 0)
            return dict(active=jnp.max(unsplit), bisect=(st["bisect"] + 1) % 3,
                        plo=plo, phi=phi, flo=flo, fhi=fhi)

        split = lax.while_loop(tie_cond, tie_body, dict(
            active=jnp.int32(1), bisect=jnp.int32(0),
            plo=jnp.zeros((1, qb), jnp.int32), phi=jnp.zeros((1, qb), jnp.int32) + n_keys,
            flo=jnp.zeros((1, qb), f32), fhi=mult))["phi"]

        def drop_body(r, carry):
            r0 = pl.multiple_of(r * kt_rows, kt_rows)
            tile = ibuf[pl.ds(r0, kt_rows), :]
            rows = r0 + lax.broadcasted_iota(jnp.int32, (kt_rows, qb), 0)
            dropped = jnp.where(tile == tv, rows, -1) >= split
            ibuf[pl.ds(r0, kt_rows), :] = jnp.where(dropped, -jnp.inf, tile)
            return carry

        lax.fori_loop(0, n_tiles, drop_body, 0)

    qpos = i * qb + lax.broadcasted_iota(jnp.int32, (1, qb), 1)
    n_adm = (qpos // CHUNK + 1) * CHUNK
    select_all = n_adm <= TOPK_MAX
    kf = float(TOPK_MAX)

    thr_ref[...] = jnp.full((1, qb), F32_LOWEST, f32)
    pending_ref[0] = jnp.int32(0)

    @pl.when(n_tiles * kt_rows > TOPK_MAX)
    def _():
        n_f = n_adm.astype(f32)
        mean = col_sum / n_f
        sigma = jnp.sqrt(jnp.maximum(col_ssq / n_f - mean * mean, 0.0))
        log_slope = tab_ref[1:2, :] * sigma

        def inside(c, lo, hi):
            return (c > lo) & (c < hi)

        c0 = mean + tab_ref[0:1, :] * sigma
        c0 = jnp.where(inside(c0, col_min, col_max), c0, 0.5 * col_min + 0.5 * col_max)
        zero = jnp.zeros((1, qb), f32)
        one = jnp.ones((1, qb), f32)
        state0 = dict(it=jnp.int32(0), active=jnp.int32(1), c=c0, lo=col_min, hi=col_max,
                      glo=n_f, ghi=zero, tlo=zero, thi=zero, boost=one, side=zero,
                      thr=jnp.full((1, qb), F32_LOWEST, f32),
                      open=jnp.where(select_all, 0.0, 1.0), tie=zero, kind=zero, emin=zero)

        def search_cond(st):
            return (st["it"] < SEARCH_MAX_ITERS) & (st["active"] > 0)

        def search_body(st):
            c, lo, hi = st["c"], st["lo"], st["hi"]
            g = count(lambda tile, rows: tile >= c)
            hit = (g == kf) & (st["open"] > 0.0)
            thr = jnp.where(hit, c, st["thr"])
            still = jnp.where(hit, 0.0, st["open"])
            above = g > kf
            lo = jnp.where(above, c, lo)
            hi = jnp.where(above, hi, c)
            glo = jnp.where(above, g, st["glo"])
            ghi = jnp.where(above, st["ghi"], g)
            tlo = jnp.where(above, 1.0, st["tlo"])
            thi = jnp.where(above, st["thi"], 1.0)
            side = jnp.where(above, 1.0, -1.0)
            repeat = side == st["side"]
            boost = jnp.where(repeat, 2.0 * st["boost"], 1.0)
            mid = 0.5 * lo + 0.5 * hi
            stale = 1.0 / jnp.minimum(boost, 256.0)
            log_lo = jnp.log(jnp.maximum(glo, 0.5) / kf)
            log_hi = jnp.log(jnp.maximum(ghi, 0.5) / kf)
            w_lo = log_lo * jnp.where(above, 1.0, stale)
            w_hi = -log_hi * jnp.where(above, stale, 1.0)
            c_two = lo + w_lo / (w_lo + w_hi) * (hi - lo)
            reach = boost * log_slope
            c_one = jnp.where(tlo > 0.0, lo + reach * log_lo, hi + reach * log_hi)
            c_one = jnp.where(inside(c_one, lo, hi), c_one, mid)
            both = (tlo > 0.0) & (thi > 0.0)
            c_new = jnp.where(both, c_two, c_one)
            c_new = jnp.where(both & (lo < 0.0) & (hi > 0.0), 0.0, c_new)
            c_new = jnp.where((lo == 0.0) & (hi > F32_TINY), F32_TINY, c_new)
            movable = inside(c_new, lo, hi) & ~((lo == 0.0) & (hi <= F32_TINY))
            missed = (still > 0.0) & (((st["kind"] == 1.0) & above) |
                                      ((st["kind"] == 2.0) & (g < kf)))
            lo = jnp.where(missed & (st["kind"] == 2.0), st["emin"], lo)
            closed = (~movable & both & (still > 0.0)) | missed
            tie = jnp.where(closed, 1.0, st["tie"])
            still = jnp.where(closed, 0.0, still)
            searching = jnp.where(movable, still, 0.0)

            from_hi = both & (kf - ghi == 1.0)
            from_lo = both & (glo - kf == 1.0) & ~from_hi
            ready = jnp.where(searching > 0.0, jnp.where(from_hi | from_lo, 1.0, 2.0), 0.0)
            code = jnp.max(ready)
            ext_ref[...] = jnp.zeros_like(ext_ref)
            finish_ref[0] = jnp.int32(0)

            @pl.when((code == 1.0) & (st["it"] + 1 >= SEARCH_FINISH_FROM))
            def _():
                def ext_body(r, carry):
                    r0 = pl.multiple_of(r * kt_rows, kt_rows)
                    tile = ibuf[pl.ds(r0, kt_rows), :]
                    below = fold(jnp.where(tile < hi, tile, -jnp.inf), jnp.max)
                    above_lo = fold(jnp.where(tile >= lo, tile, jnp.inf), jnp.min)
                    return jnp.maximum(carry[0], below), jnp.minimum(carry[1], above_lo)
                mx8, mn8 = lax.fori_loop(0, n_tiles, ext_body,
                                         (jnp.full((SUBLANES, qb), -jnp.inf, f32),
                                          jnp.full((SUBLANES, qb), jnp.inf, f32)))
                ext_ref[0:1, :] = jnp.max(mx8, axis=0, keepdims=True)
                smallest = jnp.min(mn8, axis=0, keepdims=True)
                ext_ref[1:2, :] = smallest

                def next_body(r, carry):
                    r0 = pl.multiple_of(r * kt_rows, kt_rows)
                    tile = ibuf[pl.ds(r0, kt_rows), :]
                    return jnp.minimum(carry, fold(jnp.where(tile > smallest, tile, jnp.inf),
                                                   jnp.min))
                nx8 = lax.fori_loop(0, n_tiles, next_body,
                                    jnp.full((SUBLANES, qb), jnp.inf, f32))
                ext_ref[2:3, :] = jnp.min(nx8, axis=0, keepdims=True)
                finish_ref[0] = jnp.int32(1)

            finishing = finish_ref[0] > 0
            emax = ext_ref[0:1, :]
            emin = ext_ref[1:2, :]
            next_up = ext_ref[2:3, :]
            kind = jnp.where(finishing & (searching > 0.0),
                             jnp.where(from_hi, 1.0, jnp.where(from_lo, 2.0, 0.0)), 0.0)
            c_new = jnp.where(kind == 1.0, emax, jnp.where(kind == 2.0, next_up, c_new))
            return dict(it=st["it"] + 1, active=(code > 0.0).astype(jnp.int32),
                        c=c_new, lo=lo, hi=hi, glo=glo, ghi=ghi, tlo=tlo, thi=thi,
                        boost=boost, side=side, thr=thr, open=still, tie=tie,
                        kind=kind, emin=emin)

        final = lax.while_loop(search_cond, search_body, state0)
        tied = final["tie"] > 0.0
        thr_ref[...] = jnp.where(tied, final["lo"], final["thr"])
        pending_ref[0] = (jnp.max(final["open"]) > 0.0).astype(jnp.int32)

        @pl.when(jnp.max(final["tie"]) > 0.0)
        def _():
            resolve_ties(jnp.where(tied, final["lo"], jnp.inf), kf - final["ghi"],
                         final["glo"] - final["ghi"])

    @pl.when(pending_ref[0] > 0)
    def _():
        def key_to_f32(key):
            bits = jnp.where(key < 0, key ^ jnp.int32(0x7FFFFFFF), key)
            return lax.bitcast_convert_type(bits, f32)

        def bit_body(t, prefix):
            step = lax.shift_left(jnp.int32(1), jnp.int32(31) - t)
            cand = prefix + step
            cand_f = key_to_f32(cand)
            cnt = count(lambda tile, rows: tile >= cand_f)
            return jnp.where(cnt >= kf, cand, prefix)

        prefix = lax.fori_loop(0, 32, bit_body, jnp.full((1, qb), INT32_MIN, jnp.int32))
        thr = jnp.where(select_all, F32_LOWEST, key_to_f32(prefix))
        thr_ref[...] = thr

        cnt_ge = count(lambda tile, rows: tile >= thr)
        excess = jnp.where(select_all, 0.0, cnt_ge - kf)

        @pl.when(jnp.max(excess) > 0.0)
        def _():
            cnt_gt = count(lambda tile, rows: tile > thr)
            resolve_ties(jnp.where(excess > 0.0, thr, jnp.inf), kf - cnt_gt, cnt_ge - cnt_gt)

    thr = thr_ref[...]
    ones_rows = jnp.ones((2 * SUBLANES, kt_rows), bf16)

    m_ref[1] = jnp.full(m_ref.shape[1:], NEG_BIG, f32)

    def logits_stage(kt, slot):
        r0 = pl.multiple_of(kt * kt_rows, kt_rows)
        bias_ref[...] = jnp.where(ibuf[pl.ds(r0, kt_rows), :] >= thr, 0.0, NEG_BIG)
        for h in range(N_ATTN_HEADS):
            k_pair = k_ref[pl.ds(r0, kt_rows), (h // 2) * LANES:(h // 2 + 1) * LANES]
            s_ref[slot, h] = jnp.dot(k_pair, qTm[h], preferred_element_type=f32) + bias_ref[...]
            m_old = m_ref[1 - slot, h]
            m_new = jnp.maximum(m_old, jnp.max(s_ref[slot, h], axis=0, keepdims=True))
            alpha_ref[slot, h] = jnp.exp2(m_old - m_new)
            m_ref[slot, h] = m_new

    def values_stage(kt, slot):
        r0 = pl.multiple_of(kt * kt_rows, kt_rows)
        for h in range(N_ATTN_HEADS):
            p = jnp.exp2(s_ref[slot, h] - m_ref[slot, h]).astype(bf16)
            hs = slice(h * ATTN_HEAD_DIM, (h + 1) * ATTN_HEAD_DIM)
            lhs = jnp.concatenate([vT_ref[hs, pl.ds(r0, kt_rows)], ones_rows], axis=0)
            pv = jnp.dot(lhs, p, preferred_element_type=f32)
            alpha = alpha_ref[slot, h]
            acc_ref[hs, :] = alpha * acc_ref[hs, :] + pv[:ATTN_HEAD_DIM]
            l_ref[h] = alpha * l_ref[h] + pv[ATTN_HEAD_DIM:ATTN_HEAD_DIM + 1]

    def att_body(j, carry):
        kt = 2 * j
        logits_stage(kt + 1, 1)
        values_stage(kt, 0)
        next_idx_tile(kt)
        logits_stage(kt + 2, 0)
        values_stage(kt + 1, 1)
        next_idx_tile(kt + 1)
        return carry

    logits_stage(0, 0)
    lax.fori_loop(0, i // 2, att_body, 0)

    @pl.when(i % 2 == 1)
    def _():
        logits_stage(i, 1)
        values_stage(i - 1, 0)
        next_idx_tile(i - 1)
        values_stage(i, 1)
        next_idx_tile(i)

    @pl.when(i % 2 == 0)
    def _():
        values_stage(i, 0)
        next_idx_tile(i)

    @pl.when(i + 1 < pl.num_programs(1))
    def _():
        idx_tile(i + 1, True, ibuf_next, iqTn_ref, iwTn_ref)

    for h in range(N_ATTN_HEADS):
        hs = slice(h * ATTN_HEAD_DIM, (h + 1) * ATTN_HEAD_DIM)
        acc_ref[hs, :] = acc_ref[hs, :] / l_ref[h]
    o_ref[...] = acc_ref[...].T.astype(o_ref.dtype)


def _search_hints(seq_len):
    n = (np.arange(seq_len) // CHUNK + 1) * CHUNK
    frac = np.minimum(TOPK_MAX / n, 0.5)
    nd = statistics.NormalDist()
    z_of = {f: nd.inv_cdf(1.0 - f) for f in np.unique(frac)}
    z = np.array([z_of[f] for f in frac])
    pdf = np.exp(-0.5 * z * z) / np.sqrt(2.0 * np.pi)
    return jnp.asarray(np.stack([z, frac / pdf]), f32)


def _dsa(qT, iqT, iwT, z3, vT, ik3):
    b, _, s = qT.shape
    qb = DSA_BLOCK
    last = s // qb - 1
    per_row = lambda shape, col: pl.BlockSpec(shape, lambda bi, i: (bi, 0, col),
                                              pipeline_mode=pl.Buffered(1))
    nxt = lambda rows: pl.BlockSpec((None, rows, qb), lambda bi, i: (bi, 0, jnp.minimum(i + 1, last)))
    return pl.pallas_call(
        functools.partial(_dsa_kernel, seq_len=s),
        grid=(b, s // qb),
        in_specs=[
            pl.BlockSpec((None, ATTN_W, qb), lambda bi, i: (bi, 0, i)),
            pl.BlockSpec((None, IDX_Q_W, qb), lambda bi, i: (bi, 0, i)),
            pl.BlockSpec((None, N_IDX_HEADS, qb), lambda bi, i: (bi, 0, i)),
            nxt(IDX_Q_W), nxt(N_IDX_HEADS),
            pl.BlockSpec((2, qb), lambda bi, i: (0, i)),
            per_row((None, s, COL_TILE), Z_AK),
            per_row((None, ATTN_W, s), 0),
            per_row((None, s, IDX_HEAD_DIM), 0),
        ],
        out_specs=pl.BlockSpec((None, qb, ATTN_W), lambda bi, i: (bi, i, 0)),
        out_shape=jax.ShapeDtypeStruct((b, s, ATTN_W), bf16),
        scratch_shapes=[
            pltpu.VMEM((2, s, qb), f32),
            pltpu.VMEM((4, SUBLANES, qb), f32),
            pltpu.VMEM((N_ATTN_HEADS, LANES, qb), bf16),
            pltpu.VMEM((2, N_ATTN_HEADS, qb, qb), f32),
            pltpu.VMEM((qb, qb), f32),
            pltpu.VMEM((2, N_ATTN_HEADS, 1, qb), f32),
            pltpu.VMEM((N_ATTN_HEADS, 1, qb), f32),
            pltpu.VMEM((2, N_ATTN_HEADS, 1, qb), f32),
            pltpu.VMEM((ATTN_W, qb), f32),
            pltpu.VMEM((1, qb), f32),
            pltpu.VMEM((3, qb), f32),
            pltpu.SMEM((1,), jnp.int32),
            pltpu.SMEM((1,), jnp.int32),
        ],
        compiler_params=_cparams(("parallel", "arbitrary")),
        name="dsa",
    )(qT, iqT, iwT, iqT, iwT, _search_hints(s), z3, vT, ik3)


def _gla_kernel(gin_ref, dec_ref, v_ref, r_ref, gn_ref, o_ref, state_ref):
    c = CHUNK
    n_batch = gin_ref.shape[0]

    @pl.when(pl.program_id(0) == 0)
    def _():
        state_ref[...] = jnp.zeros_like(state_ref)

    row = lax.broadcasted_iota(jnp.int32, (c, c), 0)
    col = lax.broadcasted_iota(jnp.int32, (c, c), 1)
    causal = row >= col
    lane = lax.broadcasted_iota(jnp.int32, (c, LANES), 1)
    half = (lane < GLA_HEAD_K, lane >= GLA_HEAD_K)
    contract_last = (((1,), (1,)), ((), ()))
    contract_first = (((0,), (0,)), ((), ()))
    zero = jnp.zeros((c, LANES), bf16)

    for ci, bi in [(ci, bi) for ci in range(GLA_TILE // c) for bi in range(n_batch)]:
        rows = slice(ci * c, (ci + 1) * c)
        decay = dec_ref[bi, ci:ci + 1, :]
        for pr in range(N_GLA_HEADS // 2):
            ps = slice(pr * LANES, (pr + 1) * LANES)
            part = lambda g: gin_ref[bi, rows, g * GLA_K_W + pr * LANES:
                                     g * GLA_K_W + (pr + 1) * LANES]
            q_in, q_mid, k_mid, k_out = part(0), part(1), part(2), part(3)
            st = state_ref[bi, pr]
            st_b = st.astype(bf16)
            upd = jnp.zeros_like(st)
            for hh in range(2):
                h = pr * 2 + hh
                vs = slice(h * GLA_HEAD_V, (h + 1) * GLA_HEAD_V)
                v_h = v_ref[bi, rows, vs]
                attn = lax.dot_general(jnp.where(half[hh], q_mid, zero), k_mid, contract_last,
                                       preferred_element_type=f32)
                attn = jnp.where(causal, attn, 0.0).astype(bf16)
                o = lax.dot_general(jnp.where(half[hh], q_in, zero), st_b, contract_last,
                                    preferred_element_type=f32)
                o = o + jnp.dot(attn, v_h, preferred_element_type=f32)
                upd = upd + lax.dot_general(v_h, jnp.where(half[hh], k_out, zero),
                                            contract_first, preferred_element_type=f32)
                y = _rms(o, gn_ref[...])
                r = r_ref[bi, rows, vs].astype(f32)
                o_ref[bi, rows, vs] = (y * r * _sigmoid(r)).astype(o_ref.dtype)
            state_ref[bi, pr] = decay[:, ps] * st + upd


def _gla(gin3, dec3, z3, gn):
    b, s, _ = z3.shape
    tb = GLA_TILE
    return pl.pallas_call(
        _gla_kernel,
        grid=(s // tb,),
        in_specs=[
            pl.BlockSpec((b, tb, GLA_IN_W), lambda j: (0, j, 0)),
            pl.BlockSpec((b, tb // CHUNK, GLA_K_W), lambda j: (0, j, 0)),
            pl.BlockSpec((b, tb, COL_TILE), lambda j: (0, j, Z_GV)),
            pl.BlockSpec((b, tb, COL_TILE), lambda j: (0, j, Z_GR)),
            pl.BlockSpec((1, GLA_HEAD_V), lambda j: (0, 0)),
        ],
        out_specs=pl.BlockSpec((b, tb, GLA_V_W), lambda j: (0, j, 0)),
        out_shape=jax.ShapeDtypeStruct((b, s, GLA_V_W), bf16),
        scratch_shapes=[pltpu.VMEM((b, N_GLA_HEADS // 2, GLA_HEAD_V, LANES), f32)],
        compiler_params=_cparams(("arbitrary",)),
        name="gla",
    )(gin3, dec3, z3, z3, gn)


def _tail_kernel(x_ref, oa_ref, ob_ref, ga_ref, gb_ref, wa_ref, wb_ref, wo_ref,
                 g_ref, wg_ref, wu_ref, wd_ref, p_ref, gp_ref, wpg_ref, wpp_ref, gf_ref,
                 o_ref, *, final):
    ya = jnp.dot(oa_ref[...], wa_ref[...], preferred_element_type=f32)
    yb = jnp.dot(ob_ref[...], wb_ref[...], preferred_element_type=f32)
    mg = _sigmoid(ga_ref[...].astype(f32)) * ya + _sigmoid(gb_ref[...].astype(f32)) * yb
    x = x_ref[...] + jnp.dot(mg.astype(bf16), wo_ref[...], preferred_element_type=f32)
    x = _swiglu_half_step(x, g_ref, wg_ref, wu_ref, wd_ref)
    h = _rms(x, gp_ref[...]).astype(bf16)
    gate = _sigmoid(jnp.dot(h, wpg_ref[...], preferred_element_type=f32))
    e = jnp.dot(p_ref[...].astype(bf16), wpp_ref[...], preferred_element_type=f32)
    y = x + gate * e
    if final:
        y = _rms(y, gf_ref[...])
    o_ref[...] = y


def _tail(x, oa, ob, z, wa, wb, wo, g, wg, wu, wd, p, gp, wpg, wpp, gf, final, out_shape):
    t = x.shape[0]
    tm = TOKEN_TILE
    rows = lambda width, col=0: pl.BlockSpec((tm, width), lambda i: (i, col))
    return pl.pallas_call(
        functools.partial(_tail_kernel, final=final),
        grid=(t // tm,),
        in_specs=[
            rows(D_MODEL), rows(ATTN_W), rows(GLA_V_W), rows(D_MODEL, Z_MGA), rows(D_MODEL, Z_MGB),
            _resident(wa), _resident(wb), _resident(wo),
            _resident(g), _resident(wg), _resident(wu), _resident(wd),
            rows(PLE_DIM), _resident(gp), _resident(wpg), _resident(wpp), _resident(gf),
        ],
        out_specs=_token_rows(out_shape),
        out_shape=jax.ShapeDtypeStruct(out_shape, f32),
        compiler_params=_cparams(("parallel",)),
        name="tail",
    )(x, oa, ob, z, z, wa, wb, wo, g, wg, wu, wd, p, gp, wpg, wpp, gf)


def _split_w_in(w_in):
    cuts = np.cumsum(np.array(SPLIT_SIZES))[:-1].tolist()
    aq, ak, av, iq, ik, iw, gq, gk, gv, gr, ga, mga, mgb = jnp.split(w_in, cuts, axis=-1)
    att_scale = ATTN_HEAD_DIM ** -0.5 * float(np.log2(np.e))
    idx_scale = IDX_HEAD_DIM ** -0.5 * N_IDX_HEADS ** -0.5
    c = lambda w: w.astype(bf16)
    main = jnp.concatenate([c(mga), c(mgb), c(ak), c(gv), c(gr)], axis=-1)
    feature_major = jnp.concatenate([c(aq * att_scale), c(av), c(iq)], axis=-1)
    gla_qk = jnp.concatenate([c(gq * (GLA_HEAD_K ** -0.5)), c(gk)], axis=-1)
    zeros = lambda n: jnp.zeros(w_in.shape[:-1] + (n,), bf16)
    small = jnp.concatenate(
        [c(ik), zeros(LANES - IDX_HEAD_DIM), c(iw * idx_scale), c(ga),
         zeros(LANES - N_IDX_HEADS - GLA_GATE_RANK)], axis=-1)
    return main, feature_major, gla_qk, small


def kernel(x, p, w_in, gla_gate_w2, gla_gate_b, gla_norm, w_branch_a, w_branch_b, w_out,
           norm_ff1, norm_mix, norm_ff2, norm_ple, ff1_w_gate, ff1_w_up, ff1_w_down,
           ff2_w_gate, ff2_w_up, ff2_w_down, ple_w_proj, ple_w_gate, norm_final):
    b, s, d = x.shape
    t = b * s
    depth = w_in.shape[0]
    assert d == D_MODEL and w_in.shape[1:] == (D_MODEL, sum(SPLIT_SIZES))
    assert s % TOKEN_TILE == 0 and s % GLA_TILE == 0 and s % DSA_BLOCK == 0
    assert ff1_w_gate.shape[1:] == (D_MODEL, D_FF) and D_FF % FF_TILE == 0
    w_main, w_fm, w_gqk, w_small = _split_w_in(w_in)
    cast = lambda w: w.astype(bf16)
    ff1 = (cast(ff1_w_gate), cast(ff1_w_up), cast(ff1_w_down))
    ff2 = (cast(ff2_w_gate), cast(ff2_w_up), cast(ff2_w_down))
    wa, wb, wo = cast(w_branch_a), cast(w_branch_b), cast(w_out)
    wpg, wpp = cast(ple_w_gate), cast(ple_w_proj)
    row = lambda v: v.reshape(1, -1)

    xf = x
    for l in range(depth):
        last = l == depth - 1
        xf = _ffn(xf, row(norm_ff1[l]), ff1[0][l], ff1[1][l], ff1[2][l])

        z, qT, vT, iqT, ik, iwT, gin, dec = _proj(
            xf, row(norm_mix[l]), w_main[l], w_fm[l], w_gqk[l], w_small[l],
            gla_gate_w2[l], row(gla_gate_b[l]), b)
        z3 = z.reshape(b, s, Z_WIDTH)
        oa = _dsa(qT, iqT, iwT, z3, vT, ik.reshape(b, s, IDX_HEAD_DIM))
        ob = _gla(gin.reshape(b, s, GLA_IN_W), dec.reshape(b, s // CHUNK, GLA_K_W), z3,
                  row(gla_norm[l]))
        xf = _tail(xf, oa.reshape(t, ATTN_W), ob.reshape(t, GLA_V_W), z, wa[l], wb[l], wo[l],
                   row(norm_ff2[l]), ff2[0][l], ff2[1][l], ff2[2][l],
                   p[l].reshape(t, PLE_DIM), row(norm_ple[l]), wpg[l], wpp[l],
                   row(norm_final), final=last, out_shape=(b, s, d) if last else (t, d))
    return xf
```

```python
import functools
import statistics

import jax
import jax.numpy as jnp
import numpy as np
from jax import lax
from jax.experimental import pallas as pl
from jax.experimental.pallas import tpu as pltpu

D_MODEL = 1024
D_FF = 2816
PLE_DIM = 256
EPS = 1e-6

CHUNK = 64
N_ATTN_HEADS = 8
ATTN_HEAD_DIM = 64
N_IDX_HEADS = 8
IDX_HEAD_DIM = 64
TOPK_MAX = 256
N_GLA_HEADS = 4
GLA_HEAD_K = 64
GLA_HEAD_V = 128
GLA_GATE_RANK = 16
GLA_GATE_TAU = 16.0

ATTN_W = N_ATTN_HEADS * ATTN_HEAD_DIM
IDX_Q_W = N_IDX_HEADS * IDX_HEAD_DIM
GLA_K_W = N_GLA_HEADS * GLA_HEAD_K
GLA_V_W = N_GLA_HEADS * GLA_HEAD_V
SPLIT_SIZES = (ATTN_W, ATTN_W, ATTN_W, IDX_Q_W, IDX_HEAD_DIM, N_IDX_HEADS,
               GLA_K_W, GLA_K_W, GLA_V_W, GLA_V_W, GLA_GATE_RANK, D_MODEL, D_MODEL)

LANES = 128
SUBLANES = 8
V7X_VMEM_BYTES = 64 * 1024 * 1024
VMEM_LIMIT_BYTES = V7X_VMEM_BYTES // 8 * 7

COL_TILE = 512
Z_MGA, Z_MGB = 0, 1
Z_AK, Z_GV, Z_GR = 4, 5, 6
Z_TILES = 7
Z_WIDTH = Z_TILES * COL_TILE
ZT_TILES = 3
SM_IW0 = 0
SM_GA0 = N_IDX_HEADS

TOKEN_TILE = 512
FF_TILE = 256
DSA_BLOCK = 256
GLA_TILE = 512
GLA_IN_W = 4 * GLA_K_W
SEARCH_MAX_ITERS = 24
SEARCH_FINISH_FROM = 4

NEG_BIG = -1e30
F32_LOWEST = float(np.finfo(np.float32).min)
F32_TINY = float(np.finfo(np.float32).tiny)
INT32_MIN = int(np.iinfo(np.int32).min)

f32 = jnp.float32
bf16 = jnp.bfloat16


def _rms(x, g):
    return x * lax.rsqrt(jnp.mean(x * x, axis=-1, keepdims=True) + EPS) * g


def _sigmoid(x):
    return 1.0 / (1.0 + jnp.exp(-x))


def _cparams(sem):
    return pltpu.CompilerParams(dimension_semantics=sem, vmem_limit_bytes=VMEM_LIMIT_BYTES)


def _resident(a):
    return pl.BlockSpec(a.shape, lambda i: (0, 0), pipeline_mode=pl.Buffered(1))


def _swiglu_half_step(x, g_ref, wg_ref, wu_ref, wd_ref):
    h = _rms(x, g_ref[...]).astype(bf16)
    y = x
    for f in range(D_FF // FF_TILE):
        cols = slice(f * FF_TILE, (f + 1) * FF_TILE)
        gate = jnp.dot(h, wg_ref[:, cols], preferred_element_type=f32)
        up = jnp.dot(h, wu_ref[:, cols], preferred_element_type=f32)
        a = (gate * _sigmoid(gate) * up).astype(bf16)
        y = y + 0.5 * jnp.dot(a, wd_ref[cols, :], preferred_element_type=f32)
    return y


def _ffn_kernel(x_ref, g_ref, wg_ref, wu_ref, wd_ref, o_ref):
    o_ref[...] = _swiglu_half_step(x_ref[...], g_ref, wg_ref, wu_ref, wd_ref)


def _token_rows(shape):
    if len(shape) == 2:
        return pl.BlockSpec((TOKEN_TILE, shape[1]), lambda i: (i, 0))
    per_b = shape[1] // TOKEN_TILE
    return pl.BlockSpec((None, TOKEN_TILE, shape[2]), lambda i: (i // per_b, i % per_b, 0))


def _ffn(x, g, wg, wu, wd):
    t = x.size // D_MODEL
    return pl.pallas_call(
        _ffn_kernel,
        grid=(t // TOKEN_TILE,),
        in_specs=[
            _token_rows(x.shape),
            _resident(g), _resident(wg), _resident(wu), _resident(wd),
        ],
        out_specs=pl.BlockSpec((TOKEN_TILE, D_MODEL), lambda i: (i, 0)),
        out_shape=jax.ShapeDtypeStruct((t, D_MODEL), f32),
        compiler_params=_cparams(("parallel",)),
        name="ffn",
    )(x, g, wg, wu, wd)


def _gla_operands(q, k, ga, w2, gb):
    n = q.shape[0]
    c = CHUNK
    def split(a):
        hi = a.astype(bf16)
        return hi, (a - hi.astype(f32)).astype(bf16)

    dot = lambda a, b: jnp.dot(a, b, preferred_element_type=f32)
    ga_hi, ga_lo = split(ga)
    w2_hi, w2_lo = split(w2)
    glin = dot(ga_hi, w2_hi) + dot(ga_hi, w2_lo) + dot(ga_lo, w2_hi) + gb
    logg = (jnp.minimum(glin, 0.0) - jnp.log(1.0 + jnp.exp(-jnp.abs(glin)))) / GLA_GATE_TAU
    row = lax.broadcasted_iota(jnp.int32, (n, n), 0)
    col = lax.broadcasted_iota(jnp.int32, (n, n), 1)
    tril = jnp.where((row // c == col // c) & (col <= row), 1.0, 0.0).astype(bf16)
    hi, lo = split(logg)
    bcum = dot(tril, hi) + dot(tril, lo)
    by_chunk = bcum.reshape(n // c, c, GLA_K_W)
    spread = lambda r: jnp.broadcast_to(r, (n // c, c, GLA_K_W)).reshape(n, GLA_K_W)
    b_last = by_chunk[:, c - 1:c, :]
    b_mid = spread(by_chunk[:, c // 2 - 1:c // 2, :])
    operands = jnp.concatenate(
        [q * jnp.exp(bcum), q * jnp.exp(bcum - b_mid), k * jnp.exp(b_mid - bcum),
         k * jnp.exp(spread(b_last) - bcum)], axis=1)
    return operands, jnp.exp(b_last.reshape(n // c, GLA_K_W))


def _proj_kernel(x_ref, g_ref, w_ref, wt_ref, wg_ref, ws_ref, w2_ref, gb_ref,
                 z_ref, qT_ref, vT_ref, iqT_ref, ik_ref, iwT_ref, gin_ref, dec_ref):
    h = _rms(x_ref[...], g_ref[...]).astype(bf16)
    for j in range(Z_TILES):
        cols = slice(j * COL_TILE, (j + 1) * COL_TILE)
        z_ref[:, cols] = jnp.dot(h, w_ref[:, cols], preferred_element_type=f32).astype(bf16)
    for j, out_ref in enumerate((qT_ref, vT_ref, iqT_ref)):
        cols = slice(j * COL_TILE, (j + 1) * COL_TILE)
        zt = jnp.dot(h, wt_ref[:, cols], preferred_element_type=f32)
        out_ref[...] = zt.T.astype(bf16)
    s = jnp.dot(h, ws_ref[...], preferred_element_type=f32)
    ik_ref[...] = s[:, :IDX_HEAD_DIM].astype(bf16)
    sm = s[:, LANES:]
    iwT_ref[...] = sm.T[SM_IW0:SM_IW0 + N_IDX_HEADS, :]
    gqk = jnp.dot(h, wg_ref[...], preferred_element_type=f32)
    operands, decay = _gla_operands(gqk[:, :GLA_K_W], gqk[:, GLA_K_W:],
                                    sm[:, SM_GA0:SM_GA0 + GLA_GATE_RANK],
                                    w2_ref[...], gb_ref[...])
    gin_ref[...] = operands.astype(bf16)
    dec_ref[...] = decay


def _proj(x, g, w_main, w_t, w_g, w_small, w2, gb, batch):
    t = x.shape[0]
    tm = TOKEN_TILE
    s = t // batch
    per_b = s // tm
    fm = lambda rows: pl.BlockSpec((None, rows, tm), lambda i: (i // per_b, 0, i % per_b))
    return pl.pallas_call(
        _proj_kernel,
        grid=(t // tm,),
        in_specs=[
            pl.BlockSpec((tm, D_MODEL), lambda i: (i, 0)),
            _resident(g), _resident(w_main), _resident(w_t), _resident(w_g), _resident(w_small),
            _resident(w2), _resident(gb),
        ],
        out_specs=[
            pl.BlockSpec((tm, Z_WIDTH), lambda i: (i, 0)),
            fm(ATTN_W), fm(ATTN_W), fm(IDX_Q_W),
            pl.BlockSpec((tm, IDX_HEAD_DIM), lambda i: (i, 0)),
            fm(N_IDX_HEADS),
            pl.BlockSpec((tm, GLA_IN_W), lambda i: (i, 0)),
            pl.BlockSpec((tm // CHUNK, GLA_K_W), lambda i: (i, 0)),
        ],
        out_shape=[
            jax.ShapeDtypeStruct((t, Z_WIDTH), bf16),
            jax.ShapeDtypeStruct((batch, ATTN_W, s), bf16),
            jax.ShapeDtypeStruct((batch, ATTN_W, s), bf16),
            jax.ShapeDtypeStruct((batch, IDX_Q_W, s), bf16),
            jax.ShapeDtypeStruct((t, IDX_HEAD_DIM), bf16),
            jax.ShapeDtypeStruct((batch, N_IDX_HEADS, s), f32),
            jax.ShapeDtypeStruct((t, GLA_IN_W), bf16),
            jax.ShapeDtypeStruct((t // CHUNK, GLA_K_W), f32),
        ],
        compiler_params=_cparams(("parallel",)),
        name="proj",
    )(x, g, w_main, w_t, w_g, w_small, w2, gb)


def _dsa_kernel(qT_ref, iqT_ref, iwT_ref, iqTn_ref, iwTn_ref, tab_ref, k_ref, vT_ref, ik_ref,
                o_ref, ibuf_all, stats_ref, qTm, s_ref, bias_ref, m_ref, l_ref, alpha_ref,
                acc_ref, thr_ref, ext_ref, pending_ref, finish_ref, *, seq_len):
    qb = DSA_BLOCK
    kt_rows = DSA_BLOCK
    i = pl.program_id(1)
    n_tiles = i + 1

    rowid = lax.broadcasted_iota(jnp.int32, (LANES, qb), 0)
    for h in range(N_ATTN_HEADS):
        pair = qT_ref[(h // 2) * LANES:(h // 2 + 1) * LANES, :]
        keep = (rowid < ATTN_HEAD_DIM) if h % 2 == 0 else (rowid >= ATTN_HEAD_DIM)
        qTm[h] = jnp.where(keep, pair, jnp.zeros_like(pair))
    l_ref[...] = jnp.zeros_like(l_ref)
    acc_ref[...] = jnp.zeros_like(acc_ref)

    def fold(x, op):
        return op(x.reshape(kt_rows // SUBLANES, SUBLANES, qb), axis=0)

    ibuf = ibuf_all.at[i % 2]
    ibuf_next = ibuf_all.at[1 - i % 2]

    def idx_tile(kt, diagonal, dst, iq_ref, iw_ref):
        r0 = pl.multiple_of(kt * kt_rows, kt_rows)
        ki_t = ik_ref[pl.ds(r0, kt_rows), :]
        sc = jnp.zeros((kt_rows, qb), f32)
        for h in range(N_IDX_HEADS):
            s = jnp.dot(ki_t, iq_ref[h * IDX_HEAD_DIM:(h + 1) * IDX_HEAD_DIM, :],
                        preferred_element_type=f32)
            sc = sc + iw_ref[h:h + 1, :] * jnp.maximum(s, 0.0)
        sc_hi = sc_lo = sc_0 = sc
        if diagonal:
            kc = lax.broadcasted_iota(jnp.int32, (kt_rows, qb), 0) // CHUNK
            qc = lax.broadcasted_iota(jnp.int32, (kt_rows, qb), 1) // CHUNK
            adm = kc <= qc
            sc_hi = jnp.where(adm, sc, -jnp.inf)
            sc_lo = jnp.where(adm, sc, jnp.inf)
            sc_0 = jnp.where(adm, sc, 0.0)
        dst[pl.ds(r0, kt_rows), :] = sc_hi
        stats_ref[0] = jnp.maximum(stats_ref[0], fold(sc_hi, jnp.max))
        stats_ref[1] = jnp.minimum(stats_ref[1], fold(sc_lo, jnp.min))
        stats_ref[2] = stats_ref[2] + fold(sc_0, jnp.sum)
        stats_ref[3] = stats_ref[3] + fold(sc_0 * sc_0, jnp.sum)

    def reset_stats():
        stats_ref[0] = jnp.full((SUBLANES, qb), -jnp.inf, f32)
        stats_ref[1] = jnp.full((SUBLANES, qb), jnp.inf, f32)
        stats_ref[2] = jnp.zeros((SUBLANES, qb), f32)
        stats_ref[3] = jnp.zeros((SUBLANES, qb), f32)

    @pl.when(i == 0)
    def _():
        reset_stats()
        idx_tile(0, True, ibuf, iqT_ref, iwT_ref)

    col_max = jnp.max(stats_ref[0], axis=0, keepdims=True)
    col_min = jnp.min(stats_ref[1], axis=0, keepdims=True)
    col_sum = jnp.sum(stats_ref[2], axis=0, keepdims=True)
    col_ssq = jnp.sum(stats_ref[3], axis=0, keepdims=True)
    reset_stats()

    def next_idx_tile(kt):
        idx_tile(kt, False, ibuf_next, iqTn_ref, iwTn_ref)

    def count(pred):
        def body(r, acc):
            r0 = pl.multiple_of(r * kt_rows, kt_rows)
            tile = ibuf[pl.ds(r0, kt_rows), :]
            rows = r0 + lax.broadcasted_iota(jnp.int32, (kt_rows, qb), 0)
            ind = jnp.where(pred(tile, rows), 1.0, 0.0)
            part = ind.reshape(kt_rows // SUBLANES, SUBLANES, qb)
            while part.shape[0] > 1:
                half = part.shape[0] // 2
                part = part[:half] + part[half:]
            return acc + part[0]
        acc = jnp.zeros((SUBLANES, qb), f32)
        acc = lax.cond(i % 2 == 0, lambda a: body(0, a), lambda a: a, acc)
        first = 1 - i % 2
        acc = lax.fori_loop(
            0, (i + 1) // 2,
            lambda j, a: body(first + 2 * j + 1, body(first + 2 * j, a)), acc)
        return jnp.sum(acc, axis=0, keepdims=True)

    def resolve_ties(tv, need, mult):
        is_tie = tv < jnp.inf
        n_keys = n_tiles * kt_rows

        def tie_cond(st):
            return st["active"] > 0

        def tie_body(st):
            plo, phi, flo, fhi = st["plo"], st["phi"], st["flo"], st["fhi"]
            span = phi - plo
            est = ((need - flo) / jnp.maximum(fhi - flo, 1.0) * span.astype(f32)).astype(jnp.int32)
            pick = jnp.where(st["bisect"] == 2, span // 2, est)
            cand = plo + jnp.clip(pick, 1, jnp.maximum(span - 1, 1))
            f = count(lambda tile, rows: jnp.where(tile == tv, rows, seq_len) < cand)
            lower = f < need
            exact = f == need
            plo = jnp.where(lower, cand, jnp.where(exact, cand - 1, plo))
            phi = jnp.where(lower, phi, cand)
            flo = jnp.where(lower, f, flo)
            fhi = jnp.where(lower, fhi, f)
            unsplit = jnp.where(is_tie & (phi - plo > 1), 1, 0)
            return dict(active=jnp.max(unsplit), bisect=(st["bisect"] + 1) % 3,
                        plo=plo, phi=phi, flo=flo, fhi=fhi)

        split = lax.while_loop(tie_cond, tie_body, dict(
            active=jnp.int32(1), bisect=jnp.int32(0),
            plo=jnp.zeros((1, qb), jnp.int32), phi=jnp.zeros((1, qb), jnp.int32) + n_keys,
            flo=jnp.zeros((1, qb), f32), fhi=mult))["phi"]

        def drop_body(r, carry):
            r0 = pl.multiple_of(r * kt_rows, kt_rows)
            tile = ibuf[pl.ds(r0, kt_rows), :]
            rows = r0 + lax.broadcasted_iota(jnp.int32, (kt_rows, qb), 0)
            dropped = jnp.where(tile == tv, rows, -1) >= split
            ibuf[pl.ds(r0, kt_rows), :] = jnp.where(dropped, -jnp.inf, tile)
            return carry

        lax.fori_loop(0, n_tiles, drop_body, 0)

    qpos = i * qb + lax.broadcasted_iota(jnp.int32, (1, qb), 1)
    n_adm = (qpos // CHUNK + 1) * CHUNK
    select_all = n_adm <= TOPK_MAX
    kf = float(TOPK_MAX)

    thr_ref[...] = jnp.full((1, qb), F32_LOWEST, f32)
    pending_ref[0] = jnp.int32(0)

    @pl.when(n_tiles * kt_rows > TOPK_MAX)
    def _():
        n_f = n_adm.astype(f32)
        mean = col_sum / n_f
        sigma = jnp.sqrt(jnp.maximum(col_ssq / n_f - mean * mean, 0.0))
        log_slope = tab_ref[1:2, :] * sigma

        def inside(c, lo, hi):
            return (c > lo) & (c < hi)

        c0 = mean + tab_ref[0:1, :] * sigma
        c0 = jnp.where(inside(c0, col_min, col_max), c0, 0.5 * col_min + 0.5 * col_max)
        zero = jnp.zeros((1, qb), f32)
        one = jnp.ones((1, qb), f32)
        state0 = dict(it=jnp.int32(0), active=jnp.int32(1), c=c0, lo=col_min, hi=col_max,
                      glo=n_f, ghi=zero, tlo=zero, thi=zero, boost=one, side=zero,
                      thr=jnp.full((1, qb), F32_LOWEST, f32),
                      open=jnp.where(select_all, 0.0, 1.0), tie=zero, kind=zero, emin=zero)

        def search_cond(st):
            return (st["it"] < SEARCH_MAX_ITERS) & (st["active"] > 0)

        def search_body(st):
            c, lo, hi = st["c"], st["lo"], st["hi"]
            g = count(lambda tile, rows: tile >= c)
            hit = (g == kf) & (st["open"] > 0.0)
            thr = jnp.where(hit, c, st["thr"])
            still = jnp.where(hit, 0.0, st["open"])
            above = g > kf
            lo = jnp.where(above, c, lo)
            hi = jnp.where(above, hi, c)
            glo = jnp.where(above, g, st["glo"])
            ghi = jnp.where(above, st["ghi"], g)
            tlo = jnp.where(above, 1.0, st["tlo"])
            thi = jnp.where(above, st["thi"], 1.0)
            side = jnp.where(above, 1.0, -1.0)
            repeat = side == st["side"]
            boost = jnp.where(repeat, 2.0 * st["boost"], 1.0)
            mid = 0.5 * lo + 0.5 * hi
            stale = 1.0 / jnp.minimum(boost, 256.0)
            log_lo = jnp.log(jnp.maximum(glo, 0.5) / kf)
            log_hi = jnp.log(jnp.maximum(ghi, 0.5) / kf)
            w_lo = log_lo * jnp.where(above, 1.0, stale)
            w_hi = -log_hi * jnp.where(above, stale, 1.0)
            c_two = lo + w_lo / (w_lo + w_hi) * (hi - lo)
            reach = boost * log_slope
            c_one = jnp.where(tlo > 0.0, lo + reach * log_lo, hi + reach * log_hi)
            c_one = jnp.where(inside(c_one, lo, hi), c_one, mid)
            both = (tlo > 0.0) & (thi > 0.0)
            c_new = jnp.where(both, c_two, c_one)
            c_new = jnp.where(both & (lo < 0.0) & (hi > 0.0), 0.0, c_new)
            c_new = jnp.where((lo == 0.0) & (hi > F32_TINY), F32_TINY, c_new)
            movable = inside(c_new, lo, hi) & ~((lo == 0.0) & (hi <= F32_TINY))
            missed = (still > 0.0) & (((st["kind"] == 1.0) & above) |
                                      ((st["kind"] == 2.0) & (g < kf)))
            lo = jnp.where(missed & (st["kind"] == 2.0), st["emin"], lo)
            closed = (~movable & both & (still > 0.0)) | missed
            tie = jnp.where(closed, 1.0, st["tie"])
            still = jnp.where(closed, 0.0, still)
            searching = jnp.where(movable, still, 0.0)

            from_hi = both & (kf - ghi == 1.0)
            from_lo = both & (glo - kf == 1.0) & ~from_hi
            ready = jnp.where(searching > 0.0, jnp.where(from_hi | from_lo, 1.0, 2.0), 0.0)
            code = jnp.max(ready)
            ext_ref[...] = jnp.zeros_like(ext_ref)
            finish_ref[0] = jnp.int32(0)

            @pl.when((code == 1.0) & (st["it"] + 1 >= SEARCH_FINISH_FROM))
            def _():
                def ext_body(r, carry):
                    r0 = pl.multiple_of(r * kt_rows, kt_rows)
                    tile = ibuf[pl.ds(r0, kt_rows), :]
                    below = fold(jnp.where(tile < hi, tile, -jnp.inf), jnp.max)
                    above_lo = fold(jnp.where(tile >= lo, tile, jnp.inf), jnp.min)
                    return jnp.maximum(carry[0], below), jnp.minimum(carry[1], above_lo)
                mx8, mn8 = lax.fori_loop(0, n_tiles, ext_body,
                                         (jnp.full((SUBLANES, qb), -jnp.inf, f32),
                                          jnp.full((SUBLANES, qb), jnp.inf, f32)))
                ext_ref[0:1, :] = jnp.max(mx8, axis=0, keepdims=True)
                smallest = jnp.min(mn8, axis=0, keepdims=True)
                ext_ref[1:2, :] = smallest

                def next_body(r, carry):
                    r0 = pl.multiple_of(r * kt_rows, kt_rows)
                    tile = ibuf[pl.ds(r0, kt_rows), :]
                    return jnp.minimum(carry, fold(jnp.where(tile > smallest, tile, jnp.inf),
                                                   jnp.min))
                nx8 = lax.fori_loop(0, n_tiles, next_body,
                                    jnp.full((SUBLANES, qb), jnp.inf, f32))
                ext_ref[2:3, :] = jnp.min(nx8, axis=0, keepdims=True)
                finish_ref[0] = jnp.int32(1)

            finishing = finish_ref[0] > 0
            emax = ext_ref[0:1, :]
            emin = ext_ref[1:2, :]
            next_up = ext_ref[2:3, :]
            kind = jnp.where(finishing & (searching > 0.0),
                             jnp.where(from_hi, 1.0, jnp.where(from_lo, 2.0, 0.0)), 0.0)
            c_new = jnp.where(kind == 1.0, emax, jnp.where(kind == 2.0, next_up, c_new))
            return dict(it=st["it"] + 1, active=(code > 0.0).astype(jnp.int32),
                        c=c_new, lo=lo, hi=hi, glo=glo, ghi=ghi, tlo=tlo, thi=thi,
                        boost=boost, side=side, thr=thr, open=still, tie=tie,
                        kind=kind, emin=emin)

        final = lax.while_loop(search_cond, search_body, state0)
        tied = final["tie"] > 0.0
        thr_ref[...] = jnp.where(tied, final["lo"], final["thr"])
        pending_ref[0] = (jnp.max(final["open"]) > 0.0).astype(jnp.int32)

        @pl.when(jnp.max(final["tie"]) > 0.0)
        def _():
            resolve_ties(jnp.where(tied, final["lo"], jnp.inf), kf - final["ghi"],
                         final["glo"] - final["ghi"])

    @pl.when(pending_ref[0] > 0)
    def _():
        def key_to_f32(key):
            bits = jnp.where(key < 0, key ^ jnp.int32(0x7FFFFFFF), key)
            return lax.bitcast_convert_type(bits, f32)

        def bit_body(t, prefix):
            step = lax.shift_left(jnp.int32(1), jnp.int32(31) - t)
            cand = prefix + step
            cand_f = key_to_f32(cand)
            cnt = count(lambda tile, rows: tile >= cand_f)
            return jnp.where(cnt >= kf, cand, prefix)

        prefix = lax.fori_loop(0, 32, bit_body, jnp.full((1, qb), INT32_MIN, jnp.int32))
        thr = jnp.where(select_all, F32_LOWEST, key_to_f32(prefix))
        thr_ref[...] = thr

        cnt_ge = count(lambda tile, rows: tile >= thr)
        excess = jnp.where(select_all, 0.0, cnt_ge - kf)

        @pl.when(jnp.max(excess) > 0.0)
        def _():
            cnt_gt = count(lambda tile, rows: tile > thr)
            resolve_ties(jnp.where(excess > 0.0, thr, jnp.inf), kf - cnt_gt, cnt_ge - cnt_gt)

    thr = thr_ref[...]
    ones_rows = jnp.ones((2 * SUBLANES, kt_rows), bf16)

    m_ref[1] = jnp.full(m_ref.shape[1:], NEG_BIG, f32)

    def logits_stage(kt, slot):
        r0 = pl.multiple_of(kt * kt_rows, kt_rows)
        bias_ref[...] = jnp.where(ibuf[pl.ds(r0, kt_rows), :] >= thr, 0.0, NEG_BIG)
        for h in range(N_ATTN_HEADS):
            k_pair = k_ref[pl.ds(r0, kt_rows), (h // 2) * LANES:(h // 2 + 1) * LANES]
            s_ref[slot, h] = jnp.dot(k_pair, qTm[h], preferred_element_type=f32) + bias_ref[...]
            m_old = m_ref[1 - slot, h]
            m_new = jnp.maximum(m_old, jnp.max(s_ref[slot, h], axis=0, keepdims=True))
            alpha_ref[slot, h] = jnp.exp2(m_old - m_new)
            m_ref[slot, h] = m_new

    def values_stage(kt, slot):
        r0 = pl.multiple_of(kt * kt_rows, kt_rows)
        for h in range(N_ATTN_HEADS):
            p = jnp.exp2(s_ref[slot, h] - m_ref[slot, h]).astype(bf16)
            hs = slice(h * ATTN_HEAD_DIM, (h + 1) * ATTN_HEAD_DIM)
            lhs = jnp.concatenate([vT_ref[hs, pl.ds(r0, kt_rows)], ones_rows], axis=0)
            pv = jnp.dot(lhs, p, preferred_element_type=f32)
            alpha = alpha_ref[slot, h]
            acc_ref[hs, :] = alpha * acc_ref[hs, :] + pv[:ATTN_HEAD_DIM]
            l_ref[h] = alpha * l_ref[h] + pv[ATTN_HEAD_DIM:ATTN_HEAD_DIM + 1]

    def attention(look_ahead):
        ahead = next_idx_tile if look_ahead else (lambda kt: None)

        def att_body(j, carry):
            kt = 2 * j
            logits_stage(kt + 1, 1)
            values_stage(kt, 0)
            ahead(kt)
            logits_stage(kt + 2, 0)
            values_stage(kt + 1, 1)
            ahead(kt + 1)
            return carry

        logits_stage(0, 0)
        lax.fori_loop(0, i // 2, att_body, 0)

        @pl.when(i % 2 == 1)
        def _():
            logits_stage(i, 1)
            values_stage(i - 1, 0)
            ahead(i - 1)
            values_stage(i, 1)
            ahead(i)

        @pl.when(i % 2 == 0)
        def _():
            values_stage(i, 0)
            ahead(i)

    @pl.when(i + 1 < pl.num_programs(1))
    def _():
        attention(True)
        idx_tile(i + 1, True, ibuf_next, iqTn_ref, iwTn_ref)

    @pl.when(i + 1 == pl.num_programs(1))
    def _():
        attention(False)

    for h in range(N_ATTN_HEADS):
        hs = slice(h * ATTN_HEAD_DIM, (h + 1) * ATTN_HEAD_DIM)
        acc_ref[hs, :] = acc_ref[hs, :] / l_ref[h]
    o_ref[...] = acc_ref[...].T.astype(o_ref.dtype)


def _search_hints(seq_len):
    n = (np.arange(seq_len) // CHUNK + 1) * CHUNK
    frac = np.minimum(TOPK_MAX / n, 0.5)
    nd = statistics.NormalDist()
    z_of = {f: nd.inv_cdf(1.0 - f) for f in np.unique(frac)}
    z = np.array([z_of[f] for f in frac])
    pdf = np.exp(-0.5 * z * z) / np.sqrt(2.0 * np.pi)
    return jnp.asarray(np.stack([z, frac / pdf]), f32)


def _dsa(qT, iqT, iwT, z3, vT, ik3):
    b, _, s = qT.shape
    qb = DSA_BLOCK
    last = s // qb - 1
    per_row = lambda shape, col: pl.BlockSpec(shape, lambda bi, i: (bi, 0, col),
                                              pipeline_mode=pl.Buffered(1))
    nxt = lambda rows: pl.BlockSpec((None, rows, qb), lambda bi, i: (bi, 0, jnp.minimum(i + 1, last)))
    return pl.pallas_call(
        functools.partial(_dsa_kernel, seq_len=s),
        grid=(b, s // qb),
        in_specs=[
            pl.BlockSpec((None, ATTN_W, qb), lambda bi, i: (bi, 0, i)),
            pl.BlockSpec((None, IDX_Q_W, qb), lambda bi, i: (bi, 0, i)),
            pl.BlockSpec((None, N_IDX_HEADS, qb), lambda bi, i: (bi, 0, i)),
            nxt(IDX_Q_W), nxt(N_IDX_HEADS),
            pl.BlockSpec((2, qb), lambda bi, i: (0, i)),
            per_row((None, s, COL_TILE), Z_AK),
            per_row((None, ATTN_W, s), 0),
            per_row((None, s, IDX_HEAD_DIM), 0),
        ],
        out_specs=pl.BlockSpec((None, qb, ATTN_W), lambda bi, i: (bi, i, 0)),
        out_shape=jax.ShapeDtypeStruct((b, s, ATTN_W), bf16),
        scratch_shapes=[
            pltpu.VMEM((2, s, qb), f32),
            pltpu.VMEM((4, SUBLANES, qb), f32),
            pltpu.VMEM((N_ATTN_HEADS, LANES, qb), bf16),
            pltpu.VMEM((2, N_ATTN_HEADS, qb, qb), f32),
            pltpu.VMEM((qb, qb), f32),
            pltpu.VMEM((2, N_ATTN_HEADS, 1, qb), f32),
            pltpu.VMEM((N_ATTN_HEADS, 1, qb), f32),
            pltpu.VMEM((2, N_ATTN_HEADS, 1, qb), f32),
            pltpu.VMEM((ATTN_W, qb), f32),
            pltpu.VMEM((1, qb), f32),
            pltpu.VMEM((3, qb), f32),
            pltpu.SMEM((1,), jnp.int32),
            pltpu.SMEM((1,), jnp.int32),
        ],
        compiler_params=_cparams(("parallel", "arbitrary")),
        name="dsa",
    )(qT, iqT, iwT, iqT, iwT, _search_hints(s), z3, vT, ik3)


def _gla_kernel(gin_ref, dec_ref, v_ref, r_ref, gn_ref, o_ref, state_ref):
    c = CHUNK
    n_batch = gin_ref.shape[0]

    @pl.when(pl.program_id(0) == 0)
    def _():
        state_ref[...] = jnp.zeros_like(state_ref)

    row = lax.broadcasted_iota(jnp.int32, (c, c), 0)
    col = lax.broadcasted_iota(jnp.int32, (c, c), 1)
    causal = row >= col
    lane = lax.broadcasted_iota(jnp.int32, (c, LANES), 1)
    half = (lane < GLA_HEAD_K, lane >= GLA_HEAD_K)
    contract_last = (((1,), (1,)), ((), ()))
    contract_first = (((0,), (0,)), ((), ()))
    zero = jnp.zeros((c, LANES), bf16)

    for ci, bi in [(ci, bi) for ci in range(GLA_TILE // c) for bi in range(n_batch)]:
        rows = slice(ci * c, (ci + 1) * c)
        decay = dec_ref[bi, ci:ci + 1, :]
        for pr in range(N_GLA_HEADS // 2):
            ps = slice(pr * LANES, (pr + 1) * LANES)
            part = lambda g: gin_ref[bi, rows, g * GLA_K_W + pr * LANES:
                                     g * GLA_K_W + (pr + 1) * LANES]
            q_in, q_mid, k_mid, k_out = part(0), part(1), part(2), part(3)
            st = state_ref[bi, pr]
            st_b = st.astype(bf16)
            upd = jnp.zeros_like(st)
            for hh in range(2):
                h = pr * 2 + hh
                vs = slice(h * GLA_HEAD_V, (h + 1) * GLA_HEAD_V)
                v_h = v_ref[bi, rows, vs]
                attn = lax.dot_general(jnp.where(half[hh], q_mid, zero), k_mid, contract_last,
                                       preferred_element_type=f32)
                attn = jnp.where(causal, attn, 0.0).astype(bf16)
                o = lax.dot_general(jnp.where(half[hh], q_in, zero), st_b, contract_last,
                                    preferred_element_type=f32)
                o = o + jnp.dot(attn, v_h, preferred_element_type=f32)
                upd = upd + lax.dot_general(v_h, jnp.where(half[hh], k_out, zero),
                                            contract_first, preferred_element_type=f32)
                y = _rms(o, gn_ref[...])
                r = r_ref[bi, rows, vs].astype(f32)
                o_ref[bi, rows, vs] = (y * r * _sigmoid(r)).astype(o_ref.dtype)
            state_ref[bi, pr] = decay[:, ps] * st + upd


def _gla(gin3, dec3, z3, gn):
    b, s, _ = z3.shape
    tb = GLA_TILE
    return pl.pallas_call(
        _gla_kernel,
        grid=(s // tb,),
        in_specs=[
            pl.BlockSpec((b, tb, GLA_IN_W), lambda j: (0, j, 0)),
            pl.BlockSpec((b, tb // CHUNK, GLA_K_W), lambda j: (0, j, 0)),
            pl.BlockSpec((b, tb, COL_TILE), lambda j: (0, j, Z_GV)),
            pl.BlockSpec((b, tb, COL_TILE), lambda j: (0, j, Z_GR)),
            pl.BlockSpec((1, GLA_HEAD_V), lambda j: (0, 0)),
        ],
        out_specs=pl.BlockSpec((b, tb, GLA_V_W), lambda j: (0, j, 0)),
        out_shape=jax.ShapeDtypeStruct((b, s, GLA_V_W), bf16),
        scratch_shapes=[pltpu.VMEM((b, N_GLA_HEADS // 2, GLA_HEAD_V, LANES), f32)],
        compiler_params=_cparams(("arbitrary",)),
        name="gla",
    )(gin3, dec3, z3, z3, gn)


def _tail_kernel(x_ref, oa_ref, ob_ref, ga_ref, gb_ref, wa_ref, wb_ref, wo_ref,
                 g_ref, wg_ref, wu_ref, wd_ref, p_ref, gp_ref, wpg_ref, wpp_ref, gf_ref,
                 o_ref, *, final):
    ya = jnp.dot(oa_ref[...], wa_ref[...], preferred_element_type=f32)
    yb = jnp.dot(ob_ref[...], wb_ref[...], preferred_element_type=f32)
    mg = _sigmoid(ga_ref[...].astype(f32)) * ya + _sigmoid(gb_ref[...].astype(f32)) * yb
    x = x_ref[...] + jnp.dot(mg.astype(bf16), wo_ref[...], preferred_element_type=f32)
    x = _swiglu_half_step(x, g_ref, wg_ref, wu_ref, wd_ref)
    h = _rms(x, gp_ref[...]).astype(bf16)
    gate = _sigmoid(jnp.dot(h, wpg_ref[...], preferred_element_type=f32))
    e = jnp.dot(p_ref[...].astype(bf16), wpp_ref[...], preferred_element_type=f32)
    y = x + gate * e
    if final:
        y = _rms(y, gf_ref[...])
    o_ref[...] = y


def _tail(x, oa, ob, z, wa, wb, wo, g, wg, wu, wd, p, gp, wpg, wpp, gf, final, out_shape):
    t = x.shape[0]
    tm = TOKEN_TILE
    rows = lambda width, col=0: pl.BlockSpec((tm, width), lambda i: (i, col))
    return pl.pallas_call(
        functools.partial(_tail_kernel, final=final),
        grid=(t // tm,),
        in_specs=[
            rows(D_MODEL), rows(ATTN_W), rows(GLA_V_W), rows(D_MODEL, Z_MGA), rows(D_MODEL, Z_MGB),
            _resident(wa), _resident(wb), _resident(wo),
            _resident(g), _resident(wg), _resident(wu), _resident(wd),
            rows(PLE_DIM), _resident(gp), _resident(wpg), _resident(wpp), _resident(gf),
        ],
        out_specs=_token_rows(out_shape),
        out_shape=jax.ShapeDtypeStruct(out_shape, f32),
        compiler_params=_cparams(("parallel",)),
        name="tail",
    )(x, oa, ob, z, z, wa, wb, wo, g, wg, wu, wd, p, gp, wpg, wpp, gf)


def _split_w_in(w_in):
    cuts = np.cumsum(np.array(SPLIT_SIZES))[:-1].tolist()
    aq, ak, av, iq, ik, iw, gq, gk, gv, gr, ga, mga, mgb = jnp.split(w_in, cuts, axis=-1)
    att_scale = ATTN_HEAD_DIM ** -0.5 * float(np.log2(np.e))
    idx_scale = IDX_HEAD_DIM ** -0.5 * N_IDX_HEADS ** -0.5
    c = lambda w: w.astype(bf16)
    main = jnp.concatenate([c(mga), c(mgb), c(ak), c(gv), c(gr)], axis=-1)
    feature_major = jnp.concatenate([c(aq * att_scale), c(av), c(iq)], axis=-1)
    gla_qk = jnp.concatenate([c(gq * (GLA_HEAD_K ** -0.5)), c(gk)], axis=-1)
    zeros = lambda n: jnp.zeros(w_in.shape[:-1] + (n,), bf16)
    small = jnp.concatenate(
        [c(ik), zeros(LANES - IDX_HEAD_DIM), c(iw * idx_scale), c(ga),
         zeros(LANES - N_IDX_HEADS - GLA_GATE_RANK)], axis=-1)
    return main, feature_major, gla_qk, small


def kernel(x, p, w_in, gla_gate_w2, gla_gate_b, gla_norm, w_branch_a, w_branch_b, w_out,
           norm_ff1, norm_mix, norm_ff2, norm_ple, ff1_w_gate, ff1_w_up, ff1_w_down,
           ff2_w_gate, ff2_w_up, ff2_w_down, ple_w_proj, ple_w_gate, norm_final):
    b, s, d = x.shape
    t = b * s
    depth = w_in.shape[0]
    assert d == D_MODEL and w_in.shape[1:] == (D_MODEL, sum(SPLIT_SIZES))
    assert s % TOKEN_TILE == 0 and s % GLA_TILE == 0 and s % DSA_BLOCK == 0
    assert ff1_w_gate.shape[1:] == (D_MODEL, D_FF) and D_FF % FF_TILE == 0
    w_main, w_fm, w_gqk, w_small = _split_w_in(w_in)
    cast = lambda w: w.astype(bf16)
    ff1 = (cast(ff1_w_gate), cast(ff1_w_up), cast(ff1_w_down))
    ff2 = (cast(ff2_w_gate), cast(ff2_w_up), cast(ff2_w_down))
    wa, wb, wo = cast(w_branch_a), cast(w_branch_b), cast(w_out)
    wpg, wpp = cast(ple_w_gate), cast(ple_w_proj)
    row = lambda v: v.reshape(1, -1)

    xf = x
    for l in range(depth):
        last = l == depth - 1
        xf = _ffn(xf, row(norm_ff1[l]), ff1[0][l], ff1[1][l], ff1[2][l])

        z, qT, vT, iqT, ik, iwT, gin, dec = _proj(
            xf, row(norm_mix[l]), w_main[l], w_fm[l], w_gqk[l], w_small[l],
            gla_gate_w2[l], row(gla_gate_b[l]), b)
        z3 = z.reshape(b, s, Z_WIDTH)
        oa = _dsa(qT, iqT, iwT, z3, vT, ik.reshape(b, s, IDX_HEAD_DIM))
        ob = _gla(gin.reshape(b, s, GLA_IN_W), dec.reshape(b, s // CHUNK, GLA_K_W), z3,
                  row(gla_norm[l]))
        xf = _tail(xf, oa.reshape(t, ATTN_W), ob.reshape(t, GLA_V_W), z, wa[l], wb[l], wo[l],
                   row(norm_ff2[l]), ff2[0][l], ff2[1][l], ff2[2][l],
                   p[l].reshape(t, PLE_DIM), row(norm_ple[l]), wpg[l], wpp[l],
                   row(norm_final), final=last, out_shape=(b, s, d) if last else (t, d))
    return xf
```
